```python
import math
import jax, jax.numpy as jnp
from jax import lax
import numpy as np

D_MODEL = 1024
BATCH = 8
SEQ = 2048
DEPTH = 2
DEC_BATCH = 8
DEC_SEQ = 64
PAST_LEN = 4096

CHUNK = 64
Q_BLOCK = 128
HEAD_DIM = 64
GW = D_MODEL // 4
GROUP_HEADS = GW // HEAD_DIM
D_MIX = 4 * GW
CONV_WIDTH = 31
GDN_CONV_WIDTH = 4
RWKV_W_LORA = 16
RWKV_A_LORA = 16
RWKV_G_LORA = 32
N_EXPERT_GROUPS = 4
EXPERTS_PER_GROUP = 4
N_EXPERTS = N_EXPERT_GROUPS * EXPERTS_PER_GROUP
TOP_K_IN_GROUP = 2
D_EXPERT = 256
DN_ALPHA = (2 * DEPTH) ** 0.25
DN_BETA = (8 * DEPTH) ** -0.25
LN_EPS = 1e-5
GN_EPS = 64e-5
F32 = jnp.float32

CONV_COLS = 2 * GW
GDN_COLS = 4 * GW + 2 * GROUP_HEADS
RWKV_COLS = 3 * GW + RWKV_W_LORA + RWKV_A_LORA + RWKV_G_LORA
FOX_COLS = 3 * GW + GROUP_HEADS
N_IN = CONV_COLS + GDN_COLS + RWKV_COLS + FOX_COLS

kernel_name = 'hymba_conformer_gdn_rwkv7_fox_hmoe_step'


def layer_norm(x, g, b):
    xf = x.astype(F32)
    mu = jnp.mean(xf, -1, keepdims=True)
    var = jnp.mean(jnp.square(xf - mu), -1, keepdims=True)
    return ((xf - mu) * lax.rsqrt(var + LN_EPS) * g + b).astype(x.dtype)


def l2norm(x):
    xf = x.astype(F32)
    return xf * lax.rsqrt(jnp.sum(xf * xf, -1, keepdims=True) + 1e-6)


def causal_dwconv(buf, x, w):
    xp = jnp.concatenate([buf.astype(x.dtype), x], axis=1)
    out = lax.conv_general_dilated(xp, w[:, None, :].astype(x.dtype), window_strides=(1,), padding='VALID',
                                   dimension_numbers=('NWC', 'WIO', 'NWC'), feature_group_count=x.shape[-1])
    return out, xp[:, xp.shape[1] - (w.shape[0] - 1):]


def conv_module(hA, buf, conv_w, conv_b, ln_g, ln_b):
    val, gate = jnp.split(hA, 2, axis=-1)
    u = val * jax.nn.sigmoid(gate)
    c, new_buf = causal_dwconv(buf, u, conv_w)
    c = layer_norm(c + conv_b, ln_g, ln_b)
    return jax.nn.silu(c.astype(F32)), new_buf


def gdn_chunk(S, q, k, v, g, beta):
    L = q.shape[2]
    dv = v.shape[-1]
    G = jnp.cumsum(g, axis=-1)
    idx = jnp.arange(L)
    incl = idx[:, None] >= idx[None, :]
    strict = idx[:, None] > idx[None, :]
    decay = jnp.exp(jnp.where(incl, G[..., :, None] - G[..., None, :], -jnp.inf))
    kb = k * beta[..., None]
    A = jnp.where(strict, jnp.einsum('bhid,bhjd->bhij', kb, k) * decay, 0.0)
    eG = jnp.exp(G)[..., None]
    rhs = jnp.concatenate([v * beta[..., None], kb * eG], axis=-1)
    X = lax.linalg.triangular_solve(A, rhs, left_side=True, lower=True, unit_diagonal=True)
    u = X[..., :dv] - jnp.einsum('bhlk,bhkv->bhlv', X[..., dv:], S)
    o = (jnp.einsum('bhlk,bhkv->bhlv', q * eG, S)
         + jnp.einsum('bhij,bhjv->bhiv', jnp.einsum('bhid,bhjd->bhij', q, k) * decay, u))
    gl = G[..., -1:]
    S = S * jnp.exp(gl)[..., None] + jnp.einsum('bhjk,bhjv->bhkv', k * jnp.exp(gl - G)[..., None], u)
    return S, o


def gdn_mixer(hB, conv_buf, S0, conv_w, a_log, dt_bias, norm_w):
    B_, L, _ = hB.shape
    qkv, z, a, b = jnp.split(hB, [3 * GW, 4 * GW, 4 * GW + GROUP_HEADS], axis=-1)
    qkv, new_buf = causal_dwconv(conv_buf, qkv, conv_w)
    q, k, v = jnp.split(jax.nn.silu(qkv).astype(F32), 3, axis=-1)
    to_heads = lambda t: t.reshape(B_, L, GROUP_HEADS, HEAD_DIM).transpose(0, 2, 1, 3)
    q = l2norm(to_heads(q)) * HEAD_DIM ** -0.5
    k = l2norm(to_heads(k))
    v = to_heads(v)
    g = (-jnp.exp(a_log.astype(F32)) * jax.nn.softplus(a.astype(F32) + dt_bias)).transpose(0, 2, 1)
    beta = jax.nn.sigmoid(b.astype(F32)).transpose(0, 2, 1)
    S0 = S0.astype(F32)
    if L <= CHUNK:
        S, o = gdn_chunk(S0, q, k, v, g, beta)
    else:
        n = L // CHUNK
        to_c = lambda t: jnp.moveaxis(t.reshape(t.shape[:2] + (n, CHUNK) + t.shape[3:]), 2, 0)
        S, oc = lax.scan(lambda s, xs: gdn_chunk(s, *xs), S0, tuple(to_c(t) for t in (q, k, v, g, beta)))
        o = jnp.moveaxis(oc, 0, 2).reshape(B_, GROUP_HEADS, L, HEAD_DIM)
    o = o.transpose(0, 2, 1, 3)
    o = o * lax.rsqrt(jnp.mean(o * o, -1, keepdims=True) + 1e-6) * norm_w
    o = o * jax.nn.silu(z.astype(F32)).reshape(B_, L, GROUP_HEADS, HEAD_DIM)
    return o.reshape(B_, L, GW), new_buf, S


def rwkv7_scan(S0, r, w, k, v, a, b):
    def step(S, inp):
        r_t, w_t, k_t, v_t, a_t, b_t = inp
        Sa = jnp.einsum('bhvk,bhk->bhv', S, a_t)
        S = S * w_t[:, :, None, :] + Sa[..., None] * b_t[:, :, None, :] + v_t[..., None] * k_t[:, :, None, :]
        return S, jnp.einsum('bhvk,bhk->bhv', S, r_t)
    xs = tuple(jnp.moveaxis(t.astype(F32), 1, 0) for t in (r, w, k, v, a, b))
    S, y = lax.scan(step, S0.astype(F32), xs)
    return jnp.moveaxis(y, 0, 1), S


def rwkv_mixer(hC, shift_buf, S0, mu, w0, w_up, a0, a_up, g_up, k_k, k_a, r_k, lnx_g, lnx_b):
    B_, L, _ = hC.shape
    hf = hC.astype(F32)
    prev = jnp.concatenate([shift_buf[:, None].astype(F32), hf[:, :-1]], axis=1)
    xm = hf + (prev - hf) * mu
    r, k, v, xw, xa, xg = jnp.split(xm, [GW, 2 * GW, 3 * GW, 3 * GW + RWKV_W_LORA,
                                        3 * GW + RWKV_W_LORA + RWKV_A_LORA], axis=-1)
    w_pre = w0 + jnp.tanh(xw) @ w_up
    w = jnp.exp(-jnp.exp(-jax.nn.softplus(-w_pre) - 0.5))
    a = jax.nn.sigmoid(a0 + xa @ a_up)
    gate = jax.nn.sigmoid(xg) @ g_up
    heads = lambda t: t.reshape(B_, L, GROUP_HEADS, HEAD_DIM)
    kk = l2norm(heads(k * k_k))
    k = k * (1.0 + (a - 1.0) * k_a)
    r_h, k_h, v_h, a_h = heads(r), heads(k), heads(v), heads(a)
    y, S = rwkv7_scan(S0, r_h, heads(w), k_h, v_h, -kk, kk * a_h)
    mu_y = jnp.mean(y, -1, keepdims=True)
    var_y = jnp.mean(jnp.square(y - mu_y), -1, keepdims=True)
    yn = ((y - mu_y) * lax.rsqrt(var_y + GN_EPS)).reshape(B_, L, GW) * lnx_g + lnx_b
    bonus = jnp.sum(r_h * k_h * r_k, -1, keepdims=True) * v_h
    out = (yn + bonus.reshape(B_, L, GW)) * gate
    return out, hC[:, -1], S


def fox_attend(q, Fq, qpos, k, v, Fk, kpos):
    s = jnp.einsum('bqhd,bkhd->bhqk', q, k).astype(F32) * HEAD_DIM ** -0.5
    s = s + Fq.transpose(0, 2, 1)[..., :, None] - Fk.transpose(0, 2, 1)[..., None, :]
    s = jnp.where(kpos[None, None, None, :] <= qpos[None, None, :, None], s, -jnp.inf)
    p = jax.nn.softmax(s, axis=-1)
    return jnp.einsum('bhqk,bkhd->bqhd', p.astype(v.dtype), v)


def fox_prompt(q, k, v, logf):
    B_, S, H, d = q.shape
    F = jnp.cumsum(logf, axis=1)
    nb = S // Q_BLOCK
    qb = q.reshape(B_, nb, Q_BLOCK, H, d).transpose(1, 0, 2, 3, 4)
    Fb = F.reshape(B_, nb, Q_BLOCK, H).transpose(1, 0, 2, 3)
    pb = jnp.arange(S).reshape(nb, Q_BLOCK)
    kpos = jnp.arange(S)
    out = lax.map(lambda xs: fox_attend(xs[0], xs[1], xs[2], k, v, F, kpos), (qb, Fb, pb))
    return out.transpose(1, 0, 2, 3, 4).reshape(B_, S, H * d)


def fox_sample(q, k, v, logf, ck, cv, clogf):
    B_, L, H, d = q.shape
    P = ck.shape[1]
    k_all = jnp.concatenate([ck.astype(k.dtype), k], axis=1)
    v_all = jnp.concatenate([cv.astype(v.dtype), v], axis=1)
    F = jnp.cumsum(jnp.concatenate([clogf.astype(F32), logf], axis=1), axis=1)
    out = fox_attend(q, F[:, P:], P + jnp.arange(L), k_all, v_all, F, jnp.arange(P + L))
    return out.reshape(B_, L, H * d)


def hier_moe(x, wg, bg, we, be, w_gate, w_up, w_down):
    B_, L, D = x.shape
    xt = x.reshape(-1, D)
    T = xt.shape[0]
    pg = jax.nn.softmax((xt @ wg).astype(F32) + bg, axis=-1)
    gp, gi = lax.top_k(pg, 1)
    le = ((xt @ we).astype(F32) + be).reshape(T, N_EXPERT_GROUPS, EXPERTS_PER_GROUP)
    le = jnp.take_along_axis(le, gi[:, :, None], axis=1)[:, 0]
    ep, ei = lax.top_k(jax.nn.softmax(le, axis=-1), TOP_K_IN_GROUP)
    wts = gp * ep / jnp.sum(ep, -1, keepdims=True)
    eid = gi * EXPERTS_PER_GROUP + ei
    combine = jnp.sum(jax.nn.one_hot(eid, N_EXPERTS, dtype=F32) * wts[..., None], axis=1)
    y = jnp.zeros((T, D), F32)
    for e in range(N_EXPERTS):
        h = jax.nn.silu(xt @ w_gate[e]) * (xt @ w_up[e])
        y = y + combine[:, e:e + 1] * (h @ w_down[e]).astype(F32)
    return y.reshape(B_, L, D).astype(x.dtype)


def trunk_layer(x, lp, conv_buf, gdn_buf, gdn_S, rw_shift, rw_S, fox_cache):
    h = jnp.einsum('bld,dn->bln', x, lp['w_in'])
    hA, hB, hC, hD = jnp.split(h, [CONV_COLS, CONV_COLS + GDN_COLS, CONV_COLS + GDN_COLS + RWKV_COLS], axis=-1)
    oA, conv_buf = conv_module(hA, conv_buf, lp['conv_w'], lp['conv_b'], lp['conv_ln_g'], lp['conv_ln_b'])
    oB, gdn_buf, gdn_S = gdn_mixer(hB, gdn_buf, gdn_S, lp['gdn_conv_w'], lp['gdn_a_log'], lp['gdn_dt_bias'],
                                   lp['gdn_norm_w'])
    oC, rw_shift, rw_S = rwkv_mixer(hC, rw_shift, rw_S, lp['rwkv_mu'], lp['rwkv_w0'], lp['rwkv_w_up'],
                                    lp['rwkv_a0'], lp['rwkv_a_up'], lp['rwkv_g_up'], lp['rwkv_k_k'],
                                    lp['rwkv_k_a'], lp['rwkv_r_k'], lp['rwkv_lnx_g'], lp['rwkv_lnx_b'])
    B_, L, _ = x.shape
    qD, kD, vD, fD = jnp.split(hD, [GW, 2 * GW, 3 * GW], axis=-1)
    heads = lambda t: t.reshape(B_, L, GROUP_HEADS, HEAD_DIM)
    qD, kD, vD = heads(qD), heads(kD), heads(vD)
    logf = jax.nn.log_sigmoid(fD.astype(F32) + lp['fox_b_f'])
    if fox_cache is None:
        oD = fox_prompt(qD, kD, vD, logf)
    else:
        oD = fox_sample(qD, kD, vD, logf, fox_cache[0], fox_cache[1], fox_cache[2])
    mix = jnp.concatenate([oA, oB, oC, oD.astype(F32)], axis=-1).astype(x.dtype)
    x = layer_norm(DN_ALPHA * x + mix @ lp['w_out'], lp['ln1_g'], lp['ln1_b'])
    m = hier_moe(x, lp['router_g_w'], lp['router_g_b'], lp['router_e_w'], lp['router_e_b'],
                 lp['exp_w_gate'], lp['exp_w_up'], lp['exp_w_down'])
    x = layer_norm(DN_ALPHA * x + m, lp['ln2_g'], lp['ln2_b'])
    return x, (conv_buf, gdn_buf, gdn_S, rw_shift, rw_S, kD, vD, logf)


def setup_inputs(seed: int = 0) -> dict:
    keys = iter(jax.random.split(jax.random.key(seed), 64))
    def nrm(shape, scale=1.0):
        return scale * jax.random.normal(next(keys), shape, F32)
    def uni(shape, lo, hi):
        return jax.random.uniform(next(keys), shape, F32, lo, hi)
    H, Dh, D, E, FF, G = GROUP_HEADS, HEAD_DIM, D_MODEL, N_EXPERTS, D_EXPERT, N_EXPERT_GROUPS
    fox_bias = jnp.linspace(1.0, 7.0, H, dtype=F32)
    dt = jnp.exp(uni((DEPTH, H), math.log(1e-3), math.log(1e-1)))
    return {
        'x_prompt': nrm((BATCH, SEQ, D)),
        'x_sample': nrm((DEC_BATCH, DEC_SEQ, D)),
        'cache_fox_k': nrm((DEPTH, DEC_BATCH, PAST_LEN, H, Dh)),
        'cache_fox_v': nrm((DEPTH, DEC_BATCH, PAST_LEN, H, Dh)),
        'cache_fox_logf': jax.nn.log_sigmoid(fox_bias + nrm((DEPTH, DEC_BATCH, PAST_LEN, H))),
        'state_conv': nrm((DEPTH, DEC_BATCH, CONV_WIDTH - 1, GW), 0.5),
        'state_gdn_conv': nrm((DEPTH, DEC_BATCH, GDN_CONV_WIDTH - 1, 3 * GW)),
        'state_gdn': nrm((DEPTH, DEC_BATCH, H, Dh, Dh), 0.1),
        'state_rwkv_shift': nrm((DEPTH, DEC_BATCH, RWKV_COLS)),
        'state_rwkv': nrm((DEPTH, DEC_BATCH, H, Dh, Dh), 0.1),
        'w_in': nrm((DEPTH, D, N_IN), D ** -0.5),
        'conv_w': nrm((DEPTH, CONV_WIDTH, GW), CONV_WIDTH ** -0.5),
        'conv_b': nrm((DEPTH, GW), 0.01),
        'conv_ln_g': 1.0 + nrm((DEPTH, GW), 0.01),
        'conv_ln_b': nrm((DEPTH, GW), 0.01),
        'gdn_conv_w': nrm((DEPTH, GDN_CONV_WIDTH, 3 * GW), GDN_CONV_WIDTH ** -0.5),
        'gdn_a_log': jnp.log(uni((DEPTH, H), 1.0, 16.0)),
        'gdn_dt_bias': jnp.log(jnp.expm1(dt)),
        'gdn_norm_w': 1.0 + nrm((DEPTH, Dh), 0.01),
        'rwkv_mu': uni((DEPTH, RWKV_COLS), 0.0, 1.0),
        'rwkv_w0': uni((DEPTH, GW), -6.0, -1.0),
        'rwkv_w_up': nrm((DEPTH, RWKV_W_LORA, GW), 0.1),
        'rwkv_a0': nrm((DEPTH, GW), 0.1),
        'rwkv_a_up': nrm((DEPTH, RWKV_A_LORA, GW), RWKV_A_LORA ** -0.5),
        'rwkv_g_up': nrm((DEPTH, RWKV_G_LORA, GW), RWKV_G_LORA ** -0.5),
        'rwkv_k_k': 0.85 + nrm((DEPTH, GW), 0.01),
        'rwkv_k_a': 1.0 + nrm((DEPTH, GW), 0.01),
        'rwkv_r_k': nrm((DEPTH, H, Dh), 0.1),
        'rwkv_lnx_g': 1.0 + nrm((DEPTH, GW), 0.01),
        'rwkv_lnx_b': nrm((DEPTH, GW), 0.01),
        'fox_b_f': fox_bias + nrm((DEPTH, H), 0.1),
        'w_out': nrm((DEPTH, D_MIX, D), DN_BETA * D_MIX ** -0.5),
        'ln1_g': 1.0 + nrm((DEPTH, D), 0.01),
        'ln1_b': nrm((DEPTH, D), 0.01),
        'router_g_w': nrm((DEPTH, D, G), D ** -0.5),
        'router_g_b': nrm((DEPTH, G), 0.01),
        'router_e_w': nrm((DEPTH, D, E), D ** -0.5),
        'router_e_b': nrm((DEPTH, E), 0.01),
        'exp_w_gate': nrm((DEPTH, E, D, FF), D ** -0.5),
        'exp_w_up': nrm((DEPTH, E, D, FF), D ** -0.5),
        'exp_w_down': nrm((DEPTH, E, FF, D), DN_BETA * FF ** -0.5),
        'ln2_g': 1.0 + nrm((DEPTH, D), 0.01),
        'ln2_b': nrm((DEPTH, D), 0.01),
    }


def reference(x_prompt, x_sample, cache_fox_k, cache_fox_v, cache_fox_logf, state_conv, state_gdn_conv,
              state_gdn, state_rwkv_shift, state_rwkv, w_in, conv_w, conv_b, conv_ln_g, conv_ln_b,
              gdn_conv_w, gdn_a_log, gdn_dt_bias, gdn_norm_w, rwkv_mu, rwkv_w0, rwkv_w_up, rwkv_a0,
              rwkv_a_up, rwkv_g_up, rwkv_k_k, rwkv_k_a, rwkv_r_k, rwkv_lnx_g, rwkv_lnx_b, fox_b_f,
              w_out, ln1_g, ln1_b, router_g_w, router_g_b, router_e_w, router_e_b,
              exp_w_gate, exp_w_up, exp_w_down, ln2_g, ln2_b):
    xp, xs = x_prompt, x_sample
    bp, dt = x_prompt.shape[0], x_prompt.dtype
    outs_p = [[] for _ in range(8)]
    outs_s = [[] for _ in range(8)]
    for l in range(DEPTH):
        lp = {'w_in': w_in[l], 'conv_w': conv_w[l], 'conv_b': conv_b[l], 'conv_ln_g': conv_ln_g[l],
              'conv_ln_b': conv_ln_b[l], 'gdn_conv_w': gdn_conv_w[l], 'gdn_a_log': gdn_a_log[l],
              'gdn_dt_bias': gdn_dt_bias[l], 'gdn_norm_w': gdn_norm_w[l], 'rwkv_mu': rwkv_mu[l],
              'rwkv_w0': rwkv_w0[l], 'rwkv_w_up': rwkv_w_up[l], 'rwkv_a0': rwkv_a0[l],
              'rwkv_a_up': rwkv_a_up[l], 'rwkv_g_up': rwkv_g_up[l], 'rwkv_k_k': rwkv_k_k[l],
              'rwkv_k_a': rwkv_k_a[l], 'rwkv_r_k': rwkv_r_k[l], 'rwkv_lnx_g': rwkv_lnx_g[l],
              'rwkv_lnx_b': rwkv_lnx_b[l], 'fox_b_f': fox_b_f[l], 'w_out': w_out[l],
              'ln1_g': ln1_g[l], 'ln1_b': ln1_b[l], 'router_g_w': router_g_w[l],
              'router_g_b': router_g_b[l], 'router_e_w': router_e_w[l], 'router_e_b': router_e_b[l],
              'exp_w_gate': exp_w_gate[l], 'exp_w_up': exp_w_up[l], 'exp_w_down': exp_w_down[l],
              'ln2_g': ln2_g[l], 'ln2_b': ln2_b[l]}
        xp, st_p = trunk_layer(xp, lp,
                               jnp.zeros((bp, CONV_WIDTH - 1, GW), dt),
                               jnp.zeros((bp, GDN_CONV_WIDTH - 1, 3 * GW), dt),
                               jnp.zeros((bp, GROUP_HEADS, HEAD_DIM, HEAD_DIM), F32),
                               jnp.zeros((bp, RWKV_COLS), dt),
                               jnp.zeros((bp, GROUP_HEADS, HEAD_DIM, HEAD_DIM), F32),
                               None)
        xs, st_s = trunk_layer(xs, lp, state_conv[l], state_gdn_conv[l], state_gdn[l],
                               state_rwkv_shift[l], state_rwkv[l],
                               (cache_fox_k[l], cache_fox_v[l], cache_fox_logf[l]))
        for i in range(8):
            outs_p[i].append(st_p[i])
            outs_s[i].append(st_s[i])
    p_conv, p_gdn_conv, p_gdn, p_rwkv_shift, p_rwkv, p_fox_k, p_fox_v, p_fox_logf = [jnp.stack(o) for o in outs_p]
    s_conv, s_gdn_conv, s_gdn, s_rwkv_shift, s_rwkv, s_fox_k, s_fox_v, s_fox_logf = [jnp.stack(o) for o in outs_s]
    return (xp, xs, p_conv, p_gdn_conv, p_gdn, p_rwkv_shift, p_rwkv, p_fox_k, p_fox_v, p_fox_logf,
            s_conv, s_gdn_conv, s_gdn, s_rwkv_shift, s_rwkv, s_fox_k, s_fox_v, s_fox_logf)
```

```python
import functools
import math

import jax
import jax.numpy as jnp
from jax import lax
from jax.experimental import pallas as pl
from jax.experimental.pallas import tpu as pltpu

F32 = jnp.float32
BF16 = jnp.bfloat16

D_MODEL = 1024
DEPTH = 2
CHUNK = 64
HEAD_DIM = 64
GW = D_MODEL // 4
GROUP_HEADS = GW // HEAD_DIM
CONV_WIDTH = 31
GDN_CONV_WIDTH = 4
RWKV_W_LORA = 16
RWKV_A_LORA = 16
RWKV_G_LORA = 32
N_EXPERT_GROUPS = 4
EXPERTS_PER_GROUP = 4
N_EXPERTS = N_EXPERT_GROUPS * EXPERTS_PER_GROUP
D_EXPERT = 256
DN_ALPHA = (2 * DEPTH) ** 0.25
LN_EPS = 1e-5
GN_EPS = 64e-5

CONV_COLS = 2 * GW
GDN_COLS = 4 * GW + 2 * GROUP_HEADS
RWKV_COLS = 3 * GW + RWKV_W_LORA + RWKV_A_LORA + RWKV_G_LORA
FOX_COLS = 3 * GW + GROUP_HEADS
LORA = RWKV_W_LORA + RWKV_A_LORA + RWKV_G_LORA

H_BQKV, H_CRKV, H_DQKV, H_AVAL, H_AGATE, H_BZ = 0, 768, 1536, 2304, 2560, 2816
H_MAIN = 3072
S_LORA, S_A, S_B, S_F = 0, 64, 68, 72
H_SMALL = 128
LANE = 128

VMEM_LIMIT = 56 * 1024 * 1024

NN = (((1,), (0,)), ((), ()))
NT = (((1,), (1,)), ((), ()))
TN = (((0,), (0,)), ((), ()))
NEG = -1e30


def _mm(a, b, dims=NN):
    return lax.dot_general(a.astype(BF16), b.astype(BF16), dims, preferred_element_type=F32)


def _mmh(a, b, dims=NN):
    return lax.dot_general(a, b, dims, preferred_element_type=F32, precision=lax.Precision.HIGHEST)


def _sigmoid(x):
    return 1.0 / (1.0 + jnp.exp(-x))


def _silu(x):
    return x * _sigmoid(x)


def _softplus(x):
    return jnp.maximum(x, 0.0) + jnp.log(1.0 + jnp.exp(-jnp.abs(x)))


def _iota2(shape, dim):
    return lax.broadcasted_iota(jnp.int32, shape, dim)


def _tril_ones(n, strict=False):
    r, c = _iota2((n, n), 0), _iota2((n, n), 1)
    return ((r > c) if strict else (r >= c)).astype(F32)


def _triu_ones(n):
    r, c = _iota2((n, n), 0), _iota2((n, n), 1)
    return (r <= c).astype(F32)


def _col_selector(rows, width, col0):
    r, c = _iota2((rows, width), 0), _iota2((rows, width), 1)
    return (c == r + col0).astype(F32)


def _layer_norm(x, g, b):
    mu = jnp.mean(x, axis=-1, keepdims=True)
    xc = x - mu
    var = jnp.mean(xc * xc, axis=-1, keepdims=True)
    return xc * lax.rsqrt(var + LN_EPS) * g + b


def _unit_lower_inverse(a):
    n = a.shape[0]
    r, c = _iota2((n, n), 0), _iota2((n, n), 1)
    eye = (r == c).astype(F32)
    d = jnp.where((r >> 3) == (c >> 3), a, 0.0)
    d2 = _mmh(d, d)
    d4 = _mmh(d2, d2)
    imd = eye - d
    p1 = imd + _mmh(imd, d2)
    t = p1 + _mmh(p1, d4)
    s = 3
    while (1 << s) < n:
        big = (r >> (s + 1)) == (c >> (s + 1))
        small = (r >> s) == (c >> s)
        lo = jnp.where(jnp.logical_and(big, jnp.logical_not(small)), a, 0.0)
        t = t - _mmh(t, _mmh(lo, t))
        s += 1
    return t


def _params(n_axes):
    return pltpu.CompilerParams(dimension_semantics=("arbitrary",) * n_axes,
                                vmem_limit_bytes=VMEM_LIMIT)


def _inproj_kernel(x_ref, wm_ref, ws_ref, h_ref, hs_ref):
    x = x_ref[...]
    h_ref[...] = _mm(x, wm_ref[...])
    hs_ref[...] = _mmh(x, ws_ref[...])


def _inproj(x2d, w_main, w_small):
    t = x2d.shape[0]
    tm = min(t, 512)
    return pl.pallas_call(
        _inproj_kernel,
        grid=(t // tm,),
        in_specs=[pl.BlockSpec((tm, D_MODEL), lambda i: (i, 0)),
                  pl.BlockSpec((D_MODEL, H_MAIN), lambda i: (0, 0)),
                  pl.BlockSpec((D_MODEL, H_SMALL), lambda i: (0, 0))],
        out_specs=[pl.BlockSpec((tm, H_MAIN), lambda i: (i, 0)),
                   pl.BlockSpec((tm, H_SMALL), lambda i: (i, 0))],
        out_shape=[jax.ShapeDtypeStruct((t, H_MAIN), F32), jax.ShapeDtypeStruct((t, H_SMALL), F32)],
        compiler_params=_params(1),
        name="inproj",
    )(x2d, w_main, w_small)


_HALO = 32


def _conv_kernel(val_ref, gate_ref, buf_ref, cw_ref, cb_ref, g_ref, b_ref, o_ref, nbuf_ref, win_ref, *, tl):
    i = pl.program_id(1)
    pad = _HALO - (CONV_WIDTH - 1)

    @pl.when(i == 0)
    def _():
        win_ref[0:8, :] = jnp.zeros((8, GW), F32)
        win_ref[pad:_HALO, :] = buf_ref[...]

    u = val_ref[...] * _sigmoid(gate_ref[...])
    win_ref[_HALO:_HALO + tl, :] = u
    sub = min(tl, 64)
    for r0 in range(0, tl, sub):
        acc = jnp.zeros((sub, GW), F32)
        for j in range(CONV_WIDTH):
            acc = acc + cw_ref[j:j + 1, :] * win_ref[pad + r0 + j:pad + r0 + j + sub, :]
        y = _layer_norm(acc + cb_ref[...], g_ref[...], b_ref[...])
        o_ref[r0:r0 + sub, :] = _silu(y)
    tail = win_ref[tl:tl + _HALO, :]
    win_ref[0:_HALO, :] = tail

    @pl.when(i == pl.num_programs(1) - 1)
    def _():
        nbuf_ref[...] = win_ref[pad:_HALO, :]


def _conv_mixer(h3, buf, cw, cb, g, b):
    bsz, l, _ = h3.shape
    tl = min(l, 256)
    kern = functools.partial(_conv_kernel, tl=tl)
    full = lambda shape: pl.BlockSpec(shape, lambda bi, i: (0,) * len(shape))
    return pl.pallas_call(
        kern,
        grid=(bsz, l // tl),
        in_specs=[pl.BlockSpec((None, tl, GW), lambda bi, i: (bi, i, H_AVAL // GW)),
                  pl.BlockSpec((None, tl, GW), lambda bi, i: (bi, i, H_AGATE // GW)),
                  pl.BlockSpec((None, CONV_WIDTH - 1, GW), lambda bi, i: (bi, 0, 0)),
                  full((CONV_WIDTH, GW)), full((1, GW)), full((1, GW)), full((1, GW))],
        out_specs=[pl.BlockSpec((None, tl, GW), lambda bi, i: (bi, i, 0)),
                   pl.BlockSpec((None, CONV_WIDTH - 1, GW), lambda bi, i: (bi, 0, 0))],
        out_shape=[jax.ShapeDtypeStruct((bsz, l, GW), F32),
                   jax.ShapeDtypeStruct((bsz, CONV_WIDTH - 1, GW), F32)],
        scratch_shapes=[pltpu.VMEM((tl + _HALO, GW), F32)],
        compiler_params=_params(2),
        name="conv_mixer",
    )(h3, h3, buf, cw, cb, g, b)


def _gdn_kernel(qkv_ref, z_ref, hs_ref, buf_ref, s0_ref, cw_ref, alog_r_ref, dtb_r_ref, alog_c_ref, dtb_c_ref,
                nw_ref, o_ref, sout_ref, win_ref, s_scr):
    c = pl.program_id(1)
    n = CHUNK
    kw = GDN_CONV_WIDTH - 1

    @pl.when(c == 0)
    def _():
        win_ref[0:8, :] = jnp.zeros((8, 3 * GW), F32)
        win_ref[8 - kw:8, :] = buf_ref[...]
        s_scr[...] = s0_ref[...]

    win_ref[8:8 + n, :] = qkv_ref[...]
    conv = jnp.zeros((n, 3 * GW), F32)
    for j in range(GDN_CONV_WIDTH):
        conv = conv + cw_ref[j:j + 1, :] * win_ref[8 - kw + j:8 - kw + j + n, :]
    tail = win_ref[n:n + 8, :]
    win_ref[0:8, :] = tail
    qkv = _silu(conv)

    hs = hs_ref[...]
    g_all = -jnp.exp(alog_r_ref[...]) * _softplus(hs + dtb_r_ref[...])
    gcum_all = _mmh(_tril_ones(n), g_all)
    beta_all = _sigmoid(hs)
    a_rows = _mmh(_col_selector(8, H_SMALL, S_A), hs, NT)
    g_rows = -jnp.exp(alog_c_ref[...]) * _softplus(a_rows + dtb_c_ref[...])
    gcum_rows = _mmh(g_rows, _triu_ones(n))

    r, cc = _iota2((n, n), 0), _iota2((n, n), 1)
    incl = r >= cc
    strict = r > cc
    z = z_ref[...]
    for h in range(GROUP_HEADS):
        sl = slice(h * HEAD_DIM, (h + 1) * HEAD_DIM)
        qh = qkv[:, h * HEAD_DIM:(h + 1) * HEAD_DIM]
        kh = qkv[:, GW + h * HEAD_DIM:GW + (h + 1) * HEAD_DIM]
        vh = qkv[:, 2 * GW + h * HEAD_DIM:2 * GW + (h + 1) * HEAD_DIM]
        qh = qh * lax.rsqrt(jnp.sum(qh * qh, -1, keepdims=True) + 1e-6) * (HEAD_DIM ** -0.5)
        kh = kh * lax.rsqrt(jnp.sum(kh * kh, -1, keepdims=True) + 1e-6)
        gc = gcum_all[:, S_A + h:S_A + h + 1]
        gr = gcum_rows[h:h + 1, :]
        bc = beta_all[:, S_B + h:S_B + h + 1]
        decay = jnp.exp(jnp.where(incl, gc - gr, NEG))
        kb = kh * bc
        a = jnp.where(strict, _mmh(kb, kh, NT) * decay, 0.0)
        t = _unit_lower_inverse(a)
        eg = jnp.exp(gc)
        xv = _mmh(t, vh * bc)
        xk = _mmh(t, kb * eg)
        s = s_scr[h]
        u = xv - _mmh(xk, s)
        qk = jnp.where(incl, _mmh(qh, kh, NT) * decay, 0.0)
        oh = _mmh(qh * eg, s) + _mmh(qk, u)
        gl = gc[n - 1:n, :]
        s_scr[h] = s * jnp.exp(gl) + _mmh(kh * jnp.exp(gl - gc), u, TN)
        oh = oh * lax.rsqrt(jnp.mean(oh * oh, -1, keepdims=True) + 1e-6) * nw_ref[...]
        o_ref[:, sl] = oh * _silu(z[:, sl])

    @pl.when(c == pl.num_programs(1) - 1)
    def _():
        sout_ref[...] = s_scr[...]


def _gdn_mixer(h3, hs3, buf, s0, cw, a_log, dt_bias, norm_w):
    bsz, l, _ = h3.shape
    n = CHUNK
    pad_r = lambda v, off: jnp.zeros((1, H_SMALL), F32).at[0, off:off + GROUP_HEADS].set(v)
    pad_c = lambda v: jnp.zeros((8, 1), F32).at[0:GROUP_HEADS, 0].set(v)
    full = lambda shape: pl.BlockSpec(shape, lambda bi, i: (0,) * len(shape))
    return pl.pallas_call(
        _gdn_kernel,
        grid=(bsz, l // n),
        in_specs=[pl.BlockSpec((None, n, 3 * GW), lambda bi, i: (bi, i, H_BQKV // (3 * GW))),
                  pl.BlockSpec((None, n, GW), lambda bi, i: (bi, i, H_BZ // GW)),
                  pl.BlockSpec((None, n, H_SMALL), lambda bi, i: (bi, i, 0)),
                  pl.BlockSpec((None, GDN_CONV_WIDTH - 1, 3 * GW), lambda bi, i: (bi, 0, 0)),
                  pl.BlockSpec((None, GROUP_HEADS, HEAD_DIM, HEAD_DIM), lambda bi, i: (bi, 0, 0, 0)),
                  full((GDN_CONV_WIDTH, 3 * GW)), full((1, H_SMALL)), full((1, H_SMALL)),
                  full((8, 1)), full((8, 1)), full((1, HEAD_DIM))],
        out_specs=[pl.BlockSpec((None, n, GW), lambda bi, i: (bi, i, 0)),
                   pl.BlockSpec((None, GROUP_HEADS, HEAD_DIM, HEAD_DIM), lambda bi, i: (bi, 0, 0, 0))],
        out_shape=[jax.ShapeDtypeStruct((bsz, l, GW), F32),
                   jax.ShapeDtypeStruct((bsz, GROUP_HEADS, HEAD_DIM, HEAD_DIM), F32)],
        scratch_shapes=[pltpu.VMEM((n + 8, 3 * GW), F32),
                        pltpu.VMEM((GROUP_HEADS, HEAD_DIM, HEAD_DIM), F32)],
        compiler_params=_params(2),
        name="gdn_mixer",
    )(h3, h3, hs3, buf, s0, cw, pad_r(a_log, S_A), pad_r(dt_bias, S_A), pad_c(a_log), pad_c(dt_bias),
      norm_w.reshape(1, HEAD_DIM))


def _rwkv_kernel(rkv_ref, hs_ref, sh_ref, shs_ref, s0_ref, mu_ref, mus_ref, w0_ref, wup_ref, a0_ref, aup_ref,
                 gup_ref, kk_ref, ka_ref, rk_ref, lng_ref, lnb_ref, o_ref, sout_ref, win_ref, wins_ref, s_scr):
    c = pl.program_id(1)
    n = CHUNK

    @pl.when(c == 0)
    def _():
        win_ref[0:8, :] = jnp.zeros((8, 3 * GW), F32)
        wins_ref[0:8, :] = jnp.zeros((8, H_SMALL), F32)
        win_ref[7:8, :] = sh_ref[...]
        wins_ref[7:8, :] = shs_ref[...]
        s_scr[...] = s0_ref[...]

    x = rkv_ref[...]
    xs = hs_ref[...]
    win_ref[8:8 + n, :] = x
    wins_ref[8:8 + n, :] = xs
    xm = x + (win_ref[7:7 + n, :] - x) * mu_ref[...]
    xms = xs + (wins_ref[7:7 + n, :] - xs) * mus_ref[...]
    last = win_ref[n + 7:n + 8, :]
    win_ref[7:8, :] = last
    lasts = wins_ref[n + 7:n + 8, :]
    wins_ref[7:8, :] = lasts

    rr = xm[:, 0:GW]
    kx = xm[:, GW:2 * GW]
    vv = xm[:, 2 * GW:3 * GW]
    w_pre = w0_ref[...] + _mmh(jnp.tanh(xms), wup_ref[...])
    logw = -jnp.exp(-_softplus(-w_pre) - 0.5)
    a_sig = _sigmoid(a0_ref[...] + _mmh(xms, aup_ref[...]))
    gate = _mmh(_sigmoid(xms), gup_ref[...])
    kkp = kx * kk_ref[...]
    k2 = kx * (1.0 + (a_sig - 1.0) * ka_ref[...])

    cum = _mmh(_tril_ones(n), logw)
    w_incl = jnp.exp(cum)
    w_excl = jnp.exp(cum - logw)
    w_inv = jnp.exp(-cum)
    w_last = w_incl[n - 1:n, :]

    r_i, c_i = _iota2((n, n), 0), _iota2((n, n), 1)
    incl = r_i >= c_i
    strict = r_i > c_i
    for h in range(GROUP_HEADS):
        sl = slice(h * HEAD_DIM, (h + 1) * HEAD_DIM)
        kkh = kkp[:, sl]
        kkh = kkh * lax.rsqrt(jnp.sum(kkh * kkh, -1, keepdims=True) + 1e-6)
        rh, kh, vh, ah = rr[:, sl], k2[:, sl], vv[:, sl], a_sig[:, sl]
        wi, we, wv, wl = w_incl[:, sl], w_excl[:, sl], w_inv[:, sl], w_last[:, sl]
        at = -kkh * we
        bt = kkh * ah * wv
        kt = kh * wv
        rt = rh * wi
        s0 = s_scr[h]
        aab = jnp.where(strict, _mmh(at, bt, NT), 0.0)
        aak = jnp.where(strict, _mmh(at, kt, NT), 0.0)
        t = _unit_lower_inverse(-aab)
        u = _mmh(t, _mmh(at, s0, NT) + _mmh(aak, vh))
        arb = jnp.where(incl, _mmh(rt, bt, NT), 0.0)
        ark = jnp.where(incl, _mmh(rt, kt, NT), 0.0)
        y = _mmh(rt, s0, NT) + _mmh(arb, u) + _mmh(ark, vh)
        s_scr[h] = s0 * wl + _mmh(u, bt * wl, TN) + _mmh(vh, kt * wl, TN)
        mu_y = jnp.mean(y, -1, keepdims=True)
        yc = y - mu_y
        var_y = jnp.mean(yc * yc, -1, keepdims=True)
        yn = yc * lax.rsqrt(var_y + GN_EPS) * lng_ref[:, sl] + lnb_ref[:, sl]
        bonus = jnp.sum(rh * kh * rk_ref[:, sl], -1, keepdims=True) * vh
        o_ref[:, sl] = (yn + bonus) * gate[:, sl]

    @pl.when(c == pl.num_programs(1) - 1)
    def _():
        sout_ref[...] = s_scr[...]


def _rwkv_mixer(h3, hs3, shift, s0, lp):
    bsz, l, _ = h3.shape
    n = CHUNK
    sh_main = shift[:, None, 0:3 * GW]
    sh_small = jnp.pad(shift[:, None, 3 * GW:], ((0, 0), (0, 0), (0, H_SMALL - LORA)))
    mu = lp['rwkv_mu']
    mu_main = mu[None, 0:3 * GW]
    mu_small = jnp.pad(mu[None, 3 * GW:], ((0, 0), (0, H_SMALL - LORA)))
    place = lambda w, off: jnp.zeros((H_SMALL, GW), F32).at[off:off + w.shape[0]].set(w)
    wup = place(lp['rwkv_w_up'], 0)
    aup = place(lp['rwkv_a_up'], RWKV_W_LORA)
    gup = place(lp['rwkv_g_up'], RWKV_W_LORA + RWKV_A_LORA)
    row = lambda v: v.reshape(1, GW)
    full = lambda shape: pl.BlockSpec(shape, lambda bi, i: (0,) * len(shape))
    return pl.pallas_call(
        _rwkv_kernel,
        grid=(bsz, l // n),
        in_specs=[pl.BlockSpec((None, n, 3 * GW), lambda bi, i: (bi, i, H_CRKV // (3 * GW))),
                  pl.BlockSpec((None, n, H_SMALL), lambda bi, i: (bi, i, 0)),
                  pl.BlockSpec((None, 1, 3 * GW), lambda bi, i: (bi, 0, 0)),
                  pl.BlockSpec((None, 1, H_SMALL), lambda bi, i: (bi, 0, 0)),
                  pl.BlockSpec((None, GROUP_HEADS, HEAD_DIM, HEAD_DIM), lambda bi, i: (bi, 0, 0, 0)),
                  full((1, 3 * GW)), full((1, H_SMALL)), full((1, GW)), full((H_SMALL, GW)), full((1, GW)),
                  full((H_SMALL, GW)), full((H_SMALL, GW)), full((1, GW)), full((1, GW)), full((1, GW)),
                  full((1, GW)), full((1, GW))],
        out_specs=[pl.BlockSpec((None, n, GW), lambda bi, i: (bi, i, 0)),
                   pl.BlockSpec((None, GROUP_HEADS, HEAD_DIM, HEAD_DIM), lambda bi, i: (bi, 0, 0, 0))],
        out_shape=[jax.ShapeDtypeStruct((bsz, l, GW), F32),
                   jax.ShapeDtypeStruct((bsz, GROUP_HEADS, HEAD_DIM, HEAD_DIM), F32)],
        scratch_shapes=[pltpu.VMEM((n + 8, 3 * GW), F32), pltpu.VMEM((n + 8, H_SMALL), F32),
                        pltpu.VMEM((GROUP_HEADS, HEAD_DIM, HEAD_DIM), F32)],
        compiler_params=_params(2),
        name="rwkv_mixer",
    )(h3, hs3, sh_main, sh_small, s0, mu_main, mu_small, row(lp['rwkv_w0']), wup, row(lp['rwkv_a0']), aup, gup,
      row(lp['rwkv_k_k']), row(lp['rwkv_k_a']), row(lp['rwkv_r_k']), row(lp['rwkv_lnx_g']),
      row(lp['rwkv_lnx_b']))


_FB = 128


def _log_forget(hs, bf):
    return -_softplus(-(hs + bf))


def _attend_block(qh, kb, vb, bias, m, l, acc):
    s = _mm(qh, kb, NT) + bias
    m_new = jnp.maximum(m, jnp.max(s, -1, keepdims=True))
    alpha = jnp.exp(m - m_new)
    p = jnp.exp(s - m_new)
    l = alpha * l + jnp.sum(p, -1, keepdims=True)
    acc = alpha * acc + _mm(p, vb)
    return m_new, l, acc


def _fox_prompt_kernel(q_ref, k_ref, v_ref, hs_ref, bf_ref, o_ref, lf_ref, fcol_scr, frow_scr, *, l, bq):
    i = pl.program_id(1)

    @pl.when(i == 0)
    def _():
        ltri, utri = _tril_ones(_FB), _triu_ones(_FB)
        sel = _col_selector(8, H_SMALL, S_F)
        carry_c = jnp.zeros((1, H_SMALL), F32)
        carry_r = jnp.zeros((8, 1), F32)
        for jb in range(l // _FB):
            rows = slice(jb * _FB, (jb + 1) * _FB)
            lf = _log_forget(hs_ref[rows, :], bf_ref[...])
            fc = _mmh(ltri, lf) + carry_c
            fcol_scr[rows, :] = fc
            carry_c = fc[_FB - 1:_FB, :]
            fr = _mmh(_mmh(sel, lf, NT), utri) + carry_r
            frow_scr[jb] = fr
            carry_r = fr[:, _FB - 1:_FB]
            lf_ref[rows, :] = lf[:, S_F:S_F + GROUP_HEADS]

    q = q_ref[...]
    fq = fcol_scr[pl.ds(pl.multiple_of(i * bq, bq), bq), :]
    r_i, c_i = _iota2((bq, bq), 0), _iota2((bq, bq), 1)
    causal = c_i <= r_i
    for h in range(GROUP_HEADS):
        sl = slice(h * HEAD_DIM, (h + 1) * HEAD_DIM)
        qh = q[:, sl] * (HEAD_DIM ** -0.5)
        fqh = fq[:, S_F + h:S_F + h + 1]

        def body(j, carry, sl=sl, qh=qh, fqh=fqh, h=h):
            rows = pl.ds(pl.multiple_of(j * bq, bq), bq)
            bias = fqh - frow_scr[j][h:h + 1, :]
            return _attend_block(qh, k_ref[rows, sl], v_ref[rows, sl], bias, *carry)

        init = (jnp.full((bq, 1), NEG, F32), jnp.zeros((bq, 1), F32), jnp.zeros((bq, HEAD_DIM), F32))
        m, lsum, acc = lax.fori_loop(0, i, body, init)
        rows = pl.ds(pl.multiple_of(i * bq, bq), bq)
        bias = jnp.where(causal, fqh - frow_scr[i][h:h + 1, :], NEG)
        m, lsum, acc = _attend_block(qh, k_ref[rows, sl], v_ref[rows, sl], bias, m, lsum, acc)
        o_ref[:, sl] = acc / lsum


def _fox_prompt(h3, hs3, bf_pad):
    bsz, l, _ = h3.shape
    bq = _FB
    kern = functools.partial(_fox_prompt_kernel, l=l, bq=bq)
    return pl.pallas_call(
        kern,
        grid=(bsz, l // bq),
        in_specs=[pl.BlockSpec((None, bq, GW), lambda bi, i: (bi, i, H_DQKV // GW)),
                  pl.BlockSpec((None, l, GW), lambda bi, i: (bi, 0, H_DQKV // GW + 1)),
                  pl.BlockSpec((None, l, GW), lambda bi, i: (bi, 0, H_DQKV // GW + 2)),
                  pl.BlockSpec((None, l, H_SMALL), lambda bi, i: (bi, 0, 0)),
                  pl.BlockSpec((1, H_SMALL), lambda bi, i: (0, 0))],
        out_specs=[pl.BlockSpec((None, bq, GW), lambda bi, i: (bi, i, 0)),
                   pl.BlockSpec((None, l, GROUP_HEADS), lambda bi, i: (bi, 0, 0))],
        out_shape=[jax.ShapeDtypeStruct((bsz, l, GW), F32),
                   jax.ShapeDtypeStruct((bsz, l, GROUP_HEADS), F32)],
        scratch_shapes=[pltpu.VMEM((l, H_SMALL), F32), pltpu.VMEM((l // _FB, 8, _FB), F32)],
        compiler_params=_params(2),
        name="fox_prompt",
    )(h3, h3, h3, hs3, bf_pad)


def _fox_sample_kernel(q_ref, k_ref, v_ref, hs_ref, bf_ref, ck_ref, cv_ref, clf_ref, o_ref, lf_ref, cum_scr,
                       *, l, p):
    nb = p // _FB
    utri = _triu_ones(_FB)

    def cum_body(jb, carry):
        fr = _mmh(clf_ref[jb], utri) + carry
        cum_scr[jb] = fr
        return fr[:, _FB - 1:_FB]

    total = lax.fori_loop(0, nb, cum_body, jnp.zeros((8, 1), F32))

    lf = _log_forget(hs_ref[...], bf_ref[...])
    lf_ref[...] = lf[:, S_F:S_F + GROUP_HEADS]
    cum_c = _mmh(_tril_ones(l), lf)
    cum_r = _mmh(_mmh(_col_selector(8, H_SMALL, S_F), lf, NT), _triu_ones(l))
    q = q_ref[...]
    r_i, c_i = _iota2((l, l), 0), _iota2((l, l), 1)
    causal = c_i <= r_i
    for h in range(GROUP_HEADS):
        sl = slice(h * HEAD_DIM, (h + 1) * HEAD_DIM)
        qh = q[:, sl] * (HEAD_DIM ** -0.5)
        fqh = cum_c[:, S_F + h:S_F + h + 1]
        tot_h = total[h:h + 1, :]

        def body(j, carry, sl=sl, qh=qh, fqh=fqh, h=h, tot_h=tot_h):
            rows = pl.ds(pl.multiple_of(j * _FB, _FB), _FB)
            bias = fqh + (tot_h - cum_scr[j][h:h + 1, :])
            return _attend_block(qh, ck_ref[rows, sl], cv_ref[rows, sl], bias, *carry)

        init = (jnp.full((l, 1), NEG, F32), jnp.zeros((l, 1), F32), jnp.zeros((l, HEAD_DIM), F32))
        m, lsum, acc = lax.fori_loop(0, nb, body, init)
        bias = jnp.where(causal, fqh - cum_r[h:h + 1, :], NEG)
        m, lsum, acc = _attend_block(qh, k_ref[:, sl], v_ref[:, sl], bias, m, lsum, acc)
        o_ref[:, sl] = acc / lsum


def _fox_sample(h3, hs3, bf_pad, ck, cv, clogf):
    bsz, l, _ = h3.shape
    p = ck.shape[1]
    ck2 = ck.reshape(bsz, p, GW)
    cv2 = cv.reshape(bsz, p, GW)
    clf = clogf.reshape(bsz, p // _FB, _FB, GROUP_HEADS).transpose(0, 1, 3, 2)
    clf = jnp.pad(clf, ((0, 0), (0, 0), (0, 8 - GROUP_HEADS), (0, 0)))
    kern = functools.partial(_fox_sample_kernel, l=l, p=p)
    return pl.pallas_call(
        kern,
        grid=(bsz,),
        in_specs=[pl.BlockSpec((None, l, GW), lambda bi: (bi, 0, H_DQKV // GW)),
                  pl.BlockSpec((None, l, GW), lambda bi: (bi, 0, H_DQKV // GW + 1)),
                  pl.BlockSpec((None, l, GW), lambda bi: (bi, 0, H_DQKV // GW + 2)),
                  pl.BlockSpec((None, l, H_SMALL), lambda bi: (bi, 0, 0)),
                  pl.BlockSpec((1, H_SMALL), lambda bi: (0, 0)),
                  pl.BlockSpec((None, p, GW), lambda bi: (bi, 0, 0)),
                  pl.BlockSpec((None, p, GW), lambda bi: (bi, 0, 0)),
                  pl.BlockSpec((None, p // _FB, 8, _FB), lambda bi: (bi, 0, 0, 0))],
        out_specs=[pl.BlockSpec((None, l, GW), lambda bi: (bi, 0, 0)),
                   pl.BlockSpec((None, l, GROUP_HEADS), lambda bi: (bi, 0, 0))],
        out_shape=[jax.ShapeDtypeStruct((bsz, l, GW), F32),
                   jax.ShapeDtypeStruct((bsz, l, GROUP_HEADS), F32)],
        scratch_shapes=[pltpu.VMEM((p // _FB, 8, _FB), F32)],
        compiler_params=_params(1),
        name="fox_sample",
    )(h3, h3, h3, hs3, bf_pad, ck2, cv2, clf)


def _outproj_kernel(oa_ref, ob_ref, oc_ref, od_ref, x_ref, w_ref, g_ref, b_ref, y_ref):
    acc = DN_ALPHA * x_ref[...]
    for gi, ref in enumerate((oa_ref, ob_ref, oc_ref, od_ref)):
        acc = acc + _mm(ref[...], w_ref[gi * GW:(gi + 1) * GW, :])
    y_ref[...] = _layer_norm(acc, g_ref[...], b_ref[...])


def _outproj(oa, ob, oc, od, x2d, w_out, g, b):
    t = x2d.shape[0]
    tm = min(t, 512)
    mix = pl.BlockSpec((tm, GW), lambda i: (i, 0))
    return pl.pallas_call(
        _outproj_kernel,
        grid=(t // tm,),
        in_specs=[mix, mix, mix, mix,
                  pl.BlockSpec((tm, D_MODEL), lambda i: (i, 0)),
                  pl.BlockSpec((D_MODEL, D_MODEL), lambda i: (0, 0)),
                  pl.BlockSpec((1, D_MODEL), lambda i: (0, 0)),
                  pl.BlockSpec((1, D_MODEL), lambda i: (0, 0))],
        out_specs=pl.BlockSpec((tm, D_MODEL), lambda i: (i, 0)),
        out_shape=jax.ShapeDtypeStruct((t, D_MODEL), F32),
        compiler_params=_params(1),
        name="outproj_ln",
    )(oa, ob, oc, od, x2d, w_out, g, b)


_R_G, _R_E = 0, N_EXPERT_GROUPS


def _moe_kernel(x_ref, wr_ref, br_ref, wg_ref, wu_ref, wd_ref, g_ref, b_ref, y_ref, comb_scr, xb_scr, acc_scr):
    e = pl.program_id(1)
    tm = x_ref.shape[0]
    lane = _iota2((tm, LANE), 1)

    @pl.when(e == 0)
    def _():
        x = x_ref[...]
        xb_scr[...] = x.astype(BF16)
        acc_scr[...] = jnp.zeros_like(acc_scr)
        logits = _mmh(x, wr_ref[...]) + br_ref[...]
        gmask = lane < N_EXPERT_GROUPS
        gl = jnp.where(gmask, logits, NEG)
        ge = jnp.exp(gl - jnp.max(gl, -1, keepdims=True))
        pg = ge / jnp.sum(ge, -1, keepdims=True)
        gp = jnp.max(pg, -1, keepdims=True)
        gi = jnp.min(jnp.where(jnp.logical_and(gmask, pg == gp), lane, LANE), -1, keepdims=True)
        emask = jnp.logical_and(jnp.logical_and(lane >= _R_E, lane < _R_E + N_EXPERTS),
                                ((lane - _R_E) >> 2) == gi)
        el = jnp.where(emask, logits, NEG)
        ee = jnp.exp(el - jnp.max(el, -1, keepdims=True))
        ep = ee / jnp.sum(ee, -1, keepdims=True)
        m1 = jnp.max(jnp.where(emask, ep, -1.0), -1, keepdims=True)
        i1 = jnp.min(jnp.where(jnp.logical_and(emask, ep == m1), lane, LANE), -1, keepdims=True)
        rest = jnp.logical_and(emask, lane != i1)
        m2 = jnp.max(jnp.where(rest, ep, -1.0), -1, keepdims=True)
        i2 = jnp.min(jnp.where(jnp.logical_and(rest, ep == m2), lane, LANE), -1, keepdims=True)
        den = m1 + m2
        comb_scr[...] = jnp.where(lane == i1, gp * m1 / den, jnp.where(lane == i2, gp * m2 / den, 0.0))

    ce = jnp.sum(jnp.where(lane == e + _R_E, comb_scr[...], 0.0), -1, keepdims=True)
    xb = xb_scr[...]
    hh = _silu(_mm(xb, wg_ref[...])) * _mm(xb, wu_ref[...])
    acc_scr[...] += ce * _mm(hh, wd_ref[...])

    @pl.when(e == pl.num_programs(1) - 1)
    def _():
        y_ref[...] = _layer_norm(DN_ALPHA * x_ref[...] + acc_scr[...], g_ref[...], b_ref[...])


def _moe(x2d, w_router, b_router, wg, wu, wd, g, b):
    t = x2d.shape[0]
    tm = min(t, 1024)
    return pl.pallas_call(
        _moe_kernel,
        grid=(t // tm, N_EXPERTS),
        in_specs=[pl.BlockSpec((tm, D_MODEL), lambda i, e: (i, 0)),
                  pl.BlockSpec((D_MODEL, LANE), lambda i, e: (0, 0)),
                  pl.BlockSpec((1, LANE), lambda i, e: (0, 0)),
                  pl.BlockSpec((None, D_MODEL, D_EXPERT), lambda i, e: (e, 0, 0)),
                  pl.BlockSpec((None, D_MODEL, D_EXPERT), lambda i, e: (e, 0, 0)),
                  pl.BlockSpec((None, D_EXPERT, D_MODEL), lambda i, e: (e, 0, 0)),
                  pl.BlockSpec((1, D_MODEL), lambda i, e: (0, 0)),
                  pl.BlockSpec((1, D_MODEL), lambda i, e: (0, 0))],
        out_specs=pl.BlockSpec((tm, D_MODEL), lambda i, e: (i, 0)),
        out_shape=jax.ShapeDtypeStruct((t, D_MODEL), F32),
        scratch_shapes=[pltpu.VMEM((tm, LANE), F32), pltpu.VMEM((tm, D_MODEL), BF16),
                        pltpu.VMEM((tm, D_MODEL), F32)],
        compiler_params=_params(2),
        name="moe_ln",
    )(x2d, w_router, b_router, wg, wu, wd, g, b)


def _prep_layer(lp):
    w = lp['w_in']
    c0 = CONV_COLS
    c1 = c0 + GDN_COLS
    c2 = c1 + RWKV_COLS
    main = jnp.concatenate([w[:, c0:c0 + 3 * GW], w[:, c1:c1 + 3 * GW], w[:, c2:c2 + 3 * GW],
                            w[:, 0:GW], w[:, GW:2 * GW], w[:, c0 + 3 * GW:c0 + 4 * GW]], axis=1).astype(BF16)
    small = jnp.concatenate([w[:, c1 + 3 * GW:c2], w[:, c0 + 4 * GW:c1], w[:, c2 + 3 * GW:],
                             jnp.zeros((D_MODEL, H_SMALL - LORA - 3 * GROUP_HEADS), F32)], axis=1)
    wr = jnp.concatenate([lp['router_g_w'], lp['router_e_w'],
                          jnp.zeros((D_MODEL, LANE - N_EXPERT_GROUPS - N_EXPERTS), F32)], axis=1)
    br = jnp.concatenate([lp['router_g_b'], lp['router_e_b'],
                          jnp.zeros((LANE - N_EXPERT_GROUPS - N_EXPERTS,), F32)])[None, :]
    bf_pad = jnp.zeros((1, H_SMALL), F32).at[0, S_F:S_F + GROUP_HEADS].set(lp['fox_b_f'])
    return dict(w_main=main, w_small=small, w_out=lp['w_out'].astype(BF16), wr=wr, br=br, bf_pad=bf_pad)


def _trunk_layer(x, lp, pp, conv_buf, gdn_buf, gdn_s, rw_shift, rw_s, fox_cache):
    bsz, l, d = x.shape
    x2d = x.reshape(bsz * l, d)
    h2, hs2 = _inproj(x2d, pp['w_main'], pp['w_small'])
    h3 = h2.reshape(bsz, l, H_MAIN)
    hs3 = hs2.reshape(bsz, l, H_SMALL)

    o_a, new_conv = _conv_mixer(h3, conv_buf, lp['conv_w'], lp['conv_b'][None], lp['conv_ln_g'][None],
                                lp['conv_ln_b'][None])
    o_b, new_gdn_s = _gdn_mixer(h3, hs3, gdn_buf, gdn_s, lp['gdn_conv_w'], lp['gdn_a_log'], lp['gdn_dt_bias'],
                                lp['gdn_norm_w'])
    o_c, new_rw_s = _rwkv_mixer(h3, hs3, rw_shift, rw_s, lp)
    if fox_cache is None:
        o_d, logf = _fox_prompt(h3, hs3, pp['bf_pad'])
    else:
        o_d, logf = _fox_sample(h3, hs3, pp['bf_pad'], *fox_cache)

    flat = lambda o: o.reshape(bsz * l, GW)
    x1 = _outproj(flat(o_a), flat(o_b), flat(o_c), flat(o_d), x2d, pp['w_out'], lp['ln1_g'][None],
                  lp['ln1_b'][None])
    x2 = _moe(x1, pp['wr'], pp['br'], lp['exp_w_gate'], lp['exp_w_up'], lp['exp_w_down'], lp['ln2_g'][None],
              lp['ln2_b'][None])

    kw = GDN_CONV_WIDTH - 1
    new_gdn_buf = h3[:, l - kw:, H_BQKV:H_BQKV + 3 * GW]
    new_shift = jnp.concatenate([h3[:, l - 1, H_CRKV:H_CRKV + 3 * GW], hs3[:, l - 1, S_LORA:S_LORA + LORA]], axis=-1)
    k_d = h3[:, :, H_DQKV + GW:H_DQKV + 2 * GW].reshape(bsz, l, GROUP_HEADS, HEAD_DIM)
    v_d = h3[:, :, H_DQKV + 2 * GW:H_DQKV + 3 * GW].reshape(bsz, l, GROUP_HEADS, HEAD_DIM)
    return x2.reshape(bsz, l, d), (new_conv, new_gdn_buf, new_gdn_s, new_shift, new_rw_s, k_d, v_d, logf)


_LAYER_KEYS = ('w_in', 'conv_w', 'conv_b', 'conv_ln_g', 'conv_ln_b', 'gdn_conv_w', 'gdn_a_log', 'gdn_dt_bias',
               'gdn_norm_w', 'rwkv_mu', 'rwkv_w0', 'rwkv_w_up', 'rwkv_a0', 'rwkv_a_up', 'rwkv_g_up', 'rwkv_k_k',
               'rwkv_k_a', 'rwkv_r_k', 'rwkv_lnx_g', 'rwkv_lnx_b', 'fox_b_f', 'w_out', 'ln1_g', 'ln1_b',
               'router_g_w', 'router_g_b', 'router_e_w', 'router_e_b', 'exp_w_gate', 'exp_w_up', 'exp_w_down',
               'ln2_g', 'ln2_b')


def kernel(x_prompt, x_sample, cache_fox_k, cache_fox_v, cache_fox_logf, state_conv, state_gdn_conv, state_gdn, state_rwkv_shift, state_rwkv, w_in, conv_w, conv_b, conv_ln_g, conv_ln_b, gdn_conv_w, gdn_a_log, gdn_dt_bias, gdn_norm_w, rwkv_mu, rwkv_w0, rwkv_w_up, rwkv_a0, rwkv_a_up, rwkv_g_up, rwkv_k_k, rwkv_k_a, rwkv_r_k, rwkv_lnx_g, rwkv_lnx_b, fox_b_f, w_out, ln1_g, ln1_b, router_g_w, router_g_b, router_e_w, router_e_b, exp_w_gate, exp_w_up, exp_w_down, ln2_g, ln2_b):
    weights = dict(zip(_LAYER_KEYS, (w_in, conv_w, conv_b, conv_ln_g, conv_ln_b, gdn_conv_w, gdn_a_log,
                                     gdn_dt_bias, gdn_norm_w, rwkv_mu, rwkv_w0, rwkv_w_up, rwkv_a0, rwkv_a_up,
                                     rwkv_g_up, rwkv_k_k, rwkv_k_a, rwkv_r_k, rwkv_lnx_g, rwkv_lnx_b, fox_b_f,
                                     w_out, ln1_g, ln1_b, router_g_w, router_g_b, router_e_w, router_e_b,
                                     exp_w_gate, exp_w_up, exp_w_down, ln2_g, ln2_b)))
    xp, xs = x_prompt, x_sample
    bp = x_prompt.shape[0]
    outs_p = [[] for _ in range(8)]
    outs_s = [[] for _ in range(8)]
    for l in range(DEPTH):
        lp = {k: v[l] for k, v in weights.items()}
        pp = _prep_layer(lp)
        xp, st_p = _trunk_layer(xp, lp, pp,
                                jnp.zeros((bp, CONV_WIDTH - 1, GW), F32),
                                jnp.zeros((bp, GDN_CONV_WIDTH - 1, 3 * GW), F32),
                                jnp.zeros((bp, GROUP_HEADS, HEAD_DIM, HEAD_DIM), F32),
                                jnp.zeros((bp, RWKV_COLS), F32),
                                jnp.zeros((bp, GROUP_HEADS, HEAD_DIM, HEAD_DIM), F32),
                                None)
        xs, st_s = _trunk_layer(xs, lp, pp, state_conv[l], state_gdn_conv[l], state_gdn[l], state_rwkv_shift[l],
                                state_rwkv[l], (cache_fox_k[l], cache_fox_v[l], cache_fox_logf[l]))
        for i in range(8):
            outs_p[i].append(st_p[i])
            outs_s[i].append(st_s[i])
    stacked_p = [jnp.stack(o) for o in outs_p]
    stacked_s = [jnp.stack(o) for o in outs_s]
    return (xp, xs, *stacked_p, *stacked_s)
```

```python
import functools

import jax
import jax.numpy as jnp
from jax import lax
from jax.experimental import pallas as pl
from jax.experimental.pallas import tpu as pltpu

F32 = jnp.float32
BF16 = jnp.bfloat16

D_MODEL = 1024
DEPTH = 2
CHUNK = 64
HEAD_DIM = 64
GW = D_MODEL // 4
GROUP_HEADS = GW // HEAD_DIM
CONV_WIDTH = 31
GDN_CONV_WIDTH = 4
RWKV_W_LORA = 16
RWKV_A_LORA = 16
RWKV_G_LORA = 32
N_EXPERT_GROUPS = 4
EXPERTS_PER_GROUP = 4
N_EXPERTS = N_EXPERT_GROUPS * EXPERTS_PER_GROUP
D_EXPERT = 256
DN_ALPHA = (2 * DEPTH) ** 0.25
LN_EPS = 1e-5
GN_EPS = 64e-5

CONV_COLS = 2 * GW
GDN_COLS = 4 * GW + 2 * GROUP_HEADS
RWKV_COLS = 3 * GW + RWKV_W_LORA + RWKV_A_LORA + RWKV_G_LORA
FOX_COLS = 3 * GW + GROUP_HEADS
LORA = RWKV_W_LORA + RWKV_A_LORA + RWKV_G_LORA

H_BQKV, H_CRKV, H_DQKV, H_AVAL, H_AGATE, H_BZ = 0, 768, 1536, 2304, 2560, 2816
H_MAIN = 3072
S_LORA, S_A, S_B, S_F = 0, 64, 68, 72
H_SMALL = 128
LANE = 128

VMEM_LIMIT = 56 * 1024 * 1024

NN = (((1,), (0,)), ((), ()))
NT = (((1,), (1,)), ((), ()))
TN = (((0,), (0,)), ((), ()))
NEG = -1e30


def _dot(a, b, dims=NN):
    return lax.dot_general(a, b, dims, preferred_element_type=F32)


def _mm(a, b, dims=NN):
    return _dot(a.astype(BF16), b.astype(BF16), dims)


def _mmh(a, b, dims=NN):
    return lax.dot_general(a, b, dims, preferred_element_type=F32, precision=lax.Precision.HIGHEST)


def _sigmoid(x):
    return 1.0 / (1.0 + jnp.exp(-x))


def _silu(x):
    return x * _sigmoid(x)


def _softplus(x):
    return jnp.maximum(x, 0.0) + jnp.log(1.0 + jnp.exp(-jnp.abs(x)))


def _iota2(shape, dim):
    return lax.broadcasted_iota(jnp.int32, shape, dim)


def _tril_ones(n, dtype=F32):
    r, c = _iota2((n, n), 0), _iota2((n, n), 1)
    return (r >= c).astype(dtype)


def _triu_ones(n, dtype=F32):
    r, c = _iota2((n, n), 0), _iota2((n, n), 1)
    return (r <= c).astype(dtype)


def _col_selector(rows, width, col0):
    r, c = _iota2((rows, width), 0), _iota2((rows, width), 1)
    return (c == r + col0).astype(F32)


def _layer_norm(x, g, b):
    mu = jnp.mean(x, axis=-1, keepdims=True)
    xc = x - mu
    var = jnp.mean(xc * xc, axis=-1, keepdims=True)
    return xc * lax.rsqrt(var + LN_EPS) * g + b


def _split2(x):
    hi = x.astype(BF16)
    lo = (x - hi.astype(F32)).astype(BF16)
    return hi, lo


def _split3(x):
    hi = x.astype(BF16)
    r = x - hi.astype(F32)
    mid = r.astype(BF16)
    lo = (r - mid.astype(F32)).astype(BF16)
    return hi, mid, lo


class _Wide:
    def __init__(self):
        self.i = _iota2((CHUNK, GW), 0)
        lane = _iota2((CHUNK, GW), 1)
        self.j = lane & (HEAD_DIM - 1)
        self.head = lane >> 6
        self.incl = self.i >= self.j
        self.strict = self.i > self.j
        self.eye = self.i == self.j
        r, c = _iota2((GW, GW), 0), _iota2((GW, GW), 1)
        self.block_diag = (r >> 6) == (c >> 6)


def _head_block_diag(x, w):
    t = jnp.concatenate([x] * GROUP_HEADS, axis=0)
    return jnp.where(w.block_diag, t, jnp.zeros_like(t))


def _collapse_heads(full, w):
    out = jnp.zeros((HEAD_DIM, GW), F32)
    for h in range(GROUP_HEADS):
        out = out + jnp.where(w.head == h, full[h * HEAD_DIM:(h + 1) * HEAD_DIM, :], 0.0)
    return out


def _lhs3(a):
    hi, lo = _split2(a)
    return jnp.concatenate([hi, lo], axis=0), hi


def _rhs3(b, w=None):
    hi, lo = _split2(b)
    if w is None:
        return hi, lo
    return _head_block_diag(hi, w), _head_block_diag(lo, w)


def _mm3(lhs, rhs, dims=NN):
    cat, hi = lhs
    bh, bl = rhs
    m = hi.shape[0]
    r = _dot(cat, bh, dims)
    return r[:m] + r[m:] + _dot(hi, bl, dims)


def _mm3_tn(a, b):
    ah, al = _split2(a)
    bh, bl = _split2(b)
    m = a.shape[1]
    r = _dot(jnp.concatenate([ah, al], axis=1), bh, TN)
    return r[:m] + r[m:] + _dot(ah, bl, TN)


def _mm_x01(a, m01):
    m = a.shape[0]
    r = _dot(jnp.concatenate(_split3(a), axis=0), m01)
    return r[:m] + r[m:2 * m] + r[2 * m:]


def _tri_mm(tri01, x):
    n = x.shape[1]
    r = _dot(tri01, jnp.concatenate(_split3(x), axis=1))
    return r[:, :n] + r[:, n:2 * n] + r[:, 2 * n:]


def _unit_lower_inverse_wide(mats, w):
    eye = w.eye.astype(F32)
    same8 = (w.i >> 3) == (w.j >> 3)
    ds = [jnp.where(same8, a, 0.0) for a in mats]
    d2 = [_mm3(_lhs3(d), _rhs3(d, w)) for d in ds]
    d2_r = [_rhs3(x, w) for x in d2]
    d4 = [_mm3(_lhs3(x), r) for x, r in zip(d2, d2_r)]
    imd = [eye - d for d in ds]
    p1 = [i + _mm3(_lhs3(i), r) for i, r in zip(imd, d2_r)]
    ts = [p + _mm3(_lhs3(p), _rhs3(x, w)) for p, x in zip(p1, d4)]
    for s in range(3, 6):
        big = (w.i >> (s + 1)) == (w.j >> (s + 1))
        small = (w.i >> s) == (w.j >> s)
        sel = jnp.logical_and(big, jnp.logical_not(small))
        inner = [_mm3(_lhs3(jnp.where(sel, a, 0.0)), _rhs3(t, w)) for a, t in zip(mats, ts)]
        ts = [t - _mm3(_lhs3(t), _rhs3(x, w)) for t, x in zip(ts, inner)]
    return ts


def _params(n_axes):
    return pltpu.CompilerParams(dimension_semantics=("arbitrary",) * n_axes,
                                vmem_limit_bytes=VMEM_LIMIT)


def _heads_to_wide(s):
    b = s.shape[0]
    return s.transpose(0, 2, 1, 3).reshape(b, HEAD_DIM, GW)


def _wide_to_heads(s):
    b = s.shape[0]
    return s.reshape(b, HEAD_DIM, GROUP_HEADS, HEAD_DIM).transpose(0, 2, 1, 3)


def _ones_block_diag():
    r, c = jnp.arange(GW)[:, None], jnp.arange(GW)[None, :]
    return ((r // HEAD_DIM) == (c // HEAD_DIM)).astype(BF16)


def _lane_spread(col0):
    r, c = jnp.arange(H_SMALL)[:, None], jnp.arange(GW)[None, :]
    return ((r - col0) == (c // HEAD_DIM)).astype(BF16)


def _inproj_kernel(x_ref, wm_ref, ws_ref, h_ref, hs_ref):
    x = x_ref[...]
    h_ref[...] = _mm(x, wm_ref[...])
    hs_ref[...] = _mmh(x, ws_ref[...])


def _inproj(x2d, w_main, w_small):
    t = x2d.shape[0]
    tm = min(t, 512)
    return pl.pallas_call(
        _inproj_kernel,
        grid=(t // tm,),
        in_specs=[pl.BlockSpec((tm, D_MODEL), lambda i: (i, 0)),
                  pl.BlockSpec((D_MODEL, H_MAIN), lambda i: (0, 0)),
                  pl.BlockSpec((D_MODEL, H_SMALL), lambda i: (0, 0))],
        out_specs=[pl.BlockSpec((tm, H_MAIN), lambda i: (i, 0)),
                   pl.BlockSpec((tm, H_SMALL), lambda i: (i, 0))],
        out_shape=[jax.ShapeDtypeStruct((t, H_MAIN), F32), jax.ShapeDtypeStruct((t, H_SMALL), F32)],
        compiler_params=_params(1),
        name="inproj",
    )(x2d, w_main, w_small)


_HALO = 32


def _conv_kernel(val_ref, gate_ref, buf_ref, cw_ref, cb_ref, g_ref, b_ref, o_ref, nbuf_ref, win_ref, *, tl):
    i = pl.program_id(1)
    pad = _HALO - (CONV_WIDTH - 1)

    @pl.when(i == 0)
    def _():
        win_ref[0:8, :] = jnp.zeros((8, GW), F32)
        win_ref[pad:_HALO, :] = buf_ref[...]

    u = val_ref[...] * _sigmoid(gate_ref[...])
    win_ref[_HALO:_HALO + tl, :] = u
    sub = min(tl, 64)
    for r0 in range(0, tl, sub):
        acc = jnp.zeros((sub, GW), F32)
        for j in range(CONV_WIDTH):
            acc = acc + cw_ref[j:j + 1, :] * win_ref[pad + r0 + j:pad + r0 + j + sub, :]
        y = _layer_norm(acc + cb_ref[...], g_ref[...], b_ref[...])
        o_ref[r0:r0 + sub, :] = _silu(y)
    tail = win_ref[tl:tl + _HALO, :]
    win_ref[0:_HALO, :] = tail

    @pl.when(i == pl.num_programs(1) - 1)
    def _():
        nbuf_ref[...] = win_ref[pad:_HALO, :]


def _conv_mixer(h3, buf, cw, cb, g, b):
    bsz, l, _ = h3.shape
    tl = min(l, 256)
    kern = functools.partial(_conv_kernel, tl=tl)
    full = lambda shape: pl.BlockSpec(shape, lambda bi, i: (0,) * len(shape))
    return pl.pallas_call(
        kern,
        grid=(bsz, l // tl),
        in_specs=[pl.BlockSpec((None, tl, GW), lambda bi, i: (bi, i, H_AVAL // GW)),
                  pl.BlockSpec((None, tl, GW), lambda bi, i: (bi, i, H_AGATE // GW)),
                  pl.BlockSpec((None, CONV_WIDTH - 1, GW), lambda bi, i: (bi, 0, 0)),
                  full((CONV_WIDTH, GW)), full((1, GW)), full((1, GW)), full((1, GW))],
        out_specs=[pl.BlockSpec((None, tl, GW), lambda bi, i: (bi, i, 0)),
                   pl.BlockSpec((None, CONV_WIDTH - 1, GW), lambda bi, i: (bi, 0, 0))],
        out_shape=[jax.ShapeDtypeStruct((bsz, l, GW), F32),
                   jax.ShapeDtypeStruct((bsz, CONV_WIDTH - 1, GW), F32)],
        scratch_shapes=[pltpu.VMEM((tl + _HALO, GW), F32)],
        compiler_params=_params(2),
        name="conv_mixer",
    )(h3, h3, buf, cw, cb, g, b)


def _gdn_kernel(qkv_ref, z_ref, hs_ref, buf_ref, s0_ref, cw_ref, alog_ref, dtb_ref, nw_ref, ones_ref, sela_ref,
                selb_ref, o_ref, sout_ref, win_ref, s_scr, *, nb, nc):
    c = pl.program_id(1)
    n = CHUNK
    rows = nc * n
    kw = GDN_CONV_WIDTH - 1
    items = [(b, ci) for b in range(nb) for ci in range(nc)]

    @pl.when(c == 0)
    def _():
        for b in range(nb):
            win_ref[b, 0:8, :] = jnp.zeros((8, 3 * GW), F32)
            win_ref[b, 8 - kw:8, :] = buf_ref[b]
        s_scr[...] = s0_ref[...]

    qkvs = []
    for b in range(nb):
        win_ref[b, 8:8 + rows, :] = qkv_ref[b]
        conv = jnp.zeros((rows, 3 * GW), F32)
        for j in range(GDN_CONV_WIDTH):
            conv = conv + cw_ref[j:j + 1, :] * win_ref[b, 8 - kw + j:8 - kw + j + rows, :]
        tail = win_ref[b, rows:rows + 8, :]
        win_ref[b, 0:8, :] = tail
        qkvs.append(_silu(conv))

    w = _Wide()
    ones_bd = ones_ref[...]
    tril = _tril_ones(n, BF16)
    sl = lambda ci: slice(ci * n, (ci + 1) * n)
    qs = [qkvs[b][sl(ci), 0:GW] for b, ci in items]
    ks = [qkvs[b][sl(ci), GW:2 * GW] for b, ci in items]
    vs = [qkvs[b][sl(ci), 2 * GW:3 * GW] for b, ci in items]
    sss = [_mm_x01(jnp.concatenate([q * q, k * k], axis=0), ones_bd) for q, k in zip(qs, ks)]
    qs = [q * lax.rsqrt(ss[:n] + 1e-6) * (HEAD_DIM ** -0.5) for q, ss in zip(qs, sss)]
    ks = [k * lax.rsqrt(ss[n:] + 1e-6) for k, ss in zip(ks, sss)]

    hss = [hs_ref[b, sl(ci), :] for b, ci in items]
    gs = [_mm_x01(-jnp.exp(alog_ref[...]) * _softplus(hs + dtb_ref[...]), sela_ref[...]) for hs in hss]
    betas = [_mm_x01(_sigmoid(hs), selb_ref[...]) for hs in hss]
    gcs = [_tri_mm(tril, g) for g in gs]
    grs = [jnp.sum(jnp.where(w.eye, gc, 0.0), axis=0, keepdims=True) for gc in gcs]
    decays = [jnp.exp(jnp.where(w.incl, gc - gr, NEG)) for gc, gr in zip(gcs, grs)]

    kbs = [k * beta for k, beta in zip(ks, betas)]
    grams = [_mm3(_lhs3(jnp.concatenate([kb, q], axis=0)), _rhs3(k, w), NT)
             for kb, q, k in zip(kbs, qs, ks)]
    amats = [jnp.where(w.strict, gram[:n] * decay, 0.0) for gram, decay in zip(grams, decays)]
    qks = [jnp.where(w.incl, gram[n:] * decay, 0.0) for gram, decay in zip(grams, decays)]
    t_ls = [_lhs3(t) for t in _unit_lower_inverse_wide(amats, w)]
    egs = [jnp.exp(gc) for gc in gcs]
    xvs = [_mm3(t_l, _rhs3(v * beta, w)) for t_l, v, beta in zip(t_ls, vs, betas)]
    xks = [_mm3(t_l, _rhs3(kb * eg, w)) for t_l, kb, eg in zip(t_ls, kbs, egs)]
    xq_ls = [_lhs3(jnp.concatenate([xk, q * eg], axis=0)) for xk, q, eg in zip(xks, qs, egs)]
    qk_ls = [_lhs3(qk) for qk in qks]
    gls = [gc[n - 1:n, :] for gc in gcs]
    kds = [k * jnp.exp(gl - gc) for k, gl, gc in zip(ks, gls, gcs)]

    states = [s_scr[b] for b in range(nb)]
    outs = {}
    for ci in range(nc):
        for b in range(nb):
            it = b * nc + ci
            s = states[b]
            rs = _mm3(xq_ls[it], _rhs3(s, w))
            u = xvs[it] - rs[:n]
            outs[it] = rs[n:] + _mm3(qk_ls[it], _rhs3(u, w))
            states[b] = s * jnp.exp(gls[it]) + _collapse_heads(_mm3_tn(kds[it], u), w)
    for b in range(nb):
        s_scr[b] = states[b]

    for it, (b, ci) in enumerate(items):
        o = outs[it]
        ms = _mm_x01(o * o, ones_bd) * (1.0 / HEAD_DIM)
        o_ref[b, sl(ci), :] = o * lax.rsqrt(ms + 1e-6) * nw_ref[...] * _silu(z_ref[b, sl(ci), :])

    @pl.when(c == pl.num_programs(1) - 1)
    def _():
        sout_ref[...] = s_scr[...]


def _recurrent_tiling(bsz, l):
    nc = min(l // CHUNK, 4)
    nb = 1 if nc > 1 else min(bsz, 4)
    return nb, nc


def _gdn_mixer(h3, hs3, buf, s0, cw, a_log, dt_bias, norm_w):
    bsz, l, _ = h3.shape
    nb, nc = _recurrent_tiling(bsz, l)
    rows = nc * CHUNK
    pad_r = lambda v, off: jnp.zeros((1, H_SMALL), F32).at[0, off:off + GROUP_HEADS].set(v)
    full = lambda shape: pl.BlockSpec(shape, lambda bi, i: (0,) * len(shape))
    o, s_new = pl.pallas_call(
        functools.partial(_gdn_kernel, nb=nb, nc=nc),
        grid=(bsz // nb, l // rows),
        in_specs=[pl.BlockSpec((nb, rows, 3 * GW), lambda bi, i: (bi, i, H_BQKV // (3 * GW))),
                  pl.BlockSpec((nb, rows, GW), lambda bi, i: (bi, i, H_BZ // GW)),
                  pl.BlockSpec((nb, rows, H_SMALL), lambda bi, i: (bi, i, 0)),
                  pl.BlockSpec((nb, GDN_CONV_WIDTH - 1, 3 * GW), lambda bi, i: (bi, 0, 0)),
                  pl.BlockSpec((nb, HEAD_DIM, GW), lambda bi, i: (bi, 0, 0)),
                  full((GDN_CONV_WIDTH, 3 * GW)), full((1, H_SMALL)), full((1, H_SMALL)), full((1, GW)),
                  full((GW, GW)), full((H_SMALL, GW)), full((H_SMALL, GW))],
        out_specs=[pl.BlockSpec((nb, rows, GW), lambda bi, i: (bi, i, 0)),
                   pl.BlockSpec((nb, HEAD_DIM, GW), lambda bi, i: (bi, 0, 0))],
        out_shape=[jax.ShapeDtypeStruct((bsz, l, GW), F32),
                   jax.ShapeDtypeStruct((bsz, HEAD_DIM, GW), F32)],
        scratch_shapes=[pltpu.VMEM((nb, rows + 8, 3 * GW), F32), pltpu.VMEM((nb, HEAD_DIM, GW), F32)],
        compiler_params=_params(2),
        name="gdn_mixer",
    )(h3, h3, hs3, buf, _heads_to_wide(s0), cw, pad_r(a_log, S_A), pad_r(dt_bias, S_A),
      jnp.tile(norm_w, GROUP_HEADS)[None, :], _ones_block_diag(), _lane_spread(S_A), _lane_spread(S_B))
    return o, _wide_to_heads(s_new)


def _rwkv_kernel(rkv_ref, hs_ref, sh_ref, shs_ref, s0_ref, mu_ref, mus_ref, w0_ref, wup_ref, a0_ref, aup_ref,
                 gup_ref, kk_ref, ka_ref, rk_ref, lng_ref, lnb_ref, ones_ref, o_ref, sout_ref, win_ref, wins_ref,
                 s_scr, *, nb, nc):
    c = pl.program_id(1)
    n = CHUNK

    rows = nc * n
    items = [(b, ci) for b in range(nb) for ci in range(nc)]

    @pl.when(c == 0)
    def _():
        for b in range(nb):
            win_ref[b, 0:8, :] = jnp.zeros((8, 3 * GW), F32)
            wins_ref[b, 0:8, :] = jnp.zeros((8, H_SMALL), F32)
            win_ref[b, 7:8, :] = sh_ref[b]
            wins_ref[b, 7:8, :] = shs_ref[b]
        s_scr[...] = s0_ref[...]

    wup_r, aup_r, gup_r = _rhs3(wup_ref[...]), _rhs3(aup_ref[...]), _rhs3(gup_ref[...])
    per_b = []
    for b in range(nb):
        x = rkv_ref[b]
        xs = hs_ref[b]
        win_ref[b, 8:8 + rows, :] = x
        wins_ref[b, 8:8 + rows, :] = xs
        xm = x + (win_ref[b, 7:7 + rows, :] - x) * mu_ref[...]
        xms = xs + (wins_ref[b, 7:7 + rows, :] - xs) * mus_ref[...]
        last = win_ref[b, rows + 7:rows + 8, :]
        win_ref[b, 7:8, :] = last
        lasts = wins_ref[b, rows + 7:rows + 8, :]
        wins_ref[b, 7:8, :] = lasts
        w_pre = w0_ref[...] + _mm3(_lhs3(jnp.tanh(xms)), wup_r)
        logw = -jnp.exp(-_softplus(-w_pre) - 0.5)
        a_sig = _sigmoid(a0_ref[...] + _mm3(_lhs3(xms), aup_r))
        gate = _mm3(_lhs3(_sigmoid(xms)), gup_r)
        kx = xm[:, GW:2 * GW]
        per_b.append(dict(rr=xm[:, 0:GW], vv=xm[:, 2 * GW:3 * GW], logw=logw, a_sig=a_sig, gate=gate,
                          kkp=kx * kk_ref[...], k2=kx * (1.0 + (a_sig - 1.0) * ka_ref[...])))

    w = _Wide()
    ones_bd = ones_ref[...]
    tril = _tril_ones(n, BF16)
    sl = lambda ci: slice(ci * n, (ci + 1) * n)
    get = lambda name: [per_b[b][name][sl(ci), :] for b, ci in items]
    rrs, vvs, logws, a_sigs, gates, kkps, k2s = (get(k) for k in ('rr', 'vv', 'logw', 'a_sig', 'gate', 'kkp', 'k2'))
    kks = [kkp * lax.rsqrt(_mm_x01(kkp * kkp, ones_bd) + 1e-6) for kkp in kkps]
    cums = [_tri_mm(tril, logw) for logw in logws]
    w_incls = [jnp.exp(cum) for cum in cums]
    w_lasts = [wi[n - 1:n, :] for wi in w_incls]
    w_invs = [jnp.exp(-cum) for cum in cums]
    ats = [-kk * jnp.exp(cum - logw) for kk, cum, logw in zip(kks, cums, logws)]
    bts = [kk * a_sig * wv for kk, a_sig, wv in zip(kks, a_sigs, w_invs)]
    kts = [k2 * wv for k2, wv in zip(k2s, w_invs)]
    rts = [rr * wi for rr, wi in zip(rrs, w_incls)]
    ar_ls = [_lhs3(jnp.concatenate([at, rt], axis=0)) for at, rt in zip(ats, rts)]
    gbs = [_mm3(ar_l, _rhs3(bt, w), NT) for ar_l, bt in zip(ar_ls, bts)]
    gks = [_mm3(ar_l, _rhs3(kt, w), NT) for ar_l, kt in zip(ar_ls, kts)]
    t_ls = [_lhs3(t) for t in _unit_lower_inverse_wide([jnp.where(w.strict, -gb[:n], 0.0) for gb in gbs], w)]
    arb_ls = [_lhs3(jnp.where(w.incl, gb[n:], 0.0)) for gb in gbs]
    ark_ls = [_lhs3(jnp.where(w.incl, gk[n:], 0.0)) for gk in gks]
    v_rs = [_rhs3(vv, w) for vv in vvs]
    aakvs = [_mm3(_lhs3(jnp.where(w.strict, gk[:n], 0.0)), v_r) for gk, v_r in zip(gks, v_rs)]
    arkvs = [_mm3(ark_l, v_r) for ark_l, v_r in zip(ark_ls, v_rs)]
    bks = [jnp.concatenate([bt * wl, kt * wl], axis=0) for bt, kt, wl in zip(bts, kts, w_lasts)]

    states = [s_scr[b] for b in range(nb)]
    ys = {}
    for ci in range(nc):
        for b in range(nb):
            it = b * nc + ci
            s0 = states[b]
            ars = _mm3(ar_ls[it], _rhs3(s0, w), NT)
            u = _mm3(t_ls[it], _rhs3(ars[:n] + aakvs[it], w))
            ys[it] = ars[n:] + _mm3(arb_ls[it], _rhs3(u, w)) + arkvs[it]
            upd = _mm3_tn(jnp.concatenate([u, vvs[it]], axis=0), bks[it])
            states[b] = s0 * w_lasts[it] + _collapse_heads(upd, w)
    for b in range(nb):
        s_scr[b] = states[b]

    inv_d = 1.0 / HEAD_DIM
    for it, (b, ci) in enumerate(items):
        y = ys[it]
        yc = y - _mm_x01(y, ones_bd) * inv_d
        var_y = _mm_x01(yc * yc, ones_bd) * inv_d
        yn = yc * lax.rsqrt(var_y + GN_EPS) * lng_ref[...] + lnb_ref[...]
        bonus = _mm_x01(rrs[it] * k2s[it] * rk_ref[...], ones_bd) * vvs[it]
        o_ref[b, sl(ci), :] = (yn + bonus) * gates[it]

    @pl.when(c == pl.num_programs(1) - 1)
    def _():
        sout_ref[...] = s_scr[...]


def _rwkv_mixer(h3, hs3, shift, s0, lp):
    bsz, l, _ = h3.shape
    nb, nc = _recurrent_tiling(bsz, l)
    rows = nc * CHUNK
    sh_main = shift[:, None, 0:3 * GW]
    sh_small = jnp.pad(shift[:, None, 3 * GW:], ((0, 0), (0, 0), (0, H_SMALL - LORA)))
    mu = lp['rwkv_mu']
    mu_main = mu[None, 0:3 * GW]
    mu_small = jnp.pad(mu[None, 3 * GW:], ((0, 0), (0, H_SMALL - LORA)))
    place = lambda w, off: jnp.zeros((H_SMALL, GW), F32).at[off:off + w.shape[0]].set(w)
    wup = place(lp['rwkv_w_up'], 0)
    aup = place(lp['rwkv_a_up'], RWKV_W_LORA)
    gup = place(lp['rwkv_g_up'], RWKV_W_LORA + RWKV_A_LORA)
    row = lambda v: v.reshape(1, GW)
    full = lambda shape: pl.BlockSpec(shape, lambda bi, i: (0,) * len(shape))
    o, s_new = pl.pallas_call(
        functools.partial(_rwkv_kernel, nb=nb, nc=nc),
        grid=(bsz // nb, l // rows),
        in_specs=[pl.BlockSpec((nb, rows, 3 * GW), lambda bi, i: (bi, i, H_CRKV // (3 * GW))),
                  pl.BlockSpec((nb, rows, H_SMALL), lambda bi, i: (bi, i, 0)),
                  pl.BlockSpec((nb, 1, 3 * GW), lambda bi, i: (bi, 0, 0)),
                  pl.BlockSpec((nb, 1, H_SMALL), lambda bi, i: (bi, 0, 0)),
                  pl.BlockSpec((nb, HEAD_DIM, GW), lambda bi, i: (bi, 0, 0)),
                  full((1, 3 * GW)), full((1, H_SMALL)), full((1, GW)), full((H_SMALL, GW)), full((1, GW)),
                  full((H_SMALL, GW)), full((H_SMALL, GW)), full((1, GW)), full((1, GW)), full((1, GW)),
                  full((1, GW)), full((1, GW)), full((GW, GW))],
        out_specs=[pl.BlockSpec((nb, rows, GW), lambda bi, i: (bi, i, 0)),
                   pl.BlockSpec((nb, HEAD_DIM, GW), lambda bi, i: (bi, 0, 0))],
        out_shape=[jax.ShapeDtypeStruct((bsz, l, GW), F32),
                   jax.ShapeDtypeStruct((bsz, HEAD_DIM, GW), F32)],
        scratch_shapes=[pltpu.VMEM((nb, rows + 8, 3 * GW), F32), pltpu.VMEM((nb, rows + 8, H_SMALL), F32),
                        pltpu.VMEM((nb, HEAD_DIM, GW), F32)],
        compiler_params=_params(2),
        name="rwkv_mixer",
    )(h3, hs3, sh_main, sh_small, _heads_to_wide(s0), mu_main, mu_small, row(lp['rwkv_w0']), wup,
      row(lp['rwkv_a0']), aup, gup, row(lp['rwkv_k_k']), row(lp['rwkv_k_a']), row(lp['rwkv_r_k']),
      row(lp['rwkv_lnx_g']), row(lp['rwkv_lnx_b']), _ones_block_diag())
    return o, _wide_to_heads(s_new)


_FB = 128


def _log_forget(hs, bf):
    return -_softplus(-(hs + bf))


def _attend_block(qh, kb, vb, bias, m, l, acc):
    s = _mm(qh, kb, NT) + bias
    m_new = jnp.maximum(m, jnp.max(s, -1, keepdims=True))
    alpha = jnp.exp(m - m_new)
    p = jnp.exp(s - m_new)
    l = alpha * l + jnp.sum(p, -1, keepdims=True)
    acc = alpha * acc + _mm(p, vb)
    return m_new, l, acc


def _fox_prompt_kernel(q_ref, k_ref, v_ref, hs_ref, bf_ref, o_ref, lf_ref, fcol_scr, frow_scr, *, l, bq):
    i = pl.program_id(1)

    @pl.when(i == 0)
    def _():
        ltri, utri = _tril_ones(_FB), _triu_ones(_FB)
        sel = _col_selector(8, H_SMALL, S_F)
        carry_c = jnp.zeros((1, H_SMALL), F32)
        carry_r = jnp.zeros((8, 1), F32)
        for jb in range(l // _FB):
            rows = slice(jb * _FB, (jb + 1) * _FB)
            lf = _log_forget(hs_ref[rows, :], bf_ref[...])
            fc = _mmh(ltri, lf) + carry_c
            fcol_scr[rows, :] = fc
            carry_c = fc[_FB - 1:_FB, :]
            fr = _mmh(_mmh(sel, lf, NT), utri) + carry_r
            frow_scr[jb] = fr
            carry_r = fr[:, _FB - 1:_FB]
            lf_ref[rows, :] = lf[:, S_F:S_F + GROUP_HEADS]

    q = q_ref[...]
    fq = fcol_scr[pl.ds(pl.multiple_of(i * bq, bq), bq), :]
    r_i, c_i = _iota2((bq, bq), 0), _iota2((bq, bq), 1)
    causal = c_i <= r_i
    for h in range(GROUP_HEADS):
        sl = slice(h * HEAD_DIM, (h + 1) * HEAD_DIM)
        qh = q[:, sl] * (HEAD_DIM ** -0.5)
        fqh = fq[:, S_F + h:S_F + h + 1]

        def body(j, carry, sl=sl, qh=qh, fqh=fqh, h=h):
            rows = pl.ds(pl.multiple_of(j * bq, bq), bq)
            bias = fqh - frow_scr[j][h:h + 1, :]
            return _attend_block(qh, k_ref[rows, sl], v_ref[rows, sl], bias, *carry)

        init = (jnp.full((bq, 1), NEG, F32), jnp.zeros((bq, 1), F32), jnp.zeros((bq, HEAD_DIM), F32))
        m, lsum, acc = lax.fori_loop(0, i, body, init)
        rows = pl.ds(pl.multiple_of(i * bq, bq), bq)
        bias = jnp.where(causal, fqh - frow_scr[i][h:h + 1, :], NEG)
        m, lsum, acc = _attend_block(qh, k_ref[rows, sl], v_ref[rows, sl], bias, m, lsum, acc)
        o_ref[:, sl] = acc / lsum


def _fox_prompt(h3, hs3, bf_pad):
    bsz, l, _ = h3.shape
    bq = _FB
    kern = functools.partial(_fox_prompt_kernel, l=l, bq=bq)
    return pl.pallas_call(
        kern,
        grid=(bsz, l // bq),
        in_specs=[pl.BlockSpec((None, bq, GW), lambda bi, i: (bi, i, H_DQKV // GW)),
                  pl.BlockSpec((None, l, GW), lambda bi, i: (bi, 0, H_DQKV // GW + 1)),
                  pl.BlockSpec((None, l, GW), lambda bi, i: (bi, 0, H_DQKV // GW + 2)),
                  pl.BlockSpec((None, l, H_SMALL), lambda bi, i: (bi, 0, 0)),
                  pl.BlockSpec((1, H_SMALL), lambda bi, i: (0, 0))],
        out_specs=[pl.BlockSpec((None, bq, GW), lambda bi, i: (bi, i, 0)),
                   pl.BlockSpec((None, l, GROUP_HEADS), lambda bi, i: (bi, 0, 0))],
        out_shape=[jax.ShapeDtypeStruct((bsz, l, GW), F32),
                   jax.ShapeDtypeStruct((bsz, l, GROUP_HEADS), F32)],
        scratch_shapes=[pltpu.VMEM((l, H_SMALL), F32), pltpu.VMEM((l // _FB, 8, _FB), F32)],
        compiler_params=_params(2),
        name="fox_prompt",
    )(h3, h3, h3, hs3, bf_pad)


def _fox_sample_kernel(q_ref, k_ref, v_ref, hs_ref, bf_ref, ck_ref, cv_ref, clf_ref, o_ref, lf_ref, cum_scr,
                       *, l, p):
    nb = p // _FB
    utri = _triu_ones(_FB)

    def cum_body(jb, carry):
        fr = _mmh(clf_ref[jb], utri) + carry
        cum_scr[jb] = fr
        return fr[:, _FB - 1:_FB]

    total = lax.fori_loop(0, nb, cum_body, jnp.zeros((8, 1), F32))

    lf = _log_forget(hs_ref[...], bf_ref[...])
    lf_ref[...] = lf[:, S_F:S_F + GROUP_HEADS]
    cum_c = _mmh(_tril_ones(l), lf)
    cum_r = _mmh(_mmh(_col_selector(8, H_SMALL, S_F), lf, NT), _triu_ones(l))
    q = q_ref[...]
    r_i, c_i = _iota2((l, l), 0), _iota2((l, l), 1)
    causal = c_i <= r_i
    for h in range(GROUP_HEADS):
        sl = slice(h * HEAD_DIM, (h + 1) * HEAD_DIM)
        qh = q[:, sl] * (HEAD_DIM ** -0.5)
        fqh = cum_c[:, S_F + h:S_F + h + 1]
        tot_h = total[h:h + 1, :]

        def body(j, carry, sl=sl, qh=qh, fqh=fqh, h=h, tot_h=tot_h):
            rows = pl.ds(pl.multiple_of(j * _FB, _FB), _FB)
            bias = fqh + (tot_h - cum_scr[j][h:h + 1, :])
            return _attend_block(qh, ck_ref[rows, sl], cv_ref[rows, sl], bias, *carry)

        init = (jnp.full((l, 1), NEG, F32), jnp.zeros((l, 1), F32), jnp.zeros((l, HEAD_DIM), F32))
        m, lsum, acc = lax.fori_loop(0, nb, body, init)
        bias = jnp.where(causal, fqh - cum_r[h:h + 1, :], NEG)
        m, lsum, acc = _attend_block(qh, k_ref[:, sl], v_ref[:, sl], bias, m, lsum, acc)
        o_ref[:, sl] = acc / lsum


def _fox_sample(h3, hs3, bf_pad, ck, cv, clogf):
    bsz, l, _ = h3.shape
    p = ck.shape[1]
    ck2 = ck.reshape(bsz, p, GW)
    cv2 = cv.reshape(bsz, p, GW)
    clf = clogf.reshape(bsz, p // _FB, _FB, GROUP_HEADS).transpose(0, 1, 3, 2)
    clf = jnp.pad(clf, ((0, 0), (0, 0), (0, 8 - GROUP_HEADS), (0, 0)))
    kern = functools.partial(_fox_sample_kernel, l=l, p=p)
    return pl.pallas_call(
        kern,
        grid=(bsz,),
        in_specs=[pl.BlockSpec((None, l, GW), lambda bi: (bi, 0, H_DQKV // GW)),
                  pl.BlockSpec((None, l, GW), lambda bi: (bi, 0, H_DQKV // GW + 1)),
                  pl.BlockSpec((None, l, GW), lambda bi: (bi, 0, H_DQKV // GW + 2)),
                  pl.BlockSpec((None, l, H_SMALL), lambda bi: (bi, 0, 0)),
                  pl.BlockSpec((1, H_SMALL), lambda bi: (0, 0)),
                  pl.BlockSpec((None, p, GW), lambda bi: (bi, 0, 0)),
                  pl.BlockSpec((None, p, GW), lambda bi: (bi, 0, 0)),
                  pl.BlockSpec((None, p // _FB, 8, _FB), lambda bi: (bi, 0, 0, 0))],
        out_specs=[pl.BlockSpec((None, l, GW), lambda bi: (bi, 0, 0)),
                   pl.BlockSpec((None, l, GROUP_HEADS), lambda bi: (bi, 0, 0))],
        out_shape=[jax.ShapeDtypeStruct((bsz, l, GW), F32),
                   jax.ShapeDtypeStruct((bsz, l, GROUP_HEADS), F32)],
        scratch_shapes=[pltpu.VMEM((p // _FB, 8, _FB), F32)],
        compiler_params=_params(1),
        name="fox_sample",
    )(h3, h3, h3, hs3, bf_pad, ck2, cv2, clf)


def _outproj_kernel(oa_ref, ob_ref, oc_ref, od_ref, x_ref, w_ref, g_ref, b_ref, y_ref):
    acc = DN_ALPHA * x_ref[...]
    for gi, ref in enumerate((oa_ref, ob_ref, oc_ref, od_ref)):
        acc = acc + _mm(ref[...], w_ref[gi * GW:(gi + 1) * GW, :])
    y_ref[...] = _layer_norm(acc, g_ref[...], b_ref[...])


def _outproj(oa, ob, oc, od, x2d, w_out, g, b):
    t = x2d.shape[0]
    tm = min(t, 512)
    mix = pl.BlockSpec((tm, GW), lambda i: (i, 0))
    return pl.pallas_call(
        _outproj_kernel,
        grid=(t // tm,),
        in_specs=[mix, mix, mix, mix,
                  pl.BlockSpec((tm, D_MODEL), lambda i: (i, 0)),
                  pl.BlockSpec((D_MODEL, D_MODEL), lambda i: (0, 0)),
                  pl.BlockSpec((1, D_MODEL), lambda i: (0, 0)),
                  pl.BlockSpec((1, D_MODEL), lambda i: (0, 0))],
        out_specs=pl.BlockSpec((tm, D_MODEL), lambda i: (i, 0)),
        out_shape=jax.ShapeDtypeStruct((t, D_MODEL), F32),
        compiler_params=_params(1),
        name="outproj_ln",
    )(oa, ob, oc, od, x2d, w_out, g, b)


_R_G, _R_E = 0, N_EXPERT_GROUPS


def _moe_kernel(x_ref, wr_ref, br_ref, wg_ref, wu_ref, wd_ref, g_ref, b_ref, y_ref, comb_scr, xb_scr, acc_scr):
    e = pl.program_id(1)
    tm = x_ref.shape[0]
    lane = _iota2((tm, LANE), 1)

    @pl.when(e == 0)
    def _():
        x = x_ref[...]
        xb_scr[...] = x.astype(BF16)
        acc_scr[...] = jnp.zeros_like(acc_scr)
        logits = _mmh(x, wr_ref[...]) + br_ref[...]
        gmask = lane < N_EXPERT_GROUPS
        gl = jnp.where(gmask, logits, NEG)
        ge = jnp.exp(gl - jnp.max(gl, -1, keepdims=True))
        pg = ge / jnp.sum(ge, -1, keepdims=True)
        gp = jnp.max(pg, -1, keepdims=True)
        gi = jnp.min(jnp.where(jnp.logical_and(gmask, pg == gp), lane, LANE), -1, keepdims=True)
        emask = jnp.logical_and(jnp.logical_and(lane >= _R_E, lane < _R_E + N_EXPERTS),
                                ((lane - _R_E) >> 2) == gi)
        el = jnp.where(emask, logits, NEG)
        ee = jnp.exp(el - jnp.max(el, -1, keepdims=True))
        ep = ee / jnp.sum(ee, -1, keepdims=True)
        m1 = jnp.max(jnp.where(emask, ep, -1.0), -1, keepdims=True)
        i1 = jnp.min(jnp.where(jnp.logical_and(emask, ep == m1), lane, LANE), -1, keepdims=True)
        rest = jnp.logical_and(emask, lane != i1)
        m2 = jnp.max(jnp.where(rest, ep, -1.0), -1, keepdims=True)
        i2 = jnp.min(jnp.where(jnp.logical_and(rest, ep == m2), lane, LANE), -1, keepdims=True)
        den = m1 + m2
        comb_scr[...] = jnp.where(lane == i1, gp * m1 / den, jnp.where(lane == i2, gp * m2 / den, 0.0))

    ce = jnp.sum(jnp.where(lane == e + _R_E, comb_scr[...], 0.0), -1, keepdims=True)
    xb = xb_scr[...]
    hh = _silu(_mm(xb, wg_ref[...])) * _mm(xb, wu_ref[...])
    acc_scr[...] += ce * _mm(hh, wd_ref[...])

    @pl.when(e == pl.num_programs(1) - 1)
    def _():
        y_ref[...] = _layer_norm(DN_ALPHA * x_ref[...] + acc_scr[...], g_ref[...], b_ref[...])


def _moe(x2d, w_router, b_router, wg, wu, wd, g, b):
    t = x2d.shape[0]
    tm = min(t, 1024)
    return pl.pallas_call(
        _moe_kernel,
        grid=(t // tm, N_EXPERTS),
        in_specs=[pl.BlockSpec((tm, D_MODEL), lambda i, e: (i, 0)),
                  pl.BlockSpec((D_MODEL, LANE), lambda i, e: (0, 0)),
                  pl.BlockSpec((1, LANE), lambda i, e: (0, 0)),
                  pl.BlockSpec((None, D_MODEL, D_EXPERT), lambda i, e: (e, 0, 0)),
                  pl.BlockSpec((None, D_MODEL, D_EXPERT), lambda i, e: (e, 0, 0)),
                  pl.BlockSpec((None, D_EXPERT, D_MODEL), lambda i, e: (e, 0, 0)),
                  pl.BlockSpec((1, D_MODEL), lambda i, e: (0, 0)),
                  pl.BlockSpec((1, D_MODEL), lambda i, e: (0, 0))],
        out_specs=pl.BlockSpec((tm, D_MODEL), lambda i, e: (i, 0)),
        out_shape=jax.ShapeDtypeStruct((t, D_MODEL), F32),
        scratch_shapes=[pltpu.VMEM((tm, LANE), F32), pltpu.VMEM((tm, D_MODEL), BF16),
                        pltpu.VMEM((tm, D_MODEL), F32)],
        compiler_params=_params(2),
        name="moe_ln",
    )(x2d, w_router, b_router, wg, wu, wd, g, b)


def _prep_layer(lp):
    w = lp['w_in']
    c0 = CONV_COLS
    c1 = c0 + GDN_COLS
    c2 = c1 + RWKV_COLS
    main = jnp.concatenate([w[:, c0:c0 + 3 * GW], w[:, c1:c1 + 3 * GW], w[:, c2:c2 + 3 * GW],
                            w[:, 0:GW], w[:, GW:2 * GW], w[:, c0 + 3 * GW:c0 + 4 * GW]], axis=1).astype(BF16)
    small = jnp.concatenate([w[:, c1 + 3 * GW:c2], w[:, c0 + 4 * GW:c1], w[:, c2 + 3 * GW:],
                             jnp.zeros((D_MODEL, H_SMALL - LORA - 3 * GROUP_HEADS), F32)], axis=1)
    wr = jnp.concatenate([lp['router_g_w'], lp['router_e_w'],
                          jnp.zeros((D_MODEL, LANE - N_EXPERT_GROUPS - N_EXPERTS), F32)], axis=1)
    br = jnp.concatenate([lp['router_g_b'], lp['router_e_b'],
                          jnp.zeros((LANE - N_EXPERT_GROUPS - N_EXPERTS,), F32)])[None, :]
    bf_pad = jnp.zeros((1, H_SMALL), F32).at[0, S_F:S_F + GROUP_HEADS].set(lp['fox_b_f'])
    return dict(w_main=main, w_small=small, w_out=lp['w_out'].astype(BF16), wr=wr, br=br, bf_pad=bf_pad)


def _trunk_layer(x, lp, pp, conv_buf, gdn_buf, gdn_s, rw_shift, rw_s, fox_cache):
    bsz, l, d = x.shape
    x2d = x.reshape(bsz * l, d)
    h2, hs2 = _inproj(x2d, pp['w_main'], pp['w_small'])
    h3 = h2.reshape(bsz, l, H_MAIN)
    hs3 = hs2.reshape(bsz, l, H_SMALL)

    o_a, new_conv = _conv_mixer(h3, conv_buf, lp['conv_w'], lp['conv_b'][None], lp['conv_ln_g'][None],
                                lp['conv_ln_b'][None])
    o_b, new_gdn_s = _gdn_mixer(h3, hs3, gdn_buf, gdn_s, lp['gdn_conv_w'], lp['gdn_a_log'], lp['gdn_dt_bias'],
                                lp['gdn_norm_w'])
    o_c, new_rw_s = _rwkv_mixer(h3, hs3, rw_shift, rw_s, lp)
    if fox_cache is None:
        o_d, logf = _fox_prompt(h3, hs3, pp['bf_pad'])
    else:
        o_d, logf = _fox_sample(h3, hs3, pp['bf_pad'], *fox_cache)

    flat = lambda o: o.reshape(bsz * l, GW)
    x1 = _outproj(flat(o_a), flat(o_b), flat(o_c), flat(o_d), x2d, pp['w_out'], lp['ln1_g'][None],
                  lp['ln1_b'][None])
    x2 = _moe(x1, pp['wr'], pp['br'], lp['exp_w_gate'], lp['exp_w_up'], lp['exp_w_down'], lp['ln2_g'][None],
              lp['ln2_b'][None])

    kw = GDN_CONV_WIDTH - 1
    new_gdn_buf = h3[:, l - kw:, H_BQKV:H_BQKV + 3 * GW]
    new_shift = jnp.concatenate([h3[:, l - 1, H_CRKV:H_CRKV + 3 * GW], hs3[:, l - 1, S_LORA:S_LORA + LORA]], axis=-1)
    k_d = h3[:, :, H_DQKV + GW:H_DQKV + 2 * GW].reshape(bsz, l, GROUP_HEADS, HEAD_DIM)
    v_d = h3[:, :, H_DQKV + 2 * GW:H_DQKV + 3 * GW].reshape(bsz, l, GROUP_HEADS, HEAD_DIM)
    return x2.reshape(bsz, l, d), (new_conv, new_gdn_buf, new_gdn_s, new_shift, new_rw_s, k_d, v_d, logf)


_LAYER_KEYS = ('w_in', 'conv_w', 'conv_b', 'conv_ln_g', 'conv_ln_b', 'gdn_conv_w', 'gdn_a_log', 'gdn_dt_bias',
               'gdn_norm_w', 'rwkv_mu', 'rwkv_w0', 'rwkv_w_up', 'rwkv_a0', 'rwkv_a_up', 'rwkv_g_up', 'rwkv_k_k',
               'rwkv_k_a', 'rwkv_r_k', 'rwkv_lnx_g', 'rwkv_lnx_b', 'fox_b_f', 'w_out', 'ln1_g', 'ln1_b',
               'router_g_w', 'router_g_b', 'router_e_w', 'router_e_b', 'exp_w_gate', 'exp_w_up', 'exp_w_down',
               'ln2_g', 'ln2_b')


def kernel(x_prompt, x_sample, cache_fox_k, cache_fox_v, cache_fox_logf, state_conv, state_gdn_conv, state_gdn, state_rwkv_shift, state_rwkv, w_in, conv_w, conv_b, conv_ln_g, conv_ln_b, gdn_conv_w, gdn_a_log, gdn_dt_bias, gdn_norm_w, rwkv_mu, rwkv_w0, rwkv_w_up, rwkv_a0, rwkv_a_up, rwkv_g_up, rwkv_k_k, rwkv_k_a, rwkv_r_k, rwkv_lnx_g, rwkv_lnx_b, fox_b_f, w_out, ln1_g, ln1_b, router_g_w, router_g_b, router_e_w, router_e_b, exp_w_gate, exp_w_up, exp_w_down, ln2_g, ln2_b):
    weights = dict(zip(_LAYER_KEYS, (w_in, conv_w, conv_b, conv_ln_g, conv_ln_b, gdn_conv_w, gdn_a_log,
                                     gdn_dt_bias, gdn_norm_w, rwkv_mu, rwkv_w0, rwkv_w_up, rwkv_a0, rwkv_a_up,
                                     rwkv_g_up, rwkv_k_k, rwkv_k_a, rwkv_r_k, rwkv_lnx_g, rwkv_lnx_b, fox_b_f,
                                     w_out, ln1_g, ln1_b, router_g_w, router_g_b, router_e_w, router_e_b,
                                     exp_w_gate, exp_w_up, exp_w_down, ln2_g, ln2_b)))
    xp, xs = x_prompt, x_sample
    bp = x_prompt.shape[0]
    outs_p = [[] for _ in range(8)]
    outs_s = [[] for _ in range(8)]
    for l in range(DEPTH):
        lp = {k: v[l] for k, v in weights.items()}
        pp = _prep_layer(lp)
        xp, st_p = _trunk_layer(xp, lp, pp,
                                jnp.zeros((bp, CONV_WIDTH - 1, GW), F32),
                                jnp.zeros((bp, GDN_CONV_WIDTH - 1, 3 * GW), F32),
                                jnp.zeros((bp, GROUP_HEADS, HEAD_DIM, HEAD_DIM), F32),
                                jnp.zeros((bp, RWKV_COLS), F32),
                                jnp.zeros((bp, GROUP_HEADS, HEAD_DIM, HEAD_DIM), F32),
                                None)
        xs, st_s = _trunk_layer(xs, lp, pp, state_conv[l], state_gdn_conv[l], state_gdn[l], state_rwkv_shift[l],
                                state_rwkv[l], (cache_fox_k[l], cache_fox_v[l], cache_fox_logf[l]))
        for i in range(8):
            outs_p[i].append(st_p[i])
            outs_s[i].append(st_s[i])
    stacked_p = [jnp.stack(o) for o in outs_p]
    stacked_s = [jnp.stack(o) for o in outs_s]
    return (xp, xs, *stacked_p, *stacked_s)
```

```python
import functools

import jax
import jax.numpy as jnp
from jax import lax
from jax.experimental import pallas as pl
from jax.experimental.pallas import tpu as pltpu

F32 = jnp.float32
BF16 = jnp.bfloat16

D_MODEL = 1024
DEPTH = 2
CHUNK = 64
HEAD_DIM = 64
GW = D_MODEL // 4
GROUP_HEADS = GW // HEAD_DIM
CONV_WIDTH = 31
GDN_CONV_WIDTH = 4
RWKV_W_LORA = 16
RWKV_A_LORA = 16
RWKV_G_LORA = 32
N_EXPERT_GROUPS = 4
EXPERTS_PER_GROUP = 4
N_EXPERTS = N_EXPERT_GROUPS * EXPERTS_PER_GROUP
D_EXPERT = 256
DN_ALPHA = (2 * DEPTH) ** 0.25
LN_EPS = 1e-5
GN_EPS = 64e-5

CONV_COLS = 2 * GW
GDN_COLS = 4 * GW + 2 * GROUP_HEADS
RWKV_COLS = 3 * GW + RWKV_W_LORA + RWKV_A_LORA + RWKV_G_LORA
FOX_COLS = 3 * GW + GROUP_HEADS
LORA = RWKV_W_LORA + RWKV_A_LORA + RWKV_G_LORA

H_BQKV, H_CRKV, H_DQKV, H_AVAL, H_AGATE, H_BZ = 0, 768, 1536, 2304, 2560, 2816
H_MAIN = 3072
S_LORA, S_A, S_B, S_F = 0, 64, 68, 72
H_SMALL = 128
LANE = 128

VMEM_LIMIT = 56 * 1024 * 1024

NN = (((1,), (0,)), ((), ()))
NT = (((1,), (1,)), ((), ()))
TN = (((0,), (0,)), ((), ()))
NEG = -1e30


def _dot(a, b, dims=NN):
    return lax.dot_general(a, b, dims, preferred_element_type=F32)


def _mm(a, b, dims=NN):
    return _dot(a.astype(BF16), b.astype(BF16), dims)


def _mmh(a, b, dims=NN):
    return lax.dot_general(a, b, dims, preferred_element_type=F32, precision=lax.Precision.HIGHEST)


def _sigmoid(x):
    return 1.0 / (1.0 + jnp.exp(-x))


def _silu(x):
    return x * _sigmoid(x)


def _softplus(x):
    return jnp.maximum(x, 0.0) + jnp.log(1.0 + jnp.exp(-jnp.abs(x)))


def _iota2(shape, dim):
    return lax.broadcasted_iota(jnp.int32, shape, dim)


def _tril_ones(n, dtype=F32):
    r, c = _iota2((n, n), 0), _iota2((n, n), 1)
    return (r >= c).astype(dtype)


def _triu_ones(n, dtype=F32):
    r, c = _iota2((n, n), 0), _iota2((n, n), 1)
    return (r <= c).astype(dtype)


def _col_selector(rows, width, col0):
    r, c = _iota2((rows, width), 0), _iota2((rows, width), 1)
    return (c == r + col0).astype(F32)


def _layer_norm(x, g, b):
    mu = jnp.mean(x, axis=-1, keepdims=True)
    xc = x - mu
    var = jnp.mean(xc * xc, axis=-1, keepdims=True)
    return xc * lax.rsqrt(var + LN_EPS) * g + b


def _split2(x):
    hi = x.astype(BF16)
    lo = (x - hi.astype(F32)).astype(BF16)
    return hi, lo


def _split3(x):
    hi = x.astype(BF16)
    r = x - hi.astype(F32)
    mid = r.astype(BF16)
    lo = (r - mid.astype(F32)).astype(BF16)
    return hi, mid, lo


class _Wide:
    def __init__(self):
        self.i = _iota2((CHUNK, GW), 0)
        lane = _iota2((CHUNK, GW), 1)
        self.j = lane & (HEAD_DIM - 1)
        self.head = lane >> 6
        self.incl = self.i >= self.j
        self.strict = self.i > self.j
        self.eye = self.i == self.j
        r, c = _iota2((GW, GW), 0), _iota2((GW, GW), 1)
        self.block_diag = (r >> 6) == (c >> 6)


def _head_block_diag(x, w):
    t = jnp.concatenate([x] * GROUP_HEADS, axis=0)
    return jnp.where(w.block_diag, t, jnp.zeros_like(t))


def _collapse_heads(full, w):
    out = jnp.zeros((HEAD_DIM, GW), F32)
    for h in range(GROUP_HEADS):
        out = out + jnp.where(w.head == h, full[h * HEAD_DIM:(h + 1) * HEAD_DIM, :], 0.0)
    return out


def _lhs3(a):
    hi, lo = _split2(a)
    return jnp.concatenate([hi, lo], axis=0), hi


def _rhs3(b, w=None):
    hi, lo = _split2(b)
    if w is None:
        return hi, lo
    return _head_block_diag(hi, w), _head_block_diag(lo, w)


def _mm3(lhs, rhs, dims=NN):
    cat, hi = lhs
    bh, bl = rhs
    m = hi.shape[0]
    r = _dot(cat, bh, dims)
    return r[:m] + r[m:] + _dot(hi, bl, dims)


def _mm3_tn(a, b):
    ah, al = _split2(a)
    bh, bl = _split2(b)
    m = a.shape[1]
    r = _dot(jnp.concatenate([ah, al], axis=1), bh, TN)
    return r[:m] + r[m:] + _dot(ah, bl, TN)


def _mm_x01(a, m01):
    m = a.shape[0]
    r = _dot(jnp.concatenate(_split3(a), axis=0), m01)
    return r[:m] + r[m:2 * m] + r[2 * m:]


def _tri_mm(tri01, x):
    n = x.shape[1]
    r = _dot(tri01, jnp.concatenate(_split3(x), axis=1))
    return r[:, :n] + r[:, n:2 * n] + r[:, 2 * n:]


def _unit_lower_inverse_wide(mats, w):
    eye = w.eye.astype(F32)
    same8 = (w.i >> 3) == (w.j >> 3)
    ds = [jnp.where(same8, a, 0.0) for a in mats]
    d2 = [_mm3(_lhs3(d), _rhs3(d, w)) for d in ds]
    d2_r = [_rhs3(x, w) for x in d2]
    d4 = [_mm3(_lhs3(x), r) for x, r in zip(d2, d2_r)]
    imd = [eye - d for d in ds]
    p1 = [i + _mm3(_lhs3(i), r) for i, r in zip(imd, d2_r)]
    ts = [p + _mm3(_lhs3(p), _rhs3(x, w)) for p, x in zip(p1, d4)]
    for s in range(3, 6):
        big = (w.i >> (s + 1)) == (w.j >> (s + 1))
        small = (w.i >> s) == (w.j >> s)
        sel = jnp.logical_and(big, jnp.logical_not(small))
        inner = [_mm3(_lhs3(jnp.where(sel, a, 0.0)), _rhs3(t, w)) for a, t in zip(mats, ts)]
        ts = [t - _mm3(_lhs3(t), _rhs3(x, w)) for t, x in zip(ts, inner)]
    return ts


def _params(n_axes):
    return pltpu.CompilerParams(dimension_semantics=("arbitrary",) * n_axes,
                                vmem_limit_bytes=VMEM_LIMIT)


def _heads_to_wide(s):
    b = s.shape[0]
    return s.transpose(0, 2, 1, 3).reshape(b, HEAD_DIM, GW)


def _wide_to_heads(s):
    b = s.shape[0]
    return s.reshape(b, HEAD_DIM, GROUP_HEADS, HEAD_DIM).transpose(0, 2, 1, 3)


def _ones_block_diag():
    r, c = jnp.arange(GW)[:, None], jnp.arange(GW)[None, :]
    return ((r // HEAD_DIM) == (c // HEAD_DIM)).astype(BF16)


def _lane_spread(col0):
    r, c = jnp.arange(H_SMALL)[:, None], jnp.arange(GW)[None, :]
    return ((r - col0) == (c // HEAD_DIM)).astype(BF16)


def _inproj_kernel(x_ref, wm_ref, ws_ref, h_ref, hs_ref):
    x = x_ref[...]
    h_ref[...] = _mm(x, wm_ref[...])
    hs_ref[...] = _mmh(x, ws_ref[...])


def _inproj(x2d, w_main, w_small):
    t = x2d.shape[0]
    tm = min(t, 512)
    return pl.pallas_call(
        _inproj_kernel,
        grid=(t // tm,),
        in_specs=[pl.BlockSpec((tm, D_MODEL), lambda i: (i, 0)),
                  pl.BlockSpec((D_MODEL, H_MAIN), lambda i: (0, 0)),
                  pl.BlockSpec((D_MODEL, H_SMALL), lambda i: (0, 0))],
        out_specs=[pl.BlockSpec((tm, H_MAIN), lambda i: (i, 0)),
                   pl.BlockSpec((tm, H_SMALL), lambda i: (i, 0))],
        out_shape=[jax.ShapeDtypeStruct((t, H_MAIN), F32), jax.ShapeDtypeStruct((t, H_SMALL), F32)],
        compiler_params=_params(1),
        name="inproj",
    )(x2d, w_main, w_small)


_HALO = 32


def _conv_kernel(val_ref, gate_ref, buf_ref, cw_ref, cb_ref, g_ref, b_ref, o_ref, nbuf_ref, win_ref, *, tl):
    i = pl.program_id(1)
    pad = _HALO - (CONV_WIDTH - 1)

    @pl.when(i == 0)
    def _():
        win_ref[0:8, :] = jnp.zeros((8, GW), F32)
        win_ref[pad:_HALO, :] = buf_ref[...]

    u = val_ref[...] * _sigmoid(gate_ref[...])
    win_ref[_HALO:_HALO + tl, :] = u
    sub = min(tl, 64)
    for r0 in range(0, tl, sub):
        acc = jnp.zeros((sub, GW), F32)
        for j in range(CONV_WIDTH):
            acc = acc + cw_ref[j:j + 1, :] * win_ref[pad + r0 + j:pad + r0 + j + sub, :]
        y = _layer_norm(acc + cb_ref[...], g_ref[...], b_ref[...])
        o_ref[r0:r0 + sub, :] = _silu(y)
    tail = win_ref[tl:tl + _HALO, :]
    win_ref[0:_HALO, :] = tail

    @pl.when(i == pl.num_programs(1) - 1)
    def _():
        nbuf_ref[...] = win_ref[pad:_HALO, :]


def _conv_mixer(h3, buf, cw, cb, g, b):
    bsz, l, _ = h3.shape
    tl = min(l, 256)
    kern = functools.partial(_conv_kernel, tl=tl)
    full = lambda shape: pl.BlockSpec(shape, lambda bi, i: (0,) * len(shape))
    return pl.pallas_call(
        kern,
        grid=(bsz, l // tl),
        in_specs=[pl.BlockSpec((None, tl, GW), lambda bi, i: (bi, i, H_AVAL // GW)),
                  pl.BlockSpec((None, tl, GW), lambda bi, i: (bi, i, H_AGATE // GW)),
                  pl.BlockSpec((None, CONV_WIDTH - 1, GW), lambda bi, i: (bi, 0, 0)),
                  full((CONV_WIDTH, GW)), full((1, GW)), full((1, GW)), full((1, GW))],
        out_specs=[pl.BlockSpec((None, tl, GW), lambda bi, i: (bi, i, 0)),
                   pl.BlockSpec((None, CONV_WIDTH - 1, GW), lambda bi, i: (bi, 0, 0))],
        out_shape=[jax.ShapeDtypeStruct((bsz, l, GW), F32),
                   jax.ShapeDtypeStruct((bsz, CONV_WIDTH - 1, GW), F32)],
        scratch_shapes=[pltpu.VMEM((tl + _HALO, GW), F32)],
        compiler_params=_params(2),
        name="conv_mixer",
    )(h3, h3, buf, cw, cb, g, b)


def _gdn_kernel(qkv_ref, z_ref, hs_ref, buf_ref, s0_ref, cw_ref, alog_ref, dtb_ref, nw_ref, ones_ref, sela_ref,
                selb_ref, o_ref, sout_ref, win_ref, s_scr, *, nb, nc):
    c = pl.program_id(1)
    n = CHUNK
    rows = nc * n
    kw = GDN_CONV_WIDTH - 1
    items = [(b, ci) for b in range(nb) for ci in range(nc)]

    @pl.when(c == 0)
    def _():
        for b in range(nb):
            win_ref[b, 0:8, :] = jnp.zeros((8, 3 * GW), F32)
            win_ref[b, 8 - kw:8, :] = buf_ref[b]
        s_scr[...] = s0_ref[...]

    qkvs = []
    for b in range(nb):
        win_ref[b, 8:8 + rows, :] = qkv_ref[b]
        conv = jnp.zeros((rows, 3 * GW), F32)
        for j in range(GDN_CONV_WIDTH):
            conv = conv + cw_ref[j:j + 1, :] * win_ref[b, 8 - kw + j:8 - kw + j + rows, :]
        tail = win_ref[b, rows:rows + 8, :]
        win_ref[b, 0:8, :] = tail
        qkvs.append(_silu(conv))

    w = _Wide()
    ones_bd = ones_ref[...]
    tril = _tril_ones(n, BF16)
    sl = lambda ci: slice(ci * n, (ci + 1) * n)
    qs = [qkvs[b][sl(ci), 0:GW] for b, ci in items]
    ks = [qkvs[b][sl(ci), GW:2 * GW] for b, ci in items]
    vs = [qkvs[b][sl(ci), 2 * GW:3 * GW] for b, ci in items]
    sss = [_mm_x01(jnp.concatenate([q * q, k * k], axis=0), ones_bd) for q, k in zip(qs, ks)]
    qs = [q * lax.rsqrt(ss[:n] + 1e-6) * (HEAD_DIM ** -0.5) for q, ss in zip(qs, sss)]
    ks = [k * lax.rsqrt(ss[n:] + 1e-6) for k, ss in zip(ks, sss)]

    hss = [hs_ref[b, sl(ci), :] for b, ci in items]
    gs = [_mm_x01(-jnp.exp(alog_ref[...]) * _softplus(hs + dtb_ref[...]), sela_ref[...]) for hs in hss]
    betas = [_mm_x01(_sigmoid(hs), selb_ref[...]) for hs in hss]
    gcs = [_tri_mm(tril, g) for g in gs]
    grs = [jnp.sum(jnp.where(w.eye, gc, 0.0), axis=0, keepdims=True) for gc in gcs]
    decays = [jnp.exp(jnp.where(w.incl, gc - gr, NEG)) for gc, gr in zip(gcs, grs)]

    kbs = [k * beta for k, beta in zip(ks, betas)]
    grams = [_mm3(_lhs3(jnp.concatenate([kb, q], axis=0)), _rhs3(k, w), NT)
             for kb, q, k in zip(kbs, qs, ks)]
    amats = [jnp.where(w.strict, gram[:n] * decay, 0.0) for gram, decay in zip(grams, decays)]
    qks = [jnp.where(w.incl, gram[n:] * decay, 0.0) for gram, decay in zip(grams, decays)]
    t_ls = [_lhs3(t) for t in _unit_lower_inverse_wide(amats, w)]
    egs = [jnp.exp(gc) for gc in gcs]
    xvs = [_mm3(t_l, _rhs3(v * beta, w)) for t_l, v, beta in zip(t_ls, vs, betas)]
    xks = [_mm3(t_l, _rhs3(kb * eg, w)) for t_l, kb, eg in zip(t_ls, kbs, egs)]
    xq_ls = [_lhs3(jnp.concatenate([xk, q * eg], axis=0)) for xk, q, eg in zip(xks, qs, egs)]
    qk_ls = [_lhs3(qk) for qk in qks]
    gls = [gc[n - 1:n, :] for gc in gcs]
    kds = [k * jnp.exp(gl - gc) for k, gl, gc in zip(ks, gls, gcs)]

    states = [s_scr[b] for b in range(nb)]
    outs = {}
    for ci in range(nc):
        for b in range(nb):
            it = b * nc + ci
            s = states[b]
            rs = _mm3(xq_ls[it], _rhs3(s, w))
            u = xvs[it] - rs[:n]
            outs[it] = rs[n:] + _mm3(qk_ls[it], _rhs3(u, w))
            states[b] = s * jnp.exp(gls[it]) + _collapse_heads(_mm3_tn(kds[it], u), w)
    for b in range(nb):
        s_scr[b] = states[b]

    for it, (b, ci) in enumerate(items):
        o = outs[it]
        ms = _mm_x01(o * o, ones_bd) * (1.0 / HEAD_DIM)
        o_ref[b, sl(ci), :] = o * lax.rsqrt(ms + 1e-6) * nw_ref[...] * _silu(z_ref[b, sl(ci), :])

    @pl.when(c == pl.num_programs(1) - 1)
    def _():
        sout_ref[...] = s_scr[...]


def _recurrent_tiling(bsz, l):
    nc = min(l // CHUNK, 4)
    nb = 1 if nc > 1 else min(bsz, 4)
    return nb, nc


def _gdn_mixer(h3, hs3, buf, s0, cw, a_log, dt_bias, norm_w):
    bsz, l, _ = h3.shape
    nb, nc = _recurrent_tiling(bsz, l)
    rows = nc * CHUNK
    pad_r = lambda v, off: jnp.zeros((1, H_SMALL), F32).at[0, off:off + GROUP_HEADS].set(v)
    full = lambda shape: pl.BlockSpec(shape, lambda bi, i: (0,) * len(shape))
    o, s_new = pl.pallas_call(
        functools.partial(_gdn_kernel, nb=nb, nc=nc),
        grid=(bsz // nb, l // rows),
        in_specs=[pl.BlockSpec((nb, rows, 3 * GW), lambda bi, i: (bi, i, H_BQKV // (3 * GW))),
                  pl.BlockSpec((nb, rows, GW), lambda bi, i: (bi, i, H_BZ // GW)),
                  pl.BlockSpec((nb, rows, H_SMALL), lambda bi, i: (bi, i, 0)),
                  pl.BlockSpec((nb, GDN_CONV_WIDTH - 1, 3 * GW), lambda bi, i: (bi, 0, 0)),
                  pl.BlockSpec((nb, HEAD_DIM, GW), lambda bi, i: (bi, 0, 0)),
                  full((GDN_CONV_WIDTH, 3 * GW)), full((1, H_SMALL)), full((1, H_SMALL)), full((1, GW)),
                  full((GW, GW)), full((H_SMALL, GW)), full((H_SMALL, GW))],
        out_specs=[pl.BlockSpec((nb, rows, GW), lambda bi, i: (bi, i, 0)),
                   pl.BlockSpec((nb, HEAD_DIM, GW), lambda bi, i: (bi, 0, 0))],
        out_shape=[jax.ShapeDtypeStruct((bsz, l, GW), F32),
                   jax.ShapeDtypeStruct((bsz, HEAD_DIM, GW), F32)],
        scratch_shapes=[pltpu.VMEM((nb, rows + 8, 3 * GW), F32), pltpu.VMEM((nb, HEAD_DIM, GW), F32)],
        compiler_params=_params(2),
        name="gdn_mixer",
    )(h3, h3, hs3, buf, _heads_to_wide(s0), cw, pad_r(a_log, S_A), pad_r(dt_bias, S_A),
      jnp.tile(norm_w, GROUP_HEADS)[None, :], _ones_block_diag(), _lane_spread(S_A), _lane_spread(S_B))
    return o, _wide_to_heads(s_new)


def _rwkv_kernel(rkv_ref, hs_ref, sh_ref, shs_ref, s0_ref, mu_ref, mus_ref, w0_ref, wup_ref, a0_ref, aup_ref,
                 gup_ref, kk_ref, ka_ref, rk_ref, lng_ref, lnb_ref, ones_ref, o_ref, sout_ref, win_ref, wins_ref,
                 s_scr, *, nb, nc):
    c = pl.program_id(1)
    n = CHUNK

    rows = nc * n
    items = [(b, ci) for b in range(nb) for ci in range(nc)]

    @pl.when(c == 0)
    def _():
        for b in range(nb):
            win_ref[b, 0:8, :] = jnp.zeros((8, 3 * GW), F32)
            wins_ref[b, 0:8, :] = jnp.zeros((8, H_SMALL), F32)
            win_ref[b, 7:8, :] = sh_ref[b]
            wins_ref[b, 7:8, :] = shs_ref[b]
        s_scr[...] = s0_ref[...]

    wup_r, aup_r, gup_r = _rhs3(wup_ref[...]), _rhs3(aup_ref[...]), _rhs3(gup_ref[...])
    per_b = []
    for b in range(nb):
        x = rkv_ref[b]
        xs = hs_ref[b]
        win_ref[b, 8:8 + rows, :] = x
        wins_ref[b, 8:8 + rows, :] = xs
        xm = x + (win_ref[b, 7:7 + rows, :] - x) * mu_ref[...]
        xms = xs + (wins_ref[b, 7:7 + rows, :] - xs) * mus_ref[...]
        last = win_ref[b, rows + 7:rows + 8, :]
        win_ref[b, 7:8, :] = last
        lasts = wins_ref[b, rows + 7:rows + 8, :]
        wins_ref[b, 7:8, :] = lasts
        w_pre = w0_ref[...] + _mm3(_lhs3(jnp.tanh(xms)), wup_r)
        logw = -jnp.exp(-_softplus(-w_pre) - 0.5)
        a_sig = _sigmoid(a0_ref[...] + _mm3(_lhs3(xms), aup_r))
        gate = _mm3(_lhs3(_sigmoid(xms)), gup_r)
        kx = xm[:, GW:2 * GW]
        per_b.append(dict(rr=xm[:, 0:GW], vv=xm[:, 2 * GW:3 * GW], logw=logw, a_sig=a_sig, gate=gate,
                          kkp=kx * kk_ref[...], k2=kx * (1.0 + (a_sig - 1.0) * ka_ref[...])))

    w = _Wide()
    ones_bd = ones_ref[...]
    tril = _tril_ones(n, BF16)
    sl = lambda ci: slice(ci * n, (ci + 1) * n)
    get = lambda name: [per_b[b][name][sl(ci), :] for b, ci in items]
    rrs, vvs, logws, a_sigs, gates, kkps, k2s = (get(k) for k in ('rr', 'vv', 'logw', 'a_sig', 'gate', 'kkp', 'k2'))
    kks = [kkp * lax.rsqrt(_mm_x01(kkp * kkp, ones_bd) + 1e-6) for kkp in kkps]
    cums = [_tri_mm(tril, logw) for logw in logws]
    w_incls = [jnp.exp(cum) for cum in cums]
    w_lasts = [wi[n - 1:n, :] for wi in w_incls]
    w_invs = [jnp.exp(-cum) for cum in cums]
    ats = [-kk * jnp.exp(cum - logw) for kk, cum, logw in zip(kks, cums, logws)]
    bts = [kk * a_sig * wv for kk, a_sig, wv in zip(kks, a_sigs, w_invs)]
    kts = [k2 * wv for k2, wv in zip(k2s, w_invs)]
    rts = [rr * wi for rr, wi in zip(rrs, w_incls)]
    ar_ls = [_lhs3(jnp.concatenate([at, rt], axis=0)) for at, rt in zip(ats, rts)]
    gbs = [_mm3(ar_l, _rhs3(bt, w), NT) for ar_l, bt in zip(ar_ls, bts)]
    gks = [_mm3(ar_l, _rhs3(kt, w), NT) for ar_l, kt in zip(ar_ls, kts)]
    t_ls = [_lhs3(t) for t in _unit_lower_inverse_wide([jnp.where(w.strict, -gb[:n], 0.0) for gb in gbs], w)]
    arb_ls = [_lhs3(jnp.where(w.incl, gb[n:], 0.0)) for gb in gbs]
    ark_ls = [_lhs3(jnp.where(w.incl, gk[n:], 0.0)) for gk in gks]
    v_rs = [_rhs3(vv, w) for vv in vvs]
    aakvs = [_mm3(_lhs3(jnp.where(w.strict, gk[:n], 0.0)), v_r) for gk, v_r in zip(gks, v_rs)]
    arkvs = [_mm3(ark_l, v_r) for ark_l, v_r in zip(ark_ls, v_rs)]
    bks = [jnp.concatenate([bt * wl, kt * wl], axis=0) for bt, kt, wl in zip(bts, kts, w_lasts)]

    states = [s_scr[b] for b in range(nb)]
    ys = {}
    for ci in range(nc):
        for b in range(nb):
            it = b * nc + ci
            s0 = states[b]
            ars = _mm3(ar_ls[it], _rhs3(s0, w), NT)
            u = _mm3(t_ls[it], _rhs3(ars[:n] + aakvs[it], w))
            ys[it] = ars[n:] + _mm3(arb_ls[it], _rhs3(u, w)) + arkvs[it]
            upd = _mm3_tn(jnp.concatenate([u, vvs[it]], axis=0), bks[it])
            states[b] = s0 * w_lasts[it] + _collapse_heads(upd, w)
    for b in range(nb):
        s_scr[b] = states[b]

    inv_d = 1.0 / HEAD_DIM
    for it, (b, ci) in enumerate(items):
        y = ys[it]
        yc = y - _mm_x01(y, ones_bd) * inv_d
        var_y = _mm_x01(yc * yc, ones_bd) * inv_d
        yn = yc * lax.rsqrt(var_y + GN_EPS) * lng_ref[...] + lnb_ref[...]
        bonus = _mm_x01(rrs[it] * k2s[it] * rk_ref[...], ones_bd) * vvs[it]
        o_ref[b, sl(ci), :] = (yn + bonus) * gates[it]

    @pl.when(c == pl.num_programs(1) - 1)
    def _():
        sout_ref[...] = s_scr[...]


def _rwkv_mixer(h3, hs3, shift, s0, lp):
    bsz, l, _ = h3.shape
    nb, nc = _recurrent_tiling(bsz, l)
    rows = nc * CHUNK
    sh_main = shift[:, None, 0:3 * GW]
    sh_small = jnp.pad(shift[:, None, 3 * GW:], ((0, 0), (0, 0), (0, H_SMALL - LORA)))
    mu = lp['rwkv_mu']
    mu_main = mu[None, 0:3 * GW]
    mu_small = jnp.pad(mu[None, 3 * GW:], ((0, 0), (0, H_SMALL - LORA)))
    place = lambda w, off: jnp.zeros((H_SMALL, GW), F32).at[off:off + w.shape[0]].set(w)
    wup = place(lp['rwkv_w_up'], 0)
    aup = place(lp['rwkv_a_up'], RWKV_W_LORA)
    gup = place(lp['rwkv_g_up'], RWKV_W_LORA + RWKV_A_LORA)
    row = lambda v: v.reshape(1, GW)
    full = lambda shape: pl.BlockSpec(shape, lambda bi, i: (0,) * len(shape))
    o, s_new = pl.pallas_call(
        functools.partial(_rwkv_kernel, nb=nb, nc=nc),
        grid=(bsz // nb, l // rows),
        in_specs=[pl.BlockSpec((nb, rows, 3 * GW), lambda bi, i: (bi, i, H_CRKV // (3 * GW))),
                  pl.BlockSpec((nb, rows, H_SMALL), lambda bi, i: (bi, i, 0)),
                  pl.BlockSpec((nb, 1, 3 * GW), lambda bi, i: (bi, 0, 0)),
                  pl.BlockSpec((nb, 1, H_SMALL), lambda bi, i: (bi, 0, 0)),
                  pl.BlockSpec((nb, HEAD_DIM, GW), lambda bi, i: (bi, 0, 0)),
                  full((1, 3 * GW)), full((1, H_SMALL)), full((1, GW)), full((H_SMALL, GW)), full((1, GW)),
                  full((H_SMALL, GW)), full((H_SMALL, GW)), full((1, GW)), full((1, GW)), full((1, GW)),
                  full((1, GW)), full((1, GW)), full((GW, GW))],
        out_specs=[pl.BlockSpec((nb, rows, GW), lambda bi, i: (bi, i, 0)),
                   pl.BlockSpec((nb, HEAD_DIM, GW), lambda bi, i: (bi, 0, 0))],
        out_shape=[jax.ShapeDtypeStruct((bsz, l, GW), F32),
                   jax.ShapeDtypeStruct((bsz, HEAD_DIM, GW), F32)],
        scratch_shapes=[pltpu.VMEM((nb, rows + 8, 3 * GW), F32), pltpu.VMEM((nb, rows + 8, H_SMALL), F32),
                        pltpu.VMEM((nb, HEAD_DIM, GW), F32)],
        compiler_params=_params(2),
        name="rwkv_mixer",
    )(h3, hs3, sh_main, sh_small, _heads_to_wide(s0), mu_main, mu_small, row(lp['rwkv_w0']), wup,
      row(lp['rwkv_a0']), aup, gup, row(lp['rwkv_k_k']), row(lp['rwkv_k_a']), row(lp['rwkv_r_k']),
      row(lp['rwkv_lnx_g']), row(lp['rwkv_lnx_b']), _ones_block_diag())
    return o, _wide_to_heads(s_new)


_FB = 128
_SAMPLE_BK = 512


def _log_forget(hs, bf):
    return -_softplus(-(hs + bf))


def _head_slice(h):
    return slice(h * HEAD_DIM, (h + 1) * HEAD_DIM)


def _attend_heads(qs, kblk, vblk, key_bias, mask, carry):
    m, l, acc = carry
    bq = qs[0].shape[0]
    kb16 = kblk.astype(BF16)
    vb16 = vblk.astype(BF16)
    parts = []
    for h in range(GROUP_HEADS):
        s = _dot(qs[h], kb16[:, _head_slice(h)], NT) + key_bias[h:h + 1, :]
        parts.append(s if mask is None else jnp.where(mask, s, NEG))
    s = jnp.concatenate(parts, axis=0)
    m_new = jnp.maximum(m, jnp.max(s, -1, keepdims=True))
    alpha = jnp.exp(m - m_new)
    p = jnp.exp(s - m_new)
    l = alpha * l + jnp.sum(p, -1, keepdims=True)
    pb = p.astype(BF16)
    pv = jnp.concatenate([_dot(pb[h * bq:(h + 1) * bq, :], vb16[:, _head_slice(h)])
                          for h in range(GROUP_HEADS)], axis=0)
    return m_new, l, alpha * acc + pv


def _attend_init(bq):
    rows = GROUP_HEADS * bq
    return jnp.full((rows, 1), NEG, F32), jnp.zeros((rows, 1), F32), jnp.zeros((rows, HEAD_DIM), F32)


def _attend_store(o_ref, carry, bq):
    _, l, acc = carry
    for h in range(GROUP_HEADS):
        o_ref[:, _head_slice(h)] = acc[h * bq:(h + 1) * bq, :] / l[h * bq:(h + 1) * bq, :]


def _sel_rows(sel01, x):
    t = x.shape[0]
    r = _dot(sel01, jnp.concatenate(_split3(x), axis=0), NT)
    return r[:, :t] + r[:, t:2 * t] + r[:, 2 * t:]


def _fox_prompt_kernel(q_ref, k_ref, v_ref, hs_ref, bf_ref, o_ref, lf_ref, nf_scr, *, l, bq):
    i = pl.program_id(1)
    per = bq // _FB

    @pl.when(i == 0)
    def _():
        utri = _triu_ones(_FB, BF16)
        sel = _col_selector(8, H_SMALL, S_F).astype(BF16)
        carry = jnp.zeros((8, 1), F32)
        for jb in range(l // _FB):
            rows = slice(jb * _FB, (jb + 1) * _FB)
            lf = _log_forget(hs_ref[rows, :], bf_ref[...])
            lf_ref[rows, :] = lf[:, S_F:S_F + GROUP_HEADS]
            loc = _mm_x01(_sel_rows(sel, lf), utri)
            nf_scr[jb // per, :, (jb % per) * _FB:(jb % per + 1) * _FB] = -(loc + carry)
            carry = carry + loc[:, _FB - 1:_FB]

    q = q_ref[...]
    r_i, c_i = _iota2((bq, bq), 0), _iota2((bq, bq), 1)
    qs = [(q[:, _head_slice(h)] * (HEAD_DIM ** -0.5)).astype(BF16) for h in range(GROUP_HEADS)]

    def body(j, carry):
        rows = pl.ds(pl.multiple_of(j * bq, bq), bq)
        return _attend_heads(qs, k_ref[rows, :], v_ref[rows, :], nf_scr[j], None, carry)

    carry = lax.fori_loop(0, i, body, _attend_init(bq))
    rows = pl.ds(pl.multiple_of(i * bq, bq), bq)
    carry = _attend_heads(qs, k_ref[rows, :], v_ref[rows, :], nf_scr[i], c_i <= r_i, carry)
    _attend_store(o_ref, carry, bq)


def _fox_prompt(h3, hs3, bf_pad):
    bsz, l, _ = h3.shape
    bq = min(l, 2 * _FB)
    kern = functools.partial(_fox_prompt_kernel, l=l, bq=bq)
    return pl.pallas_call(
        kern,
        grid=(bsz, l // bq),
        in_specs=[pl.BlockSpec((None, bq, GW), lambda bi, i: (bi, i, H_DQKV // GW)),
                  pl.BlockSpec((None, l, GW), lambda bi, i: (bi, 0, H_DQKV // GW + 1)),
                  pl.BlockSpec((None, l, GW), lambda bi, i: (bi, 0, H_DQKV // GW + 2)),
                  pl.BlockSpec((None, l, H_SMALL), lambda bi, i: (bi, 0, 0)),
                  pl.BlockSpec((1, H_SMALL), lambda bi, i: (0, 0))],
        out_specs=[pl.BlockSpec((None, bq, GW), lambda bi, i: (bi, i, 0)),
                   pl.BlockSpec((None, l, GROUP_HEADS), lambda bi, i: (bi, 0, 0))],
        out_shape=[jax.ShapeDtypeStruct((bsz, l, GW), F32),
                   jax.ShapeDtypeStruct((bsz, l, GROUP_HEADS), F32)],
        scratch_shapes=[pltpu.VMEM((l // bq, 8, bq), F32)],
        compiler_params=_params(2),
        name="fox_prompt",
    )(h3, h3, h3, hs3, bf_pad)


def _fox_sample_kernel(q_ref, k_ref, v_ref, hs_ref, bf_ref, ck_ref, cv_ref, clf_ref, o_ref, lf_ref, cum_scr,
                       *, l, p):
    nblk = p // _FB
    per = _SAMPLE_BK // _FB

    loc = _mm_x01(clf_ref[...].reshape(nblk * 8, _FB), _triu_ones(_FB, BF16))
    carry = jnp.zeros((8, 1), F32)
    for jb in range(nblk):
        blk = loc[jb * 8:(jb + 1) * 8, :]
        cum_scr[jb // per, :, (jb % per) * _FB:(jb % per + 1) * _FB] = blk + carry
        carry = carry + blk[:, _FB - 1:_FB]
    total = carry

    lf = _log_forget(hs_ref[...], bf_ref[...])
    lf_ref[...] = lf[:, S_F:S_F + GROUP_HEADS]
    cum_r = _mm_x01(_sel_rows(_col_selector(8, H_SMALL, S_F).astype(BF16), lf), _triu_ones(l, BF16))
    q = q_ref[...]
    r_i, c_i = _iota2((l, l), 0), _iota2((l, l), 1)
    qs = [(q[:, _head_slice(h)] * (HEAD_DIM ** -0.5)).astype(BF16) for h in range(GROUP_HEADS)]

    def body(j, carry):
        rows = pl.ds(pl.multiple_of(j * _SAMPLE_BK, _SAMPLE_BK), _SAMPLE_BK)
        suffix = total - cum_scr[j]
        return _attend_heads(qs, ck_ref[rows, :], cv_ref[rows, :], suffix, None, carry)

    carry = lax.fori_loop(0, p // _SAMPLE_BK, body, _attend_init(l))
    carry = _attend_heads(qs, k_ref[...], v_ref[...], -cum_r, c_i <= r_i, carry)
    _attend_store(o_ref, carry, l)


def _fox_sample(h3, hs3, bf_pad, ck, cv, clogf):
    bsz, l, _ = h3.shape
    p = ck.shape[1]
    ck2 = ck.reshape(bsz, p, GW)
    cv2 = cv.reshape(bsz, p, GW)
    clf = clogf.reshape(bsz, p // _FB, _FB, GROUP_HEADS).transpose(0, 1, 3, 2)
    clf = jnp.pad(clf, ((0, 0), (0, 0), (0, 8 - GROUP_HEADS), (0, 0)))
    kern = functools.partial(_fox_sample_kernel, l=l, p=p)
    return pl.pallas_call(
        kern,
        grid=(bsz,),
        in_specs=[pl.BlockSpec((None, l, GW), lambda bi: (bi, 0, H_DQKV // GW)),
                  pl.BlockSpec((None, l, GW), lambda bi: (bi, 0, H_DQKV // GW + 1)),
                  pl.BlockSpec((None, l, GW), lambda bi: (bi, 0, H_DQKV // GW + 2)),
                  pl.BlockSpec((None, l, H_SMALL), lambda bi: (bi, 0, 0)),
                  pl.BlockSpec((1, H_SMALL), lambda bi: (0, 0)),
                  pl.BlockSpec((None, p, GW), lambda bi: (bi, 0, 0)),
                  pl.BlockSpec((None, p, GW), lambda bi: (bi, 0, 0)),
                  pl.BlockSpec((None, p // _FB, 8, _FB), lambda bi: (bi, 0, 0, 0))],
        out_specs=[pl.BlockSpec((None, l, GW), lambda bi: (bi, 0, 0)),
                   pl.BlockSpec((None, l, GROUP_HEADS), lambda bi: (bi, 0, 0))],
        out_shape=[jax.ShapeDtypeStruct((bsz, l, GW), F32),
                   jax.ShapeDtypeStruct((bsz, l, GROUP_HEADS), F32)],
        scratch_shapes=[pltpu.VMEM((p // _SAMPLE_BK, 8, _SAMPLE_BK), F32)],
        compiler_params=_params(1),
        name="fox_sample",
    )(h3, h3, h3, hs3, bf_pad, ck2, cv2, clf)


def _outproj_kernel(oa_ref, ob_ref, oc_ref, od_ref, x_ref, w_ref, g_ref, b_ref, y_ref):
    acc = DN_ALPHA * x_ref[...]
    for gi, ref in enumerate((oa_ref, ob_ref, oc_ref, od_ref)):
        acc = acc + _mm(ref[...], w_ref[gi * GW:(gi + 1) * GW, :])
    y_ref[...] = _layer_norm(acc, g_ref[...], b_ref[...])


def _outproj(oa, ob, oc, od, x2d, w_out, g, b):
    t = x2d.shape[0]
    tm = min(t, 512)
    mix = pl.BlockSpec((tm, GW), lambda i: (i, 0))
    return pl.pallas_call(
        _outproj_kernel,
        grid=(t // tm,),
        in_specs=[mix, mix, mix, mix,
                  pl.BlockSpec((tm, D_MODEL), lambda i: (i, 0)),
                  pl.BlockSpec((D_MODEL, D_MODEL), lambda i: (0, 0)),
                  pl.BlockSpec((1, D_MODEL), lambda i: (0, 0)),
                  pl.BlockSpec((1, D_MODEL), lambda i: (0, 0))],
        out_specs=pl.BlockSpec((tm, D_MODEL), lambda i: (i, 0)),
        out_shape=jax.ShapeDtypeStruct((t, D_MODEL), F32),
        compiler_params=_params(1),
        name="outproj_ln",
    )(oa, ob, oc, od, x2d, w_out, g, b)


_R_G, _R_E = 0, N_EXPERT_GROUPS


def _moe_kernel(x_ref, wr_ref, br_ref, wg_ref, wu_ref, wd_ref, g_ref, b_ref, y_ref, comb_scr, xb_scr, acc_scr):
    e = pl.program_id(1)
    tm = x_ref.shape[0]
    lane = _iota2((tm, LANE), 1)

    @pl.when(e == 0)
    def _():
        x = x_ref[...]
        xb_scr[...] = x.astype(BF16)
        acc_scr[...] = jnp.zeros_like(acc_scr)
        logits = _mmh(x, wr_ref[...]) + br_ref[...]
        gmask = lane < N_EXPERT_GROUPS
        gl = jnp.where(gmask, logits, NEG)
        ge = jnp.exp(gl - jnp.max(gl, -1, keepdims=True))
        pg = ge / jnp.sum(ge, -1, keepdims=True)
        gp = jnp.max(pg, -1, keepdims=True)
        gi = jnp.min(jnp.where(jnp.logical_and(gmask, pg == gp), lane, LANE), -1, keepdims=True)
        emask = jnp.logical_and(jnp.logical_and(lane >= _R_E, lane < _R_E + N_EXPERTS),
                                ((lane - _R_E) >> 2) == gi)
        el = jnp.where(emask, logits, NEG)
        ee = jnp.exp(el - jnp.max(el, -1, keepdims=True))
        ep = ee / jnp.sum(ee, -1, keepdims=True)
        m1 = jnp.max(jnp.where(emask, ep, -1.0), -1, keepdims=True)
        i1 = jnp.min(jnp.where(jnp.logical_and(emask, ep == m1), lane, LANE), -1, keepdims=True)
        rest = jnp.logical_and(emask, lane != i1)
        m2 = jnp.max(jnp.where(rest, ep, -1.0), -1, keepdims=True)
        i2 = jnp.min(jnp.where(jnp.logical_and(rest, ep == m2), lane, LANE), -1, keepdims=True)
        den = m1 + m2
        comb_scr[...] = jnp.where(lane == i1, gp * m1 / den, jnp.where(lane == i2, gp * m2 / den, 0.0))

    ce = jnp.sum(jnp.where(lane == e + _R_E, comb_scr[...], 0.0), -1, keepdims=True)
    xb = xb_scr[...]
    hh = _silu(_mm(xb, wg_ref[...])) * _mm(xb, wu_ref[...])
    acc_scr[...] += ce * _mm(hh, wd_ref[...])

    @pl.when(e == pl.num_programs(1) - 1)
    def _():
        y_ref[...] = _layer_norm(DN_ALPHA * x_ref[...] + acc_scr[...], g_ref[...], b_ref[...])


def _moe(x2d, w_router, b_router, wg, wu, wd, g, b):
    t = x2d.shape[0]
    tm = min(t, 1024)
    return pl.pallas_call(
        _moe_kernel,
        grid=(t // tm, N_EXPERTS),
        in_specs=[pl.BlockSpec((tm, D_MODEL), lambda i, e: (i, 0)),
                  pl.BlockSpec((D_MODEL, LANE), lambda i, e: (0, 0)),
                  pl.BlockSpec((1, LANE), lambda i, e: (0, 0)),
                  pl.BlockSpec((None, D_MODEL, D_EXPERT), lambda i, e: (e, 0, 0)),
                  pl.BlockSpec((None, D_MODEL, D_EXPERT), lambda i, e: (e, 0, 0)),
                  pl.BlockSpec((None, D_EXPERT, D_MODEL), lambda i, e: (e, 0, 0)),
                  pl.BlockSpec((1, D_MODEL), lambda i, e: (0, 0)),
                  pl.BlockSpec((1, D_MODEL), lambda i, e: (0, 0))],
        out_specs=pl.BlockSpec((tm, D_MODEL), lambda i, e: (i, 0)),
        out_shape=jax.ShapeDtypeStruct((t, D_MODEL), F32),
        scratch_shapes=[pltpu.VMEM((tm, LANE), F32), pltpu.VMEM((tm, D_MODEL), BF16),
                        pltpu.VMEM((tm, D_MODEL), F32)],
        compiler_params=_params(2),
        name="moe_ln",
    )(x2d, w_router, b_router, wg, wu, wd, g, b)


def _prep_layer(lp):
    w = lp['w_in']
    c0 = CONV_COLS
    c1 = c0 + GDN_COLS
    c2 = c1 + RWKV_COLS
    main = jnp.concatenate([w[:, c0:c0 + 3 * GW], w[:, c1:c1 + 3 * GW], w[:, c2:c2 + 3 * GW],
                            w[:, 0:GW], w[:, GW:2 * GW], w[:, c0 + 3 * GW:c0 + 4 * GW]], axis=1).astype(BF16)
    small = jnp.concatenate([w[:, c1 + 3 * GW:c2], w[:, c0 + 4 * GW:c1], w[:, c2 + 3 * GW:],
                             jnp.zeros((D_MODEL, H_SMALL - LORA - 3 * GROUP_HEADS), F32)], axis=1)
    wr = jnp.concatenate([lp['router_g_w'], lp['router_e_w'],
                          jnp.zeros((D_MODEL, LANE - N_EXPERT_GROUPS - N_EXPERTS), F32)], axis=1)
    br = jnp.concatenate([lp['router_g_b'], lp['router_e_b'],
                          jnp.zeros((LANE - N_EXPERT_GROUPS - N_EXPERTS,), F32)])[None, :]
    bf_pad = jnp.zeros((1, H_SMALL), F32).at[0, S_F:S_F + GROUP_HEADS].set(lp['fox_b_f'])
    return dict(w_main=main, w_small=small, w_out=lp['w_out'].astype(BF16), wr=wr, br=br, bf_pad=bf_pad)


def _trunk_layer(x, lp, pp, conv_buf, gdn_buf, gdn_s, rw_shift, rw_s, fox_cache):
    bsz, l, d = x.shape
    x2d = x.reshape(bsz * l, d)
    h2, hs2 = _inproj(x2d, pp['w_main'], pp['w_small'])
    h3 = h2.reshape(bsz, l, H_MAIN)
    hs3 = hs2.reshape(bsz, l, H_SMALL)

    o_a, new_conv = _conv_mixer(h3, conv_buf, lp['conv_w'], lp['conv_b'][None], lp['conv_ln_g'][None],
                                lp['conv_ln_b'][None])
    o_b, new_gdn_s = _gdn_mixer(h3, hs3, gdn_buf, gdn_s, lp['gdn_conv_w'], lp['gdn_a_log'], lp['gdn_dt_bias'],
                                lp['gdn_norm_w'])
    o_c, new_rw_s = _rwkv_mixer(h3, hs3, rw_shift, rw_s, lp)
    if fox_cache is None:
        o_d, logf = _fox_prompt(h3, hs3, pp['bf_pad'])
    else:
        o_d, logf = _fox_sample(h3, hs3, pp['bf_pad'], *fox_cache)

    flat = lambda o: o.reshape(bsz * l, GW)
    x1 = _outproj(flat(o_a), flat(o_b), flat(o_c), flat(o_d), x2d, pp['w_out'], lp['ln1_g'][None],
                  lp['ln1_b'][None])
    x2 = _moe(x1, pp['wr'], pp['br'], lp['exp_w_gate'], lp['exp_w_up'], lp['exp_w_down'], lp['ln2_g'][None],
              lp['ln2_b'][None])

    kw = GDN_CONV_WIDTH - 1
    new_gdn_buf = h3[:, l - kw:, H_BQKV:H_BQKV + 3 * GW]
    new_shift = jnp.concatenate([h3[:, l - 1, H_CRKV:H_CRKV + 3 * GW], hs3[:, l - 1, S_LORA:S_LORA + LORA]], axis=-1)
    k_d = h3[:, :, H_DQKV + GW:H_DQKV + 2 * GW].reshape(bsz, l, GROUP_HEADS, HEAD_DIM)
    v_d = h3[:, :, H_DQKV + 2 * GW:H_DQKV + 3 * GW].reshape(bsz, l, GROUP_HEADS, HEAD_DIM)
    return x2.reshape(bsz, l, d), (new_conv, new_gdn_buf, new_gdn_s, new_shift, new_rw_s, k_d, v_d, logf)


_LAYER_KEYS = ('w_in', 'conv_w', 'conv_b', 'conv_ln_g', 'conv_ln_b', 'gdn_conv_w', 'gdn_a_log', 'gdn_dt_bias',
               'gdn_norm_w', 'rwkv_mu', 'rwkv_w0', 'rwkv_w_up', 'rwkv_a0', 'rwkv_a_up', 'rwkv_g_up', 'rwkv_k_k',
               'rwkv_k_a', 'rwkv_r_k', 'rwkv_lnx_g', 'rwkv_lnx_b', 'fox_b_f', 'w_out', 'ln1_g', 'ln1_b',
               'router_g_w', 'router_g_b', 'router_e_w', 'router_e_b', 'exp_w_gate', 'exp_w_up', 'exp_w_down',
               'ln2_g', 'ln2_b')


def kernel(x_prompt, x_sample, cache_fox_k, cache_fox_v, cache_fox_logf, state_conv, state_gdn_conv, state_gdn, state_rwkv_shift, state_rwkv, w_in, conv_w, conv_b, conv_ln_g, conv_ln_b, gdn_conv_w, gdn_a_log, gdn_dt_bias, gdn_norm_w, rwkv_mu, rwkv_w0, rwkv_w_up, rwkv_a0, rwkv_a_up, rwkv_g_up, rwkv_k_k, rwkv_k_a, rwkv_r_k, rwkv_lnx_g, rwkv_lnx_b, fox_b_f, w_out, ln1_g, ln1_b, router_g_w, router_g_b, router_e_w, router_e_b, exp_w_gate, exp_w_up, exp_w_down, ln2_g, ln2_b):
    weights = dict(zip(_LAYER_KEYS, (w_in, conv_w, conv_b, conv_ln_g, conv_ln_b, gdn_conv_w, gdn_a_log,
                                     gdn_dt_bias, gdn_norm_w, rwkv_mu, rwkv_w0, rwkv_w_up, rwkv_a0, rwkv_a_up,
                                     rwkv_g_up, rwkv_k_k, rwkv_k_a, rwkv_r_k, rwkv_lnx_g, rwkv_lnx_b, fox_b_f,
                                     w_out, ln1_g, ln1_b, router_g_w, router_g_b, router_e_w, router_e_b,
                                     exp_w_gate, exp_w_up, exp_w_down, ln2_g, ln2_b)))
    xp, xs = x_prompt, x_sample
    bp = x_prompt.shape[0]
    outs_p = [[] for _ in range(8)]
    outs_s = [[] for _ in range(8)]
    for l in range(DEPTH):
        lp = {k: v[l] for k, v in weights.items()}
        pp = _prep_layer(lp)
        xp, st_p = _trunk_layer(xp, lp, pp,
                                jnp.zeros((bp, CONV_WIDTH - 1, GW), F32),
                                jnp.zeros((bp, GDN_CONV_WIDTH - 1, 3 * GW), F32),
                                jnp.zeros((bp, GROUP_HEADS, HEAD_DIM, HEAD_DIM), F32),
                                jnp.zeros((bp, RWKV_COLS), F32),
                                jnp.zeros((bp, GROUP_HEADS, HEAD_DIM, HEAD_DIM), F32),
                                None)
        xs, st_s = _trunk_layer(xs, lp, pp, state_conv[l], state_gdn_conv[l], state_gdn[l], state_rwkv_shift[l],
                                state_rwkv[l], (cache_fox_k[l], cache_fox_v[l], cache_fox_logf[l]))
        for i in range(8):
            outs_p[i].append(st_p[i])
            outs_s[i].append(st_s[i])
    stacked_p = [jnp.stack(o) for o in outs_p]
    stacked_s = [jnp.stack(o) for o in outs_s]
    return (xp, xs, *stacked_p, *stacked_s)
```

```python
import functools

import jax
import jax.numpy as jnp
from jax import lax
from jax.experimental import pallas as pl
from jax.experimental.pallas import tpu as pltpu

F32 = jnp.float32
BF16 = jnp.bfloat16

D_MODEL = 1024
DEPTH = 2
CHUNK = 64
HEAD_DIM = 64
GW = D_MODEL // 4
GROUP_HEADS = GW // HEAD_DIM
CONV_WIDTH = 31
GDN_CONV_WIDTH = 4
RWKV_W_LORA = 16
RWKV_A_LORA = 16
RWKV_G_LORA = 32
N_EXPERT_GROUPS = 4
EXPERTS_PER_GROUP = 4
N_EXPERTS = N_EXPERT_GROUPS * EXPERTS_PER_GROUP
D_EXPERT = 256
DN_ALPHA = (2 * DEPTH) ** 0.25
LN_EPS = 1e-5
GN_EPS = 64e-5

CONV_COLS = 2 * GW
GDN_COLS = 4 * GW + 2 * GROUP_HEADS
RWKV_COLS = 3 * GW + RWKV_W_LORA + RWKV_A_LORA + RWKV_G_LORA
FOX_COLS = 3 * GW + GROUP_HEADS
LORA = RWKV_W_LORA + RWKV_A_LORA + RWKV_G_LORA

H_BQKV, H_CRKV, H_DQKV, H_AVAL, H_AGATE, H_BZ = 0, 768, 1536, 2304, 2560, 2816
H_MAIN = 3072
S_LORA, S_A, S_B, S_F = 0, 64, 68, 72
H_SMALL = 128
LANE = 128

VMEM_LIMIT = 56 * 1024 * 1024

NN = (((1,), (0,)), ((), ()))
NT = (((1,), (1,)), ((), ()))
TN = (((0,), (0,)), ((), ()))
NEG = -1e30


def _dot(a, b, dims=NN):
    return lax.dot_general(a, b, dims, preferred_element_type=F32)


def _mm(a, b, dims=NN):
    return _dot(a.astype(BF16), b.astype(BF16), dims)


def _mmh(a, b, dims=NN):
    return lax.dot_general(a, b, dims, preferred_element_type=F32, precision=lax.Precision.HIGHEST)


def _sigmoid(x):
    return 0.5 * (jnp.tanh(0.5 * x) + 1.0)


def _silu(x):
    return x * _sigmoid(x)


def _softplus(x):
    return jnp.maximum(x, 0.0) + jnp.log(1.0 + jnp.exp(-jnp.abs(x)))


def _iota2(shape, dim):
    return lax.broadcasted_iota(jnp.int32, shape, dim)


def _tril_ones(n, dtype=F32):
    r, c = _iota2((n, n), 0), _iota2((n, n), 1)
    return (r >= c).astype(dtype)


def _triu_ones(n, dtype=F32):
    r, c = _iota2((n, n), 0), _iota2((n, n), 1)
    return (r <= c).astype(dtype)


def _col_selector(rows, width, col0):
    r, c = _iota2((rows, width), 0), _iota2((rows, width), 1)
    return (c == r + col0).astype(F32)


def _layer_norm(x, g, b):
    mu = jnp.mean(x, axis=-1, keepdims=True)
    xc = x - mu
    var = jnp.mean(xc * xc, axis=-1, keepdims=True)
    return xc * lax.rsqrt(var + LN_EPS) * g + b


def _split2(x):
    hi = x.astype(BF16)
    lo = (x - hi.astype(F32)).astype(BF16)
    return hi, lo


def _split3(x):
    hi = x.astype(BF16)
    r = x - hi.astype(F32)
    mid = r.astype(BF16)
    lo = (r - mid.astype(F32)).astype(BF16)
    return hi, mid, lo


class _Wide:
    def __init__(self):
        self.i = _iota2((CHUNK, GW), 0)
        lane = _iota2((CHUNK, GW), 1)
        self.j = lane & (HEAD_DIM - 1)
        self.head = lane >> 6
        self.incl = self.i >= self.j
        self.strict = self.i > self.j
        self.eye = self.i == self.j
        r, c = _iota2((GW, GW), 0), _iota2((GW, GW), 1)
        self.block_diag = (r >> 6) == (c >> 6)


def _head_block_diag(x, w):
    t = jnp.concatenate([x] * GROUP_HEADS, axis=0)
    return jnp.where(w.block_diag, t, jnp.zeros_like(t))


def _collapse_heads(full, w):
    out = jnp.zeros((HEAD_DIM, GW), F32)
    for h in range(GROUP_HEADS):
        out = out + jnp.where(w.head == h, full[h * HEAD_DIM:(h + 1) * HEAD_DIM, :], 0.0)
    return out


def _lhs3(a, precise=False):
    if not precise:
        return None, a.astype(BF16)
    hi, lo = _split2(a)
    return jnp.concatenate([hi, lo], axis=0), hi


def _rhs3(b, w=None, precise=False):
    if not precise:
        hi = b.astype(BF16)
        return (hi if w is None else _head_block_diag(hi, w)), None
    hi, lo = _split2(b)
    if w is None:
        return hi, lo
    return _head_block_diag(hi, w), _head_block_diag(lo, w)


def _mm3(lhs, rhs, dims=NN):
    cat, hi = lhs
    bh, bl = rhs
    if cat is None or bl is None:
        return _dot(hi, bh, dims)
    m = hi.shape[0]
    r = _dot(cat, bh, dims)
    return r[:m] + r[m:] + _dot(hi, bl, dims)


def _mm3_tn(a, b, precise=False):
    if not precise:
        return _dot(a.astype(BF16), b.astype(BF16), TN)
    ah, al = _split2(a)
    bh, bl = _split2(b)
    m = a.shape[1]
    r = _dot(jnp.concatenate([ah, al], axis=1), bh, TN)
    return r[:m] + r[m:] + _dot(ah, bl, TN)


def _mm_x01(a, m01, pieces=2):
    m = a.shape[0]
    r = _dot(jnp.concatenate(_split3(a) if pieces == 3 else _split2(a), axis=0), m01)
    return sum(r[i * m:(i + 1) * m] for i in range(pieces))


def _tri_mm(tri01, x, pieces=2):
    n = x.shape[1]
    r = _dot(tri01, jnp.concatenate(_split3(x) if pieces == 3 else _split2(x), axis=1))
    return sum(r[:, i * n:(i + 1) * n] for i in range(pieces))


def _unit_lower_inverse_wide(mats, w):
    eye = w.eye.astype(F32)
    same8 = (w.i >> 3) == (w.j >> 3)
    ds = [jnp.where(same8, a, 0.0) for a in mats]
    d2 = [_mm3(_lhs3(d), _rhs3(d, w)) for d in ds]
    d2_r = [_rhs3(x, w) for x in d2]
    d4 = [_mm3(_lhs3(x), r) for x, r in zip(d2, d2_r)]
    imd = [eye - d for d in ds]
    p1 = [i + _mm3(_lhs3(i), r) for i, r in zip(imd, d2_r)]
    ts = [p + _mm3(_lhs3(p), _rhs3(x, w)) for p, x in zip(p1, d4)]
    for s in range(3, 6):
        big = (w.i >> (s + 1)) == (w.j >> (s + 1))
        small = (w.i >> s) == (w.j >> s)
        sel = jnp.logical_and(big, jnp.logical_not(small))
        inner = [_mm3(_lhs3(jnp.where(sel, a, 0.0)), _rhs3(t, w)) for a, t in zip(mats, ts)]
        ts = [t - _mm3(_lhs3(t), _rhs3(x, w)) for t, x in zip(ts, inner)]
    return ts


def _params(n_axes):
    return pltpu.CompilerParams(dimension_semantics=("arbitrary",) * n_axes,
                                vmem_limit_bytes=VMEM_LIMIT)


def _heads_to_wide(s):
    b = s.shape[0]
    return s.transpose(0, 2, 1, 3).reshape(b, HEAD_DIM, GW)


def _wide_to_heads(s):
    b = s.shape[0]
    return s.reshape(b, HEAD_DIM, GROUP_HEADS, HEAD_DIM).transpose(0, 2, 1, 3)


def _ones_block_diag():
    r, c = jnp.arange(GW)[:, None], jnp.arange(GW)[None, :]
    return ((r // HEAD_DIM) == (c // HEAD_DIM)).astype(BF16)


def _lane_spread(col0):
    r, c = jnp.arange(H_SMALL)[:, None], jnp.arange(GW)[None, :]
    return ((r - col0) == (c // HEAD_DIM)).astype(BF16)


def _inproj_kernel(x_ref, wm_ref, ws_ref, h_ref, hs_ref):
    x = x_ref[...]
    h_ref[...] = _mm(x, wm_ref[...])
    hs_ref[...] = _mmh(x, ws_ref[...])


def _inproj(x2d, w_main, w_small):
    t = x2d.shape[0]
    tm = min(t, 512)
    return pl.pallas_call(
        _inproj_kernel,
        grid=(t // tm,),
        in_specs=[pl.BlockSpec((tm, D_MODEL), lambda i: (i, 0)),
                  pl.BlockSpec((D_MODEL, H_MAIN), lambda i: (0, 0)),
                  pl.BlockSpec((D_MODEL, H_SMALL), lambda i: (0, 0))],
        out_specs=[pl.BlockSpec((tm, H_MAIN), lambda i: (i, 0)),
                   pl.BlockSpec((tm, H_SMALL), lambda i: (i, 0))],
        out_shape=[jax.ShapeDtypeStruct((t, H_MAIN), F32), jax.ShapeDtypeStruct((t, H_SMALL), F32)],
        compiler_params=_params(1),
        name="inproj",
    )(x2d, w_main, w_small)


_HALO = 32


def _conv_kernel(val_ref, gate_ref, buf_ref, cw_ref, cb_ref, g_ref, b_ref, o_ref, nbuf_ref, win_ref, *, tl):
    i = pl.program_id(1)
    pad = _HALO - (CONV_WIDTH - 1)

    @pl.when(i == 0)
    def _():
        win_ref[0:8, :] = jnp.zeros((8, GW), F32)
        win_ref[pad:_HALO, :] = buf_ref[...]

    u = val_ref[...] * _sigmoid(gate_ref[...])
    win_ref[_HALO:_HALO + tl, :] = u
    sub = min(tl, 64)
    for r0 in range(0, tl, sub):
        acc = jnp.zeros((sub, GW), F32)
        for j in range(CONV_WIDTH):
            acc = acc + cw_ref[j:j + 1, :] * win_ref[pad + r0 + j:pad + r0 + j + sub, :]
        y = _layer_norm(acc + cb_ref[...], g_ref[...], b_ref[...])
        o_ref[r0:r0 + sub, :] = _silu(y)
    tail = win_ref[tl:tl + _HALO, :]
    win_ref[0:_HALO, :] = tail

    @pl.when(i == pl.num_programs(1) - 1)
    def _():
        nbuf_ref[...] = win_ref[pad:_HALO, :]


def _conv_mixer(h3, buf, cw, cb, g, b):
    bsz, l, _ = h3.shape
    tl = min(l, 256)
    kern = functools.partial(_conv_kernel, tl=tl)
    full = lambda shape: pl.BlockSpec(shape, lambda bi, i: (0,) * len(shape))
    return pl.pallas_call(
        kern,
        grid=(bsz, l // tl),
        in_specs=[pl.BlockSpec((None, tl, GW), lambda bi, i: (bi, i, H_AVAL // GW)),
                  pl.BlockSpec((None, tl, GW), lambda bi, i: (bi, i, H_AGATE // GW)),
                  pl.BlockSpec((None, CONV_WIDTH - 1, GW), lambda bi, i: (bi, 0, 0)),
                  full((CONV_WIDTH, GW)), full((1, GW)), full((1, GW)), full((1, GW))],
        out_specs=[pl.BlockSpec((None, tl, GW), lambda bi, i: (bi, i, 0)),
                   pl.BlockSpec((None, CONV_WIDTH - 1, GW), lambda bi, i: (bi, 0, 0))],
        out_shape=[jax.ShapeDtypeStruct((bsz, l, GW), F32),
                   jax.ShapeDtypeStruct((bsz, CONV_WIDTH - 1, GW), F32)],
        scratch_shapes=[pltpu.VMEM((tl + _HALO, GW), F32)],
        compiler_params=_params(2),
        name="conv_mixer",
    )(h3, h3, buf, cw, cb, g, b)


def _gdn_kernel(qkv_ref, z_ref, hs_ref, buf_ref, s0_ref, cw_ref, alog_ref, dtb_ref, nw_ref, ones_ref, sela_ref,
                selb_ref, o_ref, sout_ref, win_ref, s_scr, *, nb, nc):
    c = pl.program_id(1)
    n = CHUNK
    rows = nc * n
    kw = GDN_CONV_WIDTH - 1
    items = [(b, ci) for b in range(nb) for ci in range(nc)]

    @pl.when(c == 0)
    def _():
        for b in range(nb):
            win_ref[b, 0:8, :] = jnp.zeros((8, 3 * GW), F32)
            win_ref[b, 8 - kw:8, :] = buf_ref[b]
        s_scr[...] = s0_ref[...]

    qkvs = []
    for b in range(nb):
        win_ref[b, 8:8 + rows, :] = qkv_ref[b]
        conv = jnp.zeros((rows, 3 * GW), F32)
        for j in range(GDN_CONV_WIDTH):
            conv = conv + cw_ref[j:j + 1, :] * win_ref[b, 8 - kw + j:8 - kw + j + rows, :]
        tail = win_ref[b, rows:rows + 8, :]
        win_ref[b, 0:8, :] = tail
        qkvs.append(_silu(conv))

    w = _Wide()
    ones_bd = ones_ref[...]
    tril = _tril_ones(n, BF16)
    sl = lambda ci: slice(ci * n, (ci + 1) * n)
    qs = [qkvs[b][sl(ci), 0:GW] for b, ci in items]
    ks = [qkvs[b][sl(ci), GW:2 * GW] for b, ci in items]
    vs = [qkvs[b][sl(ci), 2 * GW:3 * GW] for b, ci in items]
    sss = [_mm_x01(jnp.concatenate([q * q, k * k], axis=0), ones_bd) for q, k in zip(qs, ks)]
    qs = [q * lax.rsqrt(ss[:n] + 1e-6) * (HEAD_DIM ** -0.5) for q, ss in zip(qs, sss)]
    ks = [k * lax.rsqrt(ss[n:] + 1e-6) for k, ss in zip(ks, sss)]

    hss = [hs_ref[b, sl(ci), :] for b, ci in items]
    gs = [_mm_x01(-jnp.exp(alog_ref[...]) * _softplus(hs + dtb_ref[...]), sela_ref[...]) for hs in hss]
    betas = [_mm_x01(_sigmoid(hs), selb_ref[...]) for hs in hss]
    gcs = [_tri_mm(tril, g) for g in gs]
    grs = [jnp.sum(jnp.where(w.eye, gc, 0.0), axis=0, keepdims=True) for gc in gcs]
    decays = [jnp.exp(jnp.where(w.incl, gc - gr, NEG)) for gc, gr in zip(gcs, grs)]

    kbs = [k * beta for k, beta in zip(ks, betas)]
    grams = [_mm3(_lhs3(jnp.concatenate([kb, q], axis=0)), _rhs3(k, w), NT)
             for kb, q, k in zip(kbs, qs, ks)]
    amats = [jnp.where(w.strict, gram[:n] * decay, 0.0) for gram, decay in zip(grams, decays)]
    qks = [jnp.where(w.incl, gram[n:] * decay, 0.0) for gram, decay in zip(grams, decays)]
    t_ls = [_lhs3(t) for t in _unit_lower_inverse_wide(amats, w)]
    egs = [jnp.exp(gc) for gc in gcs]
    xvs = [_mm3(t_l, _rhs3(v * beta, w)) for t_l, v, beta in zip(t_ls, vs, betas)]
    xks = [_mm3(t_l, _rhs3(kb * eg, w)) for t_l, kb, eg in zip(t_ls, kbs, egs)]
    xq_ls = [_lhs3(jnp.concatenate([xk, q * eg], axis=0)) for xk, q, eg in zip(xks, qs, egs)]
    qk_ls = [_lhs3(qk) for qk in qks]
    gls = [gc[n - 1:n, :] for gc in gcs]
    kds = [k * jnp.exp(gl - gc) for k, gl, gc in zip(ks, gls, gcs)]

    states = [s_scr[b] for b in range(nb)]
    outs = {}
    for ci in range(nc):
        for b in range(nb):
            it = b * nc + ci
            s = states[b]
            rs = _mm3(xq_ls[it], _rhs3(s, w))
            u = xvs[it] - rs[:n]
            outs[it] = rs[n:] + _mm3(qk_ls[it], _rhs3(u, w))
            states[b] = s * jnp.exp(gls[it]) + _collapse_heads(_mm3_tn(kds[it], u), w)
    for b in range(nb):
        s_scr[b] = states[b]

    for it, (b, ci) in enumerate(items):
        o = outs[it]
        ms = _mm_x01(o * o, ones_bd) * (1.0 / HEAD_DIM)
        o_ref[b, sl(ci), :] = o * lax.rsqrt(ms + 1e-6) * nw_ref[...] * _silu(z_ref[b, sl(ci), :])

    @pl.when(c == pl.num_programs(1) - 1)
    def _():
        sout_ref[...] = s_scr[...]


def _recurrent_tiling(bsz, l):
    nc = min(l // CHUNK, 4)
    nb = 2 if nc > 1 else min(bsz, 4)
    return nb, nc


def _gdn_mixer(h3, hs3, buf, s0, cw, a_log, dt_bias, norm_w):
    bsz, l, _ = h3.shape
    nb, nc = _recurrent_tiling(bsz, l)
    rows = nc * CHUNK
    pad_r = lambda v, off: jnp.zeros((1, H_SMALL), F32).at[0, off:off + GROUP_HEADS].set(v)
    full = lambda shape: pl.BlockSpec(shape, lambda bi, i: (0,) * len(shape))
    o, s_new = pl.pallas_call(
        functools.partial(_gdn_kernel, nb=nb, nc=nc),
        grid=(bsz // nb, l // rows),
        in_specs=[pl.BlockSpec((nb, rows, 3 * GW), lambda bi, i: (bi, i, H_BQKV // (3 * GW))),
                  pl.BlockSpec((nb, rows, GW), lambda bi, i: (bi, i, H_BZ // GW)),
                  pl.BlockSpec((nb, rows, H_SMALL), lambda bi, i: (bi, i, 0)),
                  pl.BlockSpec((nb, GDN_CONV_WIDTH - 1, 3 * GW), lambda bi, i: (bi, 0, 0)),
                  pl.BlockSpec((nb, HEAD_DIM, GW), lambda bi, i: (bi, 0, 0)),
                  full((GDN_CONV_WIDTH, 3 * GW)), full((1, H_SMALL)), full((1, H_SMALL)), full((1, GW)),
                  full((GW, GW)), full((H_SMALL, GW)), full((H_SMALL, GW))],
        out_specs=[pl.BlockSpec((nb, rows, GW), lambda bi, i: (bi, i, 0)),
                   pl.BlockSpec((nb, HEAD_DIM, GW), lambda bi, i: (bi, 0, 0))],
        out_shape=[jax.ShapeDtypeStruct((bsz, l, GW), F32),
                   jax.ShapeDtypeStruct((bsz, HEAD_DIM, GW), F32)],
        scratch_shapes=[pltpu.VMEM((nb, rows + 8, 3 * GW), F32), pltpu.VMEM((nb, HEAD_DIM, GW), F32)],
        compiler_params=_params(2),
        name="gdn_mixer",
    )(h3, h3, hs3, buf, _heads_to_wide(s0), cw, pad_r(a_log, S_A), pad_r(dt_bias, S_A),
      jnp.tile(norm_w, GROUP_HEADS)[None, :], _ones_block_diag(), _lane_spread(S_A), _lane_spread(S_B))
    return o, _wide_to_heads(s_new)


def _rwkv_kernel(rkv_ref, hs_ref, sh_ref, shs_ref, s0_ref, mu_ref, mus_ref, w0_ref, wup_ref, a0_ref, aup_ref,
                 gup_ref, kk_ref, ka_ref, rk_ref, lng_ref, lnb_ref, ones_ref, o_ref, sout_ref, win_ref, wins_ref,
                 s_scr, *, nb, nc):
    c = pl.program_id(1)
    n = CHUNK

    rows = nc * n
    items = [(b, ci) for b in range(nb) for ci in range(nc)]

    @pl.when(c == 0)
    def _():
        for b in range(nb):
            win_ref[b, 0:8, :] = jnp.zeros((8, 3 * GW), F32)
            wins_ref[b, 0:8, :] = jnp.zeros((8, H_SMALL), F32)
            win_ref[b, 7:8, :] = sh_ref[b]
            wins_ref[b, 7:8, :] = shs_ref[b]
        s_scr[...] = s0_ref[...]

    wup_r, aup_r, gup_r = (_rhs3(r[...], precise=True) for r in (wup_ref, aup_ref, gup_ref))
    per_b = []
    for b in range(nb):
        x = rkv_ref[b]
        xs = hs_ref[b]
        win_ref[b, 8:8 + rows, :] = x
        wins_ref[b, 8:8 + rows, :] = xs
        xm = x + (win_ref[b, 7:7 + rows, :] - x) * mu_ref[...]
        xms = xs + (wins_ref[b, 7:7 + rows, :] - xs) * mus_ref[...]
        last = win_ref[b, rows + 7:rows + 8, :]
        win_ref[b, 7:8, :] = last
        lasts = wins_ref[b, rows + 7:rows + 8, :]
        wins_ref[b, 7:8, :] = lasts
        w_pre = w0_ref[...] + _mm3(_lhs3(jnp.tanh(xms), precise=True), wup_r)
        logw = -jnp.exp(-_softplus(-w_pre) - 0.5)
        a_sig = _sigmoid(a0_ref[...] + _mm3(_lhs3(xms, precise=True), aup_r))
        gate = _mm3(_lhs3(_sigmoid(xms), precise=True), gup_r)
        kx = xm[:, GW:2 * GW]
        per_b.append(dict(rr=xm[:, 0:GW], vv=xm[:, 2 * GW:3 * GW], logw=logw, a_sig=a_sig, gate=gate,
                          kkp=kx * kk_ref[...], k2=kx * (1.0 + (a_sig - 1.0) * ka_ref[...])))

    w = _Wide()
    ones_bd = ones_ref[...]
    tril = _tril_ones(n, BF16)
    sl = lambda ci: slice(ci * n, (ci + 1) * n)
    get = lambda name: [per_b[b][name][sl(ci), :] for b, ci in items]
    rrs, vvs, logws, a_sigs, gates, kkps, k2s = (get(k) for k in ('rr', 'vv', 'logw', 'a_sig', 'gate', 'kkp', 'k2'))
    kks = [kkp * lax.rsqrt(_mm_x01(kkp * kkp, ones_bd) + 1e-6) for kkp in kkps]
    cums = [_tri_mm(tril, logw) for logw in logws]
    w_incls = [jnp.exp(cum) for cum in cums]
    w_lasts = [wi[n - 1:n, :] for wi in w_incls]
    w_invs = [jnp.exp(-cum) for cum in cums]
    ats = [-kk * jnp.exp(cum - logw) for kk, cum, logw in zip(kks, cums, logws)]
    bts = [kk * a_sig * wv for kk, a_sig, wv in zip(kks, a_sigs, w_invs)]
    kts = [k2 * wv for k2, wv in zip(k2s, w_invs)]
    rts = [rr * wi for rr, wi in zip(rrs, w_incls)]
    ar_ls = [_lhs3(jnp.concatenate([at, rt], axis=0)) for at, rt in zip(ats, rts)]
    gbs = [_mm3(ar_l, _rhs3(bt, w), NT) for ar_l, bt in zip(ar_ls, bts)]
    gks = [_mm3(ar_l, _rhs3(kt, w), NT) for ar_l, kt in zip(ar_ls, kts)]
    t_ls = [_lhs3(t) for t in _unit_lower_inverse_wide([jnp.where(w.strict, -gb[:n], 0.0) for gb in gbs], w)]
    arb_ls = [_lhs3(jnp.where(w.incl, gb[n:], 0.0)) for gb in gbs]
    ark_ls = [_lhs3(jnp.where(w.incl, gk[n:], 0.0)) for gk in gks]
    v_rs = [_rhs3(vv, w) for vv in vvs]
    aakvs = [_mm3(_lhs3(jnp.where(w.strict, gk[:n], 0.0)), v_r) for gk, v_r in zip(gks, v_rs)]
    arkvs = [_mm3(ark_l, v_r) for ark_l, v_r in zip(ark_ls, v_rs)]
    bks = [jnp.concatenate([bt * wl, kt * wl], axis=0) for bt, kt, wl in zip(bts, kts, w_lasts)]

    states = [s_scr[b] for b in range(nb)]
    ys = {}
    for ci in range(nc):
        for b in range(nb):
            it = b * nc + ci
            s0 = states[b]
            ars = _mm3(ar_ls[it], _rhs3(s0, w), NT)
            u = _mm3(t_ls[it], _rhs3(ars[:n] + aakvs[it], w))
            ys[it] = ars[n:] + _mm3(arb_ls[it], _rhs3(u, w)) + arkvs[it]
            upd = _mm3_tn(jnp.concatenate([u, vvs[it]], axis=0), bks[it])
            states[b] = s0 * w_lasts[it] + _collapse_heads(upd, w)
    for b in range(nb):
        s_scr[b] = states[b]

    inv_d = 1.0 / HEAD_DIM
    for it, (b, ci) in enumerate(items):
        y = ys[it]
        yc = y - _mm_x01(y, ones_bd) * inv_d
        var_y = _mm_x01(yc * yc, ones_bd) * inv_d
        yn = yc * lax.rsqrt(var_y + GN_EPS) * lng_ref[...] + lnb_ref[...]
        bonus = _mm_x01(rrs[it] * k2s[it] * rk_ref[...], ones_bd) * vvs[it]
        o_ref[b, sl(ci), :] = (yn + bonus) * gates[it]

    @pl.when(c == pl.num_programs(1) - 1)
    def _():
        sout_ref[...] = s_scr[...]


def _rwkv_mixer(h3, hs3, shift, s0, lp):
    bsz, l, _ = h3.shape
    nb, nc = _recurrent_tiling(bsz, l)
    rows = nc * CHUNK
    sh_main = shift[:, None, 0:3 * GW]
    sh_small = jnp.pad(shift[:, None, 3 * GW:], ((0, 0), (0, 0), (0, H_SMALL - LORA)))
    mu = lp['rwkv_mu']
    mu_main = mu[None, 0:3 * GW]
    mu_small = jnp.pad(mu[None, 3 * GW:], ((0, 0), (0, H_SMALL - LORA)))
    place = lambda w, off: jnp.zeros((H_SMALL, GW), F32).at[off:off + w.shape[0]].set(w)
    wup = place(lp['rwkv_w_up'], 0)
    aup = place(lp['rwkv_a_up'], RWKV_W_LORA)
    gup = place(lp['rwkv_g_up'], RWKV_W_LORA + RWKV_A_LORA)
    row = lambda v: v.reshape(1, GW)
    full = lambda shape: pl.BlockSpec(shape, lambda bi, i: (0,) * len(shape))
    o, s_new = pl.pallas_call(
        functools.partial(_rwkv_kernel, nb=nb, nc=nc),
        grid=(bsz // nb, l // rows),
        in_specs=[pl.BlockSpec((nb, rows, 3 * GW), lambda bi, i: (bi, i, H_CRKV // (3 * GW))),
                  pl.BlockSpec((nb, rows, H_SMALL), lambda bi, i: (bi, i, 0)),
                  pl.BlockSpec((nb, 1, 3 * GW), lambda bi, i: (bi, 0, 0)),
                  pl.BlockSpec((nb, 1, H_SMALL), lambda bi, i: (bi, 0, 0)),
                  pl.BlockSpec((nb, HEAD_DIM, GW), lambda bi, i: (bi, 0, 0)),
                  full((1, 3 * GW)), full((1, H_SMALL)), full((1, GW)), full((H_SMALL, GW)), full((1, GW)),
                  full((H_SMALL, GW)), full((H_SMALL, GW)), full((1, GW)), full((1, GW)), full((1, GW)),
                  full((1, GW)), full((1, GW)), full((GW, GW))],
        out_specs=[pl.BlockSpec((nb, rows, GW), lambda bi, i: (bi, i, 0)),
                   pl.BlockSpec((nb, HEAD_DIM, GW), lambda bi, i: (bi, 0, 0))],
        out_shape=[jax.ShapeDtypeStruct((bsz, l, GW), F32),
                   jax.ShapeDtypeStruct((bsz, HEAD_DIM, GW), F32)],
        scratch_shapes=[pltpu.VMEM((nb, rows + 8, 3 * GW), F32), pltpu.VMEM((nb, rows + 8, H_SMALL), F32),
                        pltpu.VMEM((nb, HEAD_DIM, GW), F32)],
        compiler_params=_params(2),
        name="rwkv_mixer",
    )(h3, hs3, sh_main, sh_small, _heads_to_wide(s0), mu_main, mu_small, row(lp['rwkv_w0']), wup,
      row(lp['rwkv_a0']), aup, gup, row(lp['rwkv_k_k']), row(lp['rwkv_k_a']), row(lp['rwkv_r_k']),
      row(lp['rwkv_lnx_g']), row(lp['rwkv_lnx_b']), _ones_block_diag())
    return o, _wide_to_heads(s_new)


_FB = 128
_SAMPLE_BK = 512


def _log_forget(hs, bf):
    return -_softplus(-(hs + bf))


def _head_slice(h):
    return slice(h * HEAD_DIM, (h + 1) * HEAD_DIM)


def _attend_heads(qs, kblk, vblk, key_bias, mask, carry):
    m, l, acc = carry
    bq = qs[0].shape[0]
    kb16 = kblk.astype(BF16)
    vb16 = vblk.astype(BF16)
    parts = []
    for h in range(GROUP_HEADS):
        s = _dot(qs[h], kb16[:, _head_slice(h)], NT) + key_bias[h:h + 1, :]
        parts.append(s if mask is None else jnp.where(mask, s, NEG))
    s = jnp.concatenate(parts, axis=0)
    m_new = jnp.maximum(m, jnp.max(s, -1, keepdims=True))
    alpha = jnp.exp(m - m_new)
    p = jnp.exp(s - m_new)
    l = alpha * l + jnp.sum(p, -1, keepdims=True)
    pb = p.astype(BF16)
    pv = jnp.concatenate([_dot(pb[h * bq:(h + 1) * bq, :], vb16[:, _head_slice(h)])
                          for h in range(GROUP_HEADS)], axis=0)
    return m_new, l, alpha * acc + pv


def _attend_init(bq):
    rows = GROUP_HEADS * bq
    return jnp.full((rows, 1), NEG, F32), jnp.zeros((rows, 1), F32), jnp.zeros((rows, HEAD_DIM), F32)


def _attend_store(o_ref, carry, bq):
    _, l, acc = carry
    for h in range(GROUP_HEADS):
        o_ref[:, _head_slice(h)] = acc[h * bq:(h + 1) * bq, :] / l[h * bq:(h + 1) * bq, :]


def _sel_rows(sel01, x):
    t = x.shape[0]
    r = _dot(sel01, jnp.concatenate(_split3(x), axis=0), NT)
    return r[:, :t] + r[:, t:2 * t] + r[:, 2 * t:]


def _fox_prompt_kernel(q_ref, k_ref, v_ref, hs_ref, bf_ref, o_ref, lf_ref, nf_scr, *, l, bq):
    i = pl.program_id(1)
    per = bq // _FB

    @pl.when(i == 0)
    def _():
        utri = _triu_ones(_FB, BF16)
        sel = _col_selector(8, H_SMALL, S_F).astype(BF16)
        carry = jnp.zeros((8, 1), F32)
        for jb in range(l // _FB):
            rows = slice(jb * _FB, (jb + 1) * _FB)
            lf = _log_forget(hs_ref[rows, :], bf_ref[...])
            lf_ref[rows, :] = lf[:, S_F:S_F + GROUP_HEADS]
            loc = _mm_x01(_sel_rows(sel, lf), utri, 3)
            nf_scr[jb // per, :, (jb % per) * _FB:(jb % per + 1) * _FB] = -(loc + carry)
            carry = carry + loc[:, _FB - 1:_FB]

    q = q_ref[...]
    r_i, c_i = _iota2((bq, bq), 0), _iota2((bq, bq), 1)
    qs = [(q[:, _head_slice(h)] * (HEAD_DIM ** -0.5)).astype(BF16) for h in range(GROUP_HEADS)]

    def body(j, carry):
        rows = pl.ds(pl.multiple_of(j * bq, bq), bq)
        return _attend_heads(qs, k_ref[rows, :], v_ref[rows, :], nf_scr[j], None, carry)

    carry = lax.fori_loop(0, i, body, _attend_init(bq))
    rows = pl.ds(pl.multiple_of(i * bq, bq), bq)
    carry = _attend_heads(qs, k_ref[rows, :], v_ref[rows, :], nf_scr[i], c_i <= r_i, carry)
    _attend_store(o_ref, carry, bq)


def _fox_prompt(h3, hs3, bf_pad):
    bsz, l, _ = h3.shape
    bq = min(l, 2 * _FB)
    kern = functools.partial(_fox_prompt_kernel, l=l, bq=bq)
    return pl.pallas_call(
        kern,
        grid=(bsz, l // bq),
        in_specs=[pl.BlockSpec((None, bq, GW), lambda bi, i: (bi, i, H_DQKV // GW)),
                  pl.BlockSpec((None, l, GW), lambda bi, i: (bi, 0, H_DQKV // GW + 1)),
                  pl.BlockSpec((None, l, GW), lambda bi, i: (bi, 0, H_DQKV // GW + 2)),
                  pl.BlockSpec((None, l, H_SMALL), lambda bi, i: (bi, 0, 0)),
                  pl.BlockSpec((1, H_SMALL), lambda bi, i: (0, 0))],
        out_specs=[pl.BlockSpec((None, bq, GW), lambda bi, i: (bi, i, 0)),
                   pl.BlockSpec((None, l, GROUP_HEADS), lambda bi, i: (bi, 0, 0))],
        out_shape=[jax.ShapeDtypeStruct((bsz, l, GW), F32),
                   jax.ShapeDtypeStruct((bsz, l, GROUP_HEADS), F32)],
        scratch_shapes=[pltpu.VMEM((l // bq, 8, bq), F32)],
        compiler_params=_params(2),
        name="fox_prompt",
    )(h3, h3, h3, hs3, bf_pad)


def _fox_sample_kernel(q_ref, k_ref, v_ref, hs_ref, bf_ref, ck_ref, cv_ref, clf_ref, o_ref, lf_ref, cum_scr,
                       *, l, p):
    nblk = p // _FB
    per = _SAMPLE_BK // _FB

    loc = _mm_x01(clf_ref[...].reshape(nblk * 8, _FB), _triu_ones(_FB, BF16), 3)
    carry = jnp.zeros((8, 1), F32)
    for jb in range(nblk):
        blk = loc[jb * 8:(jb + 1) * 8, :]
        cum_scr[jb // per, :, (jb % per) * _FB:(jb % per + 1) * _FB] = blk + carry
        carry = carry + blk[:, _FB - 1:_FB]
    total = carry

    lf = _log_forget(hs_ref[...], bf_ref[...])
    lf_ref[...] = lf[:, S_F:S_F + GROUP_HEADS]
    cum_r = _mm_x01(_sel_rows(_col_selector(8, H_SMALL, S_F).astype(BF16), lf), _triu_ones(l, BF16), 3)
    q = q_ref[...]
    r_i, c_i = _iota2((l, l), 0), _iota2((l, l), 1)
    qs = [(q[:, _head_slice(h)] * (HEAD_DIM ** -0.5)).astype(BF16) for h in range(GROUP_HEADS)]

    def body(j, carry):
        rows = pl.ds(pl.multiple_of(j * _SAMPLE_BK, _SAMPLE_BK), _SAMPLE_BK)
        suffix = total - cum_scr[j]
        return _attend_heads(qs, ck_ref[rows, :], cv_ref[rows, :], suffix, None, carry)

    carry = lax.fori_loop(0, p // _SAMPLE_BK, body, _attend_init(l))
    carry = _attend_heads(qs, k_ref[...], v_ref[...], -cum_r, c_i <= r_i, carry)
    _attend_store(o_ref, carry, l)


def _fox_sample(h3, hs3, bf_pad, ck, cv, clogf):
    bsz, l, _ = h3.shape
    p = ck.shape[1]
    ck2 = ck.reshape(bsz, p, GW)
    cv2 = cv.reshape(bsz, p, GW)
    clf = clogf.reshape(bsz, p // _FB, _FB, GROUP_HEADS).transpose(0, 1, 3, 2)
    clf = jnp.pad(clf, ((0, 0), (0, 0), (0, 8 - GROUP_HEADS), (0, 0)))
    kern = functools.partial(_fox_sample_kernel, l=l, p=p)
    return pl.pallas_call(
        kern,
        grid=(bsz,),
        in_specs=[pl.BlockSpec((None, l, GW), lambda bi: (bi, 0, H_DQKV // GW)),
                  pl.BlockSpec((None, l, GW), lambda bi: (bi, 0, H_DQKV // GW + 1)),
                  pl.BlockSpec((None, l, GW), lambda bi: (bi, 0, H_DQKV // GW + 2)),
                  pl.BlockSpec((None, l, H_SMALL), lambda bi: (bi, 0, 0)),
                  pl.BlockSpec((1, H_SMALL), lambda bi: (0, 0)),
                  pl.BlockSpec((None, p, GW), lambda bi: (bi, 0, 0)),
                  pl.BlockSpec((None, p, GW), lambda bi: (bi, 0, 0)),
                  pl.BlockSpec((None, p // _FB, 8, _FB), lambda bi: (bi, 0, 0, 0))],
        out_specs=[pl.BlockSpec((None, l, GW), lambda bi: (bi, 0, 0)),
                   pl.BlockSpec((None, l, GROUP_HEADS), lambda bi: (bi, 0, 0))],
        out_shape=[jax.ShapeDtypeStruct((bsz, l, GW), F32),
                   jax.ShapeDtypeStruct((bsz, l, GROUP_HEADS), F32)],
        scratch_shapes=[pltpu.VMEM((p // _SAMPLE_BK, 8, _SAMPLE_BK), F32)],
        compiler_params=_params(1),
        name="fox_sample",
    )(h3, h3, h3, hs3, bf_pad, ck2, cv2, clf)


def _outproj_kernel(oa_ref, ob_ref, oc_ref, od_ref, x_ref, w_ref, g_ref, b_ref, y_ref):
    acc = DN_ALPHA * x_ref[...]
    for gi, ref in enumerate((oa_ref, ob_ref, oc_ref, od_ref)):
        acc = acc + _mm(ref[...], w_ref[gi * GW:(gi + 1) * GW, :])
    y_ref[...] = _layer_norm(acc, g_ref[...], b_ref[...])


def _outproj(oa, ob, oc, od, x2d, w_out, g, b):
    t = x2d.shape[0]
    tm = min(t, 512)
    mix = pl.BlockSpec((tm, GW), lambda i: (i, 0))
    return pl.pallas_call(
        _outproj_kernel,
        grid=(t // tm,),
        in_specs=[mix, mix, mix, mix,
                  pl.BlockSpec((tm, D_MODEL), lambda i: (i, 0)),
                  pl.BlockSpec((D_MODEL, D_MODEL), lambda i: (0, 0)),
                  pl.BlockSpec((1, D_MODEL), lambda i: (0, 0)),
                  pl.BlockSpec((1, D_MODEL), lambda i: (0, 0))],
        out_specs=pl.BlockSpec((tm, D_MODEL), lambda i: (i, 0)),
        out_shape=jax.ShapeDtypeStruct((t, D_MODEL), F32),
        compiler_params=_params(1),
        name="outproj_ln",
    )(oa, ob, oc, od, x2d, w_out, g, b)


_R_G, _R_E = 0, N_EXPERT_GROUPS


def _moe_kernel(x_ref, wr_ref, br_ref, wg_ref, wu_ref, wd_ref, g_ref, b_ref, y_ref, comb_scr, xb_scr, acc_scr):
    e = pl.program_id(1)
    tm = x_ref.shape[0]
    lane = _iota2((tm, LANE), 1)

    @pl.when(e == 0)
    def _():
        x = x_ref[...]
        xb_scr[...] = x.astype(BF16)
        acc_scr[...] = jnp.zeros_like(acc_scr)
        logits = _mmh(x, wr_ref[...]) + br_ref[...]
        gmask = lane < N_EXPERT_GROUPS
        gl = jnp.where(gmask, logits, NEG)
        ge = jnp.exp(gl - jnp.max(gl, -1, keepdims=True))
        pg = ge / jnp.sum(ge, -1, keepdims=True)
        gp = jnp.max(pg, -1, keepdims=True)
        gi = jnp.min(jnp.where(jnp.logical_and(gmask, pg == gp), lane, LANE), -1, keepdims=True)
        emask = jnp.logical_and(jnp.logical_and(lane >= _R_E, lane < _R_E + N_EXPERTS),
                                ((lane - _R_E) >> 2) == gi)
        el = jnp.where(emask, logits, NEG)
        ee = jnp.exp(el - jnp.max(el, -1, keepdims=True))
        ep = ee / jnp.sum(ee, -1, keepdims=True)
        m1 = jnp.max(jnp.where(emask, ep, -1.0), -1, keepdims=True)
        i1 = jnp.min(jnp.where(jnp.logical_and(emask, ep == m1), lane, LANE), -1, keepdims=True)
        rest = jnp.logical_and(emask, lane != i1)
        m2 = jnp.max(jnp.where(rest, ep, -1.0), -1, keepdims=True)
        i2 = jnp.min(jnp.where(jnp.logical_and(rest, ep == m2), lane, LANE), -1, keepdims=True)
        den = m1 + m2
        comb_scr[...] = jnp.where(lane == i1, gp * m1 / den, jnp.where(lane == i2, gp * m2 / den, 0.0))

    ce = jnp.sum(jnp.where(lane == e + _R_E, comb_scr[...], 0.0), -1, keepdims=True)
    xb = xb_scr[...]
    hh = _silu(_mm(xb, wg_ref[...])) * _mm(xb, wu_ref[...])
    acc_scr[...] += ce * _mm(hh, wd_ref[...])

    @pl.when(e == pl.num_programs(1) - 1)
    def _():
        y_ref[...] = _layer_norm(DN_ALPHA * x_ref[...] + acc_scr[...], g_ref[...], b_ref[...])


def _moe(x2d, w_router, b_router, wg, wu, wd, g, b):
    t = x2d.shape[0]
    tm = min(t, 1024)
    return pl.pallas_call(
        _moe_kernel,
        grid=(t // tm, N_EXPERTS),
        in_specs=[pl.BlockSpec((tm, D_MODEL), lambda i, e: (i, 0)),
                  pl.BlockSpec((D_MODEL, LANE), lambda i, e: (0, 0)),
                  pl.BlockSpec((1, LANE), lambda i, e: (0, 0)),
                  pl.BlockSpec((None, D_MODEL, D_EXPERT), lambda i, e: (e, 0, 0)),
                  pl.BlockSpec((None, D_MODEL, D_EXPERT), lambda i, e: (e, 0, 0)),
                  pl.BlockSpec((None, D_EXPERT, D_MODEL), lambda i, e: (e, 0, 0)),
                  pl.BlockSpec((1, D_MODEL), lambda i, e: (0, 0)),
                  pl.BlockSpec((1, D_MODEL), lambda i, e: (0, 0))],
        out_specs=pl.BlockSpec((tm, D_MODEL), lambda i, e: (i, 0)),
        out_shape=jax.ShapeDtypeStruct((t, D_MODEL), F32),
        scratch_shapes=[pltpu.VMEM((tm, LANE), F32), pltpu.VMEM((tm, D_MODEL), BF16),
                        pltpu.VMEM((tm, D_MODEL), F32)],
        compiler_params=_params(2),
        name="moe_ln",
    )(x2d, w_router, b_router, wg, wu, wd, g, b)


def _prep_layer(lp):
    w = lp['w_in']
    c0 = CONV_COLS
    c1 = c0 + GDN_COLS
    c2 = c1 + RWKV_COLS
    main = jnp.concatenate([w[:, c0:c0 + 3 * GW], w[:, c1:c1 + 3 * GW], w[:, c2:c2 + 3 * GW],
                            w[:, 0:GW], w[:, GW:2 * GW], w[:, c0 + 3 * GW:c0 + 4 * GW]], axis=1).astype(BF16)
    small = jnp.concatenate([w[:, c1 + 3 * GW:c2], w[:, c0 + 4 * GW:c1], w[:, c2 + 3 * GW:],
                             jnp.zeros((D_MODEL, H_SMALL - LORA - 3 * GROUP_HEADS), F32)], axis=1)
    wr = jnp.concatenate([lp['router_g_w'], lp['router_e_w'],
                          jnp.zeros((D_MODEL, LANE - N_EXPERT_GROUPS - N_EXPERTS), F32)], axis=1)
    br = jnp.concatenate([lp['router_g_b'], lp['router_e_b'],
                          jnp.zeros((LANE - N_EXPERT_GROUPS - N_EXPERTS,), F32)])[None, :]
    bf_pad = jnp.zeros((1, H_SMALL), F32).at[0, S_F:S_F + GROUP_HEADS].set(lp['fox_b_f'])
    return dict(w_main=main, w_small=small, w_out=lp['w_out'].astype(BF16), wr=wr, br=br, bf_pad=bf_pad)


def _trunk_layer(x, lp, pp, conv_buf, gdn_buf, gdn_s, rw_shift, rw_s, fox_cache):
    bsz, l, d = x.shape
    x2d = x.reshape(bsz * l, d)
    h2, hs2 = _inproj(x2d, pp['w_main'], pp['w_small'])
    h3 = h2.reshape(bsz, l, H_MAIN)
    hs3 = hs2.reshape(bsz, l, H_SMALL)

    o_a, new_conv = _conv_mixer(h3, conv_buf, lp['conv_w'], lp['conv_b'][None], lp['conv_ln_g'][None],
                                lp['conv_ln_b'][None])
    o_b, new_gdn_s = _gdn_mixer(h3, hs3, gdn_buf, gdn_s, lp['gdn_conv_w'], lp['gdn_a_log'], lp['gdn_dt_bias'],
                                lp['gdn_norm_w'])
    o_c, new_rw_s = _rwkv_mixer(h3, hs3, rw_shift, rw_s, lp)
    if fox_cache is None:
        o_d, logf = _fox_prompt(h3, hs3, pp['bf_pad'])
    else:
        o_d, logf = _fox_sample(h3, hs3, pp['bf_pad'], *fox_cache)

    flat = lambda o: o.reshape(bsz * l, GW)
    x1 = _outproj(flat(o_a), flat(o_b), flat(o_c), flat(o_d), x2d, pp['w_out'], lp['ln1_g'][None],
                  lp['ln1_b'][None])
    x2 = _moe(x1, pp['wr'], pp['br'], lp['exp_w_gate'], lp['exp_w_up'], lp['exp_w_down'], lp['ln2_g'][None],
              lp['ln2_b'][None])

    kw = GDN_CONV_WIDTH - 1
    new_gdn_buf = h3[:, l - kw:, H_BQKV:H_BQKV + 3 * GW]
    new_shift = jnp.concatenate([h3[:, l - 1, H_CRKV:H_CRKV + 3 * GW], hs3[:, l - 1, S_LORA:S_LORA + LORA]], axis=-1)
    k_d = h3[:, :, H_DQKV + GW:H_DQKV + 2 * GW].reshape(bsz, l, GROUP_HEADS, HEAD_DIM)
    v_d = h3[:, :, H_DQKV + 2 * GW:H_DQKV + 3 * GW].reshape(bsz, l, GROUP_HEADS, HEAD_DIM)
    return x2.reshape(bsz, l, d), (new_conv, new_gdn_buf, new_gdn_s, new_shift, new_rw_s, k_d, v_d, logf)


_LAYER_KEYS = ('w_in', 'conv_w', 'conv_b', 'conv_ln_g', 'conv_ln_b', 'gdn_conv_w', 'gdn_a_log', 'gdn_dt_bias',
               'gdn_norm_w', 'rwkv_mu', 'rwkv_w0', 'rwkv_w_up', 'rwkv_a0', 'rwkv_a_up', 'rwkv_g_up', 'rwkv_k_k',
               'rwkv_k_a', 'rwkv_r_k', 'rwkv_lnx_g', 'rwkv_lnx_b', 'fox_b_f', 'w_out', 'ln1_g', 'ln1_b',
               'router_g_w', 'router_g_b', 'router_e_w', 'router_e_b', 'exp_w_gate', 'exp_w_up', 'exp_w_down',
               'ln2_g', 'ln2_b')


def kernel(x_prompt, x_sample, cache_fox_k, cache_fox_v, cache_fox_logf, state_conv, state_gdn_conv, state_gdn, state_rwkv_shift, state_rwkv, w_in, conv_w, conv_b, conv_ln_g, conv_ln_b, gdn_conv_w, gdn_a_log, gdn_dt_bias, gdn_norm_w, rwkv_mu, rwkv_w0, rwkv_w_up, rwkv_a0, rwkv_a_up, rwkv_g_up, rwkv_k_k, rwkv_k_a, rwkv_r_k, rwkv_lnx_g, rwkv_lnx_b, fox_b_f, w_out, ln1_g, ln1_b, router_g_w, router_g_b, router_e_w, router_e_b, exp_w_gate, exp_w_up, exp_w_down, ln2_g, ln2_b):
    weights = dict(zip(_LAYER_KEYS, (w_in, conv_w, conv_b, conv_ln_g, conv_ln_b, gdn_conv_w, gdn_a_log,
                                     gdn_dt_bias, gdn_norm_w, rwkv_mu, rwkv_w0, rwkv_w_up, rwkv_a0, rwkv_a_up,
                                     rwkv_g_up, rwkv_k_k, rwkv_k_a, rwkv_r_k, rwkv_lnx_g, rwkv_lnx_b, fox_b_f,
                                     w_out, ln1_g, ln1_b, router_g_w, router_g_b, router_e_w, router_e_b,
                                     exp_w_gate, exp_w_up, exp_w_down, ln2_g, ln2_b)))
    xp, xs = x_prompt, x_sample
    bp = x_prompt.shape[0]
    outs_p = [[] for _ in range(8)]
    outs_s = [[] for _ in range(8)]
    for l in range(DEPTH):
        lp = {k: v[l] for k, v in weights.items()}
        pp = _prep_layer(lp)
        xp, st_p = _trunk_layer(xp, lp, pp,
                                jnp.zeros((bp, CONV_WIDTH - 1, GW), F32),
                                jnp.zeros((bp, GDN_CONV_WIDTH - 1, 3 * GW), F32),
                                jnp.zeros((bp, GROUP_HEADS, HEAD_DIM, HEAD_DIM), F32),
                                jnp.zeros((bp, RWKV_COLS), F32),
                                jnp.zeros((bp, GROUP_HEADS, HEAD_DIM, HEAD_DIM), F32),
                                None)
        xs, st_s = _trunk_layer(xs, lp, pp, state_conv[l], state_gdn_conv[l], state_gdn[l], state_rwkv_shift[l],
                                state_rwkv[l], (cache_fox_k[l], cache_fox_v[l], cache_fox_logf[l]))
        for i in range(8):
            outs_p[i].append(st_p[i])
            outs_s[i].append(st_s[i])
    stacked_p = [jnp.stack(o) for o in outs_p]
    stacked_s = [jnp.stack(o) for o in outs_s]
    return (xp, xs, *stacked_p, *stacked_s)
```

```python
import functools

import jax
import jax.numpy as jnp
from jax import lax
from jax.experimental import pallas as pl
from jax.experimental.pallas import tpu as pltpu

F32 = jnp.float32
BF16 = jnp.bfloat16

D_MODEL = 1024
DEPTH = 2
CHUNK = 64
HEAD_DIM = 64
GW = D_MODEL // 4
GROUP_HEADS = GW // HEAD_DIM
CONV_WIDTH = 31
GDN_CONV_WIDTH = 4
RWKV_W_LORA = 16
RWKV_A_LORA = 16
RWKV_G_LORA = 32
N_EXPERT_GROUPS = 4
EXPERTS_PER_GROUP = 4
N_EXPERTS = N_EXPERT_GROUPS * EXPERTS_PER_GROUP
D_EXPERT = 256
DN_ALPHA = (2 * DEPTH) ** 0.25
LN_EPS = 1e-5
GN_EPS = 64e-5

CONV_COLS = 2 * GW
GDN_COLS = 4 * GW + 2 * GROUP_HEADS
RWKV_COLS = 3 * GW + RWKV_W_LORA + RWKV_A_LORA + RWKV_G_LORA
FOX_COLS = 3 * GW + GROUP_HEADS
LORA = RWKV_W_LORA + RWKV_A_LORA + RWKV_G_LORA

H_BQKV, H_CRKV, H_DQ, H_AVAL, H_AGATE, H_BZ = 0, 768, 1536, 1792, 2048, 2304
H_MAIN = 2560
KV_ROWS = 2 * GW
S_LORA, S_A, S_B, S_F = 0, 64, 68, 72
H_SMALL = 128
LANE = 128

VMEM_LIMIT = 56 * 1024 * 1024

NN = (((1,), (0,)), ((), ()))
NT = (((1,), (1,)), ((), ()))
TN = (((0,), (0,)), ((), ()))
NEG = -1e30


def _dot(a, b, dims=NN):
    return lax.dot_general(a, b, dims, preferred_element_type=F32)


def _mm(a, b, dims=NN):
    return _dot(a.astype(BF16), b.astype(BF16), dims)


def _mmh(a, b, dims=NN):
    return lax.dot_general(a, b, dims, preferred_element_type=F32, precision=lax.Precision.HIGHEST)


def _sigmoid(x):
    return 0.5 * (jnp.tanh(0.5 * x) + 1.0)


def _silu(x):
    return x * _sigmoid(x)


def _softplus(x):
    return jnp.maximum(x, 0.0) + jnp.log(1.0 + jnp.exp(-jnp.abs(x)))


def _iota2(shape, dim):
    return lax.broadcasted_iota(jnp.int32, shape, dim)


def _tril_ones(n, dtype=F32):
    r, c = _iota2((n, n), 0), _iota2((n, n), 1)
    return (r >= c).astype(dtype)


def _triu_ones(n, dtype=F32):
    r, c = _iota2((n, n), 0), _iota2((n, n), 1)
    return (r <= c).astype(dtype)


def _col_selector(rows, width, col0):
    r, c = _iota2((rows, width), 0), _iota2((rows, width), 1)
    return (c == r + col0).astype(F32)


def _layer_norm(x, g, b):
    mu = jnp.mean(x, axis=-1, keepdims=True)
    xc = x - mu
    var = jnp.mean(xc * xc, axis=-1, keepdims=True)
    return xc * lax.rsqrt(var + LN_EPS) * g + b


def _split2(x):
    hi = x.astype(BF16)
    lo = (x - hi.astype(F32)).astype(BF16)
    return hi, lo


def _split3(x):
    hi = x.astype(BF16)
    r = x - hi.astype(F32)
    mid = r.astype(BF16)
    lo = (r - mid.astype(F32)).astype(BF16)
    return hi, mid, lo


class _Wide:
    def __init__(self):
        self.i = _iota2((CHUNK, GW), 0)
        lane = _iota2((CHUNK, GW), 1)
        self.j = lane & (HEAD_DIM - 1)
        self.head = lane >> 6
        self.incl = self.i >= self.j
        self.strict = self.i > self.j
        self.eye = self.i == self.j
        r, c = _iota2((GW, GW), 0), _iota2((GW, GW), 1)
        self.block_diag = (r >> 6) == (c >> 6)


def _head_block_diag(x, w):
    t = jnp.concatenate([x] * GROUP_HEADS, axis=0)
    return jnp.where(w.block_diag, t, jnp.zeros_like(t))


def _collapse_heads(full, w):
    out = jnp.zeros((HEAD_DIM, GW), F32)
    for h in range(GROUP_HEADS):
        out = out + jnp.where(w.head == h, full[h * HEAD_DIM:(h + 1) * HEAD_DIM, :], 0.0)
    return out


def _lhs3(a, precise=False):
    if not precise:
        return None, a.astype(BF16)
    hi, lo = _split2(a)
    return jnp.concatenate([hi, lo], axis=0), hi


def _rhs3(b, w=None, precise=False):
    if not precise:
        hi = b.astype(BF16)
        return (hi if w is None else _head_block_diag(hi, w)), None
    hi, lo = _split2(b)
    if w is None:
        return hi, lo
    return _head_block_diag(hi, w), _head_block_diag(lo, w)


def _mm3(lhs, rhs, dims=NN):
    cat, hi = lhs
    bh, bl = rhs
    if cat is None or bl is None:
        return _dot(hi, bh, dims)
    m = hi.shape[0]
    r = _dot(cat, bh, dims)
    return r[:m] + r[m:] + _dot(hi, bl, dims)


def _mm3_tn(a, b, precise=False):
    if not precise:
        return _dot(a.astype(BF16), b.astype(BF16), TN)
    ah, al = _split2(a)
    bh, bl = _split2(b)
    m = a.shape[1]
    r = _dot(jnp.concatenate([ah, al], axis=1), bh, TN)
    return r[:m] + r[m:] + _dot(ah, bl, TN)


def _mm_x01(a, m01, pieces=2):
    m = a.shape[0]
    r = _dot(jnp.concatenate(_split3(a) if pieces == 3 else _split2(a), axis=0), m01)
    return sum(r[i * m:(i + 1) * m] for i in range(pieces))


def _tri_mm(tri01, x, pieces=2):
    n = x.shape[1]
    r = _dot(tri01, jnp.concatenate(_split3(x) if pieces == 3 else _split2(x), axis=1))
    return sum(r[:, i * n:(i + 1) * n] for i in range(pieces))


def _unit_lower_inverse_wide(mats, w):
    eye = w.eye.astype(F32)
    same8 = (w.i >> 3) == (w.j >> 3)
    ds = [jnp.where(same8, a, 0.0) for a in mats]
    d2 = [_mm3(_lhs3(d), _rhs3(d, w)) for d in ds]
    d2_r = [_rhs3(x, w) for x in d2]
    d4 = [_mm3(_lhs3(x), r) for x, r in zip(d2, d2_r)]
    imd = [eye - d for d in ds]
    p1 = [i + _mm3(_lhs3(i), r) for i, r in zip(imd, d2_r)]
    ts = [p + _mm3(_lhs3(p), _rhs3(x, w)) for p, x in zip(p1, d4)]
    for s in range(3, 6):
        big = (w.i >> (s + 1)) == (w.j >> (s + 1))
        small = (w.i >> s) == (w.j >> s)
        sel = jnp.logical_and(big, jnp.logical_not(small))
        inner = [_mm3(_lhs3(jnp.where(sel, a, 0.0)), _rhs3(t, w)) for a, t in zip(mats, ts)]
        ts = [t - _mm3(_lhs3(t), _rhs3(x, w)) for t, x in zip(ts, inner)]
    return ts


def _params(n_axes):
    return pltpu.CompilerParams(dimension_semantics=("arbitrary",) * n_axes,
                                vmem_limit_bytes=VMEM_LIMIT)


def _heads_to_wide(s):
    b = s.shape[0]
    return s.transpose(0, 2, 1, 3).reshape(b, HEAD_DIM, GW)


def _wide_to_heads(s):
    b = s.shape[0]
    return s.reshape(b, HEAD_DIM, GROUP_HEADS, HEAD_DIM).transpose(0, 2, 1, 3)


def _ones_block_diag():
    r, c = jnp.arange(GW)[:, None], jnp.arange(GW)[None, :]
    return ((r // HEAD_DIM) == (c // HEAD_DIM)).astype(BF16)


def _lane_spread(col0):
    r, c = jnp.arange(H_SMALL)[:, None], jnp.arange(GW)[None, :]
    return ((r - col0) == (c // HEAD_DIM)).astype(BF16)


def _inproj_kernel(x_ref, wm_ref, ws_ref, wkv_ref, h_ref, hs_ref, kv_ref):
    x = x_ref[...]
    xb = x.astype(BF16)
    h_ref[...] = _dot(xb, wm_ref[...])
    hs_ref[...] = _mmh(x, ws_ref[...])
    kv_ref[...] = _dot(wkv_ref[...], xb, NT)


def _inproj(x3, w_main, w_small, w_kv_t):
    bsz, l, _ = x3.shape
    tm = min(l, 512)
    const = lambda shape: pl.BlockSpec(shape, lambda bi, i: (0, 0))
    return pl.pallas_call(
        _inproj_kernel,
        grid=(bsz, l // tm),
        in_specs=[pl.BlockSpec((None, tm, D_MODEL), lambda bi, i: (bi, i, 0)),
                  const((D_MODEL, H_MAIN)), const((D_MODEL, H_SMALL)), const((KV_ROWS, D_MODEL))],
        out_specs=[pl.BlockSpec((None, tm, H_MAIN), lambda bi, i: (bi, i, 0)),
                   pl.BlockSpec((None, tm, H_SMALL), lambda bi, i: (bi, i, 0)),
                   pl.BlockSpec((None, KV_ROWS, tm), lambda bi, i: (bi, 0, i))],
        out_shape=[jax.ShapeDtypeStruct((bsz, l, H_MAIN), F32), jax.ShapeDtypeStruct((bsz, l, H_SMALL), F32),
                   jax.ShapeDtypeStruct((bsz, KV_ROWS, l), F32)],
        compiler_params=_params(2),
        name="inproj",
    )(x3, w_main, w_small, w_kv_t)


_HALO = 32


def _conv_kernel(val_ref, gate_ref, buf_ref, cw_ref, cb_ref, g_ref, b_ref, o_ref, nbuf_ref, win_ref, *, tl):
    i = pl.program_id(1)
    pad = _HALO - (CONV_WIDTH - 1)

    @pl.when(i == 0)
    def _():
        win_ref[0:8, :] = jnp.zeros((8, GW), F32)
        win_ref[pad:_HALO, :] = buf_ref[...]

    u = val_ref[...] * _sigmoid(gate_ref[...])
    win_ref[_HALO:_HALO + tl, :] = u
    sub = min(tl, 64)
    for r0 in range(0, tl, sub):
        acc = jnp.zeros((sub, GW), F32)
        for j in range(CONV_WIDTH):
            acc = acc + cw_ref[j:j + 1, :] * win_ref[pad + r0 + j:pad + r0 + j + sub, :]
        y = _layer_norm(acc + cb_ref[...], g_ref[...], b_ref[...])
        o_ref[r0:r0 + sub, :] = _silu(y)
    tail = win_ref[tl:tl + _HALO, :]
    win_ref[0:_HALO, :] = tail

    @pl.when(i == pl.num_programs(1) - 1)
    def _():
        nbuf_ref[...] = win_ref[pad:_HALO, :]


def _conv_mixer(h3, buf, cw, cb, g, b):
    bsz, l, _ = h3.shape
    tl = min(l, 256)
    kern = functools.partial(_conv_kernel, tl=tl)
    full = lambda shape: pl.BlockSpec(shape, lambda bi, i: (0,) * len(shape))
    return pl.pallas_call(
        kern,
        grid=(bsz, l // tl),
        in_specs=[pl.BlockSpec((None, tl, GW), lambda bi, i: (bi, i, H_AVAL // GW)),
                  pl.BlockSpec((None, tl, GW), lambda bi, i: (bi, i, H_AGATE // GW)),
                  pl.BlockSpec((None, CONV_WIDTH - 1, GW), lambda bi, i: (bi, 0, 0)),
                  full((CONV_WIDTH, GW)), full((1, GW)), full((1, GW)), full((1, GW))],
        out_specs=[pl.BlockSpec((None, tl, GW), lambda bi, i: (bi, i, 0)),
                   pl.BlockSpec((None, CONV_WIDTH - 1, GW), lambda bi, i: (bi, 0, 0))],
        out_shape=[jax.ShapeDtypeStruct((bsz, l, GW), F32),
                   jax.ShapeDtypeStruct((bsz, CONV_WIDTH - 1, GW), F32)],
        scratch_shapes=[pltpu.VMEM((tl + _HALO, GW), F32)],
        compiler_params=_params(2),
        name="conv_mixer",
    )(h3, h3, buf, cw, cb, g, b)


def _gdn_kernel(qkv_ref, z_ref, hs_ref, buf_ref, s0_ref, cw_ref, alog_ref, dtb_ref, nw_ref, ones_ref, sela_ref,
                selb_ref, o_ref, sout_ref, win_ref, s_scr, *, nb, nc):
    c = pl.program_id(1)
    n = CHUNK
    rows = nc * n
    kw = GDN_CONV_WIDTH - 1
    items = [(b, ci) for b in range(nb) for ci in range(nc)]

    @pl.when(c == 0)
    def _():
        for b in range(nb):
            win_ref[b, 0:8, :] = jnp.zeros((8, 3 * GW), F32)
            win_ref[b, 8 - kw:8, :] = buf_ref[b]
        s_scr[...] = s0_ref[...]

    qkvs = []
    for b in range(nb):
        win_ref[b, 8:8 + rows, :] = qkv_ref[b]
        conv = jnp.zeros((rows, 3 * GW), F32)
        for j in range(GDN_CONV_WIDTH):
            conv = conv + cw_ref[j:j + 1, :] * win_ref[b, 8 - kw + j:8 - kw + j + rows, :]
        tail = win_ref[b, rows:rows + 8, :]
        win_ref[b, 0:8, :] = tail
        qkvs.append(_silu(conv))

    w = _Wide()
    ones_bd = ones_ref[...]
    tril = _tril_ones(n, BF16)
    sl = lambda ci: slice(ci * n, (ci + 1) * n)
    qs = [qkvs[b][sl(ci), 0:GW] for b, ci in items]
    ks = [qkvs[b][sl(ci), GW:2 * GW] for b, ci in items]
    vs = [qkvs[b][sl(ci), 2 * GW:3 * GW] for b, ci in items]
    sss = [_mm_x01(jnp.concatenate([q * q, k * k], axis=0), ones_bd) for q, k in zip(qs, ks)]
    qs = [q * lax.rsqrt(ss[:n] + 1e-6) * (HEAD_DIM ** -0.5) for q, ss in zip(qs, sss)]
    ks = [k * lax.rsqrt(ss[n:] + 1e-6) for k, ss in zip(ks, sss)]

    hss = [hs_ref[b, sl(ci), :] for b, ci in items]
    gs = [_mm_x01(-jnp.exp(alog_ref[...]) * _softplus(hs + dtb_ref[...]), sela_ref[...]) for hs in hss]
    betas = [_mm_x01(_sigmoid(hs), selb_ref[...]) for hs in hss]
    gcs = [_tri_mm(tril, g) for g in gs]
    grs = [jnp.sum(jnp.where(w.eye, gc, 0.0), axis=0, keepdims=True) for gc in gcs]
    decays = [jnp.exp(jnp.where(w.incl, gc - gr, NEG)) for gc, gr in zip(gcs, grs)]

    kbs = [k * beta for k, beta in zip(ks, betas)]
    grams = [_mm3(_lhs3(jnp.concatenate([kb, q], axis=0)), _rhs3(k, w), NT)
             for kb, q, k in zip(kbs, qs, ks)]
    amats = [jnp.where(w.strict, gram[:n] * decay, 0.0) for gram, decay in zip(grams, decays)]
    qks = [jnp.where(w.incl, gram[n:] * decay, 0.0) for gram, decay in zip(grams, decays)]
    t_ls = [_lhs3(t) for t in _unit_lower_inverse_wide(amats, w)]
    egs = [jnp.exp(gc) for gc in gcs]
    xvs = [_mm3(t_l, _rhs3(v * beta, w)) for t_l, v, beta in zip(t_ls, vs, betas)]
    xks = [_mm3(t_l, _rhs3(kb * eg, w)) for t_l, kb, eg in zip(t_ls, kbs, egs)]
    xq_ls = [_lhs3(jnp.concatenate([xk, q * eg], axis=0)) for xk, q, eg in zip(xks, qs, egs)]
    qk_ls = [_lhs3(qk) for qk in qks]
    gls = [gc[n - 1:n, :] for gc in gcs]
    kds = [k * jnp.exp(gl - gc) for k, gl, gc in zip(ks, gls, gcs)]

    states = [s_scr[b] for b in range(nb)]
    outs = {}
    for ci in range(nc):
        for b in range(nb):
            it = b * nc + ci
            s = states[b]
            rs = _mm3(xq_ls[it], _rhs3(s, w))
            u = xvs[it] - rs[:n]
            outs[it] = rs[n:] + _mm3(qk_ls[it], _rhs3(u, w))
            states[b] = s * jnp.exp(gls[it]) + _collapse_heads(_mm3_tn(kds[it], u), w)
    for b in range(nb):
        s_scr[b] = states[b]

    for it, (b, ci) in enumerate(items):
        o = outs[it]
        ms = _mm_x01(o * o, ones_bd) * (1.0 / HEAD_DIM)
        o_ref[b, sl(ci), :] = o * lax.rsqrt(ms + 1e-6) * nw_ref[...] * _silu(z_ref[b, sl(ci), :])

    @pl.when(c == pl.num_programs(1) - 1)
    def _():
        sout_ref[...] = s_scr[...]


def _recurrent_tiling(bsz, l):
    nc = min(l // CHUNK, 4)
    nb = 2 if nc > 1 else min(bsz, 4)
    return nb, nc


def _gdn_mixer(h3, hs3, buf, s0, cw, a_log, dt_bias, norm_w):
    bsz, l, _ = h3.shape
    nb, nc = _recurrent_tiling(bsz, l)
    rows = nc * CHUNK
    pad_r = lambda v, off: jnp.zeros((1, H_SMALL), F32).at[0, off:off + GROUP_HEADS].set(v)
    full = lambda shape: pl.BlockSpec(shape, lambda bi, i: (0,) * len(shape))
    o, s_new = pl.pallas_call(
        functools.partial(_gdn_kernel, nb=nb, nc=nc),
        grid=(bsz // nb, l // rows),
        in_specs=[pl.BlockSpec((nb, rows, 3 * GW), lambda bi, i: (bi, i, H_BQKV // (3 * GW))),
                  pl.BlockSpec((nb, rows, GW), lambda bi, i: (bi, i, H_BZ // GW)),
                  pl.BlockSpec((nb, rows, H_SMALL), lambda bi, i: (bi, i, 0)),
                  pl.BlockSpec((nb, GDN_CONV_WIDTH - 1, 3 * GW), lambda bi, i: (bi, 0, 0)),
                  pl.BlockSpec((nb, HEAD_DIM, GW), lambda bi, i: (bi, 0, 0)),
                  full((GDN_CONV_WIDTH, 3 * GW)), full((1, H_SMALL)), full((1, H_SMALL)), full((1, GW)),
                  full((GW, GW)), full((H_SMALL, GW)), full((H_SMALL, GW))],
        out_specs=[pl.BlockSpec((nb, rows, GW), lambda bi, i: (bi, i, 0)),
                   pl.BlockSpec((nb, HEAD_DIM, GW), lambda bi, i: (bi, 0, 0))],
        out_shape=[jax.ShapeDtypeStruct((bsz, l, GW), F32),
                   jax.ShapeDtypeStruct((bsz, HEAD_DIM, GW), F32)],
        scratch_shapes=[pltpu.VMEM((nb, rows + 8, 3 * GW), F32), pltpu.VMEM((nb, HEAD_DIM, GW), F32)],
        compiler_params=_params(2),
        name="gdn_mixer",
    )(h3, h3, hs3, buf, _heads_to_wide(s0), cw, pad_r(a_log, S_A), pad_r(dt_bias, S_A),
      jnp.tile(norm_w, GROUP_HEADS)[None, :], _ones_block_diag(), _lane_spread(S_A), _lane_spread(S_B))
    return o, _wide_to_heads(s_new)


def _rwkv_kernel(rkv_ref, hs_ref, sh_ref, shs_ref, s0_ref, mu_ref, mus_ref, w0_ref, wup_ref, a0_ref, aup_ref,
                 gup_ref, kk_ref, ka_ref, rk_ref, lng_ref, lnb_ref, ones_ref, o_ref, sout_ref, win_ref, wins_ref,
                 s_scr, *, nb, nc):
    c = pl.program_id(1)
    n = CHUNK

    rows = nc * n
    items = [(b, ci) for b in range(nb) for ci in range(nc)]

    @pl.when(c == 0)
    def _():
        for b in range(nb):
            win_ref[b, 0:8, :] = jnp.zeros((8, 3 * GW), F32)
            wins_ref[b, 0:8, :] = jnp.zeros((8, H_SMALL), F32)
            win_ref[b, 7:8, :] = sh_ref[b]
            wins_ref[b, 7:8, :] = shs_ref[b]
        s_scr[...] = s0_ref[...]

    wup_r, aup_r, gup_r = (_rhs3(r[...], precise=True) for r in (wup_ref, aup_ref, gup_ref))
    per_b = []
    for b in range(nb):
        x = rkv_ref[b]
        xs = hs_ref[b]
        win_ref[b, 8:8 + rows, :] = x
        wins_ref[b, 8:8 + rows, :] = xs
        xm = x + (win_ref[b, 7:7 + rows, :] - x) * mu_ref[...]
        xms = xs + (wins_ref[b, 7:7 + rows, :] - xs) * mus_ref[...]
        last = win_ref[b, rows + 7:rows + 8, :]
        win_ref[b, 7:8, :] = last
        lasts = wins_ref[b, rows + 7:rows + 8, :]
        wins_ref[b, 7:8, :] = lasts
        w_pre = w0_ref[...] + _mm3(_lhs3(jnp.tanh(xms), precise=True), wup_r)
        logw = -jnp.exp(-_softplus(-w_pre) - 0.5)
        a_sig = _sigmoid(a0_ref[...] + _mm3(_lhs3(xms, precise=True), aup_r))
        gate = _mm3(_lhs3(_sigmoid(xms), precise=True), gup_r)
        kx = xm[:, GW:2 * GW]
        per_b.append(dict(rr=xm[:, 0:GW], vv=xm[:, 2 * GW:3 * GW], logw=logw, a_sig=a_sig, gate=gate,
                          kkp=kx * kk_ref[...], k2=kx * (1.0 + (a_sig - 1.0) * ka_ref[...])))

    w = _Wide()
    ones_bd = ones_ref[...]
    tril = _tril_ones(n, BF16)
    sl = lambda ci: slice(ci * n, (ci + 1) * n)
    get = lambda name: [per_b[b][name][sl(ci), :] for b, ci in items]
    rrs, vvs, logws, a_sigs, gates, kkps, k2s = (get(k) for k in ('rr', 'vv', 'logw', 'a_sig', 'gate', 'kkp', 'k2'))
    kks = [kkp * lax.rsqrt(_mm_x01(kkp * kkp, ones_bd) + 1e-6) for kkp in kkps]
    cums = [_tri_mm(tril, logw) for logw in logws]
    w_incls = [jnp.exp(cum) for cum in cums]
    w_lasts = [wi[n - 1:n, :] for wi in w_incls]
    w_invs = [jnp.exp(-cum) for cum in cums]
    ats = [-kk * jnp.exp(cum - logw) for kk, cum, logw in zip(kks, cums, logws)]
    bts = [kk * a_sig * wv for kk, a_sig, wv in zip(kks, a_sigs, w_invs)]
    kts = [k2 * wv for k2, wv in zip(k2s, w_invs)]
    rts = [rr * wi for rr, wi in zip(rrs, w_incls)]
    ar_ls = [_lhs3(jnp.concatenate([at, rt], axis=0)) for at, rt in zip(ats, rts)]
    gbs = [_mm3(ar_l, _rhs3(bt, w), NT) for ar_l, bt in zip(ar_ls, bts)]
    gks = [_mm3(ar_l, _rhs3(kt, w), NT) for ar_l, kt in zip(ar_ls, kts)]
    t_ls = [_lhs3(t) for t in _unit_lower_inverse_wide([jnp.where(w.strict, -gb[:n], 0.0) for gb in gbs], w)]
    arb_ls = [_lhs3(jnp.where(w.incl, gb[n:], 0.0)) for gb in gbs]
    ark_ls = [_lhs3(jnp.where(w.incl, gk[n:], 0.0)) for gk in gks]
    v_rs = [_rhs3(vv, w) for vv in vvs]
    aakvs = [_mm3(_lhs3(jnp.where(w.strict, gk[:n], 0.0)), v_r) for gk, v_r in zip(gks, v_rs)]
    arkvs = [_mm3(ark_l, v_r) for ark_l, v_r in zip(ark_ls, v_rs)]
    bks = [jnp.concatenate([bt * wl, kt * wl], axis=0) for bt, kt, wl in zip(bts, kts, w_lasts)]

    states = [s_scr[b] for b in range(nb)]
    ys = {}
    for ci in range(nc):
        for b in range(nb):
            it = b * nc + ci
            s0 = states[b]
            ars = _mm3(ar_ls[it], _rhs3(s0, w), NT)
            u = _mm3(t_ls[it], _rhs3(ars[:n] + aakvs[it], w))
            ys[it] = ars[n:] + _mm3(arb_ls[it], _rhs3(u, w)) + arkvs[it]
            upd = _mm3_tn(jnp.concatenate([u, vvs[it]], axis=0), bks[it])
            states[b] = s0 * w_lasts[it] + _collapse_heads(upd, w)
    for b in range(nb):
        s_scr[b] = states[b]

    inv_d = 1.0 / HEAD_DIM
    for it, (b, ci) in enumerate(items):
        y = ys[it]
        yc = y - _mm_x01(y, ones_bd) * inv_d
        var_y = _mm_x01(yc * yc, ones_bd) * inv_d
        yn = yc * lax.rsqrt(var_y + GN_EPS) * lng_ref[...] + lnb_ref[...]
        bonus = _mm_x01(rrs[it] * k2s[it] * rk_ref[...], ones_bd) * vvs[it]
        o_ref[b, sl(ci), :] = (yn + bonus) * gates[it]

    @pl.when(c == pl.num_programs(1) - 1)
    def _():
        sout_ref[...] = s_scr[...]


def _rwkv_mixer(h3, hs3, shift, s0, lp):
    bsz, l, _ = h3.shape
    nb, nc = _recurrent_tiling(bsz, l)
    rows = nc * CHUNK
    sh_main = shift[:, None, 0:3 * GW]
    sh_small = jnp.pad(shift[:, None, 3 * GW:], ((0, 0), (0, 0), (0, H_SMALL - LORA)))
    mu = lp['rwkv_mu']
    mu_main = mu[None, 0:3 * GW]
    mu_small = jnp.pad(mu[None, 3 * GW:], ((0, 0), (0, H_SMALL - LORA)))
    place = lambda w, off: jnp.zeros((H_SMALL, GW), F32).at[off:off + w.shape[0]].set(w)
    wup = place(lp['rwkv_w_up'], 0)
    aup = place(lp['rwkv_a_up'], RWKV_W_LORA)
    gup = place(lp['rwkv_g_up'], RWKV_W_LORA + RWKV_A_LORA)
    row = lambda v: v.reshape(1, GW)
    full = lambda shape: pl.BlockSpec(shape, lambda bi, i: (0,) * len(shape))
    o, s_new = pl.pallas_call(
        functools.partial(_rwkv_kernel, nb=nb, nc=nc),
        grid=(bsz // nb, l // rows),
        in_specs=[pl.BlockSpec((nb, rows, 3 * GW), lambda bi, i: (bi, i, H_CRKV // (3 * GW))),
                  pl.BlockSpec((nb, rows, H_SMALL), lambda bi, i: (bi, i, 0)),
                  pl.BlockSpec((nb, 1, 3 * GW), lambda bi, i: (bi, 0, 0)),
                  pl.BlockSpec((nb, 1, H_SMALL), lambda bi, i: (bi, 0, 0)),
                  pl.BlockSpec((nb, HEAD_DIM, GW), lambda bi, i: (bi, 0, 0)),
                  full((1, 3 * GW)), full((1, H_SMALL)), full((1, GW)), full((H_SMALL, GW)), full((1, GW)),
                  full((H_SMALL, GW)), full((H_SMALL, GW)), full((1, GW)), full((1, GW)), full((1, GW)),
                  full((1, GW)), full((1, GW)), full((GW, GW))],
        out_specs=[pl.BlockSpec((nb, rows, GW), lambda bi, i: (bi, i, 0)),
                   pl.BlockSpec((nb, HEAD_DIM, GW), lambda bi, i: (bi, 0, 0))],
        out_shape=[jax.ShapeDtypeStruct((bsz, l, GW), F32),
                   jax.ShapeDtypeStruct((bsz, HEAD_DIM, GW), F32)],
        scratch_shapes=[pltpu.VMEM((nb, rows + 8, 3 * GW), F32), pltpu.VMEM((nb, rows + 8, H_SMALL), F32),
                        pltpu.VMEM((nb, HEAD_DIM, GW), F32)],
        compiler_params=_params(2),
        name="rwkv_mixer",
    )(h3, hs3, sh_main, sh_small, _heads_to_wide(s0), mu_main, mu_small, row(lp['rwkv_w0']), wup,
      row(lp['rwkv_a0']), aup, gup, row(lp['rwkv_k_k']), row(lp['rwkv_k_a']), row(lp['rwkv_r_k']),
      row(lp['rwkv_lnx_g']), row(lp['rwkv_lnx_b']), _ones_block_diag())
    return o, _wide_to_heads(s_new)


_FB = 128
_SAMPLE_BK = 512


def _log_forget(hs, bf):
    return -_softplus(-(hs + bf))


def _head_slice(h):
    return slice(h * HEAD_DIM, (h + 1) * HEAD_DIM)


def _attend_heads(qs, kblk, vblk, key_bias, mask, carry):
    m, l, acc = carry
    bq = qs[0].shape[0]
    kb16 = kblk.astype(BF16)
    vb16 = vblk.astype(BF16)
    parts = []
    for h in range(GROUP_HEADS):
        s = _dot(qs[h], kb16[_head_slice(h), :]) + key_bias[h:h + 1, :]
        parts.append(s if mask is None else jnp.where(mask, s, NEG))
    s = jnp.concatenate(parts, axis=0)
    m_new = jnp.maximum(m, jnp.max(s, -1, keepdims=True))
    alpha = jnp.exp(m - m_new)
    p = jnp.exp(s - m_new)
    l = alpha * l + jnp.sum(p, -1, keepdims=True)
    pb = p.astype(BF16)
    pv = jnp.concatenate([_dot(pb[h * bq:(h + 1) * bq, :], vb16[_head_slice(h), :], NT)
                          for h in range(GROUP_HEADS)], axis=0)
    return m_new, l, alpha * acc + pv


def _attend_init(bq):
    rows = GROUP_HEADS * bq
    return jnp.full((rows, 1), NEG, F32), jnp.zeros((rows, 1), F32), jnp.zeros((rows, HEAD_DIM), F32)


def _attend_store(o_ref, carry, bq):
    _, l, acc = carry
    for h in range(GROUP_HEADS):
        o_ref[:, _head_slice(h)] = acc[h * bq:(h + 1) * bq, :] / l[h * bq:(h + 1) * bq, :]


def _sel_rows(sel01, x):
    t = x.shape[0]
    r = _dot(sel01, jnp.concatenate(_split3(x), axis=0), NT)
    return r[:, :t] + r[:, t:2 * t] + r[:, 2 * t:]


def _fox_prompt_kernel(q_ref, kv_ref, hs_ref, bf_ref, o_ref, lf_ref, nf_scr, *, l, bq):
    i = pl.program_id(1)
    per = bq // _FB

    @pl.when(i == 0)
    def _():
        utri = _triu_ones(_FB, BF16)
        sel = _col_selector(8, H_SMALL, S_F).astype(BF16)
        carry = jnp.zeros((8, 1), F32)
        for jb in range(l // _FB):
            rows = slice(jb * _FB, (jb + 1) * _FB)
            lf = _log_forget(hs_ref[rows, :], bf_ref[...])
            lf_ref[rows, :] = lf[:, S_F:S_F + GROUP_HEADS]
            loc = _mm_x01(_sel_rows(sel, lf), utri, 3)
            nf_scr[jb // per, :, (jb % per) * _FB:(jb % per + 1) * _FB] = -(loc + carry)
            carry = carry + loc[:, _FB - 1:_FB]

    q = q_ref[...]
    r_i, c_i = _iota2((bq, bq), 0), _iota2((bq, bq), 1)
    qs = [(q[:, _head_slice(h)] * (HEAD_DIM ** -0.5)).astype(BF16) for h in range(GROUP_HEADS)]

    def step(j, mask, carry):
        keys = pl.ds(pl.multiple_of(j * bq, bq), bq)
        return _attend_heads(qs, kv_ref[0:GW, keys], kv_ref[GW:KV_ROWS, keys], nf_scr[j], mask, carry)

    carry = lax.fori_loop(0, i, lambda j, c: step(j, None, c), _attend_init(bq))
    carry = step(i, c_i <= r_i, carry)
    _attend_store(o_ref, carry, bq)


def _fox_prompt(h3, hs3, kv_t, bf_pad):
    bsz, l, _ = h3.shape
    bq = min(l, 2 * _FB)
    kern = functools.partial(_fox_prompt_kernel, l=l, bq=bq)
    return pl.pallas_call(
        kern,
        grid=(bsz, l // bq),
        in_specs=[pl.BlockSpec((None, bq, GW), lambda bi, i: (bi, i, H_DQ // GW)),
                  pl.BlockSpec((None, KV_ROWS, l), lambda bi, i: (bi, 0, 0)),
                  pl.BlockSpec((None, l, H_SMALL), lambda bi, i: (bi, 0, 0)),
                  pl.BlockSpec((1, H_SMALL), lambda bi, i: (0, 0))],
        out_specs=[pl.BlockSpec((None, bq, GW), lambda bi, i: (bi, i, 0)),
                   pl.BlockSpec((None, l, GROUP_HEADS), lambda bi, i: (bi, 0, 0))],
        out_shape=[jax.ShapeDtypeStruct((bsz, l, GW), F32),
                   jax.ShapeDtypeStruct((bsz, l, GROUP_HEADS), F32)],
        scratch_shapes=[pltpu.VMEM((l // bq, 8, bq), F32)],
        compiler_params=_params(2),
        name="fox_prompt",
    )(h3, kv_t, hs3, bf_pad)


def _fox_sample_kernel(q_ref, kv_ref, hs_ref, bf_ref, ck_ref, cv_ref, clf_ref, o_ref, lf_ref, cum_scr, *, l, p):
    nblk = p // _FB
    per = _SAMPLE_BK // _FB

    loc = _mm_x01(clf_ref[...].reshape(nblk * 8, _FB), _triu_ones(_FB, BF16), 3)
    carry = jnp.zeros((8, 1), F32)
    for jb in range(nblk):
        blk = loc[jb * 8:(jb + 1) * 8, :]
        cum_scr[jb // per, :, (jb % per) * _FB:(jb % per + 1) * _FB] = blk + carry
        carry = carry + blk[:, _FB - 1:_FB]
    total = carry

    lf = _log_forget(hs_ref[...], bf_ref[...])
    lf_ref[...] = lf[:, S_F:S_F + GROUP_HEADS]
    cum_r = _mm_x01(_sel_rows(_col_selector(8, H_SMALL, S_F).astype(BF16), lf), _triu_ones(l, BF16), 3)
    q = q_ref[...]
    r_i, c_i = _iota2((l, l), 0), _iota2((l, l), 1)
    qs = [(q[:, _head_slice(h)] * (HEAD_DIM ** -0.5)).astype(BF16) for h in range(GROUP_HEADS)]

    def body(j, carry):
        keys = pl.ds(pl.multiple_of(j * _SAMPLE_BK, _SAMPLE_BK), _SAMPLE_BK)
        suffix = total - cum_scr[j]
        return _attend_heads(qs, ck_ref[:, keys], cv_ref[:, keys], suffix, None, carry)

    carry = lax.fori_loop(0, p // _SAMPLE_BK, body, _attend_init(l))
    carry = _attend_heads(qs, kv_ref[0:GW, :], kv_ref[GW:KV_ROWS, :], -cum_r, c_i <= r_i, carry)
    _attend_store(o_ref, carry, l)


def _fox_sample(h3, hs3, kv_t, bf_pad, layer, ck_t, cv_t, clogf):
    bsz, l, _ = h3.shape
    p = ck_t.shape[-1]
    clf = clogf.reshape(bsz, p // _FB, _FB, GROUP_HEADS).transpose(0, 1, 3, 2)
    clf = jnp.pad(clf, ((0, 0), (0, 0), (0, 8 - GROUP_HEADS), (0, 0)))
    kern = functools.partial(_fox_sample_kernel, l=l, p=p)
    return pl.pallas_call(
        kern,
        grid=(bsz,),
        in_specs=[pl.BlockSpec((None, l, GW), lambda bi: (bi, 0, H_DQ // GW)),
                  pl.BlockSpec((None, KV_ROWS, l), lambda bi: (bi, 0, 0)),
                  pl.BlockSpec((None, l, H_SMALL), lambda bi: (bi, 0, 0)),
                  pl.BlockSpec((1, H_SMALL), lambda bi: (0, 0)),
                  pl.BlockSpec((None, None, GW, p), lambda bi: (layer, bi, 0, 0)),
                  pl.BlockSpec((None, None, GW, p), lambda bi: (layer, bi, 0, 0)),
                  pl.BlockSpec((None, p // _FB, 8, _FB), lambda bi: (bi, 0, 0, 0))],
        out_specs=[pl.BlockSpec((None, l, GW), lambda bi: (bi, 0, 0)),
                   pl.BlockSpec((None, l, GROUP_HEADS), lambda bi: (bi, 0, 0))],
        out_shape=[jax.ShapeDtypeStruct((bsz, l, GW), F32),
                   jax.ShapeDtypeStruct((bsz, l, GROUP_HEADS), F32)],
        scratch_shapes=[pltpu.VMEM((p // _SAMPLE_BK, 8, _SAMPLE_BK), F32)],
        compiler_params=_params(1),
        name="fox_sample",
    )(h3, kv_t, hs3, bf_pad, ck_t, cv_t, clf)


def _outproj_kernel(oa_ref, ob_ref, oc_ref, od_ref, x_ref, w_ref, g_ref, b_ref, y_ref):
    acc = DN_ALPHA * x_ref[...]
    for gi, ref in enumerate((oa_ref, ob_ref, oc_ref, od_ref)):
        acc = acc + _mm(ref[...], w_ref[gi * GW:(gi + 1) * GW, :])
    y_ref[...] = _layer_norm(acc, g_ref[...], b_ref[...])


def _outproj(oa, ob, oc, od, x2d, w_out, g, b):
    t = x2d.shape[0]
    tm = min(t, 512)
    mix = pl.BlockSpec((tm, GW), lambda i: (i, 0))
    return pl.pallas_call(
        _outproj_kernel,
        grid=(t // tm,),
        in_specs=[mix, mix, mix, mix,
                  pl.BlockSpec((tm, D_MODEL), lambda i: (i, 0)),
                  pl.BlockSpec((D_MODEL, D_MODEL), lambda i: (0, 0)),
                  pl.BlockSpec((1, D_MODEL), lambda i: (0, 0)),
                  pl.BlockSpec((1, D_MODEL), lambda i: (0, 0))],
        out_specs=pl.BlockSpec((tm, D_MODEL), lambda i: (i, 0)),
        out_shape=jax.ShapeDtypeStruct((t, D_MODEL), F32),
        compiler_params=_params(1),
        name="outproj_ln",
    )(oa, ob, oc, od, x2d, w_out, g, b)


_R_G, _R_E = 0, N_EXPERT_GROUPS


def _moe_kernel(x_ref, wr_ref, br_ref, wg_ref, wu_ref, wd_ref, g_ref, b_ref, y_ref, comb_scr, xb_scr, acc_scr):
    e = pl.program_id(1)
    tm = x_ref.shape[0]
    lane = _iota2((tm, LANE), 1)

    @pl.when(e == 0)
    def _():
        x = x_ref[...]
        xb_scr[...] = x.astype(BF16)
        acc_scr[...] = jnp.zeros_like(acc_scr)
        logits = _mmh(x, wr_ref[...]) + br_ref[...]
        gmask = lane < N_EXPERT_GROUPS
        gl = jnp.where(gmask, logits, NEG)
        ge = jnp.exp(gl - jnp.max(gl, -1, keepdims=True))
        pg = ge / jnp.sum(ge, -1, keepdims=True)
        gp = jnp.max(pg, -1, keepdims=True)
        gi = jnp.min(jnp.where(jnp.logical_and(gmask, pg == gp), lane, LANE), -1, keepdims=True)
        emask = jnp.logical_and(jnp.logical_and(lane >= _R_E, lane < _R_E + N_EXPERTS),
                                ((lane - _R_E) >> 2) == gi)
        el = jnp.where(emask, logits, NEG)
        ee = jnp.exp(el - jnp.max(el, -1, keepdims=True))
        ep = ee / jnp.sum(ee, -1, keepdims=True)
        m1 = jnp.max(jnp.where(emask, ep, -1.0), -1, keepdims=True)
        i1 = jnp.min(jnp.where(jnp.logical_and(emask, ep == m1), lane, LANE), -1, keepdims=True)
        rest = jnp.logical_and(emask, lane != i1)
        m2 = jnp.max(jnp.where(rest, ep, -1.0), -1, keepdims=True)
        i2 = jnp.min(jnp.where(jnp.logical_and(rest, ep == m2), lane, LANE), -1, keepdims=True)
        den = m1 + m2
        comb_scr[...] = jnp.where(lane == i1, gp * m1 / den, jnp.where(lane == i2, gp * m2 / den, 0.0))

    ce = jnp.sum(jnp.where(lane == e + _R_E, comb_scr[...], 0.0), -1, keepdims=True)
    xb = xb_scr[...]
    hh = _silu(_mm(xb, wg_ref[...])) * _mm(xb, wu_ref[...])
    acc_scr[...] += ce * _mm(hh, wd_ref[...])

    @pl.when(e == pl.num_programs(1) - 1)
    def _():
        y_ref[...] = _layer_norm(DN_ALPHA * x_ref[...] + acc_scr[...], g_ref[...], b_ref[...])


def _moe(x2d, w_router, b_router, layer, wg, wu, wd, g, b):
    t = x2d.shape[0]
    tm = min(t, 1024)
    return pl.pallas_call(
        _moe_kernel,
        grid=(t // tm, N_EXPERTS),
        in_specs=[pl.BlockSpec((tm, D_MODEL), lambda i, e: (i, 0)),
                  pl.BlockSpec((D_MODEL, LANE), lambda i, e: (0, 0)),
                  pl.BlockSpec((1, LANE), lambda i, e: (0, 0)),
                  pl.BlockSpec((None, None, D_MODEL, D_EXPERT), lambda i, e: (layer, e, 0, 0)),
                  pl.BlockSpec((None, None, D_MODEL, D_EXPERT), lambda i, e: (layer, e, 0, 0)),
                  pl.BlockSpec((None, None, D_EXPERT, D_MODEL), lambda i, e: (layer, e, 0, 0)),
                  pl.BlockSpec((1, D_MODEL), lambda i, e: (0, 0)),
                  pl.BlockSpec((1, D_MODEL), lambda i, e: (0, 0))],
        out_specs=pl.BlockSpec((tm, D_MODEL), lambda i, e: (i, 0)),
        out_shape=jax.ShapeDtypeStruct((t, D_MODEL), F32),
        scratch_shapes=[pltpu.VMEM((tm, LANE), F32), pltpu.VMEM((tm, D_MODEL), BF16),
                        pltpu.VMEM((tm, D_MODEL), F32)],
        compiler_params=_params(2),
        name="moe_ln",
    )(x2d, w_router, b_router, wg, wu, wd, g, b)


def _prep_layer(lp):
    w = lp['w_in']
    c0 = CONV_COLS
    c1 = c0 + GDN_COLS
    c2 = c1 + RWKV_COLS
    main = jnp.concatenate([w[:, c0:c0 + 3 * GW], w[:, c1:c1 + 3 * GW], w[:, c2:c2 + GW],
                            w[:, 0:GW], w[:, GW:2 * GW], w[:, c0 + 3 * GW:c0 + 4 * GW]], axis=1).astype(BF16)
    kv_t = w[:, c2 + GW:c2 + 3 * GW].T.astype(BF16)
    small = jnp.concatenate([w[:, c1 + 3 * GW:c2], w[:, c0 + 4 * GW:c1], w[:, c2 + 3 * GW:],
                             jnp.zeros((D_MODEL, H_SMALL - LORA - 3 * GROUP_HEADS), F32)], axis=1)
    wr = jnp.concatenate([lp['router_g_w'], lp['router_e_w'],
                          jnp.zeros((D_MODEL, LANE - N_EXPERT_GROUPS - N_EXPERTS), F32)], axis=1)
    br = jnp.concatenate([lp['router_g_b'], lp['router_e_b'],
                          jnp.zeros((LANE - N_EXPERT_GROUPS - N_EXPERTS,), F32)])[None, :]
    bf_pad = jnp.zeros((1, H_SMALL), F32).at[0, S_F:S_F + GROUP_HEADS].set(lp['fox_b_f'])
    return dict(w_main=main, w_small=small, w_kv_t=kv_t, w_out=lp['w_out'].astype(BF16), wr=wr, br=br,
                bf_pad=bf_pad)


def _trunk_layer(x, lp, pp, layer, experts, conv_buf, gdn_buf, gdn_s, rw_shift, rw_s, fox_cache):
    bsz, l, d = x.shape
    x2d = x.reshape(bsz * l, d)
    h3, hs3, kv_t = _inproj(x, pp['w_main'], pp['w_small'], pp['w_kv_t'])

    o_a, new_conv = _conv_mixer(h3, conv_buf, lp['conv_w'], lp['conv_b'][None], lp['conv_ln_g'][None],
                                lp['conv_ln_b'][None])
    o_b, new_gdn_s = _gdn_mixer(h3, hs3, gdn_buf, gdn_s, lp['gdn_conv_w'], lp['gdn_a_log'], lp['gdn_dt_bias'],
                                lp['gdn_norm_w'])
    o_c, new_rw_s = _rwkv_mixer(h3, hs3, rw_shift, rw_s, lp)
    if fox_cache is None:
        o_d, logf = _fox_prompt(h3, hs3, kv_t, pp['bf_pad'])
    else:
        o_d, logf = _fox_sample(h3, hs3, kv_t, pp['bf_pad'], layer, *fox_cache)

    flat = lambda o: o.reshape(bsz * l, GW)
    x1 = _outproj(flat(o_a), flat(o_b), flat(o_c), flat(o_d), x2d, pp['w_out'], lp['ln1_g'][None],
                  lp['ln1_b'][None])
    x2 = _moe(x1, pp['wr'], pp['br'], layer, *experts, lp['ln2_g'][None], lp['ln2_b'][None])

    kw = GDN_CONV_WIDTH - 1
    new_gdn_buf = h3[:, l - kw:, H_BQKV:H_BQKV + 3 * GW]
    new_shift = jnp.concatenate([h3[:, l - 1, H_CRKV:H_CRKV + 3 * GW], hs3[:, l - 1, S_LORA:S_LORA + LORA]], axis=-1)
    return x2.reshape(bsz, l, d), (new_conv, new_gdn_buf, new_gdn_s, new_shift, new_rw_s, kv_t, logf)


def _untranspose_kv(kv_ts):
    bsz, _, l = kv_ts[0].shape

    def gather(lo):
        stacked = jnp.stack([kv[:, lo:lo + GW, :] for kv in kv_ts])
        return stacked.reshape(len(kv_ts), bsz, GROUP_HEADS, HEAD_DIM, l).transpose(0, 1, 4, 2, 3)

    return gather(0), gather(GW)


_LAYER_KEYS = ('w_in', 'conv_w', 'conv_b', 'conv_ln_g', 'conv_ln_b', 'gdn_conv_w', 'gdn_a_log', 'gdn_dt_bias',
               'gdn_norm_w', 'rwkv_mu', 'rwkv_w0', 'rwkv_w_up', 'rwkv_a0', 'rwkv_a_up', 'rwkv_g_up', 'rwkv_k_k',
               'rwkv_k_a', 'rwkv_r_k', 'rwkv_lnx_g', 'rwkv_lnx_b', 'fox_b_f', 'w_out', 'ln1_g', 'ln1_b',
               'router_g_w', 'router_g_b', 'router_e_w', 'router_e_b', 'ln2_g', 'ln2_b')


def kernel(x_prompt, x_sample, cache_fox_k, cache_fox_v, cache_fox_logf, state_conv, state_gdn_conv, state_gdn, state_rwkv_shift, state_rwkv, w_in, conv_w, conv_b, conv_ln_g, conv_ln_b, gdn_conv_w, gdn_a_log, gdn_dt_bias, gdn_norm_w, rwkv_mu, rwkv_w0, rwkv_w_up, rwkv_a0, rwkv_a_up, rwkv_g_up, rwkv_k_k, rwkv_k_a, rwkv_r_k, rwkv_lnx_g, rwkv_lnx_b, fox_b_f, w_out, ln1_g, ln1_b, router_g_w, router_g_b, router_e_w, router_e_b, exp_w_gate, exp_w_up, exp_w_down, ln2_g, ln2_b):
    weights = dict(zip(_LAYER_KEYS, (w_in, conv_w, conv_b, conv_ln_g, conv_ln_b, gdn_conv_w, gdn_a_log,
                                     gdn_dt_bias, gdn_norm_w, rwkv_mu, rwkv_w0, rwkv_w_up, rwkv_a0, rwkv_a_up,
                                     rwkv_g_up, rwkv_k_k, rwkv_k_a, rwkv_r_k, rwkv_lnx_g, rwkv_lnx_b, fox_b_f,
                                     w_out, ln1_g, ln1_b, router_g_w, router_g_b, router_e_w, router_e_b,
                                     ln2_g, ln2_b)))
    experts = (exp_w_gate, exp_w_up, exp_w_down)
    depth, bs, past = cache_fox_k.shape[:3]
    ck_t = cache_fox_k.transpose(0, 1, 3, 4, 2).reshape(depth, bs, GW, past)
    cv_t = cache_fox_v.transpose(0, 1, 3, 4, 2).reshape(depth, bs, GW, past)
    xp, xs = x_prompt, x_sample
    bp = x_prompt.shape[0]
    outs_p = [[] for _ in range(7)]
    outs_s = [[] for _ in range(7)]
    for l in range(DEPTH):
        lp = {k: v[l] for k, v in weights.items()}
        pp = _prep_layer(lp)
        xp, st_p = _trunk_layer(xp, lp, pp, l, experts,
                                jnp.zeros((bp, CONV_WIDTH - 1, GW), F32),
                                jnp.zeros((bp, GDN_CONV_WIDTH - 1, 3 * GW), F32),
                                jnp.zeros((bp, GROUP_HEADS, HEAD_DIM, HEAD_DIM), F32),
                                jnp.zeros((bp, RWKV_COLS), F32),
                                jnp.zeros((bp, GROUP_HEADS, HEAD_DIM, HEAD_DIM), F32),
                                None)
        xs, st_s = _trunk_layer(xs, lp, pp, l, experts, state_conv[l], state_gdn_conv[l], state_gdn[l],
                                state_rwkv_shift[l], state_rwkv[l], (ck_t, cv_t, cache_fox_logf[l]))
        for i in range(7):
            outs_p[i].append(st_p[i])
            outs_s[i].append(st_s[i])

    def assemble(outs):
        conv, gdn_buf, gdn_s, shift, rw_s = (jnp.stack(o) for o in outs[:5])
        k_new, v_new = _untranspose_kv(outs[5])
        return conv, gdn_buf, gdn_s, shift, rw_s, k_new, v_new, jnp.stack(outs[6])

    return (xp, xs, *assemble(outs_p), *assemble(outs_s))
```

```python
import functools

import jax
import jax.numpy as jnp
from jax import lax
from jax.experimental import pallas as pl
from jax.experimental.pallas import tpu as pltpu

F32 = jnp.float32
BF16 = jnp.bfloat16

D_MODEL = 1024
DEPTH = 2
CHUNK = 64
HEAD_DIM = 64
GW = D_MODEL // 4
GROUP_HEADS = GW // HEAD_DIM
CONV_WIDTH = 31
GDN_CONV_WIDTH = 4
RWKV_W_LORA = 16
RWKV_A_LORA = 16
RWKV_G_LORA = 32
N_EXPERT_GROUPS = 4
EXPERTS_PER_GROUP = 4
N_EXPERTS = N_EXPERT_GROUPS * EXPERTS_PER_GROUP
D_EXPERT = 256
DN_ALPHA = (2 * DEPTH) ** 0.25
LN_EPS = 1e-5
GN_EPS = 64e-5

CONV_COLS = 2 * GW
GDN_COLS = 4 * GW + 2 * GROUP_HEADS
RWKV_COLS = 3 * GW + RWKV_W_LORA + RWKV_A_LORA + RWKV_G_LORA
FOX_COLS = 3 * GW + GROUP_HEADS
LORA = RWKV_W_LORA + RWKV_A_LORA + RWKV_G_LORA

H_BQKV, H_CRKV, H_DQ, H_AVAL, H_AGATE, H_BZ = 0, 768, 1536, 1792, 2048, 2304
H_MAIN = 2560
KV_ROWS = 2 * GW
S_LORA, S_A, S_B, S_F = 0, 64, 68, 72
H_SMALL = 128
LANE = 128

VMEM_LIMIT = 56 * 1024 * 1024

NN = (((1,), (0,)), ((), ()))
NT = (((1,), (1,)), ((), ()))
TN = (((0,), (0,)), ((), ()))
NEG = -1e30


def _dot(a, b, dims=NN):
    return lax.dot_general(a, b, dims, preferred_element_type=F32)


def _mm(a, b, dims=NN):
    return _dot(a.astype(BF16), b.astype(BF16), dims)


def _mmh(a, b, dims=NN):
    return lax.dot_general(a, b, dims, preferred_element_type=F32, precision=lax.Precision.HIGHEST)


def _sigmoid(x):
    return 0.5 * (jnp.tanh(0.5 * x) + 1.0)


def _silu(x):
    return x * _sigmoid(x)


def _softplus(x):
    return jnp.maximum(x, 0.0) + jnp.log(1.0 + jnp.exp(-jnp.abs(x)))


def _iota2(shape, dim):
    return lax.broadcasted_iota(jnp.int32, shape, dim)


def _tril_ones(n, dtype=F32):
    r, c = _iota2((n, n), 0), _iota2((n, n), 1)
    return (r >= c).astype(dtype)


def _triu_ones(n, dtype=F32):
    r, c = _iota2((n, n), 0), _iota2((n, n), 1)
    return (r <= c).astype(dtype)


def _col_selector(rows, width, col0):
    r, c = _iota2((rows, width), 0), _iota2((rows, width), 1)
    return (c == r + col0).astype(F32)


def _layer_norm(x, g, b):
    mu = jnp.mean(x, axis=-1, keepdims=True)
    xc = x - mu
    var = jnp.mean(xc * xc, axis=-1, keepdims=True)
    return xc * lax.rsqrt(var + LN_EPS) * g + b


def _split2(x):
    hi = x.astype(BF16)
    lo = (x - hi.astype(F32)).astype(BF16)
    return hi, lo


def _split3(x):
    hi = x.astype(BF16)
    r = x - hi.astype(F32)
    mid = r.astype(BF16)
    lo = (r - mid.astype(F32)).astype(BF16)
    return hi, mid, lo


class _Wide:
    def __init__(self):
        self.i = _iota2((CHUNK, GW), 0)
        lane = _iota2((CHUNK, GW), 1)
        self.j = lane & (HEAD_DIM - 1)
        self.head = lane >> 6
        self.incl = self.i >= self.j
        self.strict = self.i > self.j
        self.eye = self.i == self.j
        r, c = _iota2((GW, GW), 0), _iota2((GW, GW), 1)
        self.block_diag = (r >> 6) == (c >> 6)


def _head_block_diag(x, w):
    t = jnp.concatenate([x] * GROUP_HEADS, axis=0)
    return jnp.where(w.block_diag, t, jnp.zeros_like(t))


def _collapse_heads(full, w):
    out = jnp.zeros((HEAD_DIM, GW), F32)
    for h in range(GROUP_HEADS):
        out = out + jnp.where(w.head == h, full[h * HEAD_DIM:(h + 1) * HEAD_DIM, :], 0.0)
    return out


def _lhs3(a, precise=False):
    if not precise:
        return None, a.astype(BF16)
    hi, lo = _split2(a)
    return jnp.concatenate([hi, lo], axis=0), hi


def _rhs3(b, w=None, precise=False):
    if not precise:
        hi = b.astype(BF16)
        return (hi if w is None else _head_block_diag(hi, w)), None
    hi, lo = _split2(b)
    if w is None:
        return hi, lo
    return _head_block_diag(hi, w), _head_block_diag(lo, w)


def _mm3(lhs, rhs, dims=NN):
    cat, hi = lhs
    bh, bl = rhs
    if cat is None or bl is None:
        return _dot(hi, bh, dims)
    m = hi.shape[0]
    r = _dot(cat, bh, dims)
    return r[:m] + r[m:] + _dot(hi, bl, dims)


def _mm3_tn(a, b, precise=False):
    if not precise:
        return _dot(a.astype(BF16), b.astype(BF16), TN)
    ah, al = _split2(a)
    bh, bl = _split2(b)
    m = a.shape[1]
    r = _dot(jnp.concatenate([ah, al], axis=1), bh, TN)
    return r[:m] + r[m:] + _dot(ah, bl, TN)


def _mm_x01(a, m01, pieces=2):
    m = a.shape[0]
    r = _dot(jnp.concatenate(_split3(a) if pieces == 3 else _split2(a), axis=0), m01)
    return sum(r[i * m:(i + 1) * m] for i in range(pieces))


def _tri_mm(tri01, x, pieces=2):
    n = x.shape[1]
    r = _dot(tri01, jnp.concatenate(_split3(x) if pieces == 3 else _split2(x), axis=1))
    return sum(r[:, i * n:(i + 1) * n] for i in range(pieces))


def _unit_lower_inverse_wide(mats, w):
    eye = w.eye.astype(F32)
    same8 = (w.i >> 3) == (w.j >> 3)
    ds = [jnp.where(same8, a, 0.0) for a in mats]
    d2 = [_mm3(_lhs3(d), _rhs3(d, w)) for d in ds]
    d2_r = [_rhs3(x, w) for x in d2]
    d4 = [_mm3(_lhs3(x), r) for x, r in zip(d2, d2_r)]
    imd = [eye - d for d in ds]
    p1 = [i + _mm3(_lhs3(i), r) for i, r in zip(imd, d2_r)]
    ts = [p + _mm3(_lhs3(p), _rhs3(x, w)) for p, x in zip(p1, d4)]
    for s in range(3, 6):
        big = (w.i >> (s + 1)) == (w.j >> (s + 1))
        small = (w.i >> s) == (w.j >> s)
        sel = jnp.logical_and(big, jnp.logical_not(small))
        inner = [_mm3(_lhs3(jnp.where(sel, a, 0.0)), _rhs3(t, w)) for a, t in zip(mats, ts)]
        ts = [t - _mm3(_lhs3(t), _rhs3(x, w)) for t, x in zip(ts, inner)]
    return ts


def _params(n_axes):
    return pltpu.CompilerParams(dimension_semantics=("arbitrary",) * n_axes,
                                vmem_limit_bytes=VMEM_LIMIT)


def _heads_to_wide(s):
    b = s.shape[0]
    return s.transpose(0, 2, 1, 3).reshape(b, HEAD_DIM, GW)


def _wide_to_heads(s):
    b = s.shape[0]
    return s.reshape(b, HEAD_DIM, GROUP_HEADS, HEAD_DIM).transpose(0, 2, 1, 3)


def _ones_block_diag():
    r, c = jnp.arange(GW)[:, None], jnp.arange(GW)[None, :]
    return ((r // HEAD_DIM) == (c // HEAD_DIM)).astype(BF16)


def _lane_spread(col0):
    r, c = jnp.arange(H_SMALL)[:, None], jnp.arange(GW)[None, :]
    return ((r - col0) == (c // HEAD_DIM)).astype(BF16)


def _inproj_kernel(x_ref, wm_ref, ws_ref, wkv_ref, h_ref, hs_ref, kv_ref):
    x = x_ref[...]
    xb = x.astype(BF16)
    h_ref[...] = _dot(xb, wm_ref[...])
    hs_ref[...] = _mmh(x, ws_ref[...])
    kv_ref[...] = _dot(wkv_ref[...], xb, NT)


def _inproj(x3, w_main, w_small, w_kv_t):
    bsz, l, _ = x3.shape
    tm = min(l, 512)
    const = lambda shape: pl.BlockSpec(shape, lambda bi, i: (0, 0))
    return pl.pallas_call(
        _inproj_kernel,
        grid=(bsz, l // tm),
        in_specs=[pl.BlockSpec((None, tm, D_MODEL), lambda bi, i: (bi, i, 0)),
                  const((D_MODEL, H_MAIN)), const((D_MODEL, H_SMALL)), const((KV_ROWS, D_MODEL))],
        out_specs=[pl.BlockSpec((None, tm, H_MAIN), lambda bi, i: (bi, i, 0)),
                   pl.BlockSpec((None, tm, H_SMALL), lambda bi, i: (bi, i, 0)),
                   pl.BlockSpec((None, KV_ROWS, tm), lambda bi, i: (bi, 0, i))],
        out_shape=[jax.ShapeDtypeStruct((bsz, l, H_MAIN), F32), jax.ShapeDtypeStruct((bsz, l, H_SMALL), F32),
                   jax.ShapeDtypeStruct((bsz, KV_ROWS, l), F32)],
        compiler_params=_params(2),
        name="inproj",
    )(x3, w_main, w_small, w_kv_t)


_HALO = 32
_SUBLANES = 8


def _conv_kernel(val_ref, gate_ref, buf_ref, cw_ref, cb_ref, g_ref, b_ref, o_ref, nbuf_ref, win_ref, sh_ref, *, tl):
    i = pl.program_id(1)
    pad = _HALO - (CONV_WIDTH - 1)
    span = tl + _HALO - _SUBLANES

    @pl.when(i == 0)
    def _():
        win_ref[0:8, :] = jnp.zeros((8, GW), F32)
        win_ref[pad:_HALO, :] = buf_ref[...]

    u = val_ref[...] * _sigmoid(gate_ref[...])
    win_ref[_HALO:_HALO + tl, :] = u
    for s in range(1, _SUBLANES):
        sh_ref[s, :, :] = win_ref[s:s + span, :]
    sub = min(tl, 64)
    for r0 in range(0, tl, sub):
        acc = jnp.zeros((sub, GW), F32)
        for j in range(CONV_WIDTH):
            s, base = (pad + j) % _SUBLANES, (pad + j) // _SUBLANES * _SUBLANES + r0
            rows = win_ref[base:base + sub, :] if s == 0 else sh_ref[s, base:base + sub, :]
            acc = acc + cw_ref[j:j + 1, :] * rows
        y = _layer_norm(acc + cb_ref[...], g_ref[...], b_ref[...])
        o_ref[r0:r0 + sub, :] = _silu(y)
    tail = win_ref[tl:tl + _HALO, :]
    win_ref[0:_HALO, :] = tail

    @pl.when(i == pl.num_programs(1) - 1)
    def _():
        nbuf_ref[...] = win_ref[pad:_HALO, :]


def _conv_mixer(h3, buf, cw, cb, g, b):
    bsz, l, _ = h3.shape
    tl = min(l, 256)
    kern = functools.partial(_conv_kernel, tl=tl)
    full = lambda shape: pl.BlockSpec(shape, lambda bi, i: (0,) * len(shape))
    return pl.pallas_call(
        kern,
        grid=(bsz, l // tl),
        in_specs=[pl.BlockSpec((None, tl, GW), lambda bi, i: (bi, i, H_AVAL // GW)),
                  pl.BlockSpec((None, tl, GW), lambda bi, i: (bi, i, H_AGATE // GW)),
                  pl.BlockSpec((None, CONV_WIDTH - 1, GW), lambda bi, i: (bi, 0, 0)),
                  full((CONV_WIDTH, GW)), full((1, GW)), full((1, GW)), full((1, GW))],
        out_specs=[pl.BlockSpec((None, tl, GW), lambda bi, i: (bi, i, 0)),
                   pl.BlockSpec((None, CONV_WIDTH - 1, GW), lambda bi, i: (bi, 0, 0))],
        out_shape=[jax.ShapeDtypeStruct((bsz, l, GW), F32),
                   jax.ShapeDtypeStruct((bsz, CONV_WIDTH - 1, GW), F32)],
        scratch_shapes=[pltpu.VMEM((tl + _HALO, GW), F32),
                        pltpu.VMEM((_SUBLANES, tl + _HALO - _SUBLANES, GW), F32)],
        compiler_params=_params(2),
        name="conv_mixer",
    )(h3, h3, buf, cw, cb, g, b)


def _gdn_kernel(qkv_ref, z_ref, hs_ref, buf_ref, s0_ref, cw_ref, alog_ref, dtb_ref, nw_ref, ones_ref, sela_ref,
                selb_ref, o_ref, sout_ref, win_ref, s_scr, *, nb, nc):
    c = pl.program_id(1)
    n = CHUNK
    rows = nc * n
    kw = GDN_CONV_WIDTH - 1
    items = [(b, ci) for b in range(nb) for ci in range(nc)]

    @pl.when(c == 0)
    def _():
        for b in range(nb):
            win_ref[b, 0:8, :] = jnp.zeros((8, 3 * GW), F32)
            win_ref[b, 8 - kw:8, :] = buf_ref[b]
        s_scr[...] = s0_ref[...]

    qkvs = []
    for b in range(nb):
        win_ref[b, 8:8 + rows, :] = qkv_ref[b]
        conv = jnp.zeros((rows, 3 * GW), F32)
        for j in range(GDN_CONV_WIDTH):
            conv = conv + cw_ref[j:j + 1, :] * win_ref[b, 8 - kw + j:8 - kw + j + rows, :]
        tail = win_ref[b, rows:rows + 8, :]
        win_ref[b, 0:8, :] = tail
        qkvs.append(_silu(conv))

    w = _Wide()
    ones_bd = ones_ref[...]
    tril = _tril_ones(n, BF16)
    sl = lambda ci: slice(ci * n, (ci + 1) * n)
    qs = [qkvs[b][sl(ci), 0:GW] for b, ci in items]
    ks = [qkvs[b][sl(ci), GW:2 * GW] for b, ci in items]
    vs = [qkvs[b][sl(ci), 2 * GW:3 * GW] for b, ci in items]
    sss = [_mm_x01(jnp.concatenate([q * q, k * k], axis=0), ones_bd) for q, k in zip(qs, ks)]
    qs = [q * lax.rsqrt(ss[:n] + 1e-6) * (HEAD_DIM ** -0.5) for q, ss in zip(qs, sss)]
    ks = [k * lax.rsqrt(ss[n:] + 1e-6) for k, ss in zip(ks, sss)]

    hss = [hs_ref[b, sl(ci), :] for b, ci in items]
    gs = [_mm_x01(-jnp.exp(alog_ref[...]) * _softplus(hs + dtb_ref[...]), sela_ref[...]) for hs in hss]
    betas = [_mm_x01(_sigmoid(hs), selb_ref[...]) for hs in hss]
    gcs = [_tri_mm(tril, g) for g in gs]
    grs = [jnp.sum(jnp.where(w.eye, gc, 0.0), axis=0, keepdims=True) for gc in gcs]
    decays = [jnp.exp(jnp.where(w.incl, gc - gr, NEG)) for gc, gr in zip(gcs, grs)]

    kbs = [k * beta for k, beta in zip(ks, betas)]
    grams = [_mm3(_lhs3(jnp.concatenate([kb, q], axis=0)), _rhs3(k, w), NT)
             for kb, q, k in zip(kbs, qs, ks)]
    amats = [jnp.where(w.strict, gram[:n] * decay, 0.0) for gram, decay in zip(grams, decays)]
    qks = [jnp.where(w.incl, gram[n:] * decay, 0.0) for gram, decay in zip(grams, decays)]
    t_ls = [_lhs3(t) for t in _unit_lower_inverse_wide(amats, w)]
    egs = [jnp.exp(gc) for gc in gcs]
    xvs = [_mm3(t_l, _rhs3(v * beta, w)) for t_l, v, beta in zip(t_ls, vs, betas)]
    xks = [_mm3(t_l, _rhs3(kb * eg, w)) for t_l, kb, eg in zip(t_ls, kbs, egs)]
    xq_ls = [_lhs3(jnp.concatenate([xk, q * eg], axis=0)) for xk, q, eg in zip(xks, qs, egs)]
    qk_ls = [_lhs3(qk) for qk in qks]
    gls = [gc[n - 1:n, :] for gc in gcs]
    kds = [k * jnp.exp(gl - gc) for k, gl, gc in zip(ks, gls, gcs)]

    states = [s_scr[b] for b in range(nb)]
    outs = {}
    for ci in range(nc):
        for b in range(nb):
            it = b * nc + ci
            s = states[b]
            rs = _mm3(xq_ls[it], _rhs3(s, w))
            u = xvs[it] - rs[:n]
            outs[it] = rs[n:] + _mm3(qk_ls[it], _rhs3(u, w))
            states[b] = s * jnp.exp(gls[it]) + _collapse_heads(_mm3_tn(kds[it], u), w)
    for b in range(nb):
        s_scr[b] = states[b]

    for it, (b, ci) in enumerate(items):
        o = outs[it]
        ms = _mm_x01(o * o, ones_bd) * (1.0 / HEAD_DIM)
        o_ref[b, sl(ci), :] = o * lax.rsqrt(ms + 1e-6) * nw_ref[...] * _silu(z_ref[b, sl(ci), :])

    @pl.when(c == pl.num_programs(1) - 1)
    def _():
        sout_ref[...] = s_scr[...]


def _recurrent_tiling(bsz, l):
    nc = min(l // CHUNK, 4)
    nb = 2 if nc > 1 else min(bsz, 4)
    return nb, nc


def _gdn_mixer(h3, hs3, buf, s0, cw, a_log, dt_bias, norm_w):
    bsz, l, _ = h3.shape
    nb, nc = _recurrent_tiling(bsz, l)
    rows = nc * CHUNK
    pad_r = lambda v, off: jnp.zeros((1, H_SMALL), F32).at[0, off:off + GROUP_HEADS].set(v)
    full = lambda shape: pl.BlockSpec(shape, lambda bi, i: (0,) * len(shape))
    o, s_new = pl.pallas_call(
        functools.partial(_gdn_kernel, nb=nb, nc=nc),
        grid=(bsz // nb, l // rows),
        in_specs=[pl.BlockSpec((nb, rows, 3 * GW), lambda bi, i: (bi, i, H_BQKV // (3 * GW))),
                  pl.BlockSpec((nb, rows, GW), lambda bi, i: (bi, i, H_BZ // GW)),
                  pl.BlockSpec((nb, rows, H_SMALL), lambda bi, i: (bi, i, 0)),
                  pl.BlockSpec((nb, GDN_CONV_WIDTH - 1, 3 * GW), lambda bi, i: (bi, 0, 0)),
                  pl.BlockSpec((nb, HEAD_DIM, GW), lambda bi, i: (bi, 0, 0)),
                  full((GDN_CONV_WIDTH, 3 * GW)), full((1, H_SMALL)), full((1, H_SMALL)), full((1, GW)),
                  full((GW, GW)), full((H_SMALL, GW)), full((H_SMALL, GW))],
        out_specs=[pl.BlockSpec((nb, rows, GW), lambda bi, i: (bi, i, 0)),
                   pl.BlockSpec((nb, HEAD_DIM, GW), lambda bi, i: (bi, 0, 0))],
        out_shape=[jax.ShapeDtypeStruct((bsz, l, GW), F32),
                   jax.ShapeDtypeStruct((bsz, HEAD_DIM, GW), F32)],
        scratch_shapes=[pltpu.VMEM((nb, rows + 8, 3 * GW), F32), pltpu.VMEM((nb, HEAD_DIM, GW), F32)],
        compiler_params=_params(2),
        name="gdn_mixer",
    )(h3, h3, hs3, buf, _heads_to_wide(s0), cw, pad_r(a_log, S_A), pad_r(dt_bias, S_A),
      jnp.tile(norm_w, GROUP_HEADS)[None, :], _ones_block_diag(), _lane_spread(S_A), _lane_spread(S_B))
    return o, _wide_to_heads(s_new)


def _rwkv_kernel(rkv_ref, hs_ref, sh_ref, shs_ref, s0_ref, mu_ref, mus_ref, w0_ref, wup_ref, a0_ref, aup_ref,
                 gup_ref, kk_ref, ka_ref, rk_ref, lng_ref, lnb_ref, ones_ref, o_ref, sout_ref, win_ref, wins_ref,
                 s_scr, *, nb, nc):
    c = pl.program_id(1)
    n = CHUNK

    rows = nc * n
    items = [(b, ci) for b in range(nb) for ci in range(nc)]

    @pl.when(c == 0)
    def _():
        for b in range(nb):
            win_ref[b, 0:8, :] = jnp.zeros((8, 3 * GW), F32)
            wins_ref[b, 0:8, :] = jnp.zeros((8, H_SMALL), F32)
            win_ref[b, 7:8, :] = sh_ref[b]
            wins_ref[b, 7:8, :] = shs_ref[b]
        s_scr[...] = s0_ref[...]

    wup_r, aup_r, gup_r = (_rhs3(r[...], precise=True) for r in (wup_ref, aup_ref, gup_ref))
    per_b = []
    for b in range(nb):
        x = rkv_ref[b]
        xs = hs_ref[b]
        win_ref[b, 8:8 + rows, :] = x
        wins_ref[b, 8:8 + rows, :] = xs
        xm = x + (win_ref[b, 7:7 + rows, :] - x) * mu_ref[...]
        xms = xs + (wins_ref[b, 7:7 + rows, :] - xs) * mus_ref[...]
        last = win_ref[b, rows + 7:rows + 8, :]
        win_ref[b, 7:8, :] = last
        lasts = wins_ref[b, rows + 7:rows + 8, :]
        wins_ref[b, 7:8, :] = lasts
        w_pre = w0_ref[...] + _mm3(_lhs3(jnp.tanh(xms), precise=True), wup_r)
        logw = -jnp.exp(-_softplus(-w_pre) - 0.5)
        a_sig = _sigmoid(a0_ref[...] + _mm3(_lhs3(xms, precise=True), aup_r))
        gate = _mm3(_lhs3(_sigmoid(xms), precise=True), gup_r)
        kx = xm[:, GW:2 * GW]
        per_b.append(dict(rr=xm[:, 0:GW], vv=xm[:, 2 * GW:3 * GW], logw=logw, a_sig=a_sig, gate=gate,
                          kkp=kx * kk_ref[...], k2=kx * (1.0 + (a_sig - 1.0) * ka_ref[...])))

    w = _Wide()
    ones_bd = ones_ref[...]
    tril = _tril_ones(n, BF16)
    sl = lambda ci: slice(ci * n, (ci + 1) * n)
    get = lambda name: [per_b[b][name][sl(ci), :] for b, ci in items]
    rrs, vvs, logws, a_sigs, gates, kkps, k2s = (get(k) for k in ('rr', 'vv', 'logw', 'a_sig', 'gate', 'kkp', 'k2'))
    kks = [kkp * lax.rsqrt(_mm_x01(kkp * kkp, ones_bd) + 1e-6) for kkp in kkps]
    cums = [_tri_mm(tril, logw) for logw in logws]
    w_incls = [jnp.exp(cum) for cum in cums]
    w_lasts = [wi[n - 1:n, :] for wi in w_incls]
    w_invs = [jnp.exp(-cum) for cum in cums]
    ats = [-kk * jnp.exp(cum - logw) for kk, cum, logw in zip(kks, cums, logws)]
    bts = [kk * a_sig * wv for kk, a_sig, wv in zip(kks, a_sigs, w_invs)]
    kts = [k2 * wv for k2, wv in zip(k2s, w_invs)]
    rts = [rr * wi for rr, wi in zip(rrs, w_incls)]
    ar_ls = [_lhs3(jnp.concatenate([at, rt], axis=0)) for at, rt in zip(ats, rts)]
    gbs = [_mm3(ar_l, _rhs3(bt, w), NT) for ar_l, bt in zip(ar_ls, bts)]
    gks = [_mm3(ar_l, _rhs3(kt, w), NT) for ar_l, kt in zip(ar_ls, kts)]
    t_ls = [_lhs3(t) for t in _unit_lower_inverse_wide([jnp.where(w.strict, -gb[:n], 0.0) for gb in gbs], w)]
    arb_ls = [_lhs3(jnp.where(w.incl, gb[n:], 0.0)) for gb in gbs]
    ark_ls = [_lhs3(jnp.where(w.incl, gk[n:], 0.0)) for gk in gks]
    v_rs = [_rhs3(vv, w) for vv in vvs]
    aakvs = [_mm3(_lhs3(jnp.where(w.strict, gk[:n], 0.0)), v_r) for gk, v_r in zip(gks, v_rs)]
    arkvs = [_mm3(ark_l, v_r) for ark_l, v_r in zip(ark_ls, v_rs)]
    bks = [jnp.concatenate([bt * wl, kt * wl], axis=0) for bt, kt, wl in zip(bts, kts, w_lasts)]

    states = [s_scr[b] for b in range(nb)]
    ys = {}
    for ci in range(nc):
        for b in range(nb):
            it = b * nc + ci
            s0 = states[b]
            ars = _mm3(ar_ls[it], _rhs3(s0, w), NT)
            u = _mm3(t_ls[it], _rhs3(ars[:n] + aakvs[it], w))
            ys[it] = ars[n:] + _mm3(arb_ls[it], _rhs3(u, w)) + arkvs[it]
            upd = _mm3_tn(jnp.concatenate([u, vvs[it]], axis=0), bks[it])
            states[b] = s0 * w_lasts[it] + _collapse_heads(upd, w)
    for b in range(nb):
        s_scr[b] = states[b]

    inv_d = 1.0 / HEAD_DIM
    for it, (b, ci) in enumerate(items):
        y = ys[it]
        yc = y - _mm_x01(y, ones_bd) * inv_d
        var_y = _mm_x01(yc * yc, ones_bd) * inv_d
        yn = yc * lax.rsqrt(var_y + GN_EPS) * lng_ref[...] + lnb_ref[...]
        bonus = _mm_x01(rrs[it] * k2s[it] * rk_ref[...], ones_bd) * vvs[it]
        o_ref[b, sl(ci), :] = (yn + bonus) * gates[it]

    @pl.when(c == pl.num_programs(1) - 1)
    def _():
        sout_ref[...] = s_scr[...]


def _rwkv_mixer(h3, hs3, shift, s0, lp):
    bsz, l, _ = h3.shape
    nb, nc = _recurrent_tiling(bsz, l)
    rows = nc * CHUNK
    sh_main = shift[:, None, 0:3 * GW]
    sh_small = jnp.pad(shift[:, None, 3 * GW:], ((0, 0), (0, 0), (0, H_SMALL - LORA)))
    mu = lp['rwkv_mu']
    mu_main = mu[None, 0:3 * GW]
    mu_small = jnp.pad(mu[None, 3 * GW:], ((0, 0), (0, H_SMALL - LORA)))
    place = lambda w, off: jnp.zeros((H_SMALL, GW), F32).at[off:off + w.shape[0]].set(w)
    wup = place(lp['rwkv_w_up'], 0)
    aup = place(lp['rwkv_a_up'], RWKV_W_LORA)
    gup = place(lp['rwkv_g_up'], RWKV_W_LORA + RWKV_A_LORA)
    row = lambda v: v.reshape(1, GW)
    full = lambda shape: pl.BlockSpec(shape, lambda bi, i: (0,) * len(shape))
    o, s_new = pl.pallas_call(
        functools.partial(_rwkv_kernel, nb=nb, nc=nc),
        grid=(bsz // nb, l // rows),
        in_specs=[pl.BlockSpec((nb, rows, 3 * GW), lambda bi, i: (bi, i, H_CRKV // (3 * GW))),
                  pl.BlockSpec((nb, rows, H_SMALL), lambda bi, i: (bi, i, 0)),
                  pl.BlockSpec((nb, 1, 3 * GW), lambda bi, i: (bi, 0, 0)),
                  pl.BlockSpec((nb, 1, H_SMALL), lambda bi, i: (bi, 0, 0)),
                  pl.BlockSpec((nb, HEAD_DIM, GW), lambda bi, i: (bi, 0, 0)),
                  full((1, 3 * GW)), full((1, H_SMALL)), full((1, GW)), full((H_SMALL, GW)), full((1, GW)),
                  full((H_SMALL, GW)), full((H_SMALL, GW)), full((1, GW)), full((1, GW)), full((1, GW)),
                  full((1, GW)), full((1, GW)), full((GW, GW))],
        out_specs=[pl.BlockSpec((nb, rows, GW), lambda bi, i: (bi, i, 0)),
                   pl.BlockSpec((nb, HEAD_DIM, GW), lambda bi, i: (bi, 0, 0))],
        out_shape=[jax.ShapeDtypeStruct((bsz, l, GW), F32),
                   jax.ShapeDtypeStruct((bsz, HEAD_DIM, GW), F32)],
        scratch_shapes=[pltpu.VMEM((nb, rows + 8, 3 * GW), F32), pltpu.VMEM((nb, rows + 8, H_SMALL), F32),
                        pltpu.VMEM((nb, HEAD_DIM, GW), F32)],
        compiler_params=_params(2),
        name="rwkv_mixer",
    )(h3, hs3, sh_main, sh_small, _heads_to_wide(s0), mu_main, mu_small, row(lp['rwkv_w0']), wup,
      row(lp['rwkv_a0']), aup, gup, row(lp['rwkv_k_k']), row(lp['rwkv_k_a']), row(lp['rwkv_r_k']),
      row(lp['rwkv_lnx_g']), row(lp['rwkv_lnx_b']), _ones_block_diag())
    return o, _wide_to_heads(s_new)


_FB = 128
_SAMPLE_BK = 512


def _log_forget(hs, bf):
    return -_softplus(-(hs + bf))


def _head_slice(h):
    return slice(h * HEAD_DIM, (h + 1) * HEAD_DIM)


def _attend_heads(qs, kblk, vblk, key_bias, mask, carry):
    m, l, acc = carry
    bq = qs[0].shape[0]
    kb16 = kblk.astype(BF16)
    vb16 = vblk.astype(BF16)
    parts = []
    for h in range(GROUP_HEADS):
        s = _dot(qs[h], kb16[_head_slice(h), :]) + key_bias[h:h + 1, :]
        parts.append(s if mask is None else jnp.where(mask, s, NEG))
    s = jnp.concatenate(parts, axis=0)
    m_new = jnp.maximum(m, jnp.max(s, -1, keepdims=True))
    alpha = jnp.exp(m - m_new)
    p = jnp.exp(s - m_new)
    l = alpha * l + jnp.sum(p, -1, keepdims=True)
    pb = p.astype(BF16)
    pv = jnp.concatenate([_dot(pb[h * bq:(h + 1) * bq, :], vb16[_head_slice(h), :], NT)
                          for h in range(GROUP_HEADS)], axis=0)
    return m_new, l, alpha * acc + pv


def _attend_init(bq):
    rows = GROUP_HEADS * bq
    return jnp.full((rows, 1), NEG, F32), jnp.zeros((rows, 1), F32), jnp.zeros((rows, HEAD_DIM), F32)


def _attend_store(o_ref, carry, bq):
    _, l, acc = carry
    for h in range(GROUP_HEADS):
        o_ref[:, _head_slice(h)] = acc[h * bq:(h + 1) * bq, :] / l[h * bq:(h + 1) * bq, :]


def _sel_rows(sel01, x):
    t = x.shape[0]
    r = _dot(sel01, jnp.concatenate(_split3(x), axis=0), NT)
    return r[:, :t] + r[:, t:2 * t] + r[:, 2 * t:]


def _fox_prompt_kernel(q_ref, kv_ref, hs_ref, bf_ref, o_ref, lf_ref, nf_scr, *, l, bq):
    i = pl.program_id(1)
    per = bq // _FB

    @pl.when(i == 0)
    def _():
        utri = _triu_ones(_FB, BF16)
        sel = _col_selector(8, H_SMALL, S_F).astype(BF16)
        carry = jnp.zeros((8, 1), F32)
        for jb in range(l // _FB):
            rows = slice(jb * _FB, (jb + 1) * _FB)
            lf = _log_forget(hs_ref[rows, :], bf_ref[...])
            lf_ref[rows, :] = lf[:, S_F:S_F + GROUP_HEADS]
            loc = _mm_x01(_sel_rows(sel, lf), utri, 3)
            nf_scr[jb // per, :, (jb % per) * _FB:(jb % per + 1) * _FB] = -(loc + carry)
            carry = carry + loc[:, _FB - 1:_FB]

    q = q_ref[...]
    r_i, c_i = _iota2((bq, bq), 0), _iota2((bq, bq), 1)
    qs = [(q[:, _head_slice(h)] * (HEAD_DIM ** -0.5)).astype(BF16) for h in range(GROUP_HEADS)]

    def step(j, mask, carry):
        keys = pl.ds(pl.multiple_of(j * bq, bq), bq)
        return _attend_heads(qs, kv_ref[0:GW, keys], kv_ref[GW:KV_ROWS, keys], nf_scr[j], mask, carry)

    carry = lax.fori_loop(0, i, lambda j, c: step(j, None, c), _attend_init(bq))
    carry = step(i, c_i <= r_i, carry)
    _attend_store(o_ref, carry, bq)


def _fox_prompt(h3, hs3, kv_t, bf_pad):
    bsz, l, _ = h3.shape
    bq = min(l, 2 * _FB)
    kern = functools.partial(_fox_prompt_kernel, l=l, bq=bq)
    return pl.pallas_call(
        kern,
        grid=(bsz, l // bq),
        in_specs=[pl.BlockSpec((None, bq, GW), lambda bi, i: (bi, i, H_DQ // GW)),
                  pl.BlockSpec((None, KV_ROWS, l), lambda bi, i: (bi, 0, 0)),
                  pl.BlockSpec((None, l, H_SMALL), lambda bi, i: (bi, 0, 0)),
                  pl.BlockSpec((1, H_SMALL), lambda bi, i: (0, 0))],
        out_specs=[pl.BlockSpec((None, bq, GW), lambda bi, i: (bi, i, 0)),
                   pl.BlockSpec((None, l, GROUP_HEADS), lambda bi, i: (bi, 0, 0))],
        out_shape=[jax.ShapeDtypeStruct((bsz, l, GW), F32),
                   jax.ShapeDtypeStruct((bsz, l, GROUP_HEADS), F32)],
        scratch_shapes=[pltpu.VMEM((l // bq, 8, bq), F32)],
        compiler_params=_params(2),
        name="fox_prompt",
    )(h3, kv_t, hs3, bf_pad)


def _fox_sample_kernel(q_ref, kv_ref, hs_ref, bf_ref, ck_ref, cv_ref, clf_ref, o_ref, lf_ref, cum_scr, *, l, p):
    nblk = p // _FB
    per = _SAMPLE_BK // _FB

    loc = _mm_x01(clf_ref[...].reshape(nblk * 8, _FB), _triu_ones(_FB, BF16), 3)
    carry = jnp.zeros((8, 1), F32)
    for jb in range(nblk):
        blk = loc[jb * 8:(jb + 1) * 8, :]
        cum_scr[jb // per, :, (jb % per) * _FB:(jb % per + 1) * _FB] = blk + carry
        carry = carry + blk[:, _FB - 1:_FB]
    total = carry

    lf = _log_forget(hs_ref[...], bf_ref[...])
    lf_ref[...] = lf[:, S_F:S_F + GROUP_HEADS]
    cum_r = _mm_x01(_sel_rows(_col_selector(8, H_SMALL, S_F).astype(BF16), lf), _triu_ones(l, BF16), 3)
    q = q_ref[...]
    r_i, c_i = _iota2((l, l), 0), _iota2((l, l), 1)
    qs = [(q[:, _head_slice(h)] * (HEAD_DIM ** -0.5)).astype(BF16) for h in range(GROUP_HEADS)]

    def body(j, carry):
        keys = pl.ds(pl.multiple_of(j * _SAMPLE_BK, _SAMPLE_BK), _SAMPLE_BK)
        suffix = total - cum_scr[j]
        return _attend_heads(qs, ck_ref[:, keys], cv_ref[:, keys], suffix, None, carry)

    carry = lax.fori_loop(0, p // _SAMPLE_BK, body, _attend_init(l))
    carry = _attend_heads(qs, kv_ref[0:GW, :], kv_ref[GW:KV_ROWS, :], -cum_r, c_i <= r_i, carry)
    _attend_store(o_ref, carry, l)


def _fox_sample(h3, hs3, kv_t, bf_pad, layer, ck_t, cv_t, clogf):
    bsz, l, _ = h3.shape
    p = ck_t.shape[-1]
    clf = clogf.reshape(bsz, p // _FB, _FB, GROUP_HEADS).transpose(0, 1, 3, 2)
    clf = jnp.pad(clf, ((0, 0), (0, 0), (0, 8 - GROUP_HEADS), (0, 0)))
    kern = functools.partial(_fox_sample_kernel, l=l, p=p)
    return pl.pallas_call(
        kern,
        grid=(bsz,),
        in_specs=[pl.BlockSpec((None, l, GW), lambda bi: (bi, 0, H_DQ // GW)),
                  pl.BlockSpec((None, KV_ROWS, l), lambda bi: (bi, 0, 0)),
                  pl.BlockSpec((None, l, H_SMALL), lambda bi: (bi, 0, 0)),
                  pl.BlockSpec((1, H_SMALL), lambda bi: (0, 0)),
                  pl.BlockSpec((None, None, GW, p), lambda bi: (layer, bi, 0, 0)),
                  pl.BlockSpec((None, None, GW, p), lambda bi: (layer, bi, 0, 0)),
                  pl.BlockSpec((None, p // _FB, 8, _FB), lambda bi: (bi, 0, 0, 0))],
        out_specs=[pl.BlockSpec((None, l, GW), lambda bi: (bi, 0, 0)),
                   pl.BlockSpec((None, l, GROUP_HEADS), lambda bi: (bi, 0, 0))],
        out_shape=[jax.ShapeDtypeStruct((bsz, l, GW), F32),
                   jax.ShapeDtypeStruct((bsz, l, GROUP_HEADS), F32)],
        scratch_shapes=[pltpu.VMEM((p // _SAMPLE_BK, 8, _SAMPLE_BK), F32)],
        compiler_params=_params(1),
        name="fox_sample",
    )(h3, kv_t, hs3, bf_pad, ck_t, cv_t, clf)


def _outproj_kernel(oa_ref, ob_ref, oc_ref, od_ref, x_ref, w_ref, g_ref, b_ref, y_ref):
    acc = DN_ALPHA * x_ref[...]
    for gi, ref in enumerate((oa_ref, ob_ref, oc_ref, od_ref)):
        acc = acc + _mm(ref[...], w_ref[gi * GW:(gi + 1) * GW, :])
    y_ref[...] = _layer_norm(acc, g_ref[...], b_ref[...])


def _outproj(oa, ob, oc, od, x2d, w_out, g, b):
    t = x2d.shape[0]
    tm = min(t, 512)
    mix = pl.BlockSpec((tm, GW), lambda i: (i, 0))
    return pl.pallas_call(
        _outproj_kernel,
        grid=(t // tm,),
        in_specs=[mix, mix, mix, mix,
                  pl.BlockSpec((tm, D_MODEL), lambda i: (i, 0)),
                  pl.BlockSpec((D_MODEL, D_MODEL), lambda i: (0, 0)),
                  pl.BlockSpec((1, D_MODEL), lambda i: (0, 0)),
                  pl.BlockSpec((1, D_MODEL), lambda i: (0, 0))],
        out_specs=pl.BlockSpec((tm, D_MODEL), lambda i: (i, 0)),
        out_shape=jax.ShapeDtypeStruct((t, D_MODEL), F32),
        compiler_params=_params(1),
        name="outproj_ln",
    )(oa, ob, oc, od, x2d, w_out, g, b)


_R_G, _R_E = 0, N_EXPERT_GROUPS
_MOE_EXPERTS_PER_STEP = 2


def _moe_kernel(x_ref, wr_ref, br_ref, wg_ref, wu_ref, wd_ref, g_ref, b_ref, y_ref, comb_scr, xb_scr, acc_scr):
    e = pl.program_id(1)
    tm = x_ref.shape[0]
    lane = _iota2((tm, LANE), 1)

    @pl.when(e == 0)
    def _():
        x = x_ref[...]
        xb_scr[...] = x.astype(BF16)
        acc_scr[...] = jnp.zeros_like(acc_scr)
        logits = _mmh(x, wr_ref[...]) + br_ref[...]
        gmask = lane < N_EXPERT_GROUPS
        gl = jnp.where(gmask, logits, NEG)
        ge = jnp.exp(gl - jnp.max(gl, -1, keepdims=True))
        pg = ge / jnp.sum(ge, -1, keepdims=True)
        gp = jnp.max(pg, -1, keepdims=True)
        gi = jnp.min(jnp.where(jnp.logical_and(gmask, pg == gp), lane, LANE), -1, keepdims=True)
        emask = jnp.logical_and(jnp.logical_and(lane >= _R_E, lane < _R_E + N_EXPERTS),
                                ((lane - _R_E) >> 2) == gi)
        el = jnp.where(emask, logits, NEG)
        ee = jnp.exp(el - jnp.max(el, -1, keepdims=True))
        ep = ee / jnp.sum(ee, -1, keepdims=True)
        m1 = jnp.max(jnp.where(emask, ep, -1.0), -1, keepdims=True)
        i1 = jnp.min(jnp.where(jnp.logical_and(emask, ep == m1), lane, LANE), -1, keepdims=True)
        rest = jnp.logical_and(emask, lane != i1)
        m2 = jnp.max(jnp.where(rest, ep, -1.0), -1, keepdims=True)
        i2 = jnp.min(jnp.where(jnp.logical_and(rest, ep == m2), lane, LANE), -1, keepdims=True)
        den = m1 + m2
        comb_scr[...] = jnp.where(lane == i1, gp * m1 / den, jnp.where(lane == i2, gp * m2 / den, 0.0))

    xb = xb_scr[...]
    comb = comb_scr[...]
    hidden = []
    for k in range(_MOE_EXPERTS_PER_STEP):
        ce = jnp.sum(jnp.where(lane == e * _MOE_EXPERTS_PER_STEP + k + _R_E, comb, 0.0), -1, keepdims=True)
        hh = _silu(_mm(xb, wg_ref[k])) * _mm(xb, wu_ref[k])
        hidden.append((hh * ce).astype(BF16))
    hidden = jnp.concatenate(hidden, axis=1)
    w_down = wd_ref[...].reshape(_MOE_EXPERTS_PER_STEP * D_EXPERT, D_MODEL)
    acc_scr[...] += _mm(hidden, w_down)

    @pl.when(e == pl.num_programs(1) - 1)
    def _():
        y_ref[...] = _layer_norm(DN_ALPHA * x_ref[...] + acc_scr[...], g_ref[...], b_ref[...])


def _moe(x2d, w_router, b_router, layer, wg, wu, wd, g, b):
    t = x2d.shape[0]
    tm = min(t, 1024)
    ne = _MOE_EXPERTS_PER_STEP
    return pl.pallas_call(
        _moe_kernel,
        grid=(t // tm, N_EXPERTS // ne),
        in_specs=[pl.BlockSpec((tm, D_MODEL), lambda i, e: (i, 0)),
                  pl.BlockSpec((D_MODEL, LANE), lambda i, e: (0, 0)),
                  pl.BlockSpec((1, LANE), lambda i, e: (0, 0)),
                  pl.BlockSpec((None, ne, D_MODEL, D_EXPERT), lambda i, e: (layer, e, 0, 0)),
                  pl.BlockSpec((None, ne, D_MODEL, D_EXPERT), lambda i, e: (layer, e, 0, 0)),
                  pl.BlockSpec((None, ne, D_EXPERT, D_MODEL), lambda i, e: (layer, e, 0, 0)),
                  pl.BlockSpec((1, D_MODEL), lambda i, e: (0, 0)),
                  pl.BlockSpec((1, D_MODEL), lambda i, e: (0, 0))],
        out_specs=pl.BlockSpec((tm, D_MODEL), lambda i, e: (i, 0)),
        out_shape=jax.ShapeDtypeStruct((t, D_MODEL), F32),
        scratch_shapes=[pltpu.VMEM((tm, LANE), F32), pltpu.VMEM((tm, D_MODEL), BF16),
                        pltpu.VMEM((tm, D_MODEL), F32)],
        compiler_params=_params(2),
        name="moe_ln",
    )(x2d, w_router, b_router, wg, wu, wd, g, b)


def _prep_layer(lp):
    w = lp['w_in']
    c0 = CONV_COLS
    c1 = c0 + GDN_COLS
    c2 = c1 + RWKV_COLS
    main = jnp.concatenate([w[:, c0:c0 + 3 * GW], w[:, c1:c1 + 3 * GW], w[:, c2:c2 + GW],
                            w[:, 0:GW], w[:, GW:2 * GW], w[:, c0 + 3 * GW:c0 + 4 * GW]], axis=1).astype(BF16)
    kv_t = w[:, c2 + GW:c2 + 3 * GW].T.astype(BF16)
    small = jnp.concatenate([w[:, c1 + 3 * GW:c2], w[:, c0 + 4 * GW:c1], w[:, c2 + 3 * GW:],
                             jnp.zeros((D_MODEL, H_SMALL - LORA - 3 * GROUP_HEADS), F32)], axis=1)
    wr = jnp.concatenate([lp['router_g_w'], lp['router_e_w'],
                          jnp.zeros((D_MODEL, LANE - N_EXPERT_GROUPS - N_EXPERTS), F32)], axis=1)
    br = jnp.concatenate([lp['router_g_b'], lp['router_e_b'],
                          jnp.zeros((LANE - N_EXPERT_GROUPS - N_EXPERTS,), F32)])[None, :]
    bf_pad = jnp.zeros((1, H_SMALL), F32).at[0, S_F:S_F + GROUP_HEADS].set(lp['fox_b_f'])
    return dict(w_main=main, w_small=small, w_kv_t=kv_t, w_out=lp['w_out'].astype(BF16), wr=wr, br=br,
                bf_pad=bf_pad)


def _trunk_layer(x, lp, pp, layer, experts, conv_buf, gdn_buf, gdn_s, rw_shift, rw_s, fox_cache):
    bsz, l, d = x.shape
    x2d = x.reshape(bsz * l, d)
    h3, hs3, kv_t = _inproj(x, pp['w_main'], pp['w_small'], pp['w_kv_t'])

    o_a, new_conv = _conv_mixer(h3, conv_buf, lp['conv_w'], lp['conv_b'][None], lp['conv_ln_g'][None],
                                lp['conv_ln_b'][None])
    o_b, new_gdn_s = _gdn_mixer(h3, hs3, gdn_buf, gdn_s, lp['gdn_conv_w'], lp['gdn_a_log'], lp['gdn_dt_bias'],
                                lp['gdn_norm_w'])
    o_c, new_rw_s = _rwkv_mixer(h3, hs3, rw_shift, rw_s, lp)
    if fox_cache is None:
        o_d, logf = _fox_prompt(h3, hs3, kv_t, pp['bf_pad'])
    else:
        o_d, logf = _fox_sample(h3, hs3, kv_t, pp['bf_pad'], layer, *fox_cache)

    flat = lambda o: o.reshape(bsz * l, GW)
    x1 = _outproj(flat(o_a), flat(o_b), flat(o_c), flat(o_d), x2d, pp['w_out'], lp['ln1_g'][None],
                  lp['ln1_b'][None])
    x2 = _moe(x1, pp['wr'], pp['br'], layer, *experts, lp['ln2_g'][None], lp['ln2_b'][None])

    kw = GDN_CONV_WIDTH - 1
    new_gdn_buf = h3[:, l - kw:, H_BQKV:H_BQKV + 3 * GW]
    new_shift = jnp.concatenate([h3[:, l - 1, H_CRKV:H_CRKV + 3 * GW], hs3[:, l - 1, S_LORA:S_LORA + LORA]], axis=-1)
    return x2.reshape(bsz, l, d), (new_conv, new_gdn_buf, new_gdn_s, new_shift, new_rw_s, kv_t, logf)


def _untranspose_kv(kv_ts):
    bsz, _, l = kv_ts[0].shape

    def gather(lo):
        stacked = jnp.stack([kv[:, lo:lo + GW, :] for kv in kv_ts])
        return stacked.reshape(len(kv_ts), bsz, GROUP_HEADS, HEAD_DIM, l).transpose(0, 1, 4, 2, 3)

    return gather(0), gather(GW)


_LAYER_KEYS = ('w_in', 'conv_w', 'conv_b', 'conv_ln_g', 'conv_ln_b', 'gdn_conv_w', 'gdn_a_log', 'gdn_dt_bias',
               'gdn_norm_w', 'rwkv_mu', 'rwkv_w0', 'rwkv_w_up', 'rwkv_a0', 'rwkv_a_up', 'rwkv_g_up', 'rwkv_k_k',
               'rwkv_k_a', 'rwkv_r_k', 'rwkv_lnx_g', 'rwkv_lnx_b', 'fox_b_f', 'w_out', 'ln1_g', 'ln1_b',
               'router_g_w', 'router_g_b', 'router_e_w', 'router_e_b', 'ln2_g', 'ln2_b')


def kernel(x_prompt, x_sample, cache_fox_k, cache_fox_v, cache_fox_logf, state_conv, state_gdn_conv, state_gdn, state_rwkv_shift, state_rwkv, w_in, conv_w, conv_b, conv_ln_g, conv_ln_b, gdn_conv_w, gdn_a_log, gdn_dt_bias, gdn_norm_w, rwkv_mu, rwkv_w0, rwkv_w_up, rwkv_a0, rwkv_a_up, rwkv_g_up, rwkv_k_k, rwkv_k_a, rwkv_r_k, rwkv_lnx_g, rwkv_lnx_b, fox_b_f, w_out, ln1_g, ln1_b, router_g_w, router_g_b, router_e_w, router_e_b, exp_w_gate, exp_w_up, exp_w_down, ln2_g, ln2_b):
    weights = dict(zip(_LAYER_KEYS, (w_in, conv_w, conv_b, conv_ln_g, conv_ln_b, gdn_conv_w, gdn_a_log,
                                     gdn_dt_bias, gdn_norm_w, rwkv_mu, rwkv_w0, rwkv_w_up, rwkv_a0, rwkv_a_up,
                                     rwkv_g_up, rwkv_k_k, rwkv_k_a, rwkv_r_k, rwkv_lnx_g, rwkv_lnx_b, fox_b_f,
                                     w_out, ln1_g, ln1_b, router_g_w, router_g_b, router_e_w, router_e_b,
                                     ln2_g, ln2_b)))
    experts = (exp_w_gate, exp_w_up, exp_w_down)
    depth, bs, past = cache_fox_k.shape[:3]
    ck_t = cache_fox_k.transpose(0, 1, 3, 4, 2).reshape(depth, bs, GW, past)
    cv_t = cache_fox_v.transpose(0, 1, 3, 4, 2).reshape(depth, bs, GW, past)
    xp, xs = x_prompt, x_sample
    bp = x_prompt.shape[0]
    outs_p = [[] for _ in range(7)]
    outs_s = [[] for _ in range(7)]
    for l in range(DEPTH):
        lp = {k: v[l] for k, v in weights.items()}
        pp = _prep_layer(lp)
        xp, st_p = _trunk_layer(xp, lp, pp, l, experts,
                                jnp.zeros((bp, CONV_WIDTH - 1, GW), F32),
                                jnp.zeros((bp, GDN_CONV_WIDTH - 1, 3 * GW), F32),
                                jnp.zeros((bp, GROUP_HEADS, HEAD_DIM, HEAD_DIM), F32),
                                jnp.zeros((bp, RWKV_COLS), F32),
                                jnp.zeros((bp, GROUP_HEADS, HEAD_DIM, HEAD_DIM), F32),
                                None)
        xs, st_s = _trunk_layer(xs, lp, pp, l, experts, state_conv[l], state_gdn_conv[l], state_gdn[l],
                                state_rwkv_shift[l], state_rwkv[l], (ck_t, cv_t, cache_fox_logf[l]))
        for i in range(7):
            outs_p[i].append(st_p[i])
            outs_s[i].append(st_s[i])

    def assemble(outs):
        conv, gdn_buf, gdn_s, shift, rw_s = (jnp.stack(o) for o in outs[:5])
        k_new, v_new = _untranspose_kv(outs[5])
        return conv, gdn_buf, gdn_s, shift, rw_s, k_new, v_new, jnp.stack(outs[6])

    return (xp, xs, *assemble(outs_p), *assemble(outs_s))
```

```python
import functools

import jax
import jax.numpy as jnp
from jax import lax
from jax.experimental import pallas as pl
from jax.experimental.pallas import tpu as pltpu

F32 = jnp.float32
BF16 = jnp.bfloat16

D_MODEL = 1024
DEPTH = 2
CHUNK = 64
HEAD_DIM = 64
GW = D_MODEL // 4
GROUP_HEADS = GW // HEAD_DIM
CONV_WIDTH = 31
GDN_CONV_WIDTH = 4
RWKV_W_LORA = 16
RWKV_A_LORA = 16
RWKV_G_LORA = 32
N_EXPERT_GROUPS = 4
EXPERTS_PER_GROUP = 4
N_EXPERTS = N_EXPERT_GROUPS * EXPERTS_PER_GROUP
D_EXPERT = 256
DN_ALPHA = (2 * DEPTH) ** 0.25
LN_EPS = 1e-5
GN_EPS = 64e-5

CONV_COLS = 2 * GW
GDN_COLS = 4 * GW + 2 * GROUP_HEADS
RWKV_COLS = 3 * GW + RWKV_W_LORA + RWKV_A_LORA + RWKV_G_LORA
FOX_COLS = 3 * GW + GROUP_HEADS
LORA = RWKV_W_LORA + RWKV_A_LORA + RWKV_G_LORA

H_BQKV, H_CRKV, H_DQ, H_AVAL, H_AGATE, H_BZ = 0, 768, 1536, 1792, 2048, 2304
H_MAIN = 2560
KV_ROWS = 2 * GW
S_LORA, S_A, S_B, S_F = 0, 64, 68, 72
H_SMALL = 128
LANE = 128

VMEM_LIMIT = 56 * 1024 * 1024

NN = (((1,), (0,)), ((), ()))
NT = (((1,), (1,)), ((), ()))
TN = (((0,), (0,)), ((), ()))
NEG = -1e30


def _dot(a, b, dims=NN):
    return lax.dot_general(a, b, dims, preferred_element_type=F32)


def _mm(a, b, dims=NN):
    return _dot(a.astype(BF16), b.astype(BF16), dims)


def _mmh(a, b, dims=NN):
    return lax.dot_general(a, b, dims, preferred_element_type=F32, precision=lax.Precision.HIGHEST)


def _sigmoid(x):
    return 0.5 * (jnp.tanh(0.5 * x) + 1.0)


def _silu(x):
    return x * _sigmoid(x)


def _softplus(x):
    return jnp.maximum(x, 0.0) + jnp.log(1.0 + jnp.exp(-jnp.abs(x)))


def _iota2(shape, dim):
    return lax.broadcasted_iota(jnp.int32, shape, dim)


def _tril_ones(n, dtype=F32):
    r, c = _iota2((n, n), 0), _iota2((n, n), 1)
    return (r >= c).astype(dtype)


def _triu_ones(n, dtype=F32):
    r, c = _iota2((n, n), 0), _iota2((n, n), 1)
    return (r <= c).astype(dtype)


def _col_selector(rows, width, col0):
    r, c = _iota2((rows, width), 0), _iota2((rows, width), 1)
    return (c == r + col0).astype(F32)


def _layer_norm(x, g, b):
    mu = jnp.mean(x, axis=-1, keepdims=True)
    xc = x - mu
    var = jnp.mean(xc * xc, axis=-1, keepdims=True)
    return xc * lax.rsqrt(var + LN_EPS) * g + b


def _split2(x):
    hi = x.astype(BF16)
    lo = (x - hi.astype(F32)).astype(BF16)
    return hi, lo


def _split3(x):
    hi = x.astype(BF16)
    r = x - hi.astype(F32)
    mid = r.astype(BF16)
    lo = (r - mid.astype(F32)).astype(BF16)
    return hi, mid, lo


class _Wide:
    def __init__(self):
        self.i = _iota2((CHUNK, GW), 0)
        lane = _iota2((CHUNK, GW), 1)
        self.j = lane & (HEAD_DIM - 1)
        self.head = lane >> 6
        self.incl = self.i >= self.j
        self.strict = self.i > self.j
        self.eye = self.i == self.j
        r, c = _iota2((GW, GW), 0), _iota2((GW, GW), 1)
        self.block_diag = (r >> 6) == (c >> 6)


def _head_block_diag(x, w):
    t = jnp.concatenate([x] * GROUP_HEADS, axis=0)
    return jnp.where(w.block_diag, t, jnp.zeros_like(t))


def _collapse_heads(full, w):
    out = jnp.zeros((HEAD_DIM, GW), F32)
    for h in range(GROUP_HEADS):
        out = out + jnp.where(w.head == h, full[h * HEAD_DIM:(h + 1) * HEAD_DIM, :], 0.0)
    return out


def _lhs3(a, precise=False):
    if not precise:
        return None, a.astype(BF16)
    hi, lo = _split2(a)
    return jnp.concatenate([hi, lo], axis=0), hi


def _rhs3(b, w=None, precise=False):
    if not precise:
        hi = b.astype(BF16)
        return (hi if w is None else _head_block_diag(hi, w)), None
    hi, lo = _split2(b)
    if w is None:
        return hi, lo
    return _head_block_diag(hi, w), _head_block_diag(lo, w)


def _mm3(lhs, rhs, dims=NN):
    cat, hi = lhs
    bh, bl = rhs
    if cat is None or bl is None:
        return _dot(hi, bh, dims)
    m = hi.shape[0]
    r = _dot(cat, bh, dims)
    return r[:m] + r[m:] + _dot(hi, bl, dims)


def _mm3_tn(a, b, precise=False):
    if not precise:
        return _dot(a.astype(BF16), b.astype(BF16), TN)
    ah, al = _split2(a)
    bh, bl = _split2(b)
    m = a.shape[1]
    r = _dot(jnp.concatenate([ah, al], axis=1), bh, TN)
    return r[:m] + r[m:] + _dot(ah, bl, TN)


def _mm_x01(a, m01, pieces=2):
    m = a.shape[0]
    r = _dot(jnp.concatenate(_split3(a) if pieces == 3 else _split2(a), axis=0), m01)
    return sum(r[i * m:(i + 1) * m] for i in range(pieces))


def _tri_mm(tri01, x, pieces=2):
    n = x.shape[1]
    r = _dot(tri01, jnp.concatenate(_split3(x) if pieces == 3 else _split2(x), axis=1))
    return sum(r[:, i * n:(i + 1) * n] for i in range(pieces))


def _unit_lower_inverse_wide(mats, w):
    eye = w.eye.astype(F32)
    same8 = (w.i >> 3) == (w.j >> 3)
    ds = [jnp.where(same8, a, 0.0) for a in mats]
    d2 = [_mm3(_lhs3(d), _rhs3(d, w)) for d in ds]
    d2_r = [_rhs3(x, w) for x in d2]
    d4 = [_mm3(_lhs3(x), r) for x, r in zip(d2, d2_r)]
    imd = [eye - d for d in ds]
    p1 = [i + _mm3(_lhs3(i), r) for i, r in zip(imd, d2_r)]
    ts = [p + _mm3(_lhs3(p), _rhs3(x, w)) for p, x in zip(p1, d4)]
    for s in range(3, 6):
        big = (w.i >> (s + 1)) == (w.j >> (s + 1))
        small = (w.i >> s) == (w.j >> s)
        sel = jnp.logical_and(big, jnp.logical_not(small))
        inner = [_mm3(_lhs3(jnp.where(sel, a, 0.0)), _rhs3(t, w)) for a, t in zip(mats, ts)]
        ts = [t - _mm3(_lhs3(t), _rhs3(x, w)) for t, x in zip(ts, inner)]
    return ts


def _params(n_axes):
    return pltpu.CompilerParams(dimension_semantics=("arbitrary",) * n_axes,
                                vmem_limit_bytes=VMEM_LIMIT)


def _heads_to_wide(s):
    b = s.shape[0]
    return s.transpose(0, 2, 1, 3).reshape(b, HEAD_DIM, GW)


def _wide_to_heads(s):
    b = s.shape[0]
    return s.reshape(b, HEAD_DIM, GROUP_HEADS, HEAD_DIM).transpose(0, 2, 1, 3)


def _ones_block_diag():
    r, c = jnp.arange(GW)[:, None], jnp.arange(GW)[None, :]
    return ((r // HEAD_DIM) == (c // HEAD_DIM)).astype(BF16)


def _lane_spread(col0):
    r, c = jnp.arange(H_SMALL)[:, None], jnp.arange(GW)[None, :]
    return ((r - col0) == (c // HEAD_DIM)).astype(BF16)


def _inproj_kernel(x_ref, wm_ref, wsh_ref, wsl_ref, wkv_ref, h_ref, hs_ref, kt_ref, vt_ref):
    x = x_ref[...]
    x_hi, x_lo = _split2(x)
    tm = x.shape[0]
    h_ref[...] = _dot(x_hi, wm_ref[...], NT)
    r = _dot(jnp.concatenate([x_hi, x_lo], axis=0), wsh_ref[...], NT)
    hs_ref[...] = r[:tm] + r[tm:] + _dot(x_hi, wsl_ref[...], NT)
    kv = _dot(wkv_ref[...], x_hi, NT)
    kt_ref[...] = kv[:GW]
    vt_ref[...] = kv[GW:]


def _inproj(x3, w_main_t, w_small_t, w_kv_t):
    bsz, l, _ = x3.shape
    tm = min(l, 512)
    const = lambda shape: pl.BlockSpec(shape, lambda bi, i: (0, 0))
    ws_hi = w_small_t.astype(BF16)
    ws_lo = (w_small_t - ws_hi.astype(F32)).astype(BF16)
    kv_spec = pl.BlockSpec((None, GW, tm), lambda bi, i: (bi, 0, i))
    return pl.pallas_call(
        _inproj_kernel,
        grid=(bsz, l // tm),
        in_specs=[pl.BlockSpec((None, tm, D_MODEL), lambda bi, i: (bi, i, 0)),
                  const((H_MAIN, D_MODEL)), const((H_SMALL, D_MODEL)), const((H_SMALL, D_MODEL)),
                  const((KV_ROWS, D_MODEL))],
        out_specs=[pl.BlockSpec((None, tm, H_MAIN), lambda bi, i: (bi, i, 0)),
                   pl.BlockSpec((None, tm, H_SMALL), lambda bi, i: (bi, i, 0)), kv_spec, kv_spec],
        out_shape=[jax.ShapeDtypeStruct((bsz, l, H_MAIN), F32), jax.ShapeDtypeStruct((bsz, l, H_SMALL), F32),
                   jax.ShapeDtypeStruct((bsz, GW, l), F32), jax.ShapeDtypeStruct((bsz, GW, l), F32)],
        compiler_params=_params(2),
        name="inproj",
    )(x3, w_main_t, ws_hi, ws_lo, w_kv_t)


_HALO = 32
_SUBLANES = 8


def _conv_kernel(val_ref, gate_ref, buf_ref, cw_ref, cb_ref, g_ref, b_ref, o_ref, nbuf_ref, win_ref, sh_ref, *, tl):
    i = pl.program_id(1)
    pad = _HALO - (CONV_WIDTH - 1)
    span = tl + _HALO - _SUBLANES

    @pl.when(i == 0)
    def _():
        win_ref[0:8, :] = jnp.zeros((8, GW), F32)
        win_ref[pad:_HALO, :] = buf_ref[...]

    u = val_ref[...] * _sigmoid(gate_ref[...])
    win_ref[_HALO:_HALO + tl, :] = u
    for s in range(1, _SUBLANES):
        sh_ref[s, :, :] = win_ref[s:s + span, :]
    sub = min(tl, 64)
    for r0 in range(0, tl, sub):
        acc = jnp.zeros((sub, GW), F32)
        for j in range(CONV_WIDTH):
            s, base = (pad + j) % _SUBLANES, (pad + j) // _SUBLANES * _SUBLANES + r0
            rows = win_ref[base:base + sub, :] if s == 0 else sh_ref[s, base:base + sub, :]
            acc = acc + cw_ref[j:j + 1, :] * rows
        y = _layer_norm(acc + cb_ref[...], g_ref[...], b_ref[...])
        o_ref[r0:r0 + sub, :] = _silu(y)
    tail = win_ref[tl:tl + _HALO, :]
    win_ref[0:_HALO, :] = tail

    @pl.when(i == pl.num_programs(1) - 1)
    def _():
        nbuf_ref[...] = win_ref[pad:_HALO, :]


def _conv_mixer(h3, buf, cw, cb, g, b):
    bsz, l, _ = h3.shape
    tl = min(l, 256)
    kern = functools.partial(_conv_kernel, tl=tl)
    full = lambda shape: pl.BlockSpec(shape, lambda bi, i: (0,) * len(shape))
    return pl.pallas_call(
        kern,
        grid=(bsz, l // tl),
        in_specs=[pl.BlockSpec((None, tl, GW), lambda bi, i: (bi, i, H_AVAL // GW)),
                  pl.BlockSpec((None, tl, GW), lambda bi, i: (bi, i, H_AGATE // GW)),
                  pl.BlockSpec((None, CONV_WIDTH - 1, GW), lambda bi, i: (bi, 0, 0)),
                  full((CONV_WIDTH, GW)), full((1, GW)), full((1, GW)), full((1, GW))],
        out_specs=[pl.BlockSpec((None, tl, GW), lambda bi, i: (bi, i, 0)),
                   pl.BlockSpec((None, CONV_WIDTH - 1, GW), lambda bi, i: (bi, 0, 0))],
        out_shape=[jax.ShapeDtypeStruct((bsz, l, GW), F32),
                   jax.ShapeDtypeStruct((bsz, CONV_WIDTH - 1, GW), F32)],
        scratch_shapes=[pltpu.VMEM((tl + _HALO, GW), F32),
                        pltpu.VMEM((_SUBLANES, tl + _HALO - _SUBLANES, GW), F32)],
        compiler_params=_params(2),
        name="conv_mixer",
    )(h3, h3, buf, cw, cb, g, b)


def _gdn_kernel(qkv_ref, z_ref, hs_ref, buf_ref, s0_ref, cw_ref, alog_ref, dtb_ref, nw_ref, ones_ref, sela_ref,
                selb_ref, o_ref, sout_ref, nbuf_ref, win_ref, s_scr, *, nb, nc):
    c = pl.program_id(1)
    n = CHUNK
    rows = nc * n
    kw = GDN_CONV_WIDTH - 1
    items = [(b, ci) for b in range(nb) for ci in range(nc)]

    @pl.when(c == 0)
    def _():
        for b in range(nb):
            win_ref[b, 0:8, :] = jnp.zeros((8, 3 * GW), F32)
            win_ref[b, 8 - kw:8, :] = buf_ref[b]
        s_scr[...] = s0_ref[...]

    qkvs = []
    for b in range(nb):
        win_ref[b, 8:8 + rows, :] = qkv_ref[b]
        conv = jnp.zeros((rows, 3 * GW), F32)
        for j in range(GDN_CONV_WIDTH):
            conv = conv + cw_ref[j:j + 1, :] * win_ref[b, 8 - kw + j:8 - kw + j + rows, :]
        tail = win_ref[b, rows:rows + 8, :]
        win_ref[b, 0:8, :] = tail
        qkvs.append(_silu(conv))

    w = _Wide()
    ones_bd = ones_ref[...]
    tril = _tril_ones(n, BF16)
    sl = lambda ci: slice(ci * n, (ci + 1) * n)
    qs = [qkvs[b][sl(ci), 0:GW] for b, ci in items]
    ks = [qkvs[b][sl(ci), GW:2 * GW] for b, ci in items]
    vs = [qkvs[b][sl(ci), 2 * GW:3 * GW] for b, ci in items]
    sss = [_mm_x01(jnp.concatenate([q * q, k * k], axis=0), ones_bd) for q, k in zip(qs, ks)]
    qs = [q * lax.rsqrt(ss[:n] + 1e-6) * (HEAD_DIM ** -0.5) for q, ss in zip(qs, sss)]
    ks = [k * lax.rsqrt(ss[n:] + 1e-6) for k, ss in zip(ks, sss)]

    hss = [hs_ref[b, sl(ci), :] for b, ci in items]
    gs = [_mm_x01(-jnp.exp(alog_ref[...]) * _softplus(hs + dtb_ref[...]), sela_ref[...]) for hs in hss]
    betas = [_mm_x01(_sigmoid(hs), selb_ref[...]) for hs in hss]
    gcs = [_tri_mm(tril, g) for g in gs]
    grs = [jnp.sum(jnp.where(w.eye, gc, 0.0), axis=0, keepdims=True) for gc in gcs]
    decays = [jnp.exp(jnp.where(w.incl, gc - gr, NEG)) for gc, gr in zip(gcs, grs)]

    kbs = [k * beta for k, beta in zip(ks, betas)]
    grams = [_mm3(_lhs3(jnp.concatenate([kb, q], axis=0)), _rhs3(k, w), NT)
             for kb, q, k in zip(kbs, qs, ks)]
    amats = [jnp.where(w.strict, gram[:n] * decay, 0.0) for gram, decay in zip(grams, decays)]
    qks = [jnp.where(w.incl, gram[n:] * decay, 0.0) for gram, decay in zip(grams, decays)]
    t_ls = [_lhs3(t) for t in _unit_lower_inverse_wide(amats, w)]
    egs = [jnp.exp(gc) for gc in gcs]
    xvs = [_mm3(t_l, _rhs3(v * beta, w)) for t_l, v, beta in zip(t_ls, vs, betas)]
    xks = [_mm3(t_l, _rhs3(kb * eg, w)) for t_l, kb, eg in zip(t_ls, kbs, egs)]
    xq_ls = [_lhs3(jnp.concatenate([xk, q * eg], axis=0)) for xk, q, eg in zip(xks, qs, egs)]
    qk_ls = [_lhs3(qk) for qk in qks]
    gls = [gc[n - 1:n, :] for gc in gcs]
    kds = [k * jnp.exp(gl - gc) for k, gl, gc in zip(ks, gls, gcs)]

    states = [s_scr[b] for b in range(nb)]
    outs = {}
    for ci in range(nc):
        for b in range(nb):
            it = b * nc + ci
            s = states[b]
            rs = _mm3(xq_ls[it], _rhs3(s, w))
            u = xvs[it] - rs[:n]
            outs[it] = rs[n:] + _mm3(qk_ls[it], _rhs3(u, w))
            states[b] = s * jnp.exp(gls[it]) + _collapse_heads(_mm3_tn(kds[it], u), w)
    for b in range(nb):
        s_scr[b] = states[b]

    for it, (b, ci) in enumerate(items):
        o = outs[it]
        ms = _mm_x01(o * o, ones_bd) * (1.0 / HEAD_DIM)
        o_ref[b, sl(ci), :] = o * lax.rsqrt(ms + 1e-6) * nw_ref[...] * _silu(z_ref[b, sl(ci), :])

    @pl.when(c == pl.num_programs(1) - 1)
    def _():
        sout_ref[...] = s_scr[...]
        for b in range(nb):
            nbuf_ref[b] = win_ref[b, 8 - kw:8, :]


def _recurrent_tiling(bsz, l):
    nc = min(l // CHUNK, 4)
    nb = 2 if nc > 1 else min(bsz, 4)
    return nb, nc


def _gdn_mixer(h3, hs3, buf, s0, cw, a_log, dt_bias, norm_w):
    bsz, l, _ = h3.shape
    nb, nc = _recurrent_tiling(bsz, l)
    rows = nc * CHUNK
    pad_r = lambda v, off: jnp.zeros((1, H_SMALL), F32).at[0, off:off + GROUP_HEADS].set(v)
    full = lambda shape: pl.BlockSpec(shape, lambda bi, i: (0,) * len(shape))
    o, s_new, new_buf = pl.pallas_call(
        functools.partial(_gdn_kernel, nb=nb, nc=nc),
        grid=(bsz // nb, l // rows),
        in_specs=[pl.BlockSpec((nb, rows, 3 * GW), lambda bi, i: (bi, i, H_BQKV // (3 * GW))),
                  pl.BlockSpec((nb, rows, GW), lambda bi, i: (bi, i, H_BZ // GW)),
                  pl.BlockSpec((nb, rows, H_SMALL), lambda bi, i: (bi, i, 0)),
                  pl.BlockSpec((nb, GDN_CONV_WIDTH - 1, 3 * GW), lambda bi, i: (bi, 0, 0)),
                  pl.BlockSpec((nb, HEAD_DIM, GW), lambda bi, i: (bi, 0, 0)),
                  full((GDN_CONV_WIDTH, 3 * GW)), full((1, H_SMALL)), full((1, H_SMALL)), full((1, GW)),
                  full((GW, GW)), full((H_SMALL, GW)), full((H_SMALL, GW))],
        out_specs=[pl.BlockSpec((nb, rows, GW), lambda bi, i: (bi, i, 0)),
                   pl.BlockSpec((nb, HEAD_DIM, GW), lambda bi, i: (bi, 0, 0)),
                   pl.BlockSpec((nb, GDN_CONV_WIDTH - 1, 3 * GW), lambda bi, i: (bi, 0, 0))],
        out_shape=[jax.ShapeDtypeStruct((bsz, l, GW), F32),
                   jax.ShapeDtypeStruct((bsz, HEAD_DIM, GW), F32),
                   jax.ShapeDtypeStruct((bsz, GDN_CONV_WIDTH - 1, 3 * GW), F32)],
        scratch_shapes=[pltpu.VMEM((nb, rows + 8, 3 * GW), F32), pltpu.VMEM((nb, HEAD_DIM, GW), F32)],
        compiler_params=_params(2),
        name="gdn_mixer",
    )(h3, h3, hs3, buf, _heads_to_wide(s0), cw, pad_r(a_log, S_A), pad_r(dt_bias, S_A),
      jnp.tile(norm_w, GROUP_HEADS)[None, :], _ones_block_diag(), _lane_spread(S_A), _lane_spread(S_B))
    return o, _wide_to_heads(s_new), new_buf


def _rwkv_kernel(rkv_ref, hs_ref, sh_ref, shs_ref, s0_ref, mu_ref, mus_ref, w0_ref, wup_ref, a0_ref, aup_ref,
                 gup_ref, kk_ref, ka_ref, rk_ref, lng_ref, lnb_ref, ones_ref, o_ref, sout_ref, nsh_ref, nshs_ref,
                 win_ref, wins_ref, s_scr, *, nb, nc):
    c = pl.program_id(1)
    n = CHUNK

    rows = nc * n
    items = [(b, ci) for b in range(nb) for ci in range(nc)]

    @pl.when(c == 0)
    def _():
        for b in range(nb):
            win_ref[b, 0:8, :] = jnp.zeros((8, 3 * GW), F32)
            wins_ref[b, 0:8, :] = jnp.zeros((8, H_SMALL), F32)
            win_ref[b, 7:8, :] = sh_ref[b]
            wins_ref[b, 7:8, :] = shs_ref[b]
        s_scr[...] = s0_ref[...]

    wup_r, aup_r, gup_r = (_rhs3(r[...], precise=True) for r in (wup_ref, aup_ref, gup_ref))
    per_b = []
    for b in range(nb):
        x = rkv_ref[b]
        xs = hs_ref[b]
        win_ref[b, 8:8 + rows, :] = x
        wins_ref[b, 8:8 + rows, :] = xs
        xm = x + (win_ref[b, 7:7 + rows, :] - x) * mu_ref[...]
        xms = xs + (wins_ref[b, 7:7 + rows, :] - xs) * mus_ref[...]
        last = win_ref[b, rows + 7:rows + 8, :]
        win_ref[b, 7:8, :] = last
        lasts = wins_ref[b, rows + 7:rows + 8, :]
        wins_ref[b, 7:8, :] = lasts
        w_pre = w0_ref[...] + _mm3(_lhs3(jnp.tanh(xms), precise=True), wup_r)
        logw = -jnp.exp(-_softplus(-w_pre) - 0.5)
        a_sig = _sigmoid(a0_ref[...] + _mm3(_lhs3(xms, precise=True), aup_r))
        gate = _mm3(_lhs3(_sigmoid(xms), precise=True), gup_r)
        kx = xm[:, GW:2 * GW]
        per_b.append(dict(rr=xm[:, 0:GW], vv=xm[:, 2 * GW:3 * GW], logw=logw, a_sig=a_sig, gate=gate,
                          kkp=kx * kk_ref[...], k2=kx * (1.0 + (a_sig - 1.0) * ka_ref[...])))

    w = _Wide()
    ones_bd = ones_ref[...]
    tril = _tril_ones(n, BF16)
    sl = lambda ci: slice(ci * n, (ci + 1) * n)
    get = lambda name: [per_b[b][name][sl(ci), :] for b, ci in items]
    rrs, vvs, logws, a_sigs, gates, kkps, k2s = (get(k) for k in ('rr', 'vv', 'logw', 'a_sig', 'gate', 'kkp', 'k2'))
    kks = [kkp * lax.rsqrt(_mm_x01(kkp * kkp, ones_bd) + 1e-6) for kkp in kkps]
    cums = [_tri_mm(tril, logw) for logw in logws]
    w_incls = [jnp.exp(cum) for cum in cums]
    w_lasts = [wi[n - 1:n, :] for wi in w_incls]
    w_invs = [jnp.exp(-cum) for cum in cums]
    ats = [-kk * jnp.exp(cum - logw) for kk, cum, logw in zip(kks, cums, logws)]
    bts = [kk * a_sig * wv for kk, a_sig, wv in zip(kks, a_sigs, w_invs)]
    kts = [k2 * wv for k2, wv in zip(k2s, w_invs)]
    rts = [rr * wi for rr, wi in zip(rrs, w_incls)]
    ar_ls = [_lhs3(jnp.concatenate([at, rt], axis=0)) for at, rt in zip(ats, rts)]
    gbs = [_mm3(ar_l, _rhs3(bt, w), NT) for ar_l, bt in zip(ar_ls, bts)]
    gks = [_mm3(ar_l, _rhs3(kt, w), NT) for ar_l, kt in zip(ar_ls, kts)]
    t_ls = [_lhs3(t) for t in _unit_lower_inverse_wide([jnp.where(w.strict, -gb[:n], 0.0) for gb in gbs], w)]
    arb_ls = [_lhs3(jnp.where(w.incl, gb[n:], 0.0)) for gb in gbs]
    ark_ls = [_lhs3(jnp.where(w.incl, gk[n:], 0.0)) for gk in gks]
    v_rs = [_rhs3(vv, w) for vv in vvs]
    aakvs = [_mm3(_lhs3(jnp.where(w.strict, gk[:n], 0.0)), v_r) for gk, v_r in zip(gks, v_rs)]
    arkvs = [_mm3(ark_l, v_r) for ark_l, v_r in zip(ark_ls, v_rs)]
    bks = [jnp.concatenate([bt * wl, kt * wl], axis=0) for bt, kt, wl in zip(bts, kts, w_lasts)]

    states = [s_scr[b] for b in range(nb)]
    ys = {}
    for ci in range(nc):
        for b in range(nb):
            it = b * nc + ci
            s0 = states[b]
            ars = _mm3(ar_ls[it], _rhs3(s0, w), NT)
            u = _mm3(t_ls[it], _rhs3(ars[:n] + aakvs[it], w))
            ys[it] = ars[n:] + _mm3(arb_ls[it], _rhs3(u, w)) + arkvs[it]
            upd = _mm3_tn(jnp.concatenate([u, vvs[it]], axis=0), bks[it])
            states[b] = s0 * w_lasts[it] + _collapse_heads(upd, w)
    for b in range(nb):
        s_scr[b] = states[b]

    inv_d = 1.0 / HEAD_DIM
    for it, (b, ci) in enumerate(items):
        y = ys[it]
        yc = y - _mm_x01(y, ones_bd) * inv_d
        var_y = _mm_x01(yc * yc, ones_bd) * inv_d
        yn = yc * lax.rsqrt(var_y + GN_EPS) * lng_ref[...] + lnb_ref[...]
        bonus = _mm_x01(rrs[it] * k2s[it] * rk_ref[...], ones_bd) * vvs[it]
        o_ref[b, sl(ci), :] = (yn + bonus) * gates[it]

    @pl.when(c == pl.num_programs(1) - 1)
    def _():
        sout_ref[...] = s_scr[...]
        for b in range(nb):
            nsh_ref[b] = win_ref[b, 7:8, :]
            nshs_ref[b] = wins_ref[b, 7:8, :]


def _rwkv_mixer(h3, hs3, shift, s0, lp):
    bsz, l, _ = h3.shape
    nb, nc = _recurrent_tiling(bsz, l)
    rows = nc * CHUNK
    sh_main = shift[:, None, 0:3 * GW]
    sh_small = jnp.pad(shift[:, None, 3 * GW:], ((0, 0), (0, 0), (0, H_SMALL - LORA)))
    mu = lp['rwkv_mu']
    mu_main = mu[None, 0:3 * GW]
    mu_small = jnp.pad(mu[None, 3 * GW:], ((0, 0), (0, H_SMALL - LORA)))
    place = lambda w, off: jnp.zeros((H_SMALL, GW), F32).at[off:off + w.shape[0]].set(w)
    wup = place(lp['rwkv_w_up'], 0)
    aup = place(lp['rwkv_a_up'], RWKV_W_LORA)
    gup = place(lp['rwkv_g_up'], RWKV_W_LORA + RWKV_A_LORA)
    row = lambda v: v.reshape(1, GW)
    full = lambda shape: pl.BlockSpec(shape, lambda bi, i: (0,) * len(shape))
    o, s_new, sh_new, shs_new = pl.pallas_call(
        functools.partial(_rwkv_kernel, nb=nb, nc=nc),
        grid=(bsz // nb, l // rows),
        in_specs=[pl.BlockSpec((nb, rows, 3 * GW), lambda bi, i: (bi, i, H_CRKV // (3 * GW))),
                  pl.BlockSpec((nb, rows, H_SMALL), lambda bi, i: (bi, i, 0)),
                  pl.BlockSpec((nb, 1, 3 * GW), lambda bi, i: (bi, 0, 0)),
                  pl.BlockSpec((nb, 1, H_SMALL), lambda bi, i: (bi, 0, 0)),
                  pl.BlockSpec((nb, HEAD_DIM, GW), lambda bi, i: (bi, 0, 0)),
                  full((1, 3 * GW)), full((1, H_SMALL)), full((1, GW)), full((H_SMALL, GW)), full((1, GW)),
                  full((H_SMALL, GW)), full((H_SMALL, GW)), full((1, GW)), full((1, GW)), full((1, GW)),
                  full((1, GW)), full((1, GW)), full((GW, GW))],
        out_specs=[pl.BlockSpec((nb, rows, GW), lambda bi, i: (bi, i, 0)),
                   pl.BlockSpec((nb, HEAD_DIM, GW), lambda bi, i: (bi, 0, 0)),
                   pl.BlockSpec((nb, 1, 3 * GW), lambda bi, i: (bi, 0, 0)),
                   pl.BlockSpec((nb, 1, H_SMALL), lambda bi, i: (bi, 0, 0))],
        out_shape=[jax.ShapeDtypeStruct((bsz, l, GW), F32),
                   jax.ShapeDtypeStruct((bsz, HEAD_DIM, GW), F32),
                   jax.ShapeDtypeStruct((bsz, 1, 3 * GW), F32),
                   jax.ShapeDtypeStruct((bsz, 1, H_SMALL), F32)],
        scratch_shapes=[pltpu.VMEM((nb, rows + 8, 3 * GW), F32), pltpu.VMEM((nb, rows + 8, H_SMALL), F32),
                        pltpu.VMEM((nb, HEAD_DIM, GW), F32)],
        compiler_params=_params(2),
        name="rwkv_mixer",
    )(h3, hs3, sh_main, sh_small, _heads_to_wide(s0), mu_main, mu_small, row(lp['rwkv_w0']), wup,
      row(lp['rwkv_a0']), aup, gup, row(lp['rwkv_k_k']), row(lp['rwkv_k_a']), row(lp['rwkv_r_k']),
      row(lp['rwkv_lnx_g']), row(lp['rwkv_lnx_b']), _ones_block_diag())
    new_shift = jnp.concatenate([sh_new[:, 0, :], shs_new[:, 0, S_LORA:S_LORA + LORA]], axis=-1)
    return o, _wide_to_heads(s_new), new_shift


_FB = 128
_SAMPLE_BK = 512


def _log_forget(hs, bf):
    return -_softplus(-(hs + bf))


def _head_slice(h):
    return slice(h * HEAD_DIM, (h + 1) * HEAD_DIM)


def _attend_heads(qs, kblk, vblk, key_bias, mask, carry):
    m, l, acc = carry
    bq = qs[0].shape[0]
    kb16 = kblk.astype(BF16)
    vb16 = vblk.astype(BF16)
    parts = []
    for h in range(GROUP_HEADS):
        s = _dot(qs[h], kb16[_head_slice(h), :]) + key_bias[h:h + 1, :]
        parts.append(s if mask is None else jnp.where(mask, s, NEG))
    s = jnp.concatenate(parts, axis=0)
    m_new = jnp.maximum(m, jnp.max(s, -1, keepdims=True))
    alpha = jnp.exp(m - m_new)
    p = jnp.exp(s - m_new)
    l = alpha * l + jnp.sum(p, -1, keepdims=True)
    pb = p.astype(BF16)
    pv = jnp.concatenate([_dot(pb[h * bq:(h + 1) * bq, :], vb16[_head_slice(h), :], NT)
                          for h in range(GROUP_HEADS)], axis=0)
    return m_new, l, alpha * acc + pv


def _attend_init(bq):
    rows = GROUP_HEADS * bq
    return jnp.full((rows, 1), NEG, F32), jnp.zeros((rows, 1), F32), jnp.zeros((rows, HEAD_DIM), F32)


def _attend_store(o_ref, carry, bq):
    _, l, acc = carry
    for h in range(GROUP_HEADS):
        o_ref[:, _head_slice(h)] = acc[h * bq:(h + 1) * bq, :] / l[h * bq:(h + 1) * bq, :]


def _sel_rows(sel01, x):
    t = x.shape[0]
    r = _dot(sel01, jnp.concatenate(_split3(x), axis=0), NT)
    return r[:, :t] + r[:, t:2 * t] + r[:, 2 * t:]


def _fox_prompt_kernel(q_ref, kt_ref, vt_ref, hs_ref, bf_ref, o_ref, lf_ref, nf_scr, *, l, bq):
    i = pl.program_id(1)
    per = bq // _FB

    @pl.when(i == 0)
    def _():
        utri = _triu_ones(_FB, BF16)
        sel = _col_selector(8, H_SMALL, S_F).astype(BF16)
        carry = jnp.zeros((8, 1), F32)
        for jb in range(l // _FB):
            rows = slice(jb * _FB, (jb + 1) * _FB)
            lf = _log_forget(hs_ref[rows, :], bf_ref[...])
            lf_ref[rows, :] = lf[:, S_F:S_F + GROUP_HEADS]
            loc = _mm_x01(_sel_rows(sel, lf), utri, 3)
            nf_scr[jb // per, :, (jb % per) * _FB:(jb % per + 1) * _FB] = -(loc + carry)
            carry = carry + loc[:, _FB - 1:_FB]

    q = q_ref[...]
    r_i, c_i = _iota2((bq, bq), 0), _iota2((bq, bq), 1)
    qs = [(q[:, _head_slice(h)] * (HEAD_DIM ** -0.5)).astype(BF16) for h in range(GROUP_HEADS)]

    def step(j, mask, carry):
        keys = pl.ds(pl.multiple_of(j * bq, bq), bq)
        return _attend_heads(qs, kt_ref[:, keys], vt_ref[:, keys], nf_scr[j], mask, carry)

    carry = lax.fori_loop(0, i, lambda j, c: step(j, None, c), _attend_init(bq))
    carry = step(i, c_i <= r_i, carry)
    _attend_store(o_ref, carry, bq)


def _fox_prompt(h3, hs3, k_t, v_t, bf_pad):
    bsz, l, _ = h3.shape
    bq = min(l, 2 * _FB)
    kern = functools.partial(_fox_prompt_kernel, l=l, bq=bq)
    return pl.pallas_call(
        kern,
        grid=(bsz, l // bq),
        in_specs=[pl.BlockSpec((None, bq, GW), lambda bi, i: (bi, i, H_DQ // GW)),
                  pl.BlockSpec((None, GW, l), lambda bi, i: (bi, 0, 0)),
                  pl.BlockSpec((None, GW, l), lambda bi, i: (bi, 0, 0)),
                  pl.BlockSpec((None, l, H_SMALL), lambda bi, i: (bi, 0, 0)),
                  pl.BlockSpec((1, H_SMALL), lambda bi, i: (0, 0))],
        out_specs=[pl.BlockSpec((None, bq, GW), lambda bi, i: (bi, i, 0)),
                   pl.BlockSpec((None, l, GROUP_HEADS), lambda bi, i: (bi, 0, 0))],
        out_shape=[jax.ShapeDtypeStruct((bsz, l, GW), F32),
                   jax.ShapeDtypeStruct((bsz, l, GROUP_HEADS), F32)],
        scratch_shapes=[pltpu.VMEM((l // bq, 8, bq), F32)],
        compiler_params=_params(2),
        name="fox_prompt",
    )(h3, k_t, v_t, hs3, bf_pad)


def _fox_sample_kernel(q_ref, kt_ref, vt_ref, hs_ref, bf_ref, ck_ref, cv_ref, clf_ref, o_ref, lf_ref, cum_scr,
                       *, l, p):
    nblk = p // _FB
    per = _SAMPLE_BK // _FB

    loc = _mm_x01(clf_ref[...].reshape(nblk * 8, _FB), _triu_ones(_FB, BF16), 3)
    carry = jnp.zeros((8, 1), F32)
    for jb in range(nblk):
        blk = loc[jb * 8:(jb + 1) * 8, :]
        cum_scr[jb // per, :, (jb % per) * _FB:(jb % per + 1) * _FB] = blk + carry
        carry = carry + blk[:, _FB - 1:_FB]
    total = carry

    lf = _log_forget(hs_ref[...], bf_ref[...])
    lf_ref[...] = lf[:, S_F:S_F + GROUP_HEADS]
    cum_r = _mm_x01(_sel_rows(_col_selector(8, H_SMALL, S_F).astype(BF16), lf), _triu_ones(l, BF16), 3)
    q = q_ref[...]
    r_i, c_i = _iota2((l, l), 0), _iota2((l, l), 1)
    qs = [(q[:, _head_slice(h)] * (HEAD_DIM ** -0.5)).astype(BF16) for h in range(GROUP_HEADS)]

    def body(j, carry):
        keys = pl.ds(pl.multiple_of(j * _SAMPLE_BK, _SAMPLE_BK), _SAMPLE_BK)
        suffix = total - cum_scr[j]
        return _attend_heads(qs, ck_ref[:, keys], cv_ref[:, keys], suffix, None, carry)

    carry = lax.fori_loop(0, p // _SAMPLE_BK, body, _attend_init(l))
    carry = _attend_heads(qs, kt_ref[...], vt_ref[...], -cum_r, c_i <= r_i, carry)
    _attend_store(o_ref, carry, l)


def _fox_sample(h3, hs3, k_t, v_t, bf_pad, layer, ck_t, cv_t, clogf):
    bsz, l, _ = h3.shape
    p = ck_t.shape[-1]
    clf = clogf.reshape(bsz, p // _FB, _FB, GROUP_HEADS).transpose(0, 1, 3, 2)
    clf = jnp.pad(clf, ((0, 0), (0, 0), (0, 8 - GROUP_HEADS), (0, 0)))
    kern = functools.partial(_fox_sample_kernel, l=l, p=p)
    return pl.pallas_call(
        kern,
        grid=(bsz,),
        in_specs=[pl.BlockSpec((None, l, GW), lambda bi: (bi, 0, H_DQ // GW)),
                  pl.BlockSpec((None, GW, l), lambda bi: (bi, 0, 0)),
                  pl.BlockSpec((None, GW, l), lambda bi: (bi, 0, 0)),
                  pl.BlockSpec((None, l, H_SMALL), lambda bi: (bi, 0, 0)),
                  pl.BlockSpec((1, H_SMALL), lambda bi: (0, 0)),
                  pl.BlockSpec((None, None, GW, p), lambda bi: (layer, bi, 0, 0)),
                  pl.BlockSpec((None, None, GW, p), lambda bi: (layer, bi, 0, 0)),
                  pl.BlockSpec((None, p // _FB, 8, _FB), lambda bi: (bi, 0, 0, 0))],
        out_specs=[pl.BlockSpec((None, l, GW), lambda bi: (bi, 0, 0)),
                   pl.BlockSpec((None, l, GROUP_HEADS), lambda bi: (bi, 0, 0))],
        out_shape=[jax.ShapeDtypeStruct((bsz, l, GW), F32),
                   jax.ShapeDtypeStruct((bsz, l, GROUP_HEADS), F32)],
        scratch_shapes=[pltpu.VMEM((p // _SAMPLE_BK, 8, _SAMPLE_BK), F32)],
        compiler_params=_params(1),
        name="fox_sample",
    )(h3, k_t, v_t, hs3, bf_pad, ck_t, cv_t, clf)


def _outproj_kernel(oa_ref, ob_ref, oc_ref, od_ref, x_ref, w_ref, g_ref, b_ref, y_ref):
    acc = DN_ALPHA * x_ref[...]
    for gi, ref in enumerate((oa_ref, ob_ref, oc_ref, od_ref)):
        acc = acc + _mm(ref[...], w_ref[gi * GW:(gi + 1) * GW, :])
    y_ref[...] = _layer_norm(acc, g_ref[...], b_ref[...])


def _outproj(oa, ob, oc, od, x2d, layer, w_out, g, b):
    t = x2d.shape[0]
    tm = min(t, 512)
    mix = pl.BlockSpec((tm, GW), lambda i: (i, 0))
    return pl.pallas_call(
        _outproj_kernel,
        grid=(t // tm,),
        in_specs=[mix, mix, mix, mix,
                  pl.BlockSpec((tm, D_MODEL), lambda i: (i, 0)),
                  pl.BlockSpec((None, D_MODEL, D_MODEL), lambda i: (layer, 0, 0)),
                  pl.BlockSpec((1, D_MODEL), lambda i: (0, 0)),
                  pl.BlockSpec((1, D_MODEL), lambda i: (0, 0))],
        out_specs=pl.BlockSpec((tm, D_MODEL), lambda i: (i, 0)),
        out_shape=jax.ShapeDtypeStruct((t, D_MODEL), F32),
        compiler_params=_params(1),
        name="outproj_ln",
    )(oa, ob, oc, od, x2d, w_out, g, b)


_R_G, _R_E = 0, N_EXPERT_GROUPS
_MOE_EXPERTS_PER_STEP = 2


def _moe_kernel(x_ref, wr_ref, br_ref, wg_ref, wu_ref, wd_ref, g_ref, b_ref, y_ref, comb_scr, xb_scr, acc_scr):
    e = pl.program_id(1)
    tm = x_ref.shape[0]
    lane = _iota2((tm, LANE), 1)

    @pl.when(e == 0)
    def _():
        x = x_ref[...]
        xb_scr[...] = x.astype(BF16)
        acc_scr[...] = jnp.zeros_like(acc_scr)
        logits = _mmh(x, wr_ref[...]) + br_ref[...]
        gmask = lane < N_EXPERT_GROUPS
        gl = jnp.where(gmask, logits, NEG)
        ge = jnp.exp(gl - jnp.max(gl, -1, keepdims=True))
        pg = ge / jnp.sum(ge, -1, keepdims=True)
        gp = jnp.max(pg, -1, keepdims=True)
        gi = jnp.min(jnp.where(jnp.logical_and(gmask, pg == gp), lane, LANE), -1, keepdims=True)
        emask = jnp.logical_and(jnp.logical_and(lane >= _R_E, lane < _R_E + N_EXPERTS),
                                ((lane - _R_E) >> 2) == gi)
        el = jnp.where(emask, logits, NEG)
        ee = jnp.exp(el - jnp.max(el, -1, keepdims=True))
        ep = ee / jnp.sum(ee, -1, keepdims=True)
        m1 = jnp.max(jnp.where(emask, ep, -1.0), -1, keepdims=True)
        i1 = jnp.min(jnp.where(jnp.logical_and(emask, ep == m1), lane, LANE), -1, keepdims=True)
        rest = jnp.logical_and(emask, lane != i1)
        m2 = jnp.max(jnp.where(rest, ep, -1.0), -1, keepdims=True)
        i2 = jnp.min(jnp.where(jnp.logical_and(rest, ep == m2), lane, LANE), -1, keepdims=True)
        den = m1 + m2
        comb_scr[...] = jnp.where(lane == i1, gp * m1 / den, jnp.where(lane == i2, gp * m2 / den, 0.0))

    xb = xb_scr[...]
    comb = comb_scr[...]
    hidden = []
    for k in range(_MOE_EXPERTS_PER_STEP):
        ce = jnp.sum(jnp.where(lane == e * _MOE_EXPERTS_PER_STEP + k + _R_E, comb, 0.0), -1, keepdims=True)
        hh = _silu(_mm(xb, wg_ref[k])) * _mm(xb, wu_ref[k])
        hidden.append((hh * ce).astype(BF16))
    hidden = jnp.concatenate(hidden, axis=1)
    w_down = wd_ref[...].reshape(_MOE_EXPERTS_PER_STEP * D_EXPERT, D_MODEL)
    acc_scr[...] += _mm(hidden, w_down)

    @pl.when(e == pl.num_programs(1) - 1)
    def _():
        y_ref[...] = _layer_norm(DN_ALPHA * x_ref[...] + acc_scr[...], g_ref[...], b_ref[...])


def _moe(x2d, w_router, b_router, layer, wg, wu, wd, g, b):
    t = x2d.shape[0]
    tm = min(t, 1024)
    ne = _MOE_EXPERTS_PER_STEP
    return pl.pallas_call(
        _moe_kernel,
        grid=(t // tm, N_EXPERTS // ne),
        in_specs=[pl.BlockSpec((tm, D_MODEL), lambda i, e: (i, 0)),
                  pl.BlockSpec((D_MODEL, LANE), lambda i, e: (0, 0)),
                  pl.BlockSpec((1, LANE), lambda i, e: (0, 0)),
                  pl.BlockSpec((None, ne, D_MODEL, D_EXPERT), lambda i, e: (layer, e, 0, 0)),
                  pl.BlockSpec((None, ne, D_MODEL, D_EXPERT), lambda i, e: (layer, e, 0, 0)),
                  pl.BlockSpec((None, ne, D_EXPERT, D_MODEL), lambda i, e: (layer, e, 0, 0)),
                  pl.BlockSpec((1, D_MODEL), lambda i, e: (0, 0)),
                  pl.BlockSpec((1, D_MODEL), lambda i, e: (0, 0))],
        out_specs=pl.BlockSpec((tm, D_MODEL), lambda i, e: (i, 0)),
        out_shape=jax.ShapeDtypeStruct((t, D_MODEL), F32),
        scratch_shapes=[pltpu.VMEM((tm, LANE), F32), pltpu.VMEM((tm, D_MODEL), BF16),
                        pltpu.VMEM((tm, D_MODEL), F32)],
        compiler_params=_params(2),
        name="moe_ln",
    )(x2d, w_router, b_router, wg, wu, wd, g, b)


def _prep_layer(lp):
    w = lp['w_in_t']
    c0 = CONV_COLS
    c1 = c0 + GDN_COLS
    c2 = c1 + RWKV_COLS
    main = jnp.concatenate([w[c0:c0 + 3 * GW], w[c1:c1 + 3 * GW], w[c2:c2 + GW],
                            w[0:GW], w[GW:2 * GW], w[c0 + 3 * GW:c0 + 4 * GW]], axis=0).astype(BF16)
    kv_t = w[c2 + GW:c2 + 3 * GW].astype(BF16)
    small = jnp.concatenate([w[c1 + 3 * GW:c2], w[c0 + 4 * GW:c1], w[c2 + 3 * GW:],
                             jnp.zeros((H_SMALL - LORA - 3 * GROUP_HEADS, D_MODEL), F32)], axis=0)
    wr = jnp.concatenate([lp['router_g_w'], lp['router_e_w'],
                          jnp.zeros((D_MODEL, LANE - N_EXPERT_GROUPS - N_EXPERTS), F32)], axis=1)
    br = jnp.concatenate([lp['router_g_b'], lp['router_e_b'],
                          jnp.zeros((LANE - N_EXPERT_GROUPS - N_EXPERTS,), F32)])[None, :]
    bf_pad = jnp.zeros((1, H_SMALL), F32).at[0, S_F:S_F + GROUP_HEADS].set(lp['fox_b_f'])
    return dict(w_main=main, w_small=small, w_kv_t=kv_t, wr=wr, br=br, bf_pad=bf_pad)


def _trunk_layer(x, lp, pp, layer, big, conv_buf, gdn_buf, gdn_s, rw_shift, rw_s, fox_cache):
    bsz, l, d = x.shape
    x2d = x.reshape(bsz * l, d)
    h3, hs3, k_t, v_t = _inproj(x, pp['w_main'], pp['w_small'], pp['w_kv_t'])

    o_a, new_conv = _conv_mixer(h3, conv_buf, lp['conv_w'], lp['conv_b'][None], lp['conv_ln_g'][None],
                                lp['conv_ln_b'][None])
    o_b, new_gdn_s, new_gdn_buf = _gdn_mixer(h3, hs3, gdn_buf, gdn_s, lp['gdn_conv_w'], lp['gdn_a_log'],
                                             lp['gdn_dt_bias'], lp['gdn_norm_w'])
    o_c, new_rw_s, new_shift = _rwkv_mixer(h3, hs3, rw_shift, rw_s, lp)
    if fox_cache is None:
        o_d, logf = _fox_prompt(h3, hs3, k_t, v_t, pp['bf_pad'])
    else:
        o_d, logf = _fox_sample(h3, hs3, k_t, v_t, pp['bf_pad'], layer, *fox_cache)

    flat = lambda o: o.reshape(bsz * l, GW)
    x1 = _outproj(flat(o_a), flat(o_b), flat(o_c), flat(o_d), x2d, layer, big['w_out'], lp['ln1_g'][None],
                  lp['ln1_b'][None])
    x2 = _moe(x1, pp['wr'], pp['br'], layer, *big['experts'], lp['ln2_g'][None], lp['ln2_b'][None])
    return x2.reshape(bsz, l, d), (new_conv, new_gdn_buf, new_gdn_s, new_shift, new_rw_s, k_t, v_t, logf)


def _untranspose(xs_t):
    bsz, _, l = xs_t[0].shape
    return jnp.stack(xs_t).reshape(len(xs_t), bsz, GROUP_HEADS, HEAD_DIM, l).transpose(0, 1, 4, 2, 3)


_LAYER_KEYS = ('w_in_t', 'conv_w', 'conv_b', 'conv_ln_g', 'conv_ln_b', 'gdn_conv_w', 'gdn_a_log', 'gdn_dt_bias',
               'gdn_norm_w', 'rwkv_mu', 'rwkv_w0', 'rwkv_w_up', 'rwkv_a0', 'rwkv_a_up', 'rwkv_g_up', 'rwkv_k_k',
               'rwkv_k_a', 'rwkv_r_k', 'rwkv_lnx_g', 'rwkv_lnx_b', 'fox_b_f', 'ln1_g', 'ln1_b',
               'router_g_w', 'router_g_b', 'router_e_w', 'router_e_b', 'ln2_g', 'ln2_b')


def kernel(x_prompt, x_sample, cache_fox_k, cache_fox_v, cache_fox_logf, state_conv, state_gdn_conv, state_gdn, state_rwkv_shift, state_rwkv, w_in, conv_w, conv_b, conv_ln_g, conv_ln_b, gdn_conv_w, gdn_a_log, gdn_dt_bias, gdn_norm_w, rwkv_mu, rwkv_w0, rwkv_w_up, rwkv_a0, rwkv_a_up, rwkv_g_up, rwkv_k_k, rwkv_k_a, rwkv_r_k, rwkv_lnx_g, rwkv_lnx_b, fox_b_f, w_out, ln1_g, ln1_b, router_g_w, router_g_b, router_e_w, router_e_b, exp_w_gate, exp_w_up, exp_w_down, ln2_g, ln2_b):
    w_in_t = w_in.transpose(0, 2, 1)
    weights = dict(zip(_LAYER_KEYS, (w_in_t, conv_w, conv_b, conv_ln_g, conv_ln_b, gdn_conv_w, gdn_a_log,
                                     gdn_dt_bias, gdn_norm_w, rwkv_mu, rwkv_w0, rwkv_w_up, rwkv_a0, rwkv_a_up,
                                     rwkv_g_up, rwkv_k_k, rwkv_k_a, rwkv_r_k, rwkv_lnx_g, rwkv_lnx_b, fox_b_f,
                                     ln1_g, ln1_b, router_g_w, router_g_b, router_e_w, router_e_b,
                                     ln2_g, ln2_b)))
    big = dict(w_out=w_out, experts=(exp_w_gate, exp_w_up, exp_w_down))
    depth, bs, past = cache_fox_k.shape[:3]
    ck_t = cache_fox_k.transpose(0, 1, 3, 4, 2).reshape(depth, bs, GW, past)
    cv_t = cache_fox_v.transpose(0, 1, 3, 4, 2).reshape(depth, bs, GW, past)
    xp, xs = x_prompt, x_sample
    bp = x_prompt.shape[0]
    outs_p = [[] for _ in range(8)]
    outs_s = [[] for _ in range(8)]
    for l in range(DEPTH):
        lp = {k: v[l] for k, v in weights.items()}
        pp = _prep_layer(lp)
        xp, st_p = _trunk_layer(xp, lp, pp, l, big,
                                jnp.zeros((bp, CONV_WIDTH - 1, GW), F32),
                                jnp.zeros((bp, GDN_CONV_WIDTH - 1, 3 * GW), F32),
                                jnp.zeros((bp, GROUP_HEADS, HEAD_DIM, HEAD_DIM), F32),
                                jnp.zeros((bp, RWKV_COLS), F32),
                                jnp.zeros((bp, GROUP_HEADS, HEAD_DIM, HEAD_DIM), F32),
                                None)
        xs, st_s = _trunk_layer(xs, lp, pp, l, big, state_conv[l], state_gdn_conv[l], state_gdn[l],
                                state_rwkv_shift[l], state_rwkv[l], (ck_t, cv_t, cache_fox_logf[l]))
        for i in range(8):
            outs_p[i].append(st_p[i])
            outs_s[i].append(st_s[i])

    def assemble(outs):
        conv, gdn_buf, gdn_s, shift, rw_s = (jnp.stack(o) for o in outs[:5])
        return conv, gdn_buf, gdn_s, shift, rw_s, _untranspose(outs[5]), _untranspose(outs[6]), jnp.stack(outs[7])

    return (xp, xs, *assemble(outs_p), *assemble(outs_s))
```

```python
import functools

import jax
import jax.numpy as jnp
from jax import lax
from jax.experimental import pallas as pl
from jax.experimental.pallas import tpu as pltpu

F32 = jnp.float32
BF16 = jnp.bfloat16

D_MODEL = 1024
DEPTH = 2
CHUNK = 64
HEAD_DIM = 64
GW = D_MODEL // 4
GROUP_HEADS = GW // HEAD_DIM
CONV_WIDTH = 31
GDN_CONV_WIDTH = 4
RWKV_W_LORA = 16
RWKV_A_LORA = 16
RWKV_G_LORA = 32
N_EXPERT_GROUPS = 4
EXPERTS_PER_GROUP = 4
N_EXPERTS = N_EXPERT_GROUPS * EXPERTS_PER_GROUP
D_EXPERT = 256
DN_ALPHA = (2 * DEPTH) ** 0.25
LN_EPS = 1e-5
GN_EPS = 64e-5

CONV_COLS = 2 * GW
GDN_COLS = 4 * GW + 2 * GROUP_HEADS
RWKV_COLS = 3 * GW + RWKV_W_LORA + RWKV_A_LORA + RWKV_G_LORA
FOX_COLS = 3 * GW + GROUP_HEADS
LORA = RWKV_W_LORA + RWKV_A_LORA + RWKV_G_LORA

H_BQKV, H_CRKV, H_DQ, H_AVAL, H_AGATE, H_BZ = 0, 768, 1536, 1792, 2048, 2304
H_MAIN = 2560
KV_ROWS = 2 * GW
S_LORA, S_A, S_B, S_F = 0, 64, 68, 72
H_SMALL = 128
LANE = 128

VMEM_LIMIT = 56 * 1024 * 1024

NN = (((1,), (0,)), ((), ()))
NT = (((1,), (1,)), ((), ()))
TN = (((0,), (0,)), ((), ()))
NEG = -1e30


def _dot(a, b, dims=NN):
    return lax.dot_general(a, b, dims, preferred_element_type=F32)


def _mm(a, b, dims=NN):
    return _dot(a.astype(BF16), b.astype(BF16), dims)


def _mmh(a, b, dims=NN):
    return lax.dot_general(a, b, dims, preferred_element_type=F32, precision=lax.Precision.HIGHEST)


def _sigmoid(x):
    return 0.5 * (jnp.tanh(0.5 * x) + 1.0)


def _silu(x):
    return x * _sigmoid(x)


def _softplus(x):
    return jnp.maximum(x, 0.0) + jnp.log(1.0 + jnp.exp(-jnp.abs(x)))


def _iota2(shape, dim):
    return lax.broadcasted_iota(jnp.int32, shape, dim)


def _tril_ones(n, dtype=F32):
    r, c = _iota2((n, n), 0), _iota2((n, n), 1)
    return (r >= c).astype(dtype)


def _triu_ones(n, dtype=F32):
    r, c = _iota2((n, n), 0), _iota2((n, n), 1)
    return (r <= c).astype(dtype)


def _col_selector(rows, width, col0):
    r, c = _iota2((rows, width), 0), _iota2((rows, width), 1)
    return (c == r + col0).astype(F32)


def _layer_norm(x, g, b):
    mu = jnp.mean(x, axis=-1, keepdims=True)
    xc = x - mu
    var = jnp.mean(xc * xc, axis=-1, keepdims=True)
    return xc * lax.rsqrt(var + LN_EPS) * g + b


def _split2(x):
    hi = x.astype(BF16)
    lo = (x - hi.astype(F32)).astype(BF16)
    return hi, lo


def _split3(x):
    hi = x.astype(BF16)
    r = x - hi.astype(F32)
    mid = r.astype(BF16)
    lo = (r - mid.astype(F32)).astype(BF16)
    return hi, mid, lo


class _Wide:
    def __init__(self):
        self.i = _iota2((CHUNK, GW), 0)
        lane = _iota2((CHUNK, GW), 1)
        self.j = lane & (HEAD_DIM - 1)
        self.head = lane >> 6
        self.incl = self.i >= self.j
        self.strict = self.i > self.j
        self.eye = self.i == self.j
        r, c = _iota2((GW, GW), 0), _iota2((GW, GW), 1)
        self.block_diag = (r >> 6) == (c >> 6)


def _head_block_diag(x, w):
    t = jnp.concatenate([x] * GROUP_HEADS, axis=0)
    return jnp.where(w.block_diag, t, jnp.zeros_like(t))


def _collapse_heads(full, w):
    out = jnp.zeros((HEAD_DIM, GW), F32)
    for h in range(GROUP_HEADS):
        out = out + jnp.where(w.head == h, full[h * HEAD_DIM:(h + 1) * HEAD_DIM, :], 0.0)
    return out


def _lhs3(a, precise=False):
    if not precise:
        return None, a.astype(BF16)
    hi, lo = _split2(a)
    return jnp.concatenate([hi, lo], axis=0), hi


def _rhs3(b, w=None, precise=False):
    if not precise:
        hi = b.astype(BF16)
        return (hi if w is None else _head_block_diag(hi, w)), None
    hi, lo = _split2(b)
    if w is None:
        return hi, lo
    return _head_block_diag(hi, w), _head_block_diag(lo, w)


def _mm3(lhs, rhs, dims=NN):
    cat, hi = lhs
    bh, bl = rhs
    if cat is None or bl is None:
        return _dot(hi, bh, dims)
    m = hi.shape[0]
    r = _dot(cat, bh, dims)
    return r[:m] + r[m:] + _dot(hi, bl, dims)


def _mm3_tn(a, b, precise=False):
    if not precise:
        return _dot(a.astype(BF16), b.astype(BF16), TN)
    ah, al = _split2(a)
    bh, bl = _split2(b)
    m = a.shape[1]
    r = _dot(jnp.concatenate([ah, al], axis=1), bh, TN)
    return r[:m] + r[m:] + _dot(ah, bl, TN)


def _mm_x01(a, m01, pieces=2):
    m = a.shape[0]
    r = _dot(jnp.concatenate(_split3(a) if pieces == 3 else _split2(a), axis=0), m01)
    return sum(r[i * m:(i + 1) * m] for i in range(pieces))


def _tri_mm(tri01, x, pieces=2):
    n = x.shape[1]
    r = _dot(tri01, jnp.concatenate(_split3(x) if pieces == 3 else _split2(x), axis=1))
    return sum(r[:, i * n:(i + 1) * n] for i in range(pieces))


def _unit_lower_inverse_wide(mats, w):
    eye = w.eye.astype(F32)
    same8 = (w.i >> 3) == (w.j >> 3)
    ds = [jnp.where(same8, a, 0.0) for a in mats]
    d2 = [_mm3(_lhs3(d), _rhs3(d, w)) for d in ds]
    d2_r = [_rhs3(x, w) for x in d2]
    d4 = [_mm3(_lhs3(x), r) for x, r in zip(d2, d2_r)]
    imd = [eye - d for d in ds]
    p1 = [i + _mm3(_lhs3(i), r) for i, r in zip(imd, d2_r)]
    ts = [p + _mm3(_lhs3(p), _rhs3(x, w)) for p, x in zip(p1, d4)]
    for s in range(3, 6):
        big = (w.i >> (s + 1)) == (w.j >> (s + 1))
        small = (w.i >> s) == (w.j >> s)
        sel = jnp.logical_and(big, jnp.logical_not(small))
        inner = [_mm3(_lhs3(jnp.where(sel, a, 0.0)), _rhs3(t, w)) for a, t in zip(mats, ts)]
        ts = [t - _mm3(_lhs3(t), _rhs3(x, w)) for t, x in zip(ts, inner)]
    return ts


def _params(n_axes):
    return pltpu.CompilerParams(dimension_semantics=("arbitrary",) * n_axes,
                                vmem_limit_bytes=VMEM_LIMIT)


def _heads_to_wide(s):
    b = s.shape[0]
    return s.transpose(0, 2, 1, 3).reshape(b, HEAD_DIM, GW)


def _wide_to_heads(s):
    b = s.shape[0]
    return s.reshape(b, HEAD_DIM, GROUP_HEADS, HEAD_DIM).transpose(0, 2, 1, 3)


def _ones_block_diag():
    r, c = jnp.arange(GW)[:, None], jnp.arange(GW)[None, :]
    return ((r // HEAD_DIM) == (c // HEAD_DIM)).astype(BF16)


def _lane_spread(col0):
    r, c = jnp.arange(H_SMALL)[:, None], jnp.arange(GW)[None, :]
    return ((r - col0) == (c // HEAD_DIM)).astype(BF16)


def _inproj_kernel(x_ref, wm_ref, wsh_ref, wsl_ref, wkv_ref, h_ref, hs_ref, kt_ref, vt_ref):
    x = x_ref[...]
    x_hi, x_lo = _split2(x)
    tm = x.shape[0]
    h_ref[...] = _dot(x_hi, wm_ref[...], NT)
    r = _dot(jnp.concatenate([x_hi, x_lo], axis=0), wsh_ref[...], NT)
    hs_ref[...] = r[:tm] + r[tm:] + _dot(x_hi, wsl_ref[...], NT)
    kv = _dot(wkv_ref[...], x_hi, NT)
    kt_ref[...] = kv[:GW]
    vt_ref[...] = kv[GW:]


def _inproj(x3, w_main_t, w_small_t, w_kv_t):
    bsz, l, _ = x3.shape
    tm = min(l, 512)
    const = lambda shape: pl.BlockSpec(shape, lambda bi, i: (0, 0))
    ws_hi = w_small_t.astype(BF16)
    ws_lo = (w_small_t - ws_hi.astype(F32)).astype(BF16)
    kv_spec = pl.BlockSpec((None, GW, tm), lambda bi, i: (bi, 0, i))
    return pl.pallas_call(
        _inproj_kernel,
        grid=(bsz, l // tm),
        in_specs=[pl.BlockSpec((None, tm, D_MODEL), lambda bi, i: (bi, i, 0)),
                  const((H_MAIN, D_MODEL)), const((H_SMALL, D_MODEL)), const((H_SMALL, D_MODEL)),
                  const((KV_ROWS, D_MODEL))],
        out_specs=[pl.BlockSpec((None, tm, H_MAIN), lambda bi, i: (bi, i, 0)),
                   pl.BlockSpec((None, tm, H_SMALL), lambda bi, i: (bi, i, 0)), kv_spec, kv_spec],
        out_shape=[jax.ShapeDtypeStruct((bsz, l, H_MAIN), F32), jax.ShapeDtypeStruct((bsz, l, H_SMALL), F32),
                   jax.ShapeDtypeStruct((bsz, GW, l), F32), jax.ShapeDtypeStruct((bsz, GW, l), F32)],
        compiler_params=_params(2),
        name="inproj",
    )(x3, w_main_t, ws_hi, ws_lo, w_kv_t)


_HALO = 32
_SUBLANES = 8


def _conv_kernel(val_ref, gate_ref, buf_ref, cw_ref, cb_ref, g_ref, b_ref, o_ref, nbuf_ref, win_ref, sh_ref, *, tl):
    i = pl.program_id(1)
    pad = _HALO - (CONV_WIDTH - 1)
    span = tl + _HALO - _SUBLANES

    @pl.when(i == 0)
    def _():
        win_ref[0:8, :] = jnp.zeros((8, GW), F32)
        win_ref[pad:_HALO, :] = buf_ref[...]

    u = val_ref[...] * _sigmoid(gate_ref[...])
    win_ref[_HALO:_HALO + tl, :] = u
    for s in range(1, _SUBLANES):
        sh_ref[s, :, :] = win_ref[s:s + span, :]
    sub = min(tl, 64)
    for r0 in range(0, tl, sub):
        acc = jnp.zeros((sub, GW), F32)
        for j in range(CONV_WIDTH):
            s, base = (pad + j) % _SUBLANES, (pad + j) // _SUBLANES * _SUBLANES + r0
            rows = win_ref[base:base + sub, :] if s == 0 else sh_ref[s, base:base + sub, :]
            acc = acc + cw_ref[j:j + 1, :] * rows
        y = _layer_norm(acc + cb_ref[...], g_ref[...], b_ref[...])
        o_ref[r0:r0 + sub, :] = _silu(y)
    tail = win_ref[tl:tl + _HALO, :]
    win_ref[0:_HALO, :] = tail

    @pl.when(i == pl.num_programs(1) - 1)
    def _():
        nbuf_ref[...] = win_ref[pad:_HALO, :]


def _conv_mixer(h3, buf, cw, cb, g, b):
    bsz, l, _ = h3.shape
    tl = min(l, 256)
    kern = functools.partial(_conv_kernel, tl=tl)
    full = lambda shape: pl.BlockSpec(shape, lambda bi, i: (0,) * len(shape))
    return pl.pallas_call(
        kern,
        grid=(bsz, l // tl),
        in_specs=[pl.BlockSpec((None, tl, GW), lambda bi, i: (bi, i, H_AVAL // GW)),
                  pl.BlockSpec((None, tl, GW), lambda bi, i: (bi, i, H_AGATE // GW)),
                  pl.BlockSpec((None, CONV_WIDTH - 1, GW), lambda bi, i: (bi, 0, 0)),
                  full((CONV_WIDTH, GW)), full((1, GW)), full((1, GW)), full((1, GW))],
        out_specs=[pl.BlockSpec((None, tl, GW), lambda bi, i: (bi, i, 0)),
                   pl.BlockSpec((None, CONV_WIDTH - 1, GW), lambda bi, i: (bi, 0, 0))],
        out_shape=[jax.ShapeDtypeStruct((bsz, l, GW), F32),
                   jax.ShapeDtypeStruct((bsz, CONV_WIDTH - 1, GW), F32)],
        scratch_shapes=[pltpu.VMEM((tl + _HALO, GW), F32),
                        pltpu.VMEM((_SUBLANES, tl + _HALO - _SUBLANES, GW), F32)],
        compiler_params=_params(2),
        name="conv_mixer",
    )(h3, h3, buf, cw, cb, g, b)


def _gdn_kernel(qkv_ref, z_ref, hs_ref, buf_ref, s0_ref, cw_ref, alog_ref, dtb_ref, nw_ref, ones_ref, sela_ref,
                selb_ref, o_ref, sout_ref, nbuf_ref, win_ref, s_scr, *, nb, nc):
    c = pl.program_id(1)
    n = CHUNK
    rows = nc * n
    kw = GDN_CONV_WIDTH - 1
    items = [(b, ci) for b in range(nb) for ci in range(nc)]

    @pl.when(c == 0)
    def _():
        for b in range(nb):
            win_ref[b, 0:8, :] = jnp.zeros((8, 3 * GW), F32)
            win_ref[b, 8 - kw:8, :] = buf_ref[b]
        s_scr[...] = s0_ref[...]

    qkvs = []
    for b in range(nb):
        win_ref[b, 8:8 + rows, :] = qkv_ref[b]
        conv = jnp.zeros((rows, 3 * GW), F32)
        for j in range(GDN_CONV_WIDTH):
            conv = conv + cw_ref[j:j + 1, :] * win_ref[b, 8 - kw + j:8 - kw + j + rows, :]
        tail = win_ref[b, rows:rows + 8, :]
        win_ref[b, 0:8, :] = tail
        qkvs.append(_silu(conv))

    w = _Wide()
    ones_bd = ones_ref[...]
    tril = _tril_ones(n, BF16)
    sl = lambda ci: slice(ci * n, (ci + 1) * n)
    qs = [qkvs[b][sl(ci), 0:GW] for b, ci in items]
    ks = [qkvs[b][sl(ci), GW:2 * GW] for b, ci in items]
    vs = [qkvs[b][sl(ci), 2 * GW:3 * GW] for b, ci in items]
    sss = [_mm_x01(jnp.concatenate([q * q, k * k], axis=0), ones_bd) for q, k in zip(qs, ks)]
    qs = [q * lax.rsqrt(ss[:n] + 1e-6) * (HEAD_DIM ** -0.5) for q, ss in zip(qs, sss)]
    ks = [k * lax.rsqrt(ss[n:] + 1e-6) for k, ss in zip(ks, sss)]

    hss = [hs_ref[b, sl(ci), :] for b, ci in items]
    gs = [_mm_x01(-jnp.exp(alog_ref[...]) * _softplus(hs + dtb_ref[...]), sela_ref[...]) for hs in hss]
    betas = [_mm_x01(_sigmoid(hs), selb_ref[...]) for hs in hss]
    gcs = [_tri_mm(tril, g) for g in gs]
    grs = [jnp.sum(jnp.where(w.eye, gc, 0.0), axis=0, keepdims=True) for gc in gcs]
    decays = [jnp.exp(jnp.where(w.incl, gc - gr, NEG)) for gc, gr in zip(gcs, grs)]

    kbs = [k * beta for k, beta in zip(ks, betas)]
    grams = [_mm3(_lhs3(jnp.concatenate([kb, q], axis=0)), _rhs3(k, w), NT)
             for kb, q, k in zip(kbs, qs, ks)]
    amats = [jnp.where(w.strict, gram[:n] * decay, 0.0) for gram, decay in zip(grams, decays)]
    qks = [jnp.where(w.incl, gram[n:] * decay, 0.0) for gram, decay in zip(grams, decays)]
    t_ls = [_lhs3(t) for t in _unit_lower_inverse_wide(amats, w)]
    egs = [jnp.exp(gc) for gc in gcs]
    xvs = [_mm3(t_l, _rhs3(v * beta, w)) for t_l, v, beta in zip(t_ls, vs, betas)]
    xks = [_mm3(t_l, _rhs3(kb * eg, w)) for t_l, kb, eg in zip(t_ls, kbs, egs)]
    xq_ls = [_lhs3(jnp.concatenate([xk, q * eg], axis=0)) for xk, q, eg in zip(xks, qs, egs)]
    qk_ls = [_lhs3(qk) for qk in qks]
    gls = [gc[n - 1:n, :] for gc in gcs]
    kds = [k * jnp.exp(gl - gc) for k, gl, gc in zip(ks, gls, gcs)]

    states = [s_scr[b] for b in range(nb)]
    outs = {}
    for ci in range(nc):
        for b in range(nb):
            it = b * nc + ci
            s = states[b]
            rs = _mm3(xq_ls[it], _rhs3(s, w))
            u = xvs[it] - rs[:n]
            outs[it] = rs[n:] + _mm3(qk_ls[it], _rhs3(u, w))
            states[b] = s * jnp.exp(gls[it]) + _collapse_heads(_mm3_tn(kds[it], u), w)
    for b in range(nb):
        s_scr[b] = states[b]

    for it, (b, ci) in enumerate(items):
        o = outs[it]
        ms = _mm_x01(o * o, ones_bd) * (1.0 / HEAD_DIM)
        o_ref[b, sl(ci), :] = o * lax.rsqrt(ms + 1e-6) * nw_ref[...] * _silu(z_ref[b, sl(ci), :])

    @pl.when(c == pl.num_programs(1) - 1)
    def _():
        sout_ref[...] = s_scr[...]
        for b in range(nb):
            nbuf_ref[b] = win_ref[b, 8 - kw:8, :]


def _recurrent_tiling(bsz, l):
    nc = min(l // CHUNK, 4)
    nb = 2 if nc > 1 else min(bsz, 4)
    return nb, nc


def _gdn_mixer(h3, hs3, buf, s0, cw, a_log, dt_bias, norm_w):
    bsz, l, _ = h3.shape
    nb, nc = _recurrent_tiling(bsz, l)
    rows = nc * CHUNK
    pad_r = lambda v, off: jnp.zeros((1, H_SMALL), F32).at[0, off:off + GROUP_HEADS].set(v)
    full = lambda shape: pl.BlockSpec(shape, lambda bi, i: (0,) * len(shape))
    o, s_new, new_buf = pl.pallas_call(
        functools.partial(_gdn_kernel, nb=nb, nc=nc),
        grid=(bsz // nb, l // rows),
        in_specs=[pl.BlockSpec((nb, rows, 3 * GW), lambda bi, i: (bi, i, H_BQKV // (3 * GW))),
                  pl.BlockSpec((nb, rows, GW), lambda bi, i: (bi, i, H_BZ // GW)),
                  pl.BlockSpec((nb, rows, H_SMALL), lambda bi, i: (bi, i, 0)),
                  pl.BlockSpec((nb, GDN_CONV_WIDTH - 1, 3 * GW), lambda bi, i: (bi, 0, 0)),
                  pl.BlockSpec((nb, HEAD_DIM, GW), lambda bi, i: (bi, 0, 0)),
                  full((GDN_CONV_WIDTH, 3 * GW)), full((1, H_SMALL)), full((1, H_SMALL)), full((1, GW)),
                  full((GW, GW)), full((H_SMALL, GW)), full((H_SMALL, GW))],
        out_specs=[pl.BlockSpec((nb, rows, GW), lambda bi, i: (bi, i, 0)),
                   pl.BlockSpec((nb, HEAD_DIM, GW), lambda bi, i: (bi, 0, 0)),
                   pl.BlockSpec((nb, GDN_CONV_WIDTH - 1, 3 * GW), lambda bi, i: (bi, 0, 0))],
        out_shape=[jax.ShapeDtypeStruct((bsz, l, GW), F32),
                   jax.ShapeDtypeStruct((bsz, HEAD_DIM, GW), F32),
                   jax.ShapeDtypeStruct((bsz, GDN_CONV_WIDTH - 1, 3 * GW), F32)],
        scratch_shapes=[pltpu.VMEM((nb, rows + 8, 3 * GW), F32), pltpu.VMEM((nb, HEAD_DIM, GW), F32)],
        compiler_params=_params(2),
        name="gdn_mixer",
    )(h3, h3, hs3, buf, _heads_to_wide(s0), cw, pad_r(a_log, S_A), pad_r(dt_bias, S_A),
      jnp.tile(norm_w, GROUP_HEADS)[None, :], _ones_block_diag(), _lane_spread(S_A), _lane_spread(S_B))
    return o, _wide_to_heads(s_new), new_buf


def _rwkv_kernel(rkv_ref, hs_ref, sh_ref, shs_ref, s0_ref, mu_ref, mus_ref, w0_ref, wup_ref, a0_ref, aup_ref,
                 gup_ref, kk_ref, ka_ref, rk_ref, lng_ref, lnb_ref, ones_ref, o_ref, sout_ref, nsh_ref, nshs_ref,
                 win_ref, wins_ref, s_scr, *, nb, nc):
    c = pl.program_id(1)
    n = CHUNK

    rows = nc * n
    items = [(b, ci) for b in range(nb) for ci in range(nc)]

    @pl.when(c == 0)
    def _():
        for b in range(nb):
            win_ref[b, 0:8, :] = jnp.zeros((8, 3 * GW), F32)
            wins_ref[b, 0:8, :] = jnp.zeros((8, H_SMALL), F32)
            win_ref[b, 7:8, :] = sh_ref[b]
            wins_ref[b, 7:8, :] = shs_ref[b]
        s_scr[...] = s0_ref[...]

    wup_r, aup_r, gup_r = (_rhs3(r[...], precise=True) for r in (wup_ref, aup_ref, gup_ref))
    per_b = []
    for b in range(nb):
        x = rkv_ref[b]
        xs = hs_ref[b]
        win_ref[b, 8:8 + rows, :] = x
        wins_ref[b, 8:8 + rows, :] = xs
        xm = x + (win_ref[b, 7:7 + rows, :] - x) * mu_ref[...]
        xms = xs + (wins_ref[b, 7:7 + rows, :] - xs) * mus_ref[...]
        last = win_ref[b, rows + 7:rows + 8, :]
        win_ref[b, 7:8, :] = last
        lasts = wins_ref[b, rows + 7:rows + 8, :]
        wins_ref[b, 7:8, :] = lasts
        w_pre = w0_ref[...] + _mm3(_lhs3(jnp.tanh(xms), precise=True), wup_r)
        logw = -jnp.exp(-_softplus(-w_pre) - 0.5)
        a_sig = _sigmoid(a0_ref[...] + _mm3(_lhs3(xms, precise=True), aup_r))
        gate = _mm3(_lhs3(_sigmoid(xms), precise=True), gup_r)
        kx = xm[:, GW:2 * GW]
        per_b.append(dict(rr=xm[:, 0:GW], vv=xm[:, 2 * GW:3 * GW], logw=logw, a_sig=a_sig, gate=gate,
                          kkp=kx * kk_ref[...], k2=kx * (1.0 + (a_sig - 1.0) * ka_ref[...])))

    w = _Wide()
    ones_bd = ones_ref[...]
    tril = _tril_ones(n, BF16)
    sl = lambda ci: slice(ci * n, (ci + 1) * n)
    get = lambda name: [per_b[b][name][sl(ci), :] for b, ci in items]
    rrs, vvs, logws, a_sigs, gates, kkps, k2s = (get(k) for k in ('rr', 'vv', 'logw', 'a_sig', 'gate', 'kkp', 'k2'))
    kks = [kkp * lax.rsqrt(_mm_x01(kkp * kkp, ones_bd) + 1e-6) for kkp in kkps]
    cums = [_tri_mm(tril, logw) for logw in logws]
    w_incls = [jnp.exp(cum) for cum in cums]
    w_lasts = [wi[n - 1:n, :] for wi in w_incls]
    w_invs = [jnp.exp(-cum) for cum in cums]
    ats = [-kk * jnp.exp(cum - logw) for kk, cum, logw in zip(kks, cums, logws)]
    bts = [kk * a_sig * wv for kk, a_sig, wv in zip(kks, a_sigs, w_invs)]
    kts = [k2 * wv for k2, wv in zip(k2s, w_invs)]
    rts = [rr * wi for rr, wi in zip(rrs, w_incls)]
    ar_ls = [_lhs3(jnp.concatenate([at, rt], axis=0)) for at, rt in zip(ats, rts)]
    gbs = [_mm3(ar_l, _rhs3(bt, w), NT) for ar_l, bt in zip(ar_ls, bts)]
    gks = [_mm3(ar_l, _rhs3(kt, w), NT) for ar_l, kt in zip(ar_ls, kts)]
    t_ls = [_lhs3(t) for t in _unit_lower_inverse_wide([jnp.where(w.strict, -gb[:n], 0.0) for gb in gbs], w)]
    arb_ls = [_lhs3(jnp.where(w.incl, gb[n:], 0.0)) for gb in gbs]
    ark_ls = [_lhs3(jnp.where(w.incl, gk[n:], 0.0)) for gk in gks]
    v_rs = [_rhs3(vv, w) for vv in vvs]
    aakvs = [_mm3(_lhs3(jnp.where(w.strict, gk[:n], 0.0)), v_r) for gk, v_r in zip(gks, v_rs)]
    arkvs = [_mm3(ark_l, v_r) for ark_l, v_r in zip(ark_ls, v_rs)]
    bks = [jnp.concatenate([bt * wl, kt * wl], axis=0) for bt, kt, wl in zip(bts, kts, w_lasts)]

    states = [s_scr[b] for b in range(nb)]
    ys = {}
    for ci in range(nc):
        for b in range(nb):
            it = b * nc + ci
            s0 = states[b]
            ars = _mm3(ar_ls[it], _rhs3(s0, w), NT)
            u = _mm3(t_ls[it], _rhs3(ars[:n] + aakvs[it], w))
            ys[it] = ars[n:] + _mm3(arb_ls[it], _rhs3(u, w)) + arkvs[it]
            upd = _mm3_tn(jnp.concatenate([u, vvs[it]], axis=0), bks[it])
            states[b] = s0 * w_lasts[it] + _collapse_heads(upd, w)
    for b in range(nb):
        s_scr[b] = states[b]

    inv_d = 1.0 / HEAD_DIM
    for it, (b, ci) in enumerate(items):
        y = ys[it]
        yc = y - _mm_x01(y, ones_bd) * inv_d
        var_y = _mm_x01(yc * yc, ones_bd) * inv_d
        yn = yc * lax.rsqrt(var_y + GN_EPS) * lng_ref[...] + lnb_ref[...]
        bonus = _mm_x01(rrs[it] * k2s[it] * rk_ref[...], ones_bd) * vvs[it]
        o_ref[b, sl(ci), :] = (yn + bonus) * gates[it]

    @pl.when(c == pl.num_programs(1) - 1)
    def _():
        sout_ref[...] = s_scr[...]
        for b in range(nb):
            nsh_ref[b] = win_ref[b, 7:8, :]
            nshs_ref[b] = wins_ref[b, 7:8, :]


def _rwkv_mixer(h3, hs3, shift, s0, lp):
    bsz, l, _ = h3.shape
    nb, nc = _recurrent_tiling(bsz, l)
    rows = nc * CHUNK
    sh_main = shift[:, None, 0:3 * GW]
    sh_small = jnp.pad(shift[:, None, 3 * GW:], ((0, 0), (0, 0), (0, H_SMALL - LORA)))
    mu = lp['rwkv_mu']
    mu_main = mu[None, 0:3 * GW]
    mu_small = jnp.pad(mu[None, 3 * GW:], ((0, 0), (0, H_SMALL - LORA)))
    place = lambda w, off: jnp.zeros((H_SMALL, GW), F32).at[off:off + w.shape[0]].set(w)
    wup = place(lp['rwkv_w_up'], 0)
    aup = place(lp['rwkv_a_up'], RWKV_W_LORA)
    gup = place(lp['rwkv_g_up'], RWKV_W_LORA + RWKV_A_LORA)
    row = lambda v: v.reshape(1, GW)
    full = lambda shape: pl.BlockSpec(shape, lambda bi, i: (0,) * len(shape))
    o, s_new, sh_new, shs_new = pl.pallas_call(
        functools.partial(_rwkv_kernel, nb=nb, nc=nc),
        grid=(bsz // nb, l // rows),
        in_specs=[pl.BlockSpec((nb, rows, 3 * GW), lambda bi, i: (bi, i, H_CRKV // (3 * GW))),
                  pl.BlockSpec((nb, rows, H_SMALL), lambda bi, i: (bi, i, 0)),
                  pl.BlockSpec((nb, 1, 3 * GW), lambda bi, i: (bi, 0, 0)),
                  pl.BlockSpec((nb, 1, H_SMALL), lambda bi, i: (bi, 0, 0)),
                  pl.BlockSpec((nb, HEAD_DIM, GW), lambda bi, i: (bi, 0, 0)),
                  full((1, 3 * GW)), full((1, H_SMALL)), full((1, GW)), full((H_SMALL, GW)), full((1, GW)),
                  full((H_SMALL, GW)), full((H_SMALL, GW)), full((1, GW)), full((1, GW)), full((1, GW)),
                  full((1, GW)), full((1, GW)), full((GW, GW))],
        out_specs=[pl.BlockSpec((nb, rows, GW), lambda bi, i: (bi, i, 0)),
                   pl.BlockSpec((nb, HEAD_DIM, GW), lambda bi, i: (bi, 0, 0)),
                   pl.BlockSpec((nb, 1, 3 * GW), lambda bi, i: (bi, 0, 0)),
                   pl.BlockSpec((nb, 1, H_SMALL), lambda bi, i: (bi, 0, 0))],
        out_shape=[jax.ShapeDtypeStruct((bsz, l, GW), F32),
                   jax.ShapeDtypeStruct((bsz, HEAD_DIM, GW), F32),
                   jax.ShapeDtypeStruct((bsz, 1, 3 * GW), F32),
                   jax.ShapeDtypeStruct((bsz, 1, H_SMALL), F32)],
        scratch_shapes=[pltpu.VMEM((nb, rows + 8, 3 * GW), F32), pltpu.VMEM((nb, rows + 8, H_SMALL), F32),
                        pltpu.VMEM((nb, HEAD_DIM, GW), F32)],
        compiler_params=_params(2),
        name="rwkv_mixer",
    )(h3, hs3, sh_main, sh_small, _heads_to_wide(s0), mu_main, mu_small, row(lp['rwkv_w0']), wup,
      row(lp['rwkv_a0']), aup, gup, row(lp['rwkv_k_k']), row(lp['rwkv_k_a']), row(lp['rwkv_r_k']),
      row(lp['rwkv_lnx_g']), row(lp['rwkv_lnx_b']), _ones_block_diag())
    new_shift = jnp.concatenate([sh_new[:, 0, :], shs_new[:, 0, S_LORA:S_LORA + LORA]], axis=-1)
    return o, _wide_to_heads(s_new), new_shift


_FB = 128
_SAMPLE_BK = 512


def _log_forget(hs, bf):
    return -_softplus(-(hs + bf))


def _head_slice(h):
    return slice(h * HEAD_DIM, (h + 1) * HEAD_DIM)


def _attend_heads(qs, kblk, vblk, key_bias, mask, carry):
    m, l, acc = carry
    bq = qs[0].shape[0]
    kb16 = kblk.astype(BF16)
    vb16 = vblk.astype(BF16)
    parts = []
    for h in range(GROUP_HEADS):
        s = _dot(qs[h], kb16[_head_slice(h), :]) + key_bias[h:h + 1, :]
        parts.append(s if mask is None else jnp.where(mask, s, NEG))
    s = jnp.concatenate(parts, axis=0)
    m_new = jnp.maximum(m, jnp.max(s, -1, keepdims=True))
    alpha = jnp.exp(m - m_new)
    p = jnp.exp(s - m_new)
    l = alpha * l + jnp.sum(p, -1, keepdims=True)
    pb = p.astype(BF16)
    pv = jnp.concatenate([_dot(pb[h * bq:(h + 1) * bq, :], vb16[_head_slice(h), :], NT)
                          for h in range(GROUP_HEADS)], axis=0)
    return m_new, l, alpha * acc + pv


def _attend_init(bq):
    rows = GROUP_HEADS * bq
    return jnp.full((rows, 1), NEG, F32), jnp.zeros((rows, 1), F32), jnp.zeros((rows, HEAD_DIM), F32)


def _attend_store(o_ref, carry, bq):
    _, l, acc = carry
    for h in range(GROUP_HEADS):
        o_ref[:, _head_slice(h)] = acc[h * bq:(h + 1) * bq, :] / l[h * bq:(h + 1) * bq, :]


def _sel_rows(sel01, x):
    t = x.shape[0]
    r = _dot(sel01, jnp.concatenate(_split3(x), axis=0), NT)
    return r[:, :t] + r[:, t:2 * t] + r[:, 2 * t:]


def _fox_prompt_kernel(q_ref, kt_ref, vt_ref, hs_ref, bf_ref, o_ref, lf_ref, nf_scr, *, l, bq):
    i = pl.program_id(1)
    per = bq // _FB

    @pl.when(i == 0)
    def _():
        utri = _triu_ones(_FB, BF16)
        sel = _col_selector(8, H_SMALL, S_F).astype(BF16)
        carry = jnp.zeros((8, 1), F32)
        for jb in range(l // _FB):
            rows = slice(jb * _FB, (jb + 1) * _FB)
            lf = _log_forget(hs_ref[rows, :], bf_ref[...])
            lf_ref[rows, :] = lf[:, S_F:S_F + GROUP_HEADS]
            loc = _mm_x01(_sel_rows(sel, lf), utri, 3)
            nf_scr[jb // per, :, (jb % per) * _FB:(jb % per + 1) * _FB] = -(loc + carry)
            carry = carry + loc[:, _FB - 1:_FB]

    q = q_ref[...]
    r_i, c_i = _iota2((bq, bq), 0), _iota2((bq, bq), 1)
    qs = [(q[:, _head_slice(h)] * (HEAD_DIM ** -0.5)).astype(BF16) for h in range(GROUP_HEADS)]

    def step(j, mask, carry):
        keys = pl.ds(pl.multiple_of(j * bq, bq), bq)
        return _attend_heads(qs, kt_ref[:, keys], vt_ref[:, keys], nf_scr[j], mask, carry)

    carry = lax.fori_loop(0, i, lambda j, c: step(j, None, c), _attend_init(bq))
    carry = step(i, c_i <= r_i, carry)
    _attend_store(o_ref, carry, bq)


def _fox_prompt(h3, hs3, k_t, v_t, bf_pad):
    bsz, l, _ = h3.shape
    bq = min(l, 2 * _FB)
    kern = functools.partial(_fox_prompt_kernel, l=l, bq=bq)
    return pl.pallas_call(
        kern,
        grid=(bsz, l // bq),
        in_specs=[pl.BlockSpec((None, bq, GW), lambda bi, i: (bi, i, H_DQ // GW)),
                  pl.BlockSpec((None, GW, l), lambda bi, i: (bi, 0, 0)),
                  pl.BlockSpec((None, GW, l), lambda bi, i: (bi, 0, 0)),
                  pl.BlockSpec((None, l, H_SMALL), lambda bi, i: (bi, 0, 0)),
                  pl.BlockSpec((1, H_SMALL), lambda bi, i: (0, 0))],
        out_specs=[pl.BlockSpec((None, bq, GW), lambda bi, i: (bi, i, 0)),
                   pl.BlockSpec((None, l, GROUP_HEADS), lambda bi, i: (bi, 0, 0))],
        out_shape=[jax.ShapeDtypeStruct((bsz, l, GW), F32),
                   jax.ShapeDtypeStruct((bsz, l, GROUP_HEADS), F32)],
        scratch_shapes=[pltpu.VMEM((l // bq, 8, bq), F32)],
        compiler_params=_params(2),
        name="fox_prompt",
    )(h3, k_t, v_t, hs3, bf_pad)


def _fox_sample_kernel(q_ref, kt_ref, vt_ref, hs_ref, bf_ref, ck_ref, cv_ref, clf_ref, o_ref, lf_ref, cum_scr,
                       *, l, p):
    nblk = p // _FB
    per = _SAMPLE_BK // _FB

    loc = _mm_x01(clf_ref[...].reshape(nblk * 8, _FB), _triu_ones(_FB, BF16), 3)
    carry = jnp.zeros((8, 1), F32)
    for jb in range(nblk):
        blk = loc[jb * 8:(jb + 1) * 8, :]
        cum_scr[jb // per, :, (jb % per) * _FB:(jb % per + 1) * _FB] = blk + carry
        carry = carry + blk[:, _FB - 1:_FB]
    total = carry

    lf = _log_forget(hs_ref[...], bf_ref[...])
    lf_ref[...] = lf[:, S_F:S_F + GROUP_HEADS]
    cum_r = _mm_x01(_sel_rows(_col_selector(8, H_SMALL, S_F).astype(BF16), lf), _triu_ones(l, BF16), 3)
    q = q_ref[...]
    r_i, c_i = _iota2((l, l), 0), _iota2((l, l), 1)
    qs = [(q[:, _head_slice(h)] * (HEAD_DIM ** -0.5)).astype(BF16) for h in range(GROUP_HEADS)]

    def body(j, carry):
        keys = pl.ds(pl.multiple_of(j * _SAMPLE_BK, _SAMPLE_BK), _SAMPLE_BK)
        suffix = total - cum_scr[j]
        return _attend_heads(qs, ck_ref[:, keys], cv_ref[:, keys], suffix, None, carry)

    carry = lax.fori_loop(0, p // _SAMPLE_BK, body, _attend_init(l))
    carry = _attend_heads(qs, kt_ref[...], vt_ref[...], -cum_r, c_i <= r_i, carry)
    _attend_store(o_ref, carry, l)


def _fox_sample(h3, hs3, k_t, v_t, bf_pad, layer, ck_t, cv_t, clogf):
    bsz, l, _ = h3.shape
    p = ck_t.shape[-1]
    clf = clogf.reshape(bsz, p // _FB, _FB, GROUP_HEADS).transpose(0, 1, 3, 2)
    clf = jnp.pad(clf, ((0, 0), (0, 0), (0, 8 - GROUP_HEADS), (0, 0)))
    kern = functools.partial(_fox_sample_kernel, l=l, p=p)
    return pl.pallas_call(
        kern,
        grid=(bsz,),
        in_specs=[pl.BlockSpec((None, l, GW), lambda bi: (bi, 0, H_DQ // GW)),
                  pl.BlockSpec((None, GW, l), lambda bi: (bi, 0, 0)),
                  pl.BlockSpec((None, GW, l), lambda bi: (bi, 0, 0)),
                  pl.BlockSpec((None, l, H_SMALL), lambda bi: (bi, 0, 0)),
                  pl.BlockSpec((1, H_SMALL), lambda bi: (0, 0)),
                  pl.BlockSpec((None, None, GW, p), lambda bi: (layer, bi, 0, 0)),
                  pl.BlockSpec((None, None, GW, p), lambda bi: (layer, bi, 0, 0)),
                  pl.BlockSpec((None, p // _FB, 8, _FB), lambda bi: (bi, 0, 0, 0))],
        out_specs=[pl.BlockSpec((None, l, GW), lambda bi: (bi, 0, 0)),
                   pl.BlockSpec((None, l, GROUP_HEADS), lambda bi: (bi, 0, 0))],
        out_shape=[jax.ShapeDtypeStruct((bsz, l, GW), F32),
                   jax.ShapeDtypeStruct((bsz, l, GROUP_HEADS), F32)],
        scratch_shapes=[pltpu.VMEM((p // _SAMPLE_BK, 8, _SAMPLE_BK), F32)],
        compiler_params=_params(1),
        name="fox_sample",
    )(h3, k_t, v_t, hs3, bf_pad, ck_t, cv_t, clf)


def _outproj_kernel(oa_ref, ob_ref, oc_ref, od_ref, x_ref, w_ref, g_ref, b_ref, y_ref, wb_scr):
    @pl.when(pl.program_id(0) == 0)
    def _():
        wb_scr[...] = w_ref[...].astype(BF16)

    mix = jnp.concatenate([ref[...].astype(BF16) for ref in (oa_ref, ob_ref, oc_ref, od_ref)], axis=1)
    y_ref[...] = _layer_norm(DN_ALPHA * x_ref[...] + _dot(mix, wb_scr[...]), g_ref[...], b_ref[...])


def _outproj(oa, ob, oc, od, x2d, layer, w_out, g, b):
    t = x2d.shape[0]
    tm = min(t, 512)
    mix = pl.BlockSpec((tm, GW), lambda i: (i, 0))
    return pl.pallas_call(
        _outproj_kernel,
        grid=(t // tm,),
        in_specs=[mix, mix, mix, mix,
                  pl.BlockSpec((tm, D_MODEL), lambda i: (i, 0)),
                  pl.BlockSpec((None, D_MODEL, D_MODEL), lambda i: (layer, 0, 0)),
                  pl.BlockSpec((1, D_MODEL), lambda i: (0, 0)),
                  pl.BlockSpec((1, D_MODEL), lambda i: (0, 0))],
        out_specs=pl.BlockSpec((tm, D_MODEL), lambda i: (i, 0)),
        out_shape=jax.ShapeDtypeStruct((t, D_MODEL), F32),
        scratch_shapes=[pltpu.VMEM((D_MODEL, D_MODEL), BF16)],
        compiler_params=_params(1),
        name="outproj_ln",
    )(oa, ob, oc, od, x2d, w_out, g, b)


_R_G, _R_E, _R_ROWS = 0, 8, 32
_MOE_EXPERTS_PER_STEP = 2


def _moe_kernel(x_ref, wrh_ref, wrl_ref, brc_ref, wg_ref, wu_ref, wd_ref, g_ref, b_ref, y_ref, comb_scr, xb_scr,
                acc_scr):
    e = pl.program_id(1)
    tm = x_ref.shape[0]
    lane = _iota2((tm, LANE), 1)

    @pl.when(e == 0)
    def _():
        x_hi, x_lo = _split2(x_ref[...])
        xb_scr[...] = x_hi
        acc_scr[...] = jnp.zeros_like(acc_scr)
        wr_hi = wrh_ref[...]
        r = _dot(jnp.concatenate([wr_hi, wrl_ref[...]], axis=0), x_hi, NT)
        lt = r[:_R_ROWS] + r[_R_ROWS:] + _dot(wr_hi, x_lo, NT) + brc_ref[...]
        gl = lt[_R_G:_R_G + N_EXPERT_GROUPS]
        grow = _iota2(gl.shape, 0)
        ge = jnp.exp(gl - jnp.max(gl, 0, keepdims=True))
        pg = ge / jnp.sum(ge, 0, keepdims=True)
        gp = jnp.max(pg, 0, keepdims=True)
        gi = jnp.min(jnp.where(pg == gp, grow, N_EXPERT_GROUPS), 0, keepdims=True)
        le = lt[_R_E:_R_E + N_EXPERTS]
        erow = _iota2(le.shape, 0)
        emask = (erow >> 2) == gi
        el = jnp.where(emask, le, NEG)
        ee = jnp.exp(el - jnp.max(el, 0, keepdims=True))
        ep = ee / jnp.sum(ee, 0, keepdims=True)
        m1 = jnp.max(jnp.where(emask, ep, -1.0), 0, keepdims=True)
        i1 = jnp.min(jnp.where(jnp.logical_and(emask, ep == m1), erow, N_EXPERTS), 0, keepdims=True)
        rest = jnp.logical_and(emask, erow != i1)
        m2 = jnp.max(jnp.where(rest, ep, -1.0), 0, keepdims=True)
        i2 = jnp.min(jnp.where(jnp.logical_and(rest, ep == m2), erow, N_EXPERTS), 0, keepdims=True)
        den = m1 + m2
        comb_t = jnp.where(erow == i1, gp * m1 / den, jnp.where(erow == i2, gp * m2 / den, 0.0))
        pieces = jnp.concatenate(_split3(comb_t), axis=0)
        sel = ((_iota2((3 * N_EXPERTS, LANE), 0) & (N_EXPERTS - 1)) == _iota2((3 * N_EXPERTS, LANE), 1)).astype(BF16)
        comb_scr[...] = _dot(pieces, sel, TN)

    xb = xb_scr[...]
    comb = comb_scr[...]
    hidden = []
    for k in range(_MOE_EXPERTS_PER_STEP):
        ce = jnp.sum(jnp.where(lane == e * _MOE_EXPERTS_PER_STEP + k, comb, 0.0), -1, keepdims=True)
        hh = _silu(_mm(xb, wg_ref[k])) * _mm(xb, wu_ref[k])
        hidden.append((hh * ce).astype(BF16))
    hidden = jnp.concatenate(hidden, axis=1)
    w_down = wd_ref[...].reshape(_MOE_EXPERTS_PER_STEP * D_EXPERT, D_MODEL)
    acc_scr[...] += _mm(hidden, w_down)

    @pl.when(e == pl.num_programs(1) - 1)
    def _():
        y_ref[...] = _layer_norm(DN_ALPHA * x_ref[...] + acc_scr[...], g_ref[...], b_ref[...])


def _moe(x2d, wr_hi, wr_lo, br_col, layer, wg, wu, wd, g, b):
    t = x2d.shape[0]
    tm = min(t, 1024)
    ne = _MOE_EXPERTS_PER_STEP
    return pl.pallas_call(
        _moe_kernel,
        grid=(t // tm, N_EXPERTS // ne),
        in_specs=[pl.BlockSpec((tm, D_MODEL), lambda i, e: (i, 0)),
                  pl.BlockSpec((_R_ROWS, D_MODEL), lambda i, e: (0, 0)),
                  pl.BlockSpec((_R_ROWS, D_MODEL), lambda i, e: (0, 0)),
                  pl.BlockSpec((_R_ROWS, 1), lambda i, e: (0, 0)),
                  pl.BlockSpec((None, ne, D_MODEL, D_EXPERT), lambda i, e: (layer, e, 0, 0)),
                  pl.BlockSpec((None, ne, D_MODEL, D_EXPERT), lambda i, e: (layer, e, 0, 0)),
                  pl.BlockSpec((None, ne, D_EXPERT, D_MODEL), lambda i, e: (layer, e, 0, 0)),
                  pl.BlockSpec((1, D_MODEL), lambda i, e: (0, 0)),
                  pl.BlockSpec((1, D_MODEL), lambda i, e: (0, 0))],
        out_specs=pl.BlockSpec((tm, D_MODEL), lambda i, e: (i, 0)),
        out_shape=jax.ShapeDtypeStruct((t, D_MODEL), F32),
        scratch_shapes=[pltpu.VMEM((tm, LANE), F32), pltpu.VMEM((tm, D_MODEL), BF16),
                        pltpu.VMEM((tm, D_MODEL), F32)],
        compiler_params=_params(2),
        name="moe_ln",
    )(x2d, wr_hi, wr_lo, br_col, wg, wu, wd, g, b)


def _prep_layer(lp):
    w = lp['w_in_t']
    c0 = CONV_COLS
    c1 = c0 + GDN_COLS
    c2 = c1 + RWKV_COLS
    main = jnp.concatenate([w[c0:c0 + 3 * GW], w[c1:c1 + 3 * GW], w[c2:c2 + GW],
                            w[0:GW], w[GW:2 * GW], w[c0 + 3 * GW:c0 + 4 * GW]], axis=0).astype(BF16)
    kv_t = w[c2 + GW:c2 + 3 * GW].astype(BF16)
    small = jnp.concatenate([w[c1 + 3 * GW:c2], w[c0 + 4 * GW:c1], w[c2 + 3 * GW:],
                             jnp.zeros((H_SMALL - LORA - 3 * GROUP_HEADS, D_MODEL), F32)], axis=0)
    gap = jnp.zeros((_R_E - N_EXPERT_GROUPS, D_MODEL), F32)
    tail = jnp.zeros((_R_ROWS - _R_E - N_EXPERTS, D_MODEL), F32)
    wr_t = jnp.concatenate([lp['router_g_w'].T, gap, lp['router_e_w'].T, tail], axis=0)
    wr_hi = wr_t.astype(BF16)
    wr_lo = (wr_t - wr_hi.astype(F32)).astype(BF16)
    br = jnp.concatenate([lp['router_g_b'], gap[:, 0], lp['router_e_b'], tail[:, 0]])[:, None]
    bf_pad = jnp.zeros((1, H_SMALL), F32).at[0, S_F:S_F + GROUP_HEADS].set(lp['fox_b_f'])
    return dict(w_main=main, w_small=small, w_kv_t=kv_t, wr_hi=wr_hi, wr_lo=wr_lo, br=br, bf_pad=bf_pad)


def _trunk_layer(x, lp, pp, layer, big, conv_buf, gdn_buf, gdn_s, rw_shift, rw_s, fox_cache):
    bsz, l, d = x.shape
    x2d = x.reshape(bsz * l, d)
    h3, hs3, k_t, v_t = _inproj(x, pp['w_main'], pp['w_small'], pp['w_kv_t'])

    o_a, new_conv = _conv_mixer(h3, conv_buf, lp['conv_w'], lp['conv_b'][None], lp['conv_ln_g'][None],
                                lp['conv_ln_b'][None])
    o_b, new_gdn_s, new_gdn_buf = _gdn_mixer(h3, hs3, gdn_buf, gdn_s, lp['gdn_conv_w'], lp['gdn_a_log'],
                                             lp['gdn_dt_bias'], lp['gdn_norm_w'])
    o_c, new_rw_s, new_shift = _rwkv_mixer(h3, hs3, rw_shift, rw_s, lp)
    if fox_cache is None:
        o_d, logf = _fox_prompt(h3, hs3, k_t, v_t, pp['bf_pad'])
    else:
        o_d, logf = _fox_sample(h3, hs3, k_t, v_t, pp['bf_pad'], layer, *fox_cache)

    flat = lambda o: o.reshape(bsz * l, GW)
    x1 = _outproj(flat(o_a), flat(o_b), flat(o_c), flat(o_d), x2d, layer, big['w_out'], lp['ln1_g'][None],
                  lp['ln1_b'][None])
    x2 = _moe(x1, pp['wr_hi'], pp['wr_lo'], pp['br'], layer, *big['experts'], lp['ln2_g'][None],
              lp['ln2_b'][None])
    return x2.reshape(bsz, l, d), (new_conv, new_gdn_buf, new_gdn_s, new_shift, new_rw_s, k_t, v_t, logf)


def _untranspose(xs_t):
    bsz, _, l = xs_t[0].shape
    return jnp.stack(xs_t).reshape(len(xs_t), bsz, GROUP_HEADS, HEAD_DIM, l).transpose(0, 1, 4, 2, 3)


_LAYER_KEYS = ('w_in_t', 'conv_w', 'conv_b', 'conv_ln_g', 'conv_ln_b', 'gdn_conv_w', 'gdn_a_log', 'gdn_dt_bias',
               'gdn_norm_w', 'rwkv_mu', 'rwkv_w0', 'rwkv_w_up', 'rwkv_a0', 'rwkv_a_up', 'rwkv_g_up', 'rwkv_k_k',
               'rwkv_k_a', 'rwkv_r_k', 'rwkv_lnx_g', 'rwkv_lnx_b', 'fox_b_f', 'ln1_g', 'ln1_b',
               'router_g_w', 'router_g_b', 'router_e_w', 'router_e_b', 'ln2_g', 'ln2_b')


def kernel(x_prompt, x_sample, cache_fox_k, cache_fox_v, cache_fox_logf, state_conv, state_gdn_conv, state_gdn, state_rwkv_shift, state_rwkv, w_in, conv_w, conv_b, conv_ln_g, conv_ln_b, gdn_conv_w, gdn_a_log, gdn_dt_bias, gdn_norm_w, rwkv_mu, rwkv_w0, rwkv_w_up, rwkv_a0, rwkv_a_up, rwkv_g_up, rwkv_k_k, rwkv_k_a, rwkv_r_k, rwkv_lnx_g, rwkv_lnx_b, fox_b_f, w_out, ln1_g, ln1_b, router_g_w, router_g_b, router_e_w, router_e_b, exp_w_gate, exp_w_up, exp_w_down, ln2_g, ln2_b):
    w_in_t = w_in.transpose(0, 2, 1)
    weights = dict(zip(_LAYER_KEYS, (w_in_t, conv_w, conv_b, conv_ln_g, conv_ln_b, gdn_conv_w, gdn_a_log,
                                     gdn_dt_bias, gdn_norm_w, rwkv_mu, rwkv_w0, rwkv_w_up, rwkv_a0, rwkv_a_up,
                                     rwkv_g_up, rwkv_k_k, rwkv_k_a, rwkv_r_k, rwkv_lnx_g, rwkv_lnx_b, fox_b_f,
                                     ln1_g, ln1_b, router_g_w, router_g_b, router_e_w, router_e_b,
                                     ln2_g, ln2_b)))
    big = dict(w_out=w_out, experts=(exp_w_gate, exp_w_up, exp_w_down))
    depth, bs, past = cache_fox_k.shape[:3]
    ck_t = cache_fox_k.transpose(0, 1, 3, 4, 2).reshape(depth, bs, GW, past)
    cv_t = cache_fox_v.transpose(0, 1, 3, 4, 2).reshape(depth, bs, GW, past)
    xp, xs = x_prompt, x_sample
    bp = x_prompt.shape[0]
    outs_p = [[] for _ in range(8)]
    outs_s = [[] for _ in range(8)]
    for l in range(DEPTH):
        lp = {k: v[l] for k, v in weights.items()}
        pp = _prep_layer(lp)
        xp, st_p = _trunk_layer(xp, lp, pp, l, big,
                                jnp.zeros((bp, CONV_WIDTH - 1, GW), F32),
                                jnp.zeros((bp, GDN_CONV_WIDTH - 1, 3 * GW), F32),
                                jnp.zeros((bp, GROUP_HEADS, HEAD_DIM, HEAD_DIM), F32),
                                jnp.zeros((bp, RWKV_COLS), F32),
                                jnp.zeros((bp, GROUP_HEADS, HEAD_DIM, HEAD_DIM), F32),
                                None)
        xs, st_s = _trunk_layer(xs, lp, pp, l, big, state_conv[l], state_gdn_conv[l], state_gdn[l],
                                state_rwkv_shift[l], state_rwkv[l], (ck_t, cv_t, cache_fox_logf[l]))
        for i in range(8):
            outs_p[i].append(st_p[i])
            outs_s[i].append(st_s[i])

    def assemble(outs):
        conv, gdn_buf, gdn_s, shift, rw_s = (jnp.stack(o) for o in outs[:5])
        return conv, gdn_buf, gdn_s, shift, rw_s, _untranspose(outs[5]), _untranspose(outs[6]), jnp.stack(outs[7])

    return (xp, xs, *assemble(outs_p), *assemble(outs_s))
```

```python
import functools

import jax
import jax.numpy as jnp
from jax import lax
from jax.experimental import pallas as pl
from jax.experimental.pallas import tpu as pltpu

F32 = jnp.float32
BF16 = jnp.bfloat16

D_MODEL = 1024
DEPTH = 2
CHUNK = 64
HEAD_DIM = 64
GW = D_MODEL // 4
GROUP_HEADS = GW // HEAD_DIM
CONV_WIDTH = 31
GDN_CONV_WIDTH = 4
RWKV_W_LORA = 16
RWKV_A_LORA = 16
RWKV_G_LORA = 32
N_EXPERT_GROUPS = 4
EXPERTS_PER_GROUP = 4
N_EXPERTS = N_EXPERT_GROUPS * EXPERTS_PER_GROUP
D_EXPERT = 256
DN_ALPHA = (2 * DEPTH) ** 0.25
LN_EPS = 1e-5
GN_EPS = 64e-5

CONV_COLS = 2 * GW
GDN_COLS = 4 * GW + 2 * GROUP_HEADS
RWKV_COLS = 3 * GW + RWKV_W_LORA + RWKV_A_LORA + RWKV_G_LORA
FOX_COLS = 3 * GW + GROUP_HEADS
LORA = RWKV_W_LORA + RWKV_A_LORA + RWKV_G_LORA

H_BQKV, H_CRKV, H_DQ, H_AVAL, H_AGATE, H_BZ = 0, 768, 1536, 1792, 2048, 2304
H_MAIN = 2560
KV_ROWS = 2 * GW
S_LORA, S_A, S_B, S_F = 0, 64, 68, 72
H_SMALL = 128
LANE = 128

VMEM_LIMIT = 56 * 1024 * 1024

NN = (((1,), (0,)), ((), ()))
NT = (((1,), (1,)), ((), ()))
TN = (((0,), (0,)), ((), ()))
NEG = -1e30


def _dot(a, b, dims=NN):
    return lax.dot_general(a, b, dims, preferred_element_type=F32)


def _mm(a, b, dims=NN):
    return _dot(a.astype(BF16), b.astype(BF16), dims)


def _mmh(a, b, dims=NN):
    return lax.dot_general(a, b, dims, preferred_element_type=F32, precision=lax.Precision.HIGHEST)


def _sigmoid(x):
    return 0.5 * (jnp.tanh(0.5 * x) + 1.0)


def _silu(x):
    return x * _sigmoid(x)


def _softplus(x):
    return jnp.maximum(x, 0.0) + jnp.log(1.0 + jnp.exp(-jnp.abs(x)))


def _iota2(shape, dim):
    return lax.broadcasted_iota(jnp.int32, shape, dim)


def _tril_ones(n, dtype=F32):
    r, c = _iota2((n, n), 0), _iota2((n, n), 1)
    return (r >= c).astype(dtype)


def _triu_ones(n, dtype=F32):
    r, c = _iota2((n, n), 0), _iota2((n, n), 1)
    return (r <= c).astype(dtype)


def _col_selector(rows, width, col0):
    r, c = _iota2((rows, width), 0), _iota2((rows, width), 1)
    return (c == r + col0).astype(F32)


def _layer_norm(x, g, b):
    mu = jnp.mean(x, axis=-1, keepdims=True)
    xc = x - mu
    var = jnp.mean(xc * xc, axis=-1, keepdims=True)
    return xc * lax.rsqrt(var + LN_EPS) * g + b


def _split2(x):
    hi = x.astype(BF16)
    lo = (x - hi.astype(F32)).astype(BF16)
    return hi, lo


def _split3(x):
    hi = x.astype(BF16)
    r = x - hi.astype(F32)
    mid = r.astype(BF16)
    lo = (r - mid.astype(F32)).astype(BF16)
    return hi, mid, lo


class _Wide:
    def __init__(self):
        self.i = _iota2((CHUNK, GW), 0)
        lane = _iota2((CHUNK, GW), 1)
        self.j = lane & (HEAD_DIM - 1)
        self.head = lane >> 6
        self.incl = self.i >= self.j
        self.strict = self.i > self.j
        self.eye = self.i == self.j
        r, c = _iota2((GW, GW), 0), _iota2((GW, GW), 1)
        self.block_diag = (r >> 6) == (c >> 6)


def _head_block_diag(x, w):
    t = jnp.concatenate([x] * GROUP_HEADS, axis=0)
    return jnp.where(w.block_diag, t, jnp.zeros_like(t))


def _collapse_heads(full, w):
    out = jnp.zeros((HEAD_DIM, GW), F32)
    for h in range(GROUP_HEADS):
        out = out + jnp.where(w.head == h, full[h * HEAD_DIM:(h + 1) * HEAD_DIM, :], 0.0)
    return out


def _lhs3(a, precise=False):
    if not precise:
        return None, a.astype(BF16)
    hi, lo = _split2(a)
    return jnp.concatenate([hi, lo], axis=0), hi


def _rhs3(b, w=None, precise=False):
    if not precise:
        hi = b.astype(BF16)
        return (hi if w is None else _head_block_diag(hi, w)), None
    hi, lo = _split2(b)
    if w is None:
        return hi, lo
    return _head_block_diag(hi, w), _head_block_diag(lo, w)


def _mm3(lhs, rhs, dims=NN):
    cat, hi = lhs
    bh, bl = rhs
    if cat is None or bl is None:
        return _dot(hi, bh, dims)
    m = hi.shape[0]
    r = _dot(cat, bh, dims)
    return r[:m] + r[m:] + _dot(hi, bl, dims)


def _mm3_tn(a, b, precise=False):
    if not precise:
        return _dot(a.astype(BF16), b.astype(BF16), TN)
    ah, al = _split2(a)
    bh, bl = _split2(b)
    m = a.shape[1]
    r = _dot(jnp.concatenate([ah, al], axis=1), bh, TN)
    return r[:m] + r[m:] + _dot(ah, bl, TN)


def _mm_x01(a, m01, pieces=2):
    m = a.shape[0]
    r = _dot(jnp.concatenate(_split3(a) if pieces == 3 else _split2(a), axis=0), m01)
    return sum(r[i * m:(i + 1) * m] for i in range(pieces))


def _tri_mm(tri01, x, pieces=2):
    n = x.shape[1]
    r = _dot(tri01, jnp.concatenate(_split3(x) if pieces == 3 else _split2(x), axis=1))
    return sum(r[:, i * n:(i + 1) * n] for i in range(pieces))


def _unit_lower_inverse_wide(mats, w):
    eye = w.eye.astype(F32)
    same8 = (w.i >> 3) == (w.j >> 3)
    ds = [jnp.where(same8, a, 0.0) for a in mats]
    d2 = [_mm3(_lhs3(d), _rhs3(d, w)) for d in ds]
    d2_r = [_rhs3(x, w) for x in d2]
    d4 = [_mm3(_lhs3(x), r) for x, r in zip(d2, d2_r)]
    imd = [eye - d for d in ds]
    p1 = [i + _mm3(_lhs3(i), r) for i, r in zip(imd, d2_r)]
    ts = [p + _mm3(_lhs3(p), _rhs3(x, w)) for p, x in zip(p1, d4)]
    for s in range(3, 6):
        big = (w.i >> (s + 1)) == (w.j >> (s + 1))
        small = (w.i >> s) == (w.j >> s)
        sel = jnp.logical_and(big, jnp.logical_not(small))
        inner = [_mm3(_lhs3(jnp.where(sel, a, 0.0)), _rhs3(t, w)) for a, t in zip(mats, ts)]
        ts = [t - _mm3(_lhs3(t), _rhs3(x, w)) for t, x in zip(ts, inner)]
    return ts


def _params(n_axes):
    return pltpu.CompilerParams(dimension_semantics=("arbitrary",) * n_axes,
                                vmem_limit_bytes=VMEM_LIMIT)


def _heads_to_wide(s):
    b = s.shape[0]
    return s.transpose(0, 2, 1, 3).reshape(b, HEAD_DIM, GW)


def _wide_to_heads(s):
    b = s.shape[0]
    return s.reshape(b, HEAD_DIM, GROUP_HEADS, HEAD_DIM).transpose(0, 2, 1, 3)


def _ones_block_diag():
    r, c = jnp.arange(GW)[:, None], jnp.arange(GW)[None, :]
    return ((r // HEAD_DIM) == (c // HEAD_DIM)).astype(BF16)


def _lane_spread(col0):
    r, c = jnp.arange(H_SMALL)[:, None], jnp.arange(GW)[None, :]
    return ((r - col0) == (c // HEAD_DIM)).astype(BF16)


def _inproj_kernel(x_ref, wm_ref, wsh_ref, wsl_ref, wkv_ref, h_ref, hs_ref, kt_ref, vt_ref):
    x = x_ref[...]
    x_hi, x_lo = _split2(x)
    tm = x.shape[0]
    h_ref[...] = _dot(x_hi, wm_ref[...], NT)
    r = _dot(jnp.concatenate([x_hi, x_lo], axis=0), wsh_ref[...], NT)
    hs_ref[...] = r[:tm] + r[tm:] + _dot(x_hi, wsl_ref[...], NT)
    kv = _dot(wkv_ref[...], x_hi, NT)
    kt_ref[...] = kv[:GW]
    vt_ref[...] = kv[GW:]


def _inproj(x3, w_main_t, w_small_t, w_kv_t):
    bsz, l, _ = x3.shape
    tm = min(l, 512)
    const = lambda shape: pl.BlockSpec(shape, lambda bi, i: (0, 0))
    ws_hi = w_small_t.astype(BF16)
    ws_lo = (w_small_t - ws_hi.astype(F32)).astype(BF16)
    kv_spec = pl.BlockSpec((None, GW, tm), lambda bi, i: (bi, 0, i))
    return pl.pallas_call(
        _inproj_kernel,
        grid=(bsz, l // tm),
        in_specs=[pl.BlockSpec((None, tm, D_MODEL), lambda bi, i: (bi, i, 0)),
                  const((H_MAIN, D_MODEL)), const((H_SMALL, D_MODEL)), const((H_SMALL, D_MODEL)),
                  const((KV_ROWS, D_MODEL))],
        out_specs=[pl.BlockSpec((None, tm, H_MAIN), lambda bi, i: (bi, i, 0)),
                   pl.BlockSpec((None, tm, H_SMALL), lambda bi, i: (bi, i, 0)), kv_spec, kv_spec],
        out_shape=[jax.ShapeDtypeStruct((bsz, l, H_MAIN), F32), jax.ShapeDtypeStruct((bsz, l, H_SMALL), F32),
                   jax.ShapeDtypeStruct((bsz, GW, l), F32), jax.ShapeDtypeStruct((bsz, GW, l), F32)],
        compiler_params=_params(2),
        name="inproj",
    )(x3, w_main_t, ws_hi, ws_lo, w_kv_t)


_HALO = 32
_SUBLANES = 8


def _conv_kernel(val_ref, gate_ref, buf_ref, cw_ref, cb_ref, g_ref, b_ref, o_ref, nbuf_ref, win_ref, sh_ref, *, tl):
    i = pl.program_id(1)
    pad = _HALO - (CONV_WIDTH - 1)
    span = tl + _HALO - _SUBLANES

    @pl.when(i == 0)
    def _():
        win_ref[0:8, :] = jnp.zeros((8, GW), F32)
        win_ref[pad:_HALO, :] = buf_ref[...]

    u = val_ref[...] * _sigmoid(gate_ref[...])
    win_ref[_HALO:_HALO + tl, :] = u
    for s in range(1, _SUBLANES):
        sh_ref[s, :, :] = win_ref[s:s + span, :]
    sub = min(tl, 64)
    for r0 in range(0, tl, sub):
        acc = jnp.zeros((sub, GW), F32)
        for j in range(CONV_WIDTH):
            s, base = (pad + j) % _SUBLANES, (pad + j) // _SUBLANES * _SUBLANES + r0
            rows = win_ref[base:base + sub, :] if s == 0 else sh_ref[s, base:base + sub, :]
            acc = acc + cw_ref[j:j + 1, :] * rows
        y = _layer_norm(acc + cb_ref[...], g_ref[...], b_ref[...])
        o_ref[r0:r0 + sub, :] = _silu(y).astype(o_ref.dtype)
    tail = win_ref[tl:tl + _HALO, :]
    win_ref[0:_HALO, :] = tail

    @pl.when(i == pl.num_programs(1) - 1)
    def _():
        nbuf_ref[...] = win_ref[pad:_HALO, :]


def _conv_mixer(h3, buf, cw, cb, g, b):
    bsz, l, _ = h3.shape
    tl = min(l, 256)
    kern = functools.partial(_conv_kernel, tl=tl)
    full = lambda shape: pl.BlockSpec(shape, lambda bi, i: (0,) * len(shape))
    return pl.pallas_call(
        kern,
        grid=(bsz, l // tl),
        in_specs=[pl.BlockSpec((None, tl, GW), lambda bi, i: (bi, i, H_AVAL // GW)),
                  pl.BlockSpec((None, tl, GW), lambda bi, i: (bi, i, H_AGATE // GW)),
                  pl.BlockSpec((None, CONV_WIDTH - 1, GW), lambda bi, i: (bi, 0, 0)),
                  full((CONV_WIDTH, GW)), full((1, GW)), full((1, GW)), full((1, GW))],
        out_specs=[pl.BlockSpec((None, tl, GW), lambda bi, i: (bi, i, 0)),
                   pl.BlockSpec((None, CONV_WIDTH - 1, GW), lambda bi, i: (bi, 0, 0))],
        out_shape=[jax.ShapeDtypeStruct((bsz, l, GW), BF16),
                   jax.ShapeDtypeStruct((bsz, CONV_WIDTH - 1, GW), F32)],
        scratch_shapes=[pltpu.VMEM((tl + _HALO, GW), F32),
                        pltpu.VMEM((_SUBLANES, tl + _HALO - _SUBLANES, GW), F32)],
        compiler_params=_params(2),
        name="conv_mixer",
    )(h3, h3, buf, cw, cb, g, b)


def _gdn_kernel(qkv_ref, z_ref, hs_ref, buf_ref, s0_ref, cw_ref, alog_ref, dtb_ref, nw_ref, ones_ref, sela_ref,
                selb_ref, o_ref, sout_ref, nbuf_ref, win_ref, s_scr, *, nb, nc):
    c = pl.program_id(1)
    n = CHUNK
    rows = nc * n
    kw = GDN_CONV_WIDTH - 1
    items = [(b, ci) for b in range(nb) for ci in range(nc)]

    @pl.when(c == 0)
    def _():
        for b in range(nb):
            win_ref[b, 0:8, :] = jnp.zeros((8, 3 * GW), F32)
            win_ref[b, 8 - kw:8, :] = buf_ref[b]
        s_scr[...] = s0_ref[...]

    qkvs = []
    for b in range(nb):
        win_ref[b, 8:8 + rows, :] = qkv_ref[b]
        conv = jnp.zeros((rows, 3 * GW), F32)
        for j in range(GDN_CONV_WIDTH):
            conv = conv + cw_ref[j:j + 1, :] * win_ref[b, 8 - kw + j:8 - kw + j + rows, :]
        tail = win_ref[b, rows:rows + 8, :]
        win_ref[b, 0:8, :] = tail
        qkvs.append(_silu(conv))

    w = _Wide()
    ones_bd = ones_ref[...]
    tril = _tril_ones(n, BF16)
    sl = lambda ci: slice(ci * n, (ci + 1) * n)
    qs = [qkvs[b][sl(ci), 0:GW] for b, ci in items]
    ks = [qkvs[b][sl(ci), GW:2 * GW] for b, ci in items]
    vs = [qkvs[b][sl(ci), 2 * GW:3 * GW] for b, ci in items]
    sss = [_mm_x01(jnp.concatenate([q * q, k * k], axis=0), ones_bd) for q, k in zip(qs, ks)]
    qs = [q * lax.rsqrt(ss[:n] + 1e-6) * (HEAD_DIM ** -0.5) for q, ss in zip(qs, sss)]
    ks = [k * lax.rsqrt(ss[n:] + 1e-6) for k, ss in zip(ks, sss)]

    hss = [hs_ref[b, sl(ci), :] for b, ci in items]
    gs = [_mm_x01(-jnp.exp(alog_ref[...]) * _softplus(hs + dtb_ref[...]), sela_ref[...]) for hs in hss]
    betas = [_mm_x01(_sigmoid(hs), selb_ref[...]) for hs in hss]
    gcs = [_tri_mm(tril, g) for g in gs]
    grs = [jnp.sum(jnp.where(w.eye, gc, 0.0), axis=0, keepdims=True) for gc in gcs]
    decays = [jnp.exp(jnp.where(w.incl, gc - gr, NEG)) for gc, gr in zip(gcs, grs)]

    kbs = [k * beta for k, beta in zip(ks, betas)]
    grams = [_mm3(_lhs3(jnp.concatenate([kb, q], axis=0)), _rhs3(k, w), NT)
             for kb, q, k in zip(kbs, qs, ks)]
    amats = [jnp.where(w.strict, gram[:n] * decay, 0.0) for gram, decay in zip(grams, decays)]
    qks = [jnp.where(w.incl, gram[n:] * decay, 0.0) for gram, decay in zip(grams, decays)]
    t_ls = [_lhs3(t) for t in _unit_lower_inverse_wide(amats, w)]
    egs = [jnp.exp(gc) for gc in gcs]
    xvs = [_mm3(t_l, _rhs3(v * beta, w)) for t_l, v, beta in zip(t_ls, vs, betas)]
    xks = [_mm3(t_l, _rhs3(kb * eg, w)) for t_l, kb, eg in zip(t_ls, kbs, egs)]
    xq_ls = [_lhs3(jnp.concatenate([xk, q * eg], axis=0)) for xk, q, eg in zip(xks, qs, egs)]
    qk_ls = [_lhs3(qk) for qk in qks]
    gls = [gc[n - 1:n, :] for gc in gcs]
    kds = [k * jnp.exp(gl - gc) for k, gl, gc in zip(ks, gls, gcs)]

    states = [s_scr[b] for b in range(nb)]
    outs = {}
    for ci in range(nc):
        for b in range(nb):
            it = b * nc + ci
            s = states[b]
            rs = _mm3(xq_ls[it], _rhs3(s, w))
            u = xvs[it] - rs[:n]
            outs[it] = rs[n:] + _mm3(qk_ls[it], _rhs3(u, w))
            states[b] = s * jnp.exp(gls[it]) + _collapse_heads(_mm3_tn(kds[it], u), w)
    for b in range(nb):
        s_scr[b] = states[b]

    for it, (b, ci) in enumerate(items):
        o = outs[it]
        ms = _mm_x01(o * o, ones_bd) * (1.0 / HEAD_DIM)
        o = o * lax.rsqrt(ms + 1e-6) * nw_ref[...] * _silu(z_ref[b, sl(ci), :])
        o_ref[b, sl(ci), :] = o.astype(o_ref.dtype)

    @pl.when(c == pl.num_programs(1) - 1)
    def _():
        sout_ref[...] = s_scr[...]
        for b in range(nb):
            nbuf_ref[b] = win_ref[b, 8 - kw:8, :]


def _recurrent_tiling(bsz, l):
    nc = min(l // CHUNK, 4)
    nb = 2 if nc > 1 else min(bsz, 4)
    return nb, nc


def _gdn_mixer(h3, hs3, buf, s0, cw, a_log, dt_bias, norm_w):
    bsz, l, _ = h3.shape
    nb, nc = _recurrent_tiling(bsz, l)
    rows = nc * CHUNK
    pad_r = lambda v, off: jnp.zeros((1, H_SMALL), F32).at[0, off:off + GROUP_HEADS].set(v)
    full = lambda shape: pl.BlockSpec(shape, lambda bi, i: (0,) * len(shape))
    o, s_new, new_buf = pl.pallas_call(
        functools.partial(_gdn_kernel, nb=nb, nc=nc),
        grid=(bsz // nb, l // rows),
        in_specs=[pl.BlockSpec((nb, rows, 3 * GW), lambda bi, i: (bi, i, H_BQKV // (3 * GW))),
                  pl.BlockSpec((nb, rows, GW), lambda bi, i: (bi, i, H_BZ // GW)),
                  pl.BlockSpec((nb, rows, H_SMALL), lambda bi, i: (bi, i, 0)),
                  pl.BlockSpec((nb, GDN_CONV_WIDTH - 1, 3 * GW), lambda bi, i: (bi, 0, 0)),
                  pl.BlockSpec((nb, HEAD_DIM, GW), lambda bi, i: (bi, 0, 0)),
                  full((GDN_CONV_WIDTH, 3 * GW)), full((1, H_SMALL)), full((1, H_SMALL)), full((1, GW)),
                  full((GW, GW)), full((H_SMALL, GW)), full((H_SMALL, GW))],
        out_specs=[pl.BlockSpec((nb, rows, GW), lambda bi, i: (bi, i, 0)),
                   pl.BlockSpec((nb, HEAD_DIM, GW), lambda bi, i: (bi, 0, 0)),
                   pl.BlockSpec((nb, GDN_CONV_WIDTH - 1, 3 * GW), lambda bi, i: (bi, 0, 0))],
        out_shape=[jax.ShapeDtypeStruct((bsz, l, GW), BF16),
                   jax.ShapeDtypeStruct((bsz, HEAD_DIM, GW), F32),
                   jax.ShapeDtypeStruct((bsz, GDN_CONV_WIDTH - 1, 3 * GW), F32)],
        scratch_shapes=[pltpu.VMEM((nb, rows + 8, 3 * GW), F32), pltpu.VMEM((nb, HEAD_DIM, GW), F32)],
        compiler_params=_params(2),
        name="gdn_mixer",
    )(h3, h3, hs3, buf, _heads_to_wide(s0), cw, pad_r(a_log, S_A), pad_r(dt_bias, S_A),
      jnp.tile(norm_w, GROUP_HEADS)[None, :], _ones_block_diag(), _lane_spread(S_A), _lane_spread(S_B))
    return o, _wide_to_heads(s_new), new_buf


def _rwkv_kernel(rkv_ref, hs_ref, sh_ref, shs_ref, s0_ref, mu_ref, mus_ref, w0_ref, wup_ref, a0_ref, aup_ref,
                 gup_ref, kk_ref, ka_ref, rk_ref, lng_ref, lnb_ref, ones_ref, o_ref, sout_ref, nsh_ref, nshs_ref,
                 win_ref, wins_ref, s_scr, *, nb, nc):
    c = pl.program_id(1)
    n = CHUNK

    rows = nc * n
    items = [(b, ci) for b in range(nb) for ci in range(nc)]

    @pl.when(c == 0)
    def _():
        for b in range(nb):
            win_ref[b, 0:8, :] = jnp.zeros((8, 3 * GW), F32)
            wins_ref[b, 0:8, :] = jnp.zeros((8, H_SMALL), F32)
            win_ref[b, 7:8, :] = sh_ref[b]
            wins_ref[b, 7:8, :] = shs_ref[b]
        s_scr[...] = s0_ref[...]

    wup_r, aup_r, gup_r = (_rhs3(r[...], precise=True) for r in (wup_ref, aup_ref, gup_ref))
    per_b = []
    for b in range(nb):
        x = rkv_ref[b]
        xs = hs_ref[b]
        win_ref[b, 8:8 + rows, :] = x
        wins_ref[b, 8:8 + rows, :] = xs
        xm = x + (win_ref[b, 7:7 + rows, :] - x) * mu_ref[...]
        xms = xs + (wins_ref[b, 7:7 + rows, :] - xs) * mus_ref[...]
        last = win_ref[b, rows + 7:rows + 8, :]
        win_ref[b, 7:8, :] = last
        lasts = wins_ref[b, rows + 7:rows + 8, :]
        wins_ref[b, 7:8, :] = lasts
        w_pre = w0_ref[...] + _mm3(_lhs3(jnp.tanh(xms), precise=True), wup_r)
        logw = -jnp.exp(-_softplus(-w_pre) - 0.5)
        a_sig = _sigmoid(a0_ref[...] + _mm3(_lhs3(xms, precise=True), aup_r))
        gate = _mm3(_lhs3(_sigmoid(xms), precise=True), gup_r)
        kx = xm[:, GW:2 * GW]
        per_b.append(dict(rr=xm[:, 0:GW], vv=xm[:, 2 * GW:3 * GW], logw=logw, a_sig=a_sig, gate=gate,
                          kkp=kx * kk_ref[...], k2=kx * (1.0 + (a_sig - 1.0) * ka_ref[...])))

    w = _Wide()
    ones_bd = ones_ref[...]
    tril = _tril_ones(n, BF16)
    sl = lambda ci: slice(ci * n, (ci + 1) * n)
    get = lambda name: [per_b[b][name][sl(ci), :] for b, ci in items]
    rrs, vvs, logws, a_sigs, gates, kkps, k2s = (get(k) for k in ('rr', 'vv', 'logw', 'a_sig', 'gate', 'kkp', 'k2'))
    kks = [kkp * lax.rsqrt(_mm_x01(kkp * kkp, ones_bd) + 1e-6) for kkp in kkps]
    cums = [_tri_mm(tril, logw) for logw in logws]
    w_incls = [jnp.exp(cum) for cum in cums]
    w_lasts = [wi[n - 1:n, :] for wi in w_incls]
    w_invs = [jnp.exp(-cum) for cum in cums]
    ats = [-kk * jnp.exp(cum - logw) for kk, cum, logw in zip(kks, cums, logws)]
    bts = [kk * a_sig * wv for kk, a_sig, wv in zip(kks, a_sigs, w_invs)]
    kts = [k2 * wv for k2, wv in zip(k2s, w_invs)]
    rts = [rr * wi for rr, wi in zip(rrs, w_incls)]
    ar_ls = [_lhs3(jnp.concatenate([at, rt], axis=0)) for at, rt in zip(ats, rts)]
    gbs = [_mm3(ar_l, _rhs3(bt, w), NT) for ar_l, bt in zip(ar_ls, bts)]
    gks = [_mm3(ar_l, _rhs3(kt, w), NT) for ar_l, kt in zip(ar_ls, kts)]
    t_ls = [_lhs3(t) for t in _unit_lower_inverse_wide([jnp.where(w.strict, -gb[:n], 0.0) for gb in gbs], w)]
    arb_ls = [_lhs3(jnp.where(w.incl, gb[n:], 0.0)) for gb in gbs]
    ark_ls = [_lhs3(jnp.where(w.incl, gk[n:], 0.0)) for gk in gks]
    v_rs = [_rhs3(vv, w) for vv in vvs]
    aakvs = [_mm3(_lhs3(jnp.where(w.strict, gk[:n], 0.0)), v_r) for gk, v_r in zip(gks, v_rs)]
    arkvs = [_mm3(ark_l, v_r) for ark_l, v_r in zip(ark_ls, v_rs)]
    bks = [jnp.concatenate([bt * wl, kt * wl], axis=0) for bt, kt, wl in zip(bts, kts, w_lasts)]

    states = [s_scr[b] for b in range(nb)]
    ys = {}
    for ci in range(nc):
        for b in range(nb):
            it = b * nc + ci
            s0 = states[b]
            ars = _mm3(ar_ls[it], _rhs3(s0, w), NT)
            u = _mm3(t_ls[it], _rhs3(ars[:n] + aakvs[it], w))
            ys[it] = ars[n:] + _mm3(arb_ls[it], _rhs3(u, w)) + arkvs[it]
            upd = _mm3_tn(jnp.concatenate([u, vvs[it]], axis=0), bks[it])
            states[b] = s0 * w_lasts[it] + _collapse_heads(upd, w)
    for b in range(nb):
        s_scr[b] = states[b]

    inv_d = 1.0 / HEAD_DIM
    for it, (b, ci) in enumerate(items):
        y = ys[it]
        yc = y - _mm_x01(y, ones_bd) * inv_d
        var_y = _mm_x01(yc * yc, ones_bd) * inv_d
        yn = yc * lax.rsqrt(var_y + GN_EPS) * lng_ref[...] + lnb_ref[...]
        bonus = _mm_x01(rrs[it] * k2s[it] * rk_ref[...], ones_bd) * vvs[it]
        o_ref[b, sl(ci), :] = ((yn + bonus) * gates[it]).astype(o_ref.dtype)

    @pl.when(c == pl.num_programs(1) - 1)
    def _():
        sout_ref[...] = s_scr[...]
        for b in range(nb):
            nsh_ref[b] = win_ref[b, 7:8, :]
            nshs_ref[b] = wins_ref[b, 7:8, :]


def _rwkv_mixer(h3, hs3, shift, s0, lp):
    bsz, l, _ = h3.shape
    nb, nc = _recurrent_tiling(bsz, l)
    rows = nc * CHUNK
    sh_main = shift[:, None, 0:3 * GW]
    sh_small = jnp.pad(shift[:, None, 3 * GW:], ((0, 0), (0, 0), (0, H_SMALL - LORA)))
    mu = lp['rwkv_mu']
    mu_main = mu[None, 0:3 * GW]
    mu_small = jnp.pad(mu[None, 3 * GW:], ((0, 0), (0, H_SMALL - LORA)))
    place = lambda w, off: jnp.zeros((H_SMALL, GW), F32).at[off:off + w.shape[0]].set(w)
    wup = place(lp['rwkv_w_up'], 0)
    aup = place(lp['rwkv_a_up'], RWKV_W_LORA)
    gup = place(lp['rwkv_g_up'], RWKV_W_LORA + RWKV_A_LORA)
    row = lambda v: v.reshape(1, GW)
    full = lambda shape: pl.BlockSpec(shape, lambda bi, i: (0,) * len(shape))
    o, s_new, sh_new, shs_new = pl.pallas_call(
        functools.partial(_rwkv_kernel, nb=nb, nc=nc),
        grid=(bsz // nb, l // rows),
        in_specs=[pl.BlockSpec((nb, rows, 3 * GW), lambda bi, i: (bi, i, H_CRKV // (3 * GW))),
                  pl.BlockSpec((nb, rows, H_SMALL), lambda bi, i: (bi, i, 0)),
                  pl.BlockSpec((nb, 1, 3 * GW), lambda bi, i: (bi, 0, 0)),
                  pl.BlockSpec((nb, 1, H_SMALL), lambda bi, i: (bi, 0, 0)),
                  pl.BlockSpec((nb, HEAD_DIM, GW), lambda bi, i: (bi, 0, 0)),
                  full((1, 3 * GW)), full((1, H_SMALL)), full((1, GW)), full((H_SMALL, GW)), full((1, GW)),
                  full((H_SMALL, GW)), full((H_SMALL, GW)), full((1, GW)), full((1, GW)), full((1, GW)),
                  full((1, GW)), full((1, GW)), full((GW, GW))],
        out_specs=[pl.BlockSpec((nb, rows, GW), lambda bi, i: (bi, i, 0)),
                   pl.BlockSpec((nb, HEAD_DIM, GW), lambda bi, i: (bi, 0, 0)),
                   pl.BlockSpec((nb, 1, 3 * GW), lambda bi, i: (bi, 0, 0)),
                   pl.BlockSpec((nb, 1, H_SMALL), lambda bi, i: (bi, 0, 0))],
        out_shape=[jax.ShapeDtypeStruct((bsz, l, GW), BF16),
                   jax.ShapeDtypeStruct((bsz, HEAD_DIM, GW), F32),
                   jax.ShapeDtypeStruct((bsz, 1, 3 * GW), F32),
                   jax.ShapeDtypeStruct((bsz, 1, H_SMALL), F32)],
        scratch_shapes=[pltpu.VMEM((nb, rows + 8, 3 * GW), F32), pltpu.VMEM((nb, rows + 8, H_SMALL), F32),
                        pltpu.VMEM((nb, HEAD_DIM, GW), F32)],
        compiler_params=_params(2),
        name="rwkv_mixer",
    )(h3, hs3, sh_main, sh_small, _heads_to_wide(s0), mu_main, mu_small, row(lp['rwkv_w0']), wup,
      row(lp['rwkv_a0']), aup, gup, row(lp['rwkv_k_k']), row(lp['rwkv_k_a']), row(lp['rwkv_r_k']),
      row(lp['rwkv_lnx_g']), row(lp['rwkv_lnx_b']), _ones_block_diag())
    new_shift = jnp.concatenate([sh_new[:, 0, :], shs_new[:, 0, S_LORA:S_LORA + LORA]], axis=-1)
    return o, _wide_to_heads(s_new), new_shift


_FB = 128
_SAMPLE_BK = 512


def _log_forget(hs, bf):
    return -_softplus(-(hs + bf))


def _head_slice(h):
    return slice(h * HEAD_DIM, (h + 1) * HEAD_DIM)


def _attend_heads(qs, kblk, vblk, key_bias, mask, carry):
    m, l, acc = carry
    bq = qs[0].shape[0]
    kb16 = kblk.astype(BF16)
    vb16 = vblk.astype(BF16)
    parts = []
    for h in range(GROUP_HEADS):
        s = _dot(qs[h], kb16[_head_slice(h), :]) + key_bias[h:h + 1, :]
        parts.append(s if mask is None else jnp.where(mask, s, NEG))
    s = jnp.concatenate(parts, axis=0)
    m_new = jnp.maximum(m, jnp.max(s, -1, keepdims=True))
    alpha = jnp.exp(m - m_new)
    p = jnp.exp(s - m_new)
    l = alpha * l + jnp.sum(p, -1, keepdims=True)
    pb = p.astype(BF16)
    pv = jnp.concatenate([_dot(pb[h * bq:(h + 1) * bq, :], vb16[_head_slice(h), :], NT)
                          for h in range(GROUP_HEADS)], axis=0)
    return m_new, l, alpha * acc + pv


def _attend_init(bq):
    rows = GROUP_HEADS * bq
    return jnp.full((rows, 1), NEG, F32), jnp.zeros((rows, 1), F32), jnp.zeros((rows, HEAD_DIM), F32)


def _attend_store(o_ref, carry, bq):
    _, l, acc = carry
    for h in range(GROUP_HEADS):
        o_ref[:, _head_slice(h)] = (acc[h * bq:(h + 1) * bq, :] / l[h * bq:(h + 1) * bq, :]).astype(o_ref.dtype)


def _sel_rows(sel01, x):
    t = x.shape[0]
    r = _dot(sel01, jnp.concatenate(_split3(x), axis=0), NT)
    return r[:, :t] + r[:, t:2 * t] + r[:, 2 * t:]


def _fox_prompt_kernel(q_ref, kt_ref, vt_ref, hs_ref, bf_ref, o_ref, lf_ref, nf_scr, *, l, bq):
    i = pl.program_id(1)
    per = bq // _FB

    @pl.when(i == 0)
    def _():
        utri = _triu_ones(_FB, BF16)
        sel = _col_selector(8, H_SMALL, S_F).astype(BF16)
        carry = jnp.zeros((8, 1), F32)
        for jb in range(l // _FB):
            rows = slice(jb * _FB, (jb + 1) * _FB)
            lf = _log_forget(hs_ref[rows, :], bf_ref[...])
            lf_ref[rows, :] = lf[:, S_F:S_F + GROUP_HEADS]
            loc = _mm_x01(_sel_rows(sel, lf), utri, 3)
            nf_scr[jb // per, :, (jb % per) * _FB:(jb % per + 1) * _FB] = -(loc + carry)
            carry = carry + loc[:, _FB - 1:_FB]

    q = q_ref[...]
    r_i, c_i = _iota2((bq, bq), 0), _iota2((bq, bq), 1)
    qs = [(q[:, _head_slice(h)] * (HEAD_DIM ** -0.5)).astype(BF16) for h in range(GROUP_HEADS)]

    def step(j, mask, carry):
        keys = pl.ds(pl.multiple_of(j * bq, bq), bq)
        return _attend_heads(qs, kt_ref[:, keys], vt_ref[:, keys], nf_scr[j], mask, carry)

    carry = lax.fori_loop(0, i, lambda j, c: step(j, None, c), _attend_init(bq))
    carry = step(i, c_i <= r_i, carry)
    _attend_store(o_ref, carry, bq)


def _fox_prompt(h3, hs3, k_t, v_t, bf_pad):
    bsz, l, _ = h3.shape
    bq = min(l, 2 * _FB)
    kern = functools.partial(_fox_prompt_kernel, l=l, bq=bq)
    return pl.pallas_call(
        kern,
        grid=(bsz, l // bq),
        in_specs=[pl.BlockSpec((None, bq, GW), lambda bi, i: (bi, i, H_DQ // GW)),
                  pl.BlockSpec((None, GW, l), lambda bi, i: (bi, 0, 0)),
                  pl.BlockSpec((None, GW, l), lambda bi, i: (bi, 0, 0)),
                  pl.BlockSpec((None, l, H_SMALL), lambda bi, i: (bi, 0, 0)),
                  pl.BlockSpec((1, H_SMALL), lambda bi, i: (0, 0))],
        out_specs=[pl.BlockSpec((None, bq, GW), lambda bi, i: (bi, i, 0)),
                   pl.BlockSpec((None, l, GROUP_HEADS), lambda bi, i: (bi, 0, 0))],
        out_shape=[jax.ShapeDtypeStruct((bsz, l, GW), BF16),
                   jax.ShapeDtypeStruct((bsz, l, GROUP_HEADS), F32)],
        scratch_shapes=[pltpu.VMEM((l // bq, 8, bq), F32)],
        compiler_params=_params(2),
        name="fox_prompt",
    )(h3, k_t, v_t, hs3, bf_pad)


def _fox_sample_kernel(q_ref, kt_ref, vt_ref, hs_ref, bf_ref, ck_ref, cv_ref, clf_ref, o_ref, lf_ref, cum_scr,
                       *, l, p):
    nblk = p // _FB
    per = _SAMPLE_BK // _FB

    loc = _mm_x01(clf_ref[...].reshape(nblk * 8, _FB), _triu_ones(_FB, BF16), 3)
    carry = jnp.zeros((8, 1), F32)
    for jb in range(nblk):
        blk = loc[jb * 8:(jb + 1) * 8, :]
        cum_scr[jb // per, :, (jb % per) * _FB:(jb % per + 1) * _FB] = blk + carry
        carry = carry + blk[:, _FB - 1:_FB]
    total = carry

    lf = _log_forget(hs_ref[...], bf_ref[...])
    lf_ref[...] = lf[:, S_F:S_F + GROUP_HEADS]
    cum_r = _mm_x01(_sel_rows(_col_selector(8, H_SMALL, S_F).astype(BF16), lf), _triu_ones(l, BF16), 3)
    q = q_ref[...]
    r_i, c_i = _iota2((l, l), 0), _iota2((l, l), 1)
    qs = [(q[:, _head_slice(h)] * (HEAD_DIM ** -0.5)).astype(BF16) for h in range(GROUP_HEADS)]

    def body(j, carry):
        keys = pl.ds(pl.multiple_of(j * _SAMPLE_BK, _SAMPLE_BK), _SAMPLE_BK)
        suffix = total - cum_scr[j]
        return _attend_heads(qs, ck_ref[:, keys], cv_ref[:, keys], suffix, None, carry)

    carry = lax.fori_loop(0, p // _SAMPLE_BK, body, _attend_init(l))
    carry = _attend_heads(qs, kt_ref[...], vt_ref[...], -cum_r, c_i <= r_i, carry)
    _attend_store(o_ref, carry, l)


def _fox_sample(h3, hs3, k_t, v_t, bf_pad, layer, ck_t, cv_t, clogf):
    bsz, l, _ = h3.shape
    p = ck_t.shape[-1]
    clf = clogf.reshape(bsz, p // _FB, _FB, GROUP_HEADS).transpose(0, 1, 3, 2)
    clf = jnp.pad(clf, ((0, 0), (0, 0), (0, 8 - GROUP_HEADS), (0, 0)))
    kern = functools.partial(_fox_sample_kernel, l=l, p=p)
    return pl.pallas_call(
        kern,
        grid=(bsz,),
        in_specs=[pl.BlockSpec((None, l, GW), lambda bi: (bi, 0, H_DQ // GW)),
                  pl.BlockSpec((None, GW, l), lambda bi: (bi, 0, 0)),
                  pl.BlockSpec((None, GW, l), lambda bi: (bi, 0, 0)),
                  pl.BlockSpec((None, l, H_SMALL), lambda bi: (bi, 0, 0)),
                  pl.BlockSpec((1, H_SMALL), lambda bi: (0, 0)),
                  pl.BlockSpec((None, None, GW, p), lambda bi: (layer, bi, 0, 0)),
                  pl.BlockSpec((None, None, GW, p), lambda bi: (layer, bi, 0, 0)),
                  pl.BlockSpec((None, p // _FB, 8, _FB), lambda bi: (bi, 0, 0, 0))],
        out_specs=[pl.BlockSpec((None, l, GW), lambda bi: (bi, 0, 0)),
                   pl.BlockSpec((None, l, GROUP_HEADS), lambda bi: (bi, 0, 0))],
        out_shape=[jax.ShapeDtypeStruct((bsz, l, GW), BF16),
                   jax.ShapeDtypeStruct((bsz, l, GROUP_HEADS), F32)],
        scratch_shapes=[pltpu.VMEM((p // _SAMPLE_BK, 8, _SAMPLE_BK), F32)],
        compiler_params=_params(1),
        name="fox_sample",
    )(h3, k_t, v_t, hs3, bf_pad, ck_t, cv_t, clf)


_R_G, _R_E, _R_ROWS = 0, 8, 32
_MOE_EXPERTS_PER_STEP = 2


def _ffn_kernel(oa_ref, ob_ref, oc_ref, od_ref, x_ref, wo_ref, g1_ref, b1_ref, wrh_ref, wrl_ref, brc_ref, wg_ref,
                wu_ref, wd_ref, g2_ref, b2_ref, y_ref, comb_scr, xb_scr, x1_scr):
    e = pl.program_id(1)
    tm = x_ref.shape[0]
    lane = _iota2((tm, LANE), 1)

    @pl.when(e == 0)
    def _():
        mix = jnp.concatenate([ref[...] for ref in (oa_ref, ob_ref, oc_ref, od_ref)], axis=1)
        x1 = _layer_norm(DN_ALPHA * x_ref[...] + _dot(mix, wo_ref[...]), g1_ref[...], b1_ref[...])
        x1_scr[...] = x1
        y_ref[...] = jnp.zeros_like(y_ref)
        x_hi, x_lo = _split2(x1)
        xb_scr[...] = x_hi
        wr_hi = wrh_ref[...]
        r = _dot(jnp.concatenate([wr_hi, wrl_ref[...]], axis=0), x_hi, NT)
        lt = r[:_R_ROWS] + r[_R_ROWS:] + _dot(wr_hi, x_lo, NT) + brc_ref[...]
        gl = lt[_R_G:_R_G + N_EXPERT_GROUPS]
        grow = _iota2(gl.shape, 0)
        ge = jnp.exp(gl - jnp.max(gl, 0, keepdims=True))
        pg = ge / jnp.sum(ge, 0, keepdims=True)
        gp = jnp.max(pg, 0, keepdims=True)
        gi = jnp.min(jnp.where(pg == gp, grow, N_EXPERT_GROUPS), 0, keepdims=True)
        le = lt[_R_E:_R_E + N_EXPERTS]
        erow = _iota2(le.shape, 0)
        emask = (erow >> 2) == gi
        el = jnp.where(emask, le, NEG)
        ee = jnp.exp(el - jnp.max(el, 0, keepdims=True))
        ep = ee / jnp.sum(ee, 0, keepdims=True)
        m1 = jnp.max(jnp.where(emask, ep, -1.0), 0, keepdims=True)
        i1 = jnp.min(jnp.where(jnp.logical_and(emask, ep == m1), erow, N_EXPERTS), 0, keepdims=True)
        rest = jnp.logical_and(emask, erow != i1)
        m2 = jnp.max(jnp.where(rest, ep, -1.0), 0, keepdims=True)
        i2 = jnp.min(jnp.where(jnp.logical_and(rest, ep == m2), erow, N_EXPERTS), 0, keepdims=True)
        den = m1 + m2
        comb_t = jnp.where(erow == i1, gp * m1 / den, jnp.where(erow == i2, gp * m2 / den, 0.0))
        pieces = jnp.concatenate(_split3(comb_t), axis=0)
        sel = ((_iota2((3 * N_EXPERTS, LANE), 0) & (N_EXPERTS - 1)) == _iota2((3 * N_EXPERTS, LANE), 1)).astype(BF16)
        comb_scr[...] = _dot(pieces, sel, TN)

    xb = xb_scr[...]
    comb = comb_scr[...]
    hidden = []
    for k in range(_MOE_EXPERTS_PER_STEP):
        ce = jnp.sum(jnp.where(lane == e * _MOE_EXPERTS_PER_STEP + k, comb, 0.0), -1, keepdims=True)
        hh = _silu(_mm(xb, wg_ref[k])) * _mm(xb, wu_ref[k])
        hidden.append((hh * ce).astype(BF16))
    hidden = jnp.concatenate(hidden, axis=1)
    w_down = wd_ref[...].reshape(_MOE_EXPERTS_PER_STEP * D_EXPERT, D_MODEL)
    y_ref[...] += _mm(hidden, w_down)

    @pl.when(e == pl.num_programs(1) - 1)
    def _():
        y_ref[...] = _layer_norm(DN_ALPHA * x1_scr[...] + y_ref[...], g2_ref[...], b2_ref[...])


def _ffn(mixers, x2d, w_out_bf16, g1, b1, wr_hi, wr_lo, br_col, layer, wg, wu, wd, g2, b2):
    t = x2d.shape[0]
    tm = min(t, 1024)
    ne = _MOE_EXPERTS_PER_STEP
    mix = pl.BlockSpec((tm, GW), lambda i, e: (i, 0))
    vec = pl.BlockSpec((1, D_MODEL), lambda i, e: (0, 0))
    return pl.pallas_call(
        _ffn_kernel,
        grid=(t // tm, N_EXPERTS // ne),
        in_specs=[mix, mix, mix, mix,
                  pl.BlockSpec((tm, D_MODEL), lambda i, e: (i, 0)),
                  pl.BlockSpec((D_MODEL, D_MODEL), lambda i, e: (0, 0)), vec, vec,
                  pl.BlockSpec((_R_ROWS, D_MODEL), lambda i, e: (0, 0)),
                  pl.BlockSpec((_R_ROWS, D_MODEL), lambda i, e: (0, 0)),
                  pl.BlockSpec((_R_ROWS, 1), lambda i, e: (0, 0)),
                  pl.BlockSpec((None, ne, D_MODEL, D_EXPERT), lambda i, e: (layer, e, 0, 0)),
                  pl.BlockSpec((None, ne, D_MODEL, D_EXPERT), lambda i, e: (layer, e, 0, 0)),
                  pl.BlockSpec((None, ne, D_EXPERT, D_MODEL), lambda i, e: (layer, e, 0, 0)), vec, vec],
        out_specs=pl.BlockSpec((tm, D_MODEL), lambda i, e: (i, 0)),
        out_shape=jax.ShapeDtypeStruct((t, D_MODEL), F32),
        scratch_shapes=[pltpu.VMEM((tm, LANE), F32), pltpu.VMEM((tm, D_MODEL), BF16),
                        pltpu.VMEM((tm, D_MODEL), F32)],
        compiler_params=_params(2),
        name="ffn",
    )(*mixers, x2d, w_out_bf16, g1, b1, wr_hi, wr_lo, br_col, wg, wu, wd, g2, b2)


def _prep_layer(lp):
    w = lp['w_in_t']
    c0 = CONV_COLS
    c1 = c0 + GDN_COLS
    c2 = c1 + RWKV_COLS
    main = jnp.concatenate([w[c0:c0 + 3 * GW], w[c1:c1 + 3 * GW], w[c2:c2 + GW],
                            w[0:GW], w[GW:2 * GW], w[c0 + 3 * GW:c0 + 4 * GW]], axis=0).astype(BF16)
    kv_t = w[c2 + GW:c2 + 3 * GW].astype(BF16)
    small = jnp.concatenate([w[c1 + 3 * GW:c2], w[c0 + 4 * GW:c1], w[c2 + 3 * GW:],
                             jnp.zeros((H_SMALL - LORA - 3 * GROUP_HEADS, D_MODEL), F32)], axis=0)
    gap = jnp.zeros((_R_E - N_EXPERT_GROUPS, D_MODEL), F32)
    tail = jnp.zeros((_R_ROWS - _R_E - N_EXPERTS, D_MODEL), F32)
    wr_t = jnp.concatenate([lp['router_g_w'].T, gap, lp['router_e_w'].T, tail], axis=0)
    wr_hi = wr_t.astype(BF16)
    wr_lo = (wr_t - wr_hi.astype(F32)).astype(BF16)
    br = jnp.concatenate([lp['router_g_b'], gap[:, 0], lp['router_e_b'], tail[:, 0]])[:, None]
    bf_pad = jnp.zeros((1, H_SMALL), F32).at[0, S_F:S_F + GROUP_HEADS].set(lp['fox_b_f'])
    return dict(w_main=main, w_small=small, w_kv_t=kv_t, wr_hi=wr_hi, wr_lo=wr_lo, br=br, bf_pad=bf_pad)


def _trunk_layer(x, lp, pp, layer, big, conv_buf, gdn_buf, gdn_s, rw_shift, rw_s, fox_cache):
    bsz, l, d = x.shape
    x2d = x.reshape(bsz * l, d)
    h3, hs3, k_t, v_t = _inproj(x, pp['w_main'], pp['w_small'], pp['w_kv_t'])

    o_a, new_conv = _conv_mixer(h3, conv_buf, lp['conv_w'], lp['conv_b'][None], lp['conv_ln_g'][None],
                                lp['conv_ln_b'][None])
    o_b, new_gdn_s, new_gdn_buf = _gdn_mixer(h3, hs3, gdn_buf, gdn_s, lp['gdn_conv_w'], lp['gdn_a_log'],
                                             lp['gdn_dt_bias'], lp['gdn_norm_w'])
    o_c, new_rw_s, new_shift = _rwkv_mixer(h3, hs3, rw_shift, rw_s, lp)
    if fox_cache is None:
        o_d, logf = _fox_prompt(h3, hs3, k_t, v_t, pp['bf_pad'])
    else:
        o_d, logf = _fox_sample(h3, hs3, k_t, v_t, pp['bf_pad'], layer, *fox_cache)

    mixers = [o.reshape(bsz * l, GW) for o in (o_a, o_b, o_c, o_d)]
    x2 = _ffn(mixers, x2d, big['w_out'][layer], lp['ln1_g'][None], lp['ln1_b'][None], pp['wr_hi'], pp['wr_lo'],
              pp['br'], layer, *big['experts'], lp['ln2_g'][None], lp['ln2_b'][None])
    return x2.reshape(bsz, l, d), (new_conv, new_gdn_buf, new_gdn_s, new_shift, new_rw_s, k_t, v_t, logf)


def _untranspose(xs_t):
    bsz, _, l = xs_t[0].shape
    return jnp.stack(xs_t).reshape(len(xs_t), bsz, GROUP_HEADS, HEAD_DIM, l).transpose(0, 1, 4, 2, 3)


_LAYER_KEYS = ('w_in_t', 'conv_w', 'conv_b', 'conv_ln_g', 'conv_ln_b', 'gdn_conv_w', 'gdn_a_log', 'gdn_dt_bias',
               'gdn_norm_w', 'rwkv_mu', 'rwkv_w0', 'rwkv_w_up', 'rwkv_a0', 'rwkv_a_up', 'rwkv_g_up', 'rwkv_k_k',
               'rwkv_k_a', 'rwkv_r_k', 'rwkv_lnx_g', 'rwkv_lnx_b', 'fox_b_f', 'ln1_g', 'ln1_b',
               'router_g_w', 'router_g_b', 'router_e_w', 'router_e_b', 'ln2_g', 'ln2_b')


def kernel(x_prompt, x_sample, cache_fox_k, cache_fox_v, cache_fox_logf, state_conv, state_gdn_conv, state_gdn, state_rwkv_shift, state_rwkv, w_in, conv_w, conv_b, conv_ln_g, conv_ln_b, gdn_conv_w, gdn_a_log, gdn_dt_bias, gdn_norm_w, rwkv_mu, rwkv_w0, rwkv_w_up, rwkv_a0, rwkv_a_up, rwkv_g_up, rwkv_k_k, rwkv_k_a, rwkv_r_k, rwkv_lnx_g, rwkv_lnx_b, fox_b_f, w_out, ln1_g, ln1_b, router_g_w, router_g_b, router_e_w, router_e_b, exp_w_gate, exp_w_up, exp_w_down, ln2_g, ln2_b):
    w_in_t = w_in.transpose(0, 2, 1)
    weights = dict(zip(_LAYER_KEYS, (w_in_t, conv_w, conv_b, conv_ln_g, conv_ln_b, gdn_conv_w, gdn_a_log,
                                     gdn_dt_bias, gdn_norm_w, rwkv_mu, rwkv_w0, rwkv_w_up, rwkv_a0, rwkv_a_up,
                                     rwkv_g_up, rwkv_k_k, rwkv_k_a, rwkv_r_k, rwkv_lnx_g, rwkv_lnx_b, fox_b_f,
                                     ln1_g, ln1_b, router_g_w, router_g_b, router_e_w, router_e_b,
                                     ln2_g, ln2_b)))
    big = dict(w_out=w_out.astype(BF16), experts=(exp_w_gate, exp_w_up, exp_w_down))
    depth, bs, past = cache_fox_k.shape[:3]
    ck_t = cache_fox_k.transpose(0, 1, 3, 4, 2).reshape(depth, bs, GW, past)
    cv_t = cache_fox_v.transpose(0, 1, 3, 4, 2).reshape(depth, bs, GW, past)
    xp, xs = x_prompt, x_sample
    bp = x_prompt.shape[0]
    outs_p = [[] for _ in range(8)]
    outs_s = [[] for _ in range(8)]
    for l in range(DEPTH):
        lp = {k: v[l] for k, v in weights.items()}
        pp = _prep_layer(lp)
        xp, st_p = _trunk_layer(xp, lp, pp, l, big,
                                jnp.zeros((bp, CONV_WIDTH - 1, GW), F32),
                                jnp.zeros((bp, GDN_CONV_WIDTH - 1, 3 * GW), F32),
                                jnp.zeros((bp, GROUP_HEADS, HEAD_DIM, HEAD_DIM), F32),
                                jnp.zeros((bp, RWKV_COLS), F32),
                                jnp.zeros((bp, GROUP_HEADS, HEAD_DIM, HEAD_DIM), F32),
                                None)
        xs, st_s = _trunk_layer(xs, lp, pp, l, big, state_conv[l], state_gdn_conv[l], state_gdn[l],
                                state_rwkv_shift[l], state_rwkv[l], (ck_t, cv_t, cache_fox_logf[l]))
        for i in range(8):
            outs_p[i].append(st_p[i])
            outs_s[i].append(st_s[i])

    def assemble(outs):
        conv, gdn_buf, gdn_s, shift, rw_s = (jnp.stack(o) for o in outs[:5])
        return conv, gdn_buf, gdn_s, shift, rw_s, _untranspose(outs[5]), _untranspose(outs[6]), jnp.stack(outs[7])

    return (xp, xs, *assemble(outs_p), *assemble(outs_s))
```

```python
import functools

import jax
import jax.numpy as jnp
from jax import lax
from jax.experimental import pallas as pl
from jax.experimental.pallas import tpu as pltpu

F32 = jnp.float32
BF16 = jnp.bfloat16

D_MODEL = 1024
DEPTH = 2
CHUNK = 64
HEAD_DIM = 64
GW = D_MODEL // 4
GROUP_HEADS = GW // HEAD_DIM
CONV_WIDTH = 31
GDN_CONV_WIDTH = 4
RWKV_W_LORA = 16
RWKV_A_LORA = 16
RWKV_G_LORA = 32
N_EXPERT_GROUPS = 4
EXPERTS_PER_GROUP = 4
N_EXPERTS = N_EXPERT_GROUPS * EXPERTS_PER_GROUP
D_EXPERT = 256
DN_ALPHA = (2 * DEPTH) ** 0.25
LN_EPS = 1e-5
GN_EPS = 64e-5

CONV_COLS = 2 * GW
GDN_COLS = 4 * GW + 2 * GROUP_HEADS
RWKV_COLS = 3 * GW + RWKV_W_LORA + RWKV_A_LORA + RWKV_G_LORA
FOX_COLS = 3 * GW + GROUP_HEADS
LORA = RWKV_W_LORA + RWKV_A_LORA + RWKV_G_LORA

H_BQKV, H_CRKV, H_DQ, H_AVAL, H_AGATE, H_BZ = 0, 768, 1536, 1792, 2048, 2304
H_MAIN = 2560
KV_ROWS = 2 * GW
S_LORA, S_A, S_B, S_F = 0, 64, 68, 72
H_SMALL = 128
LANE = 128

VMEM_LIMIT = 56 * 1024 * 1024

NN = (((1,), (0,)), ((), ()))
NT = (((1,), (1,)), ((), ()))
TN = (((0,), (0,)), ((), ()))
NEG = -1e30


def _dot(a, b, dims=NN):
    return lax.dot_general(a, b, dims, preferred_element_type=F32)


def _mm(a, b, dims=NN):
    return _dot(a.astype(BF16), b.astype(BF16), dims)


def _mmh(a, b, dims=NN):
    return lax.dot_general(a, b, dims, preferred_element_type=F32, precision=lax.Precision.HIGHEST)


def _sigmoid(x):
    return 0.5 * (jnp.tanh(0.5 * x) + 1.0)


def _silu(x):
    return x * _sigmoid(x)


def _softplus(x):
    return jnp.maximum(x, 0.0) + jnp.log(1.0 + jnp.exp(-jnp.abs(x)))


def _iota2(shape, dim):
    return lax.broadcasted_iota(jnp.int32, shape, dim)


def _tril_ones(n, dtype=F32):
    r, c = _iota2((n, n), 0), _iota2((n, n), 1)
    return (r >= c).astype(dtype)


def _triu_ones(n, dtype=F32):
    r, c = _iota2((n, n), 0), _iota2((n, n), 1)
    return (r <= c).astype(dtype)


def _col_selector(rows, width, col0):
    r, c = _iota2((rows, width), 0), _iota2((rows, width), 1)
    return (c == r + col0).astype(F32)


def _layer_norm(x, g, b):
    mu = jnp.mean(x, axis=-1, keepdims=True)
    xc = x - mu
    var = jnp.mean(xc * xc, axis=-1, keepdims=True)
    return xc * lax.rsqrt(var + LN_EPS) * g + b


def _split2(x):
    hi = x.astype(BF16)
    lo = (x - hi.astype(F32)).astype(BF16)
    return hi, lo


def _split3(x):
    hi = x.astype(BF16)
    r = x - hi.astype(F32)
    mid = r.astype(BF16)
    lo = (r - mid.astype(F32)).astype(BF16)
    return hi, mid, lo


class _Wide:
    def __init__(self):
        self.i = _iota2((CHUNK, GW), 0)
        lane = _iota2((CHUNK, GW), 1)
        self.j = lane & (HEAD_DIM - 1)
        self.head = lane >> 6
        self.incl = self.i >= self.j
        self.strict = self.i > self.j
        self.eye = self.i == self.j
        r, c = _iota2((GW, GW), 0), _iota2((GW, GW), 1)
        self.block_diag = (r >> 6) == (c >> 6)


def _head_block_diag(x, w):
    t = jnp.concatenate([x] * GROUP_HEADS, axis=0)
    return jnp.where(w.block_diag, t, jnp.zeros_like(t))


def _collapse_heads(full, w):
    out = jnp.zeros((HEAD_DIM, GW), F32)
    for h in range(GROUP_HEADS):
        out = out + jnp.where(w.head == h, full[h * HEAD_DIM:(h + 1) * HEAD_DIM, :], 0.0)
    return out


def _lhs3(a, precise=False):
    if not precise:
        return None, a.astype(BF16)
    hi, lo = _split2(a)
    return jnp.concatenate([hi, lo], axis=0), hi


def _rhs3(b, w=None, precise=False):
    if not precise:
        hi = b.astype(BF16)
        return (hi if w is None else _head_block_diag(hi, w)), None
    hi, lo = _split2(b)
    if w is None:
        return hi, lo
    return _head_block_diag(hi, w), _head_block_diag(lo, w)


def _mm3(lhs, rhs, dims=NN):
    cat, hi = lhs
    bh, bl = rhs
    if cat is None or bl is None:
        return _dot(hi, bh, dims)
    m = hi.shape[0]
    r = _dot(cat, bh, dims)
    return r[:m] + r[m:] + _dot(hi, bl, dims)


def _mm3_tn(a, b, precise=False):
    if not precise:
        return _dot(a.astype(BF16), b.astype(BF16), TN)
    ah, al = _split2(a)
    bh, bl = _split2(b)
    m = a.shape[1]
    r = _dot(jnp.concatenate([ah, al], axis=1), bh, TN)
    return r[:m] + r[m:] + _dot(ah, bl, TN)


def _mm_x01(a, m01, pieces=2):
    m = a.shape[0]
    r = _dot(jnp.concatenate(_split3(a) if pieces == 3 else _split2(a), axis=0), m01)
    return sum(r[i * m:(i + 1) * m] for i in range(pieces))


def _tri_mm(tri01, x, pieces=2):
    n = x.shape[1]
    r = _dot(tri01, jnp.concatenate(_split3(x) if pieces == 3 else _split2(x), axis=1))
    return sum(r[:, i * n:(i + 1) * n] for i in range(pieces))


def _unit_lower_inverse_wide(mats, w):
    eye = w.eye.astype(F32)
    same8 = (w.i >> 3) == (w.j >> 3)
    ds = [jnp.where(same8, a, 0.0) for a in mats]
    d2 = [_mm3(_lhs3(d), _rhs3(d, w)) for d in ds]
    d2_r = [_rhs3(x, w) for x in d2]
    d4 = [_mm3(_lhs3(x), r) for x, r in zip(d2, d2_r)]
    imd = [eye - d for d in ds]
    p1 = [i + _mm3(_lhs3(i), r) for i, r in zip(imd, d2_r)]
    ts = [p + _mm3(_lhs3(p), _rhs3(x, w)) for p, x in zip(p1, d4)]
    for s in range(3, 6):
        big = (w.i >> (s + 1)) == (w.j >> (s + 1))
        small = (w.i >> s) == (w.j >> s)
        sel = jnp.logical_and(big, jnp.logical_not(small))
        inner = [_mm3(_lhs3(jnp.where(sel, a, 0.0)), _rhs3(t, w)) for a, t in zip(mats, ts)]
        ts = [t - _mm3(_lhs3(t), _rhs3(x, w)) for t, x in zip(ts, inner)]
    return ts


def _params(n_axes):
    return pltpu.CompilerParams(dimension_semantics=("arbitrary",) * n_axes,
                                vmem_limit_bytes=VMEM_LIMIT)


def _heads_to_wide(s):
    b = s.shape[0]
    return s.transpose(0, 2, 1, 3).reshape(b, HEAD_DIM, GW)


def _wide_to_heads(s):
    b = s.shape[0]
    return s.reshape(b, HEAD_DIM, GROUP_HEADS, HEAD_DIM).transpose(0, 2, 1, 3)


def _ones_block_diag():
    r, c = jnp.arange(GW)[:, None], jnp.arange(GW)[None, :]
    return ((r // HEAD_DIM) == (c // HEAD_DIM)).astype(BF16)


def _lane_spread(col0):
    r, c = jnp.arange(H_SMALL)[:, None], jnp.arange(GW)[None, :]
    return ((r - col0) == (c // HEAD_DIM)).astype(BF16)


def _inproj_kernel(x_ref, wm_ref, wsh_ref, wsl_ref, wkv_ref, h_ref, hs_ref, kt_ref, vt_ref):
    x = x_ref[...]
    x_hi, x_lo = _split2(x)
    tm = x.shape[0]
    h_ref[...] = _dot(x_hi, wm_ref[...], NT)
    r = _dot(jnp.concatenate([x_hi, x_lo], axis=0), wsh_ref[...], NT)
    hs_ref[...] = r[:tm] + r[tm:] + _dot(x_hi, wsl_ref[...], NT)
    kv = _dot(wkv_ref[...], x_hi, NT)
    kt_ref[...] = kv[:GW]
    vt_ref[...] = kv[GW:]


def _inproj(x3, w_main_t, w_small_t, w_kv_t):
    bsz, l, _ = x3.shape
    tm = min(l, 512)
    const = lambda shape: pl.BlockSpec(shape, lambda bi, i: (0, 0))
    ws_hi = w_small_t.astype(BF16)
    ws_lo = (w_small_t - ws_hi.astype(F32)).astype(BF16)
    kv_spec = pl.BlockSpec((None, GW, tm), lambda bi, i: (bi, 0, i))
    return pl.pallas_call(
        _inproj_kernel,
        grid=(bsz, l // tm),
        in_specs=[pl.BlockSpec((None, tm, D_MODEL), lambda bi, i: (bi, i, 0)),
                  const((H_MAIN, D_MODEL)), const((H_SMALL, D_MODEL)), const((H_SMALL, D_MODEL)),
                  const((KV_ROWS, D_MODEL))],
        out_specs=[pl.BlockSpec((None, tm, H_MAIN), lambda bi, i: (bi, i, 0)),
                   pl.BlockSpec((None, tm, H_SMALL), lambda bi, i: (bi, i, 0)), kv_spec, kv_spec],
        out_shape=[jax.ShapeDtypeStruct((bsz, l, H_MAIN), F32), jax.ShapeDtypeStruct((bsz, l, H_SMALL), F32),
                   jax.ShapeDtypeStruct((bsz, GW, l), F32), jax.ShapeDtypeStruct((bsz, GW, l), F32)],
        compiler_params=_params(2),
        name="inproj",
    )(x3, w_main_t, ws_hi, ws_lo, w_kv_t)


_HALO = 32
_SUBLANES = 8


def _conv_kernel(val_ref, gate_ref, buf_ref, cw_ref, cb_ref, g_ref, b_ref, o_ref, nbuf_ref, win_ref, sh_ref, *, tl):
    i = pl.program_id(1)
    pad = _HALO - (CONV_WIDTH - 1)
    span = tl + _HALO - _SUBLANES

    @pl.when(i == 0)
    def _():
        win_ref[0:8, :] = jnp.zeros((8, GW), F32)
        win_ref[pad:_HALO, :] = buf_ref[...]

    u = val_ref[...] * _sigmoid(gate_ref[...])
    win_ref[_HALO:_HALO + tl, :] = u
    for s in range(1, _SUBLANES):
        sh_ref[s, :, :] = win_ref[s:s + span, :]
    sub = min(tl, 64)
    for r0 in range(0, tl, sub):
        acc = jnp.zeros((sub, GW), F32)
        for j in range(CONV_WIDTH):
            s, base = (pad + j) % _SUBLANES, (pad + j) // _SUBLANES * _SUBLANES + r0
            rows = win_ref[base:base + sub, :] if s == 0 else sh_ref[s, base:base + sub, :]
            acc = acc + cw_ref[j:j + 1, :] * rows
        y = _layer_norm(acc + cb_ref[...], g_ref[...], b_ref[...])
        o_ref[r0:r0 + sub, :] = _silu(y).astype(o_ref.dtype)
    tail = win_ref[tl:tl + _HALO, :]
    win_ref[0:_HALO, :] = tail

    @pl.when(i == pl.num_programs(1) - 1)
    def _():
        nbuf_ref[...] = win_ref[pad:_HALO, :]


def _conv_mixer(h3, buf, cw, cb, g, b):
    bsz, l, _ = h3.shape
    tl = min(l, 256)
    kern = functools.partial(_conv_kernel, tl=tl)
    full = lambda shape: pl.BlockSpec(shape, lambda bi, i: (0,) * len(shape))
    return pl.pallas_call(
        kern,
        grid=(bsz, l // tl),
        in_specs=[pl.BlockSpec((None, tl, GW), lambda bi, i: (bi, i, H_AVAL // GW)),
                  pl.BlockSpec((None, tl, GW), lambda bi, i: (bi, i, H_AGATE // GW)),
                  pl.BlockSpec((None, CONV_WIDTH - 1, GW), lambda bi, i: (bi, 0, 0)),
                  full((CONV_WIDTH, GW)), full((1, GW)), full((1, GW)), full((1, GW))],
        out_specs=[pl.BlockSpec((None, tl, GW), lambda bi, i: (bi, i, 0)),
                   pl.BlockSpec((None, CONV_WIDTH - 1, GW), lambda bi, i: (bi, 0, 0))],
        out_shape=[jax.ShapeDtypeStruct((bsz, l, GW), BF16),
                   jax.ShapeDtypeStruct((bsz, CONV_WIDTH - 1, GW), F32)],
        scratch_shapes=[pltpu.VMEM((tl + _HALO, GW), F32),
                        pltpu.VMEM((_SUBLANES, tl + _HALO - _SUBLANES, GW), F32)],
        compiler_params=_params(2),
        name="conv_mixer",
    )(h3, h3, buf, cw, cb, g, b)


def _gdn_kernel(qkv_ref, z_ref, hs_ref, buf_ref, s0_ref, cw_ref, alog_ref, dtb_ref, nw_ref, ones_ref, sela_ref,
                selb_ref, o_ref, sout_ref, nbuf_ref, win_ref, s_scr, *, nb, nc):
    c = pl.program_id(1)
    n = CHUNK
    rows = nc * n
    kw = GDN_CONV_WIDTH - 1
    items = [(b, ci) for b in range(nb) for ci in range(nc)]

    @pl.when(c == 0)
    def _():
        for b in range(nb):
            win_ref[b, 0:8, :] = jnp.zeros((8, 3 * GW), F32)
            win_ref[b, 8 - kw:8, :] = buf_ref[b]
        s_scr[...] = s0_ref[...]

    qkvs = []
    for b in range(nb):
        win_ref[b, 8:8 + rows, :] = qkv_ref[b]
        conv = jnp.zeros((rows, 3 * GW), F32)
        for j in range(GDN_CONV_WIDTH):
            conv = conv + cw_ref[j:j + 1, :] * win_ref[b, 8 - kw + j:8 - kw + j + rows, :]
        tail = win_ref[b, rows:rows + 8, :]
        win_ref[b, 0:8, :] = tail
        qkvs.append(_silu(conv))

    w = _Wide()
    ones_bd = ones_ref[...]
    tril = _tril_ones(n, BF16)
    sl = lambda ci: slice(ci * n, (ci + 1) * n)
    qs = [qkvs[b][sl(ci), 0:GW] for b, ci in items]
    ks = [qkvs[b][sl(ci), GW:2 * GW] for b, ci in items]
    vs = [qkvs[b][sl(ci), 2 * GW:3 * GW] for b, ci in items]
    sss = [_mm_x01(jnp.concatenate([q * q, k * k], axis=0), ones_bd) for q, k in zip(qs, ks)]
    qs = [q * lax.rsqrt(ss[:n] + 1e-6) * (HEAD_DIM ** -0.5) for q, ss in zip(qs, sss)]
    ks = [k * lax.rsqrt(ss[n:] + 1e-6) for k, ss in zip(ks, sss)]

    hss = [hs_ref[b, sl(ci), :] for b, ci in items]
    gs = [_mm_x01(-jnp.exp(alog_ref[...]) * _softplus(hs + dtb_ref[...]), sela_ref[...]) for hs in hss]
    betas = [_mm_x01(_sigmoid(hs), selb_ref[...]) for hs in hss]
    gcs = [_tri_mm(tril, g) for g in gs]
    grs = [jnp.sum(jnp.where(w.eye, gc, 0.0), axis=0, keepdims=True) for gc in gcs]
    decays = [jnp.exp(jnp.where(w.incl, gc - gr, NEG)) for gc, gr in zip(gcs, grs)]

    kbs = [k * beta for k, beta in zip(ks, betas)]
    grams = [_mm3(_lhs3(jnp.concatenate([kb, q], axis=0)), _rhs3(k, w), NT)
             for kb, q, k in zip(kbs, qs, ks)]
    amats = [jnp.where(w.strict, gram[:n] * decay, 0.0) for gram, decay in zip(grams, decays)]
    qks = [jnp.where(w.incl, gram[n:] * decay, 0.0) for gram, decay in zip(grams, decays)]
    t_ls = [_lhs3(t) for t in _unit_lower_inverse_wide(amats, w)]
    egs = [jnp.exp(gc) for gc in gcs]
    xvs = [_mm3(t_l, _rhs3(v * beta, w)) for t_l, v, beta in zip(t_ls, vs, betas)]
    xks = [_mm3(t_l, _rhs3(kb * eg, w)) for t_l, kb, eg in zip(t_ls, kbs, egs)]
    xq_ls = [_lhs3(jnp.concatenate([xk, q * eg], axis=0)) for xk, q, eg in zip(xks, qs, egs)]
    qk_ls = [_lhs3(qk) for qk in qks]
    gls = [gc[n - 1:n, :] for gc in gcs]
    kds = [k * jnp.exp(gl - gc) for k, gl, gc in zip(ks, gls, gcs)]

    states = [s_scr[b] for b in range(nb)]
    outs = {}
    for ci in range(nc):
        for b in range(nb):
            it = b * nc + ci
            s = states[b]
            rs = _mm3(xq_ls[it], _rhs3(s, w))
            u = xvs[it] - rs[:n]
            outs[it] = rs[n:] + _mm3(qk_ls[it], _rhs3(u, w))
            states[b] = s * jnp.exp(gls[it]) + _collapse_heads(_mm3_tn(kds[it], u), w)
    for b in range(nb):
        s_scr[b] = states[b]

    for it, (b, ci) in enumerate(items):
        o = outs[it]
        ms = _mm_x01(o * o, ones_bd) * (1.0 / HEAD_DIM)
        o = o * lax.rsqrt(ms + 1e-6) * nw_ref[...] * _silu(z_ref[b, sl(ci), :])
        o_ref[b, sl(ci), :] = o.astype(o_ref.dtype)

    @pl.when(c == pl.num_programs(1) - 1)
    def _():
        sout_ref[...] = s_scr[...]
        for b in range(nb):
            nbuf_ref[b] = win_ref[b, 8 - kw:8, :]


def _recurrent_tiling(bsz, l):
    nc = min(l // CHUNK, 4)
    nb = 2 if nc > 1 else min(bsz, 4)
    return nb, nc


def _gdn_mixer(h3, hs3, buf, s0, cw, a_log, dt_bias, norm_w):
    bsz, l, _ = h3.shape
    nb, nc = _recurrent_tiling(bsz, l)
    rows = nc * CHUNK
    pad_r = lambda v, off: jnp.zeros((1, H_SMALL), F32).at[0, off:off + GROUP_HEADS].set(v)
    full = lambda shape: pl.BlockSpec(shape, lambda bi, i: (0,) * len(shape))
    o, s_new, new_buf = pl.pallas_call(
        functools.partial(_gdn_kernel, nb=nb, nc=nc),
        grid=(bsz // nb, l // rows),
        in_specs=[pl.BlockSpec((nb, rows, 3 * GW), lambda bi, i: (bi, i, H_BQKV // (3 * GW))),
                  pl.BlockSpec((nb, rows, GW), lambda bi, i: (bi, i, H_BZ // GW)),
                  pl.BlockSpec((nb, rows, H_SMALL), lambda bi, i: (bi, i, 0)),
                  pl.BlockSpec((nb, GDN_CONV_WIDTH - 1, 3 * GW), lambda bi, i: (bi, 0, 0)),
                  pl.BlockSpec((nb, HEAD_DIM, GW), lambda bi, i: (bi, 0, 0)),
                  full((GDN_CONV_WIDTH, 3 * GW)), full((1, H_SMALL)), full((1, H_SMALL)), full((1, GW)),
                  full((GW, GW)), full((H_SMALL, GW)), full((H_SMALL, GW))],
        out_specs=[pl.BlockSpec((nb, rows, GW), lambda bi, i: (bi, i, 0)),
                   pl.BlockSpec((nb, HEAD_DIM, GW), lambda bi, i: (bi, 0, 0)),
                   pl.BlockSpec((nb, GDN_CONV_WIDTH - 1, 3 * GW), lambda bi, i: (bi, 0, 0))],
        out_shape=[jax.ShapeDtypeStruct((bsz, l, GW), BF16),
                   jax.ShapeDtypeStruct((bsz, HEAD_DIM, GW), F32),
                   jax.ShapeDtypeStruct((bsz, GDN_CONV_WIDTH - 1, 3 * GW), F32)],
        scratch_shapes=[pltpu.VMEM((nb, rows + 8, 3 * GW), F32), pltpu.VMEM((nb, HEAD_DIM, GW), F32)],
        compiler_params=_params(2),
        name="gdn_mixer",
    )(h3, h3, hs3, buf, _heads_to_wide(s0), cw, pad_r(a_log, S_A), pad_r(dt_bias, S_A),
      jnp.tile(norm_w, GROUP_HEADS)[None, :], _ones_block_diag(), _lane_spread(S_A), _lane_spread(S_B))
    return o, _wide_to_heads(s_new), new_buf


def _rwkv_kernel(rkv_ref, hs_ref, sh_ref, shs_ref, s0_ref, mu_ref, mus_ref, w0_ref, wup_ref, a0_ref, aup_ref,
                 gup_ref, kk_ref, ka_ref, rk_ref, lng_ref, lnb_ref, ones_ref, o_ref, sout_ref, nsh_ref, nshs_ref,
                 win_ref, wins_ref, s_scr, *, nb, nc):
    c = pl.program_id(1)
    n = CHUNK

    rows = nc * n
    items = [(b, ci) for b in range(nb) for ci in range(nc)]

    @pl.when(c == 0)
    def _():
        for b in range(nb):
            win_ref[b, 0:8, :] = jnp.zeros((8, 3 * GW), F32)
            wins_ref[b, 0:8, :] = jnp.zeros((8, H_SMALL), F32)
            win_ref[b, 7:8, :] = sh_ref[b]
            wins_ref[b, 7:8, :] = shs_ref[b]
        s_scr[...] = s0_ref[...]

    wup_r, aup_r, gup_r = (_rhs3(r[...], precise=True) for r in (wup_ref, aup_ref, gup_ref))
    per_b = []
    for b in range(nb):
        x = rkv_ref[b]
        xs = hs_ref[b]
        win_ref[b, 8:8 + rows, :] = x
        wins_ref[b, 8:8 + rows, :] = xs
        xm = x + (win_ref[b, 7:7 + rows, :] - x) * mu_ref[...]
        xms = xs + (wins_ref[b, 7:7 + rows, :] - xs) * mus_ref[...]
        last = win_ref[b, rows + 7:rows + 8, :]
        win_ref[b, 7:8, :] = last
        lasts = wins_ref[b, rows + 7:rows + 8, :]
        wins_ref[b, 7:8, :] = lasts
        w_pre = w0_ref[...] + _mm3(_lhs3(jnp.tanh(xms), precise=True), wup_r)
        logw = -jnp.exp(-_softplus(-w_pre) - 0.5)
        a_sig = _sigmoid(a0_ref[...] + _mm3(_lhs3(xms, precise=True), aup_r))
        gate = _mm3(_lhs3(_sigmoid(xms), precise=True), gup_r)
        kx = xm[:, GW:2 * GW]
        per_b.append(dict(rr=xm[:, 0:GW], vv=xm[:, 2 * GW:3 * GW], logw=logw, a_sig=a_sig, gate=gate,
                          kkp=kx * kk_ref[...], k2=kx * (1.0 + (a_sig - 1.0) * ka_ref[...])))

    w = _Wide()
    ones_bd = ones_ref[...]
    tril = _tril_ones(n, BF16)
    sl = lambda ci: slice(ci * n, (ci + 1) * n)
    get = lambda name: [per_b[b][name][sl(ci), :] for b, ci in items]
    rrs, vvs, logws, a_sigs, gates, kkps, k2s = (get(k) for k in ('rr', 'vv', 'logw', 'a_sig', 'gate', 'kkp', 'k2'))
    kks = [kkp * lax.rsqrt(_mm_x01(kkp * kkp, ones_bd) + 1e-6) for kkp in kkps]
    cums = [_tri_mm(tril, logw) for logw in logws]
    w_incls = [jnp.exp(cum) for cum in cums]
    w_lasts = [wi[n - 1:n, :] for wi in w_incls]
    w_invs = [jnp.exp(-cum) for cum in cums]
    ats = [-kk * jnp.exp(cum - logw) for kk, cum, logw in zip(kks, cums, logws)]
    bts = [kk * a_sig * wv for kk, a_sig, wv in zip(kks, a_sigs, w_invs)]
    kts = [k2 * wv for k2, wv in zip(k2s, w_invs)]
    rts = [rr * wi for rr, wi in zip(rrs, w_incls)]
    ar_ls = [_lhs3(jnp.concatenate([at, rt], axis=0)) for at, rt in zip(ats, rts)]
    gbs = [_mm3(ar_l, _rhs3(bt, w), NT) for ar_l, bt in zip(ar_ls, bts)]
    gks = [_mm3(ar_l, _rhs3(kt, w), NT) for ar_l, kt in zip(ar_ls, kts)]
    t_ls = [_lhs3(t) for t in _unit_lower_inverse_wide([jnp.where(w.strict, -gb[:n], 0.0) for gb in gbs], w)]
    arb_ls = [_lhs3(jnp.where(w.incl, gb[n:], 0.0)) for gb in gbs]
    ark_ls = [_lhs3(jnp.where(w.incl, gk[n:], 0.0)) for gk in gks]
    v_rs = [_rhs3(vv, w) for vv in vvs]
    aakvs = [_mm3(_lhs3(jnp.where(w.strict, gk[:n], 0.0)), v_r) for gk, v_r in zip(gks, v_rs)]
    arkvs = [_mm3(ark_l, v_r) for ark_l, v_r in zip(ark_ls, v_rs)]
    bks = [jnp.concatenate([bt * wl, kt * wl], axis=0) for bt, kt, wl in zip(bts, kts, w_lasts)]

    states = [s_scr[b] for b in range(nb)]
    ys = {}
    for ci in range(nc):
        for b in range(nb):
            it = b * nc + ci
            s0 = states[b]
            ars = _mm3(ar_ls[it], _rhs3(s0, w), NT)
            u = _mm3(t_ls[it], _rhs3(ars[:n] + aakvs[it], w))
            ys[it] = ars[n:] + _mm3(arb_ls[it], _rhs3(u, w)) + arkvs[it]
            upd = _mm3_tn(jnp.concatenate([u, vvs[it]], axis=0), bks[it])
            states[b] = s0 * w_lasts[it] + _collapse_heads(upd, w)
    for b in range(nb):
        s_scr[b] = states[b]

    inv_d = 1.0 / HEAD_DIM
    for it, (b, ci) in enumerate(items):
        y = ys[it]
        yc = y - _mm_x01(y, ones_bd) * inv_d
        var_y = _mm_x01(yc * yc, ones_bd) * inv_d
        yn = yc * lax.rsqrt(var_y + GN_EPS) * lng_ref[...] + lnb_ref[...]
        bonus = _mm_x01(rrs[it] * k2s[it] * rk_ref[...], ones_bd) * vvs[it]
        o_ref[b, sl(ci), :] = ((yn + bonus) * gates[it]).astype(o_ref.dtype)

    @pl.when(c == pl.num_programs(1) - 1)
    def _():
        sout_ref[...] = s_scr[...]
        for b in range(nb):
            nsh_ref[b] = win_ref[b, 7:8, :]
            nshs_ref[b] = wins_ref[b, 7:8, :]


def _rwkv_mixer(h3, hs3, shift, s0, lp):
    bsz, l, _ = h3.shape
    nb, nc = _recurrent_tiling(bsz, l)
    rows = nc * CHUNK
    sh_main = shift[:, None, 0:3 * GW]
    sh_small = jnp.pad(shift[:, None, 3 * GW:], ((0, 0), (0, 0), (0, H_SMALL - LORA)))
    mu = lp['rwkv_mu']
    mu_main = mu[None, 0:3 * GW]
    mu_small = jnp.pad(mu[None, 3 * GW:], ((0, 0), (0, H_SMALL - LORA)))
    place = lambda w, off: jnp.zeros((H_SMALL, GW), F32).at[off:off + w.shape[0]].set(w)
    wup = place(lp['rwkv_w_up'], 0)
    aup = place(lp['rwkv_a_up'], RWKV_W_LORA)
    gup = place(lp['rwkv_g_up'], RWKV_W_LORA + RWKV_A_LORA)
    row = lambda v: v.reshape(1, GW)
    full = lambda shape: pl.BlockSpec(shape, lambda bi, i: (0,) * len(shape))
    o, s_new, sh_new, shs_new = pl.pallas_call(
        functools.partial(_rwkv_kernel, nb=nb, nc=nc),
        grid=(bsz // nb, l // rows),
        in_specs=[pl.BlockSpec((nb, rows, 3 * GW), lambda bi, i: (bi, i, H_CRKV // (3 * GW))),
                  pl.BlockSpec((nb, rows, H_SMALL), lambda bi, i: (bi, i, 0)),
                  pl.BlockSpec((nb, 1, 3 * GW), lambda bi, i: (bi, 0, 0)),
                  pl.BlockSpec((nb, 1, H_SMALL), lambda bi, i: (bi, 0, 0)),
                  pl.BlockSpec((nb, HEAD_DIM, GW), lambda bi, i: (bi, 0, 0)),
                  full((1, 3 * GW)), full((1, H_SMALL)), full((1, GW)), full((H_SMALL, GW)), full((1, GW)),
                  full((H_SMALL, GW)), full((H_SMALL, GW)), full((1, GW)), full((1, GW)), full((1, GW)),
                  full((1, GW)), full((1, GW)), full((GW, GW))],
        out_specs=[pl.BlockSpec((nb, rows, GW), lambda bi, i: (bi, i, 0)),
                   pl.BlockSpec((nb, HEAD_DIM, GW), lambda bi, i: (bi, 0, 0)),
                   pl.BlockSpec((nb, 1, 3 * GW), lambda bi, i: (bi, 0, 0)),
                   pl.BlockSpec((nb, 1, H_SMALL), lambda bi, i: (bi, 0, 0))],
        out_shape=[jax.ShapeDtypeStruct((bsz, l, GW), BF16),
                   jax.ShapeDtypeStruct((bsz, HEAD_DIM, GW), F32),
                   jax.ShapeDtypeStruct((bsz, 1, 3 * GW), F32),
                   jax.ShapeDtypeStruct((bsz, 1, H_SMALL), F32)],
        scratch_shapes=[pltpu.VMEM((nb, rows + 8, 3 * GW), F32), pltpu.VMEM((nb, rows + 8, H_SMALL), F32),
                        pltpu.VMEM((nb, HEAD_DIM, GW), F32)],
        compiler_params=_params(2),
        name="rwkv_mixer",
    )(h3, hs3, sh_main, sh_small, _heads_to_wide(s0), mu_main, mu_small, row(lp['rwkv_w0']), wup,
      row(lp['rwkv_a0']), aup, gup, row(lp['rwkv_k_k']), row(lp['rwkv_k_a']), row(lp['rwkv_r_k']),
      row(lp['rwkv_lnx_g']), row(lp['rwkv_lnx_b']), _ones_block_diag())
    new_shift = jnp.concatenate([sh_new[:, 0, :], shs_new[:, 0, S_LORA:S_LORA + LORA]], axis=-1)
    return o, _wide_to_heads(s_new), new_shift


_FB = 128
_SAMPLE_BK = 1024


def _log_forget(hs, bf):
    return -_softplus(-(hs + bf))


def _head_slice(h):
    return slice(h * HEAD_DIM, (h + 1) * HEAD_DIM)


def _attend_heads(qs, kblk, vblk, key_bias, mask, carry):
    m, l, acc = carry
    bq = qs[0].shape[0]
    kb16 = kblk.astype(BF16)
    vb16 = vblk.astype(BF16)
    parts = []
    for h in range(GROUP_HEADS):
        s = _dot(qs[h], kb16[_head_slice(h), :]) + key_bias[h:h + 1, :]
        parts.append(s if mask is None else jnp.where(mask, s, NEG))
    s = jnp.concatenate(parts, axis=0)
    m_new = jnp.maximum(m, jnp.max(s, -1, keepdims=True))
    alpha = jnp.exp(m - m_new)
    p = jnp.exp(s - m_new)
    l = alpha * l + jnp.sum(p, -1, keepdims=True)
    pb = p.astype(BF16)
    pv = jnp.concatenate([_dot(pb[h * bq:(h + 1) * bq, :], vb16[_head_slice(h), :], NT)
                          for h in range(GROUP_HEADS)], axis=0)
    return m_new, l, alpha * acc + pv


def _attend_init(bq):
    rows = GROUP_HEADS * bq
    return jnp.full((rows, 1), NEG, F32), jnp.zeros((rows, 1), F32), jnp.zeros((rows, HEAD_DIM), F32)


def _attend_store(o_ref, carry, bq):
    _, l, acc = carry
    for h in range(GROUP_HEADS):
        o_ref[:, _head_slice(h)] = (acc[h * bq:(h + 1) * bq, :] / l[h * bq:(h + 1) * bq, :]).astype(o_ref.dtype)


def _sel_rows(sel01, x):
    t = x.shape[0]
    r = _dot(sel01, jnp.concatenate(_split3(x), axis=0), NT)
    return r[:, :t] + r[:, t:2 * t] + r[:, 2 * t:]


def _fox_prompt_kernel(q_ref, kt_ref, vt_ref, hs_ref, bf_ref, o_ref, lf_ref, nf_scr, *, l, bq):
    i = pl.program_id(1)
    per = bq // _FB

    @pl.when(i == 0)
    def _():
        utri = _triu_ones(_FB, BF16)
        sel = _col_selector(8, H_SMALL, S_F).astype(BF16)
        carry = jnp.zeros((8, 1), F32)
        for jb in range(l // _FB):
            rows = slice(jb * _FB, (jb + 1) * _FB)
            lf = _log_forget(hs_ref[rows, :], bf_ref[...])
            lf_ref[rows, :] = lf[:, S_F:S_F + GROUP_HEADS]
            loc = _mm_x01(_sel_rows(sel, lf), utri, 3)
            nf_scr[jb // per, :, (jb % per) * _FB:(jb % per + 1) * _FB] = -(loc + carry)
            carry = carry + loc[:, _FB - 1:_FB]

    q = q_ref[...]
    r_i, c_i = _iota2((bq, bq), 0), _iota2((bq, bq), 1)
    qs = [(q[:, _head_slice(h)] * (HEAD_DIM ** -0.5)).astype(BF16) for h in range(GROUP_HEADS)]

    def step(j, mask, carry):
        keys = pl.ds(pl.multiple_of(j * bq, bq), bq)
        return _attend_heads(qs, kt_ref[:, keys], vt_ref[:, keys], nf_scr[j], mask, carry)

    carry = lax.fori_loop(0, i, lambda j, c: step(j, None, c), _attend_init(bq))
    carry = step(i, c_i <= r_i, carry)
    _attend_store(o_ref, carry, bq)


def _fox_prompt(h3, hs3, k_t, v_t, bf_pad):
    bsz, l, _ = h3.shape
    bq = min(l, 4 * _FB)
    kern = functools.partial(_fox_prompt_kernel, l=l, bq=bq)
    return pl.pallas_call(
        kern,
        grid=(bsz, l // bq),
        in_specs=[pl.BlockSpec((None, bq, GW), lambda bi, i: (bi, i, H_DQ // GW)),
                  pl.BlockSpec((None, GW, l), lambda bi, i: (bi, 0, 0)),
                  pl.BlockSpec((None, GW, l), lambda bi, i: (bi, 0, 0)),
                  pl.BlockSpec((None, l, H_SMALL), lambda bi, i: (bi, 0, 0)),
                  pl.BlockSpec((1, H_SMALL), lambda bi, i: (0, 0))],
        out_specs=[pl.BlockSpec((None, bq, GW), lambda bi, i: (bi, i, 0)),
                   pl.BlockSpec((None, l, GROUP_HEADS), lambda bi, i: (bi, 0, 0))],
        out_shape=[jax.ShapeDtypeStruct((bsz, l, GW), BF16),
                   jax.ShapeDtypeStruct((bsz, l, GROUP_HEADS), F32)],
        scratch_shapes=[pltpu.VMEM((l // bq, 8, bq), F32)],
        compiler_params=_params(2),
        name="fox_prompt",
    )(h3, k_t, v_t, hs3, bf_pad)


def _fox_sample_kernel(q_ref, kt_ref, vt_ref, hs_ref, bf_ref, ck_ref, cv_ref, clf_ref, o_ref, lf_ref, cum_scr,
                       *, l, p):
    nblk = p // _FB
    per = _SAMPLE_BK // _FB

    loc = _mm_x01(clf_ref[...].reshape(nblk * 8, _FB), _triu_ones(_FB, BF16), 3)
    carry = jnp.zeros((8, 1), F32)
    for jb in range(nblk):
        blk = loc[jb * 8:(jb + 1) * 8, :]
        cum_scr[jb // per, :, (jb % per) * _FB:(jb % per + 1) * _FB] = blk + carry
        carry = carry + blk[:, _FB - 1:_FB]
    total = carry

    lf = _log_forget(hs_ref[...], bf_ref[...])
    lf_ref[...] = lf[:, S_F:S_F + GROUP_HEADS]
    cum_r = _mm_x01(_sel_rows(_col_selector(8, H_SMALL, S_F).astype(BF16), lf), _triu_ones(l, BF16), 3)
    q = q_ref[...]
    r_i, c_i = _iota2((l, l), 0), _iota2((l, l), 1)
    qs = [(q[:, _head_slice(h)] * (HEAD_DIM ** -0.5)).astype(BF16) for h in range(GROUP_HEADS)]

    def body(j, carry):
        keys = pl.ds(pl.multiple_of(j * _SAMPLE_BK, _SAMPLE_BK), _SAMPLE_BK)
        suffix = total - cum_scr[j]
        return _attend_heads(qs, ck_ref[:, keys], cv_ref[:, keys], suffix, None, carry)

    carry = lax.fori_loop(0, p // _SAMPLE_BK, body, _attend_init(l))
    carry = _attend_heads(qs, kt_ref[...], vt_ref[...], -cum_r, c_i <= r_i, carry)
    _attend_store(o_ref, carry, l)


def _fox_sample(h3, hs3, k_t, v_t, bf_pad, layer, ck_t, cv_t, clogf):
    bsz, l, _ = h3.shape
    p = ck_t.shape[-1]
    clf = clogf.reshape(bsz, p // _FB, _FB, GROUP_HEADS).transpose(0, 1, 3, 2)
    clf = jnp.pad(clf, ((0, 0), (0, 0), (0, 8 - GROUP_HEADS), (0, 0)))
    kern = functools.partial(_fox_sample_kernel, l=l, p=p)
    return pl.pallas_call(
        kern,
        grid=(bsz,),
        in_specs=[pl.BlockSpec((None, l, GW), lambda bi: (bi, 0, H_DQ // GW)),
                  pl.BlockSpec((None, GW, l), lambda bi: (bi, 0, 0)),
                  pl.BlockSpec((None, GW, l), lambda bi: (bi, 0, 0)),
                  pl.BlockSpec((None, l, H_SMALL), lambda bi: (bi, 0, 0)),
                  pl.BlockSpec((1, H_SMALL), lambda bi: (0, 0)),
                  pl.BlockSpec((None, None, GW, p), lambda bi: (layer, bi, 0, 0)),
                  pl.BlockSpec((None, None, GW, p), lambda bi: (layer, bi, 0, 0)),
                  pl.BlockSpec((None, p // _FB, 8, _FB), lambda bi: (bi, 0, 0, 0))],
        out_specs=[pl.BlockSpec((None, l, GW), lambda bi: (bi, 0, 0)),
                   pl.BlockSpec((None, l, GROUP_HEADS), lambda bi: (bi, 0, 0))],
        out_shape=[jax.ShapeDtypeStruct((bsz, l, GW), BF16),
                   jax.ShapeDtypeStruct((bsz, l, GROUP_HEADS), F32)],
        scratch_shapes=[pltpu.VMEM((p // _SAMPLE_BK, 8, _SAMPLE_BK), F32)],
        compiler_params=_params(1),
        name="fox_sample",
    )(h3, k_t, v_t, hs3, bf_pad, ck_t, cv_t, clf)


_R_G, _R_E, _R_ROWS = 0, 8, 32
_MOE_EXPERTS_PER_STEP = 2


def _ffn_kernel(oa_ref, ob_ref, oc_ref, od_ref, x_ref, wo_ref, g1_ref, b1_ref, wrh_ref, wrl_ref, brc_ref, wg_ref,
                wu_ref, wd_ref, g2_ref, b2_ref, y_ref, comb_scr, xb_scr, x1_scr):
    e = pl.program_id(1)
    tm = x_ref.shape[0]
    lane = _iota2((tm, LANE), 1)

    @pl.when(e == 0)
    def _():
        mix = jnp.concatenate([ref[...] for ref in (oa_ref, ob_ref, oc_ref, od_ref)], axis=1)
        x1 = _layer_norm(DN_ALPHA * x_ref[...] + _dot(mix, wo_ref[...]), g1_ref[...], b1_ref[...])
        x1_scr[...] = x1
        y_ref[...] = jnp.zeros_like(y_ref)
        x_hi, x_lo = _split2(x1)
        xb_scr[...] = x_hi
        wr_hi = wrh_ref[...]
        r = _dot(jnp.concatenate([wr_hi, wrl_ref[...]], axis=0), x_hi, NT)
        lt = r[:_R_ROWS] + r[_R_ROWS:] + _dot(wr_hi, x_lo, NT) + brc_ref[...]
        gl = lt[_R_G:_R_G + N_EXPERT_GROUPS]
        grow = _iota2(gl.shape, 0)
        ge = jnp.exp(gl - jnp.max(gl, 0, keepdims=True))
        pg = ge / jnp.sum(ge, 0, keepdims=True)
        gp = jnp.max(pg, 0, keepdims=True)
        gi = jnp.min(jnp.where(pg == gp, grow, N_EXPERT_GROUPS), 0, keepdims=True)
        le = lt[_R_E:_R_E + N_EXPERTS]
        erow = _iota2(le.shape, 0)
        emask = (erow >> 2) == gi
        el = jnp.where(emask, le, NEG)
        ee = jnp.exp(el - jnp.max(el, 0, keepdims=True))
        ep = ee / jnp.sum(ee, 0, keepdims=True)
        m1 = jnp.max(jnp.where(emask, ep, -1.0), 0, keepdims=True)
        i1 = jnp.min(jnp.where(jnp.logical_and(emask, ep == m1), erow, N_EXPERTS), 0, keepdims=True)
        rest = jnp.logical_and(emask, erow != i1)
        m2 = jnp.max(jnp.where(rest, ep, -1.0), 0, keepdims=True)
        i2 = jnp.min(jnp.where(jnp.logical_and(rest, ep == m2), erow, N_EXPERTS), 0, keepdims=True)
        den = m1 + m2
        comb_t = jnp.where(erow == i1, gp * m1 / den, jnp.where(erow == i2, gp * m2 / den, 0.0))
        pieces = jnp.concatenate(_split3(comb_t), axis=0)
        sel = ((_iota2((3 * N_EXPERTS, LANE), 0) & (N_EXPERTS - 1)) == _iota2((3 * N_EXPERTS, LANE), 1)).astype(BF16)
        comb_scr[...] = _dot(pieces, sel, TN)

    xb = xb_scr[...]
    comb = comb_scr[...]
    hidden = []
    for k in range(_MOE_EXPERTS_PER_STEP):
        ce = jnp.sum(jnp.where(lane == e * _MOE_EXPERTS_PER_STEP + k, comb, 0.0), -1, keepdims=True)
        hh = _silu(_mm(xb, wg_ref[k])) * _mm(xb, wu_ref[k])
        hidden.append((hh * ce).astype(BF16))
    hidden = jnp.concatenate(hidden, axis=1)
    w_down = wd_ref[...].reshape(_MOE_EXPERTS_PER_STEP * D_EXPERT, D_MODEL)
    y_ref[...] += _mm(hidden, w_down)

    @pl.when(e == pl.num_programs(1) - 1)
    def _():
        y_ref[...] = _layer_norm(DN_ALPHA * x1_scr[...] + y_ref[...], g2_ref[...], b2_ref[...])


def _ffn(mixers, x2d, w_out_bf16, g1, b1, wr_hi, wr_lo, br_col, layer, wg, wu, wd, g2, b2):
    t = x2d.shape[0]
    tm = min(t, 1024)
    ne = _MOE_EXPERTS_PER_STEP
    mix = pl.BlockSpec((tm, GW), lambda i, e: (i, 0))
    vec = pl.BlockSpec((1, D_MODEL), lambda i, e: (0, 0))
    return pl.pallas_call(
        _ffn_kernel,
        grid=(t // tm, N_EXPERTS // ne),
        in_specs=[mix, mix, mix, mix,
                  pl.BlockSpec((tm, D_MODEL), lambda i, e: (i, 0)),
                  pl.BlockSpec((D_MODEL, D_MODEL), lambda i, e: (0, 0)), vec, vec,
                  pl.BlockSpec((_R_ROWS, D_MODEL), lambda i, e: (0, 0)),
                  pl.BlockSpec((_R_ROWS, D_MODEL), lambda i, e: (0, 0)),
                  pl.BlockSpec((_R_ROWS, 1), lambda i, e: (0, 0)),
                  pl.BlockSpec((None, ne, D_MODEL, D_EXPERT), lambda i, e: (layer, e, 0, 0)),
                  pl.BlockSpec((None, ne, D_MODEL, D_EXPERT), lambda i, e: (layer, e, 0, 0)),
                  pl.BlockSpec((None, ne, D_EXPERT, D_MODEL), lambda i, e: (layer, e, 0, 0)), vec, vec],
        out_specs=pl.BlockSpec((tm, D_MODEL), lambda i, e: (i, 0)),
        out_shape=jax.ShapeDtypeStruct((t, D_MODEL), F32),
        scratch_shapes=[pltpu.VMEM((tm, LANE), F32), pltpu.VMEM((tm, D_MODEL), BF16),
                        pltpu.VMEM((tm, D_MODEL), F32)],
        compiler_params=_params(2),
        name="ffn",
    )(*mixers, x2d, w_out_bf16, g1, b1, wr_hi, wr_lo, br_col, wg, wu, wd, g2, b2)


def _prep_layer(lp):
    w = lp['w_in_t']
    c0 = CONV_COLS
    c1 = c0 + GDN_COLS
    c2 = c1 + RWKV_COLS
    main = jnp.concatenate([w[c0:c0 + 3 * GW], w[c1:c1 + 3 * GW], w[c2:c2 + GW],
                            w[0:GW], w[GW:2 * GW], w[c0 + 3 * GW:c0 + 4 * GW]], axis=0).astype(BF16)
    kv_t = w[c2 + GW:c2 + 3 * GW].astype(BF16)
    small = jnp.concatenate([w[c1 + 3 * GW:c2], w[c0 + 4 * GW:c1], w[c2 + 3 * GW:],
                             jnp.zeros((H_SMALL - LORA - 3 * GROUP_HEADS, D_MODEL), F32)], axis=0)
    gap = jnp.zeros((_R_E - N_EXPERT_GROUPS, D_MODEL), F32)
    tail = jnp.zeros((_R_ROWS - _R_E - N_EXPERTS, D_MODEL), F32)
    wr_t = jnp.concatenate([lp['router_g_w'].T, gap, lp['router_e_w'].T, tail], axis=0)
    wr_hi = wr_t.astype(BF16)
    wr_lo = (wr_t - wr_hi.astype(F32)).astype(BF16)
    br = jnp.concatenate([lp['router_g_b'], gap[:, 0], lp['router_e_b'], tail[:, 0]])[:, None]
    bf_pad = jnp.zeros((1, H_SMALL), F32).at[0, S_F:S_F + GROUP_HEADS].set(lp['fox_b_f'])
    return dict(w_main=main, w_small=small, w_kv_t=kv_t, wr_hi=wr_hi, wr_lo=wr_lo, br=br, bf_pad=bf_pad)


def _trunk_layer(x, lp, pp, layer, big, conv_buf, gdn_buf, gdn_s, rw_shift, rw_s, fox_cache):
    bsz, l, d = x.shape
    x2d = x.reshape(bsz * l, d)
    h3, hs3, k_t, v_t = _inproj(x, pp['w_main'], pp['w_small'], pp['w_kv_t'])

    o_a, new_conv = _conv_mixer(h3, conv_buf, lp['conv_w'], lp['conv_b'][None], lp['conv_ln_g'][None],
                                lp['conv_ln_b'][None])
    o_b, new_gdn_s, new_gdn_buf = _gdn_mixer(h3, hs3, gdn_buf, gdn_s, lp['gdn_conv_w'], lp['gdn_a_log'],
                                             lp['gdn_dt_bias'], lp['gdn_norm_w'])
    o_c, new_rw_s, new_shift = _rwkv_mixer(h3, hs3, rw_shift, rw_s, lp)
    if fox_cache is None:
        o_d, logf = _fox_prompt(h3, hs3, k_t, v_t, pp['bf_pad'])
    else:
        o_d, logf = _fox_sample(h3, hs3, k_t, v_t, pp['bf_pad'], layer, *fox_cache)

    mixers = [o.reshape(bsz * l, GW) for o in (o_a, o_b, o_c, o_d)]
    x2 = _ffn(mixers, x2d, big['w_out'][layer], lp['ln1_g'][None], lp['ln1_b'][None], pp['wr_hi'], pp['wr_lo'],
              pp['br'], layer, *big['experts'], lp['ln2_g'][None], lp['ln2_b'][None])
    return x2.reshape(bsz, l, d), (new_conv, new_gdn_buf, new_gdn_s, new_shift, new_rw_s, k_t, v_t, logf)


def _untranspose(xs_t):
    bsz, _, l = xs_t[0].shape
    return jnp.stack(xs_t).reshape(len(xs_t), bsz, GROUP_HEADS, HEAD_DIM, l).transpose(0, 1, 4, 2, 3)


_LAYER_KEYS = ('w_in_t', 'conv_w', 'conv_b', 'conv_ln_g', 'conv_ln_b', 'gdn_conv_w', 'gdn_a_log', 'gdn_dt_bias',
               'gdn_norm_w', 'rwkv_mu', 'rwkv_w0', 'rwkv_w_up', 'rwkv_a0', 'rwkv_a_up', 'rwkv_g_up', 'rwkv_k_k',
               'rwkv_k_a', 'rwkv_r_k', 'rwkv_lnx_g', 'rwkv_lnx_b', 'fox_b_f', 'ln1_g', 'ln1_b',
               'router_g_w', 'router_g_b', 'router_e_w', 'router_e_b', 'ln2_g', 'ln2_b')


def kernel(x_prompt, x_sample, cache_fox_k, cache_fox_v, cache_fox_logf, state_conv, state_gdn_conv, state_gdn, state_rwkv_shift, state_rwkv, w_in, conv_w, conv_b, conv_ln_g, conv_ln_b, gdn_conv_w, gdn_a_log, gdn_dt_bias, gdn_norm_w, rwkv_mu, rwkv_w0, rwkv_w_up, rwkv_a0, rwkv_a_up, rwkv_g_up, rwkv_k_k, rwkv_k_a, rwkv_r_k, rwkv_lnx_g, rwkv_lnx_b, fox_b_f, w_out, ln1_g, ln1_b, router_g_w, router_g_b, router_e_w, router_e_b, exp_w_gate, exp_w_up, exp_w_down, ln2_g, ln2_b):
    w_in_t = w_in.transpose(0, 2, 1)
    weights = dict(zip(_LAYER_KEYS, (w_in_t, conv_w, conv_b, conv_ln_g, conv_ln_b, gdn_conv_w, gdn_a_log,
                                     gdn_dt_bias, gdn_norm_w, rwkv_mu, rwkv_w0, rwkv_w_up, rwkv_a0, rwkv_a_up,
                                     rwkv_g_up, rwkv_k_k, rwkv_k_a, rwkv_r_k, rwkv_lnx_g, rwkv_lnx_b, fox_b_f,
                                     ln1_g, ln1_b, router_g_w, router_g_b, router_e_w, router_e_b,
                                     ln2_g, ln2_b)))
    big = dict(w_out=w_out.astype(BF16), experts=(exp_w_gate, exp_w_up, exp_w_down))
    depth, bs, past = cache_fox_k.shape[:3]
    ck_t = cache_fox_k.transpose(0, 1, 3, 4, 2).reshape(depth, bs, GW, past)
    cv_t = cache_fox_v.transpose(0, 1, 3, 4, 2).reshape(depth, bs, GW, past)
    xp, xs = x_prompt, x_sample
    bp = x_prompt.shape[0]
    outs_p = [[] for _ in range(8)]
    outs_s = [[] for _ in range(8)]
    for l in range(DEPTH):
        lp = {k: v[l] for k, v in weights.items()}
        pp = _prep_layer(lp)
        xp, st_p = _trunk_layer(xp, lp, pp, l, big,
                                jnp.zeros((bp, CONV_WIDTH - 1, GW), F32),
                                jnp.zeros((bp, GDN_CONV_WIDTH - 1, 3 * GW), F32),
                                jnp.zeros((bp, GROUP_HEADS, HEAD_DIM, HEAD_DIM), F32),
                                jnp.zeros((bp, RWKV_COLS), F32),
                                jnp.zeros((bp, GROUP_HEADS, HEAD_DIM, HEAD_DIM), F32),
                                None)
        xs, st_s = _trunk_layer(xs, lp, pp, l, big, state_conv[l], state_gdn_conv[l], state_gdn[l],
                                state_rwkv_shift[l], state_rwkv[l], (ck_t, cv_t, cache_fox_logf[l]))
        for i in range(8):
            outs_p[i].append(st_p[i])
            outs_s[i].append(st_s[i])

    def assemble(outs):
        conv, gdn_buf, gdn_s, shift, rw_s = (jnp.stack(o) for o in outs[:5])
        return conv, gdn_buf, gdn_s, shift, rw_s, _untranspose(outs[5]), _untranspose(outs[6]), jnp.stack(outs[7])

    return (xp, xs, *assemble(outs_p), *assemble(outs_s))
```

```python
import functools

import jax
import jax.numpy as jnp
from jax import lax
from jax.experimental import pallas as pl
from jax.experimental.pallas import tpu as pltpu

F32 = jnp.float32
BF16 = jnp.bfloat16

D_MODEL = 1024
DEPTH = 2
CHUNK = 64
HEAD_DIM = 64
GW = D_MODEL // 4
GROUP_HEADS = GW // HEAD_DIM
CONV_WIDTH = 31
GDN_CONV_WIDTH = 4
RWKV_W_LORA = 16
RWKV_A_LORA = 16
RWKV_G_LORA = 32
N_EXPERT_GROUPS = 4
EXPERTS_PER_GROUP = 4
N_EXPERTS = N_EXPERT_GROUPS * EXPERTS_PER_GROUP
D_EXPERT = 256
DN_ALPHA = (2 * DEPTH) ** 0.25
LN_EPS = 1e-5
GN_EPS = 64e-5

CONV_COLS = 2 * GW
GDN_COLS = 4 * GW + 2 * GROUP_HEADS
RWKV_COLS = 3 * GW + RWKV_W_LORA + RWKV_A_LORA + RWKV_G_LORA
FOX_COLS = 3 * GW + GROUP_HEADS
LORA = RWKV_W_LORA + RWKV_A_LORA + RWKV_G_LORA

H_BQKV, H_CRKV, H_DK, H_AVAL, H_AGATE, H_BZ = 0, 768, 1536, 1792, 2048, 2304
H_MAIN = 2560
QKV_ROWS = 3 * GW
S_LORA, S_A, S_B, S_F = 0, 64, 68, 72
H_SMALL = 128
LANE = 128

VMEM_LIMIT = 56 * 1024 * 1024

NN = (((1,), (0,)), ((), ()))
NT = (((1,), (1,)), ((), ()))
TN = (((0,), (0,)), ((), ()))
NEG = -1e30


def _dot(a, b, dims=NN):
    return lax.dot_general(a, b, dims, preferred_element_type=F32)


def _mm(a, b, dims=NN):
    return _dot(a.astype(BF16), b.astype(BF16), dims)


def _mmh(a, b, dims=NN):
    return lax.dot_general(a, b, dims, preferred_element_type=F32, precision=lax.Precision.HIGHEST)


def _sigmoid(x):
    return 0.5 * (jnp.tanh(0.5 * x) + 1.0)


def _silu(x):
    return x * _sigmoid(x)


def _softplus(x):
    return jnp.maximum(x, 0.0) + jnp.log(1.0 + jnp.exp(-jnp.abs(x)))


def _iota2(shape, dim):
    return lax.broadcasted_iota(jnp.int32, shape, dim)


def _tril_ones(n, dtype=F32):
    r, c = _iota2((n, n), 0), _iota2((n, n), 1)
    return (r >= c).astype(dtype)


def _triu_ones(n, dtype=F32):
    r, c = _iota2((n, n), 0), _iota2((n, n), 1)
    return (r <= c).astype(dtype)


def _col_selector(rows, width, col0):
    r, c = _iota2((rows, width), 0), _iota2((rows, width), 1)
    return (c == r + col0).astype(F32)


def _layer_norm(x, g, b):
    mu = jnp.mean(x, axis=-1, keepdims=True)
    xc = x - mu
    var = jnp.mean(xc * xc, axis=-1, keepdims=True)
    return xc * lax.rsqrt(var + LN_EPS) * g + b


def _split2(x):
    hi = x.astype(BF16)
    lo = (x - hi.astype(F32)).astype(BF16)
    return hi, lo


def _split3(x):
    hi = x.astype(BF16)
    r = x - hi.astype(F32)
    mid = r.astype(BF16)
    lo = (r - mid.astype(F32)).astype(BF16)
    return hi, mid, lo


class _Wide:
    def __init__(self):
        self.i = _iota2((CHUNK, GW), 0)
        lane = _iota2((CHUNK, GW), 1)
        self.j = lane & (HEAD_DIM - 1)
        self.head = lane >> 6
        self.incl = self.i >= self.j
        self.strict = self.i > self.j
        self.eye = self.i == self.j
        r, c = _iota2((GW, GW), 0), _iota2((GW, GW), 1)
        self.block_diag = (r >> 6) == (c >> 6)


def _head_block_diag(x, w):
    t = jnp.concatenate([x] * GROUP_HEADS, axis=0)
    return jnp.where(w.block_diag, t, jnp.zeros_like(t))


def _collapse_heads(full, w):
    out = jnp.zeros((HEAD_DIM, GW), F32)
    for h in range(GROUP_HEADS):
        out = out + jnp.where(w.head == h, full[h * HEAD_DIM:(h + 1) * HEAD_DIM, :], 0.0)
    return out


def _lhs3(a, precise=False):
    if not precise:
        return None, a.astype(BF16)
    hi, lo = _split2(a)
    return jnp.concatenate([hi, lo], axis=0), hi


def _rhs3(b, w=None, precise=False):
    if not precise:
        hi = b.astype(BF16)
        return (hi if w is None else _head_block_diag(hi, w)), None
    hi, lo = _split2(b)
    if w is None:
        return hi, lo
    return _head_block_diag(hi, w), _head_block_diag(lo, w)


def _mm3(lhs, rhs, dims=NN):
    cat, hi = lhs
    bh, bl = rhs
    if cat is None or bl is None:
        return _dot(hi, bh, dims)
    m = hi.shape[0]
    r = _dot(cat, bh, dims)
    return r[:m] + r[m:] + _dot(hi, bl, dims)


def _mm3_tn(a, b, precise=False):
    if not precise:
        return _dot(a.astype(BF16), b.astype(BF16), TN)
    ah, al = _split2(a)
    bh, bl = _split2(b)
    m = a.shape[1]
    r = _dot(jnp.concatenate([ah, al], axis=1), bh, TN)
    return r[:m] + r[m:] + _dot(ah, bl, TN)


def _mm_x01(a, m01, pieces=2):
    m = a.shape[0]
    r = _dot(jnp.concatenate(_split3(a) if pieces == 3 else _split2(a), axis=0), m01)
    return sum(r[i * m:(i + 1) * m] for i in range(pieces))


def _tri_mm(tri01, x, pieces=2):
    n = x.shape[1]
    r = _dot(tri01, jnp.concatenate(_split3(x) if pieces == 3 else _split2(x), axis=1))
    return sum(r[:, i * n:(i + 1) * n] for i in range(pieces))


def _unit_lower_inverse_wide(mats, w):
    eye = w.eye.astype(F32)
    same8 = (w.i >> 3) == (w.j >> 3)
    ds = [jnp.where(same8, a, 0.0) for a in mats]
    d2 = [_mm3(_lhs3(d), _rhs3(d, w)) for d in ds]
    d2_r = [_rhs3(x, w) for x in d2]
    d4 = [_mm3(_lhs3(x), r) for x, r in zip(d2, d2_r)]
    imd = [eye - d for d in ds]
    p1 = [i + _mm3(_lhs3(i), r) for i, r in zip(imd, d2_r)]
    ts = [p + _mm3(_lhs3(p), _rhs3(x, w)) for p, x in zip(p1, d4)]
    for s in range(3, 6):
        big = (w.i >> (s + 1)) == (w.j >> (s + 1))
        small = (w.i >> s) == (w.j >> s)
        sel = jnp.logical_and(big, jnp.logical_not(small))
        inner = [_mm3(_lhs3(jnp.where(sel, a, 0.0)), _rhs3(t, w)) for a, t in zip(mats, ts)]
        ts = [t - _mm3(_lhs3(t), _rhs3(x, w)) for t, x in zip(ts, inner)]
    return ts


def _params(n_axes):
    return pltpu.CompilerParams(dimension_semantics=("arbitrary",) * n_axes,
                                vmem_limit_bytes=VMEM_LIMIT)


def _heads_to_wide(s):
    b = s.shape[0]
    return s.transpose(0, 2, 1, 3).reshape(b, HEAD_DIM, GW)


def _wide_to_heads(s):
    b = s.shape[0]
    return s.reshape(b, HEAD_DIM, GROUP_HEADS, HEAD_DIM).transpose(0, 2, 1, 3)


def _ones_block_diag():
    r, c = jnp.arange(GW)[:, None], jnp.arange(GW)[None, :]
    return ((r // HEAD_DIM) == (c // HEAD_DIM)).astype(BF16)


def _lane_spread(col0):
    r, c = jnp.arange(H_SMALL)[:, None], jnp.arange(GW)[None, :]
    return ((r - col0) == (c // HEAD_DIM)).astype(BF16)


def _inproj_kernel(x_ref, wm_ref, wsh_ref, wsl_ref, wkv_ref, h_ref, hs_ref, qt_ref, kt_ref, vt_ref):
    x = x_ref[...]
    x_hi, x_lo = _split2(x)
    tm = x.shape[0]
    h_ref[...] = _dot(x_hi, wm_ref[...], NT)
    r = _dot(jnp.concatenate([x_hi, x_lo], axis=0), wsh_ref[...], NT)
    hs_ref[...] = r[:tm] + r[tm:] + _dot(x_hi, wsl_ref[...], NT)
    qkv = _dot(wkv_ref[...], x_hi, NT)
    qt_ref[...] = qkv[:GW]
    kt_ref[...] = qkv[GW:2 * GW]
    vt_ref[...] = qkv[2 * GW:]


def _inproj(x3, w_main_t, w_small_t, w_kv_t):
    bsz, l, _ = x3.shape
    tm = min(l, 512)
    const = lambda shape: pl.BlockSpec(shape, lambda bi, i: (0, 0))
    ws_hi = w_small_t.astype(BF16)
    ws_lo = (w_small_t - ws_hi.astype(F32)).astype(BF16)
    kv_spec = pl.BlockSpec((None, GW, tm), lambda bi, i: (bi, 0, i))
    return pl.pallas_call(
        _inproj_kernel,
        grid=(bsz, l // tm),
        in_specs=[pl.BlockSpec((None, tm, D_MODEL), lambda bi, i: (bi, i, 0)),
                  const((H_MAIN, D_MODEL)), const((H_SMALL, D_MODEL)), const((H_SMALL, D_MODEL)),
                  const((QKV_ROWS, D_MODEL))],
        out_specs=[pl.BlockSpec((None, tm, H_MAIN), lambda bi, i: (bi, i, 0)),
                   pl.BlockSpec((None, tm, H_SMALL), lambda bi, i: (bi, i, 0)), kv_spec, kv_spec, kv_spec],
        out_shape=[jax.ShapeDtypeStruct((bsz, l, H_MAIN), F32), jax.ShapeDtypeStruct((bsz, l, H_SMALL), F32)]
        + [jax.ShapeDtypeStruct((bsz, GW, l), F32)] * 3,
        compiler_params=_params(2),
        name="inproj",
    )(x3, w_main_t, ws_hi, ws_lo, w_kv_t)


_HALO = 32
_SUBLANES = 8


def _conv_kernel(val_ref, gate_ref, buf_ref, cw_ref, cb_ref, g_ref, b_ref, o_ref, nbuf_ref, win_ref, sh_ref, *, tl):
    i = pl.program_id(1)
    pad = _HALO - (CONV_WIDTH - 1)
    span = tl + _HALO - _SUBLANES

    @pl.when(i == 0)
    def _():
        win_ref[0:8, :] = jnp.zeros((8, GW), F32)
        win_ref[pad:_HALO, :] = buf_ref[...]

    u = val_ref[...] * _sigmoid(gate_ref[...])
    win_ref[_HALO:_HALO + tl, :] = u
    for s in range(1, _SUBLANES):
        sh_ref[s, :, :] = win_ref[s:s + span, :]
    sub = min(tl, 64)
    for r0 in range(0, tl, sub):
        acc = jnp.zeros((sub, GW), F32)
        for j in range(CONV_WIDTH):
            s, base = (pad + j) % _SUBLANES, (pad + j) // _SUBLANES * _SUBLANES + r0
            rows = win_ref[base:base + sub, :] if s == 0 else sh_ref[s, base:base + sub, :]
            acc = acc + cw_ref[j:j + 1, :] * rows
        y = _layer_norm(acc + cb_ref[...], g_ref[...], b_ref[...])
        o_ref[r0:r0 + sub, :] = _silu(y).astype(o_ref.dtype)
    tail = win_ref[tl:tl + _HALO, :]
    win_ref[0:_HALO, :] = tail

    @pl.when(i == pl.num_programs(1) - 1)
    def _():
        nbuf_ref[...] = win_ref[pad:_HALO, :]


def _conv_mixer(h3, buf, cw, cb, g, b):
    bsz, l, _ = h3.shape
    tl = min(l, 256)
    kern = functools.partial(_conv_kernel, tl=tl)
    full = lambda shape: pl.BlockSpec(shape, lambda bi, i: (0,) * len(shape))
    return pl.pallas_call(
        kern,
        grid=(bsz, l // tl),
        in_specs=[pl.BlockSpec((None, tl, GW), lambda bi, i: (bi, i, H_AVAL // GW)),
                  pl.BlockSpec((None, tl, GW), lambda bi, i: (bi, i, H_AGATE // GW)),
                  pl.BlockSpec((None, CONV_WIDTH - 1, GW), lambda bi, i: (bi, 0, 0)),
                  full((CONV_WIDTH, GW)), full((1, GW)), full((1, GW)), full((1, GW))],
        out_specs=[pl.BlockSpec((None, tl, GW), lambda bi, i: (bi, i, 0)),
                   pl.BlockSpec((None, CONV_WIDTH - 1, GW), lambda bi, i: (bi, 0, 0))],
        out_shape=[jax.ShapeDtypeStruct((bsz, l, GW), BF16),
                   jax.ShapeDtypeStruct((bsz, CONV_WIDTH - 1, GW), F32)],
        scratch_shapes=[pltpu.VMEM((tl + _HALO, GW), F32),
                        pltpu.VMEM((_SUBLANES, tl + _HALO - _SUBLANES, GW), F32)],
        compiler_params=_params(2),
        name="conv_mixer",
    )(h3, h3, buf, cw, cb, g, b)


def _gdn_kernel(qkv_ref, z_ref, hs_ref, buf_ref, s0_ref, cw_ref, alog_ref, dtb_ref, nw_ref, ones_ref, sela_ref,
                selb_ref, o_ref, sout_ref, nbuf_ref, win_ref, s_scr, *, nb, nc):
    c = pl.program_id(1)
    n = CHUNK
    rows = nc * n
    kw = GDN_CONV_WIDTH - 1
    items = [(b, ci) for b in range(nb) for ci in range(nc)]

    @pl.when(c == 0)
    def _():
        for b in range(nb):
            win_ref[b, 0:8, :] = jnp.zeros((8, 3 * GW), F32)
            win_ref[b, 8 - kw:8, :] = buf_ref[b]
        s_scr[...] = s0_ref[...]

    qkvs = []
    for b in range(nb):
        win_ref[b, 8:8 + rows, :] = qkv_ref[b]
        conv = jnp.zeros((rows, 3 * GW), F32)
        for j in range(GDN_CONV_WIDTH):
            conv = conv + cw_ref[j:j + 1, :] * win_ref[b, 8 - kw + j:8 - kw + j + rows, :]
        tail = win_ref[b, rows:rows + 8, :]
        win_ref[b, 0:8, :] = tail
        qkvs.append(_silu(conv))

    w = _Wide()
    ones_bd = ones_ref[...]
    tril = _tril_ones(n, BF16)
    sl = lambda ci: slice(ci * n, (ci + 1) * n)
    qs = [qkvs[b][sl(ci), 0:GW] for b, ci in items]
    ks = [qkvs[b][sl(ci), GW:2 * GW] for b, ci in items]
    vs = [qkvs[b][sl(ci), 2 * GW:3 * GW] for b, ci in items]
    sss = [_mm_x01(jnp.concatenate([q * q, k * k], axis=0), ones_bd) for q, k in zip(qs, ks)]
    qs = [q * lax.rsqrt(ss[:n] + 1e-6) * (HEAD_DIM ** -0.5) for q, ss in zip(qs, sss)]
    ks = [k * lax.rsqrt(ss[n:] + 1e-6) for k, ss in zip(ks, sss)]

    hss = [hs_ref[b, sl(ci), :] for b, ci in items]
    gs = [_mm_x01(-jnp.exp(alog_ref[...]) * _softplus(hs + dtb_ref[...]), sela_ref[...]) for hs in hss]
    betas = [_mm_x01(_sigmoid(hs), selb_ref[...]) for hs in hss]
    gcs = [_tri_mm(tril, g) for g in gs]
    grs = [jnp.sum(jnp.where(w.eye, gc, 0.0), axis=0, keepdims=True) for gc in gcs]
    decays = [jnp.exp(jnp.where(w.incl, gc - gr, NEG)) for gc, gr in zip(gcs, grs)]

    kbs = [k * beta for k, beta in zip(ks, betas)]
    grams = [_mm3(_lhs3(jnp.concatenate([kb, q], axis=0)), _rhs3(k, w), NT)
             for kb, q, k in zip(kbs, qs, ks)]
    amats = [jnp.where(w.strict, gram[:n] * decay, 0.0) for gram, decay in zip(grams, decays)]
    qks = [jnp.where(w.incl, gram[n:] * decay, 0.0) for gram, decay in zip(grams, decays)]
    t_ls = [_lhs3(t) for t in _unit_lower_inverse_wide(amats, w)]
    egs = [jnp.exp(gc) for gc in gcs]
    xvs = [_mm3(t_l, _rhs3(v * beta, w)) for t_l, v, beta in zip(t_ls, vs, betas)]
    xks = [_mm3(t_l, _rhs3(kb * eg, w)) for t_l, kb, eg in zip(t_ls, kbs, egs)]
    xq_ls = [_lhs3(jnp.concatenate([xk, q * eg], axis=0)) for xk, q, eg in zip(xks, qs, egs)]
    qk_ls = [_lhs3(qk) for qk in qks]
    gls = [gc[n - 1:n, :] for gc in gcs]
    kds = [k * jnp.exp(gl - gc) for k, gl, gc in zip(ks, gls, gcs)]

    states = [s_scr[b] for b in range(nb)]
    outs = {}
    for ci in range(nc):
        for b in range(nb):
            it = b * nc + ci
            s = states[b]
            rs = _mm3(xq_ls[it], _rhs3(s, w))
            u = xvs[it] - rs[:n]
            outs[it] = rs[n:] + _mm3(qk_ls[it], _rhs3(u, w))
            states[b] = s * jnp.exp(gls[it]) + _collapse_heads(_mm3_tn(kds[it], u), w)
    for b in range(nb):
        s_scr[b] = states[b]

    for it, (b, ci) in enumerate(items):
        o = outs[it]
        ms = _mm_x01(o * o, ones_bd) * (1.0 / HEAD_DIM)
        o = o * lax.rsqrt(ms + 1e-6) * nw_ref[...] * _silu(z_ref[b, sl(ci), :])
        o_ref[b, sl(ci), :] = o.astype(o_ref.dtype)

    @pl.when(c == pl.num_programs(1) - 1)
    def _():
        sout_ref[...] = s_scr[...]
        for b in range(nb):
            nbuf_ref[b] = win_ref[b, 8 - kw:8, :]


def _recurrent_tiling(bsz, l):
    nc = min(l // CHUNK, 4)
    nb = 2 if nc > 1 else min(bsz, 4)
    return nb, nc


def _gdn_mixer(h3, hs3, buf, s0, cw, a_log, dt_bias, norm_w):
    bsz, l, _ = h3.shape
    nb, nc = _recurrent_tiling(bsz, l)
    rows = nc * CHUNK
    pad_r = lambda v, off: jnp.zeros((1, H_SMALL), F32).at[0, off:off + GROUP_HEADS].set(v)
    full = lambda shape: pl.BlockSpec(shape, lambda bi, i: (0,) * len(shape))
    o, s_new, new_buf = pl.pallas_call(
        functools.partial(_gdn_kernel, nb=nb, nc=nc),
        grid=(bsz // nb, l // rows),
        in_specs=[pl.BlockSpec((nb, rows, 3 * GW), lambda bi, i: (bi, i, H_BQKV // (3 * GW))),
                  pl.BlockSpec((nb, rows, GW), lambda bi, i: (bi, i, H_BZ // GW)),
                  pl.BlockSpec((nb, rows, H_SMALL), lambda bi, i: (bi, i, 0)),
                  pl.BlockSpec((nb, GDN_CONV_WIDTH - 1, 3 * GW), lambda bi, i: (bi, 0, 0)),
                  pl.BlockSpec((nb, HEAD_DIM, GW), lambda bi, i: (bi, 0, 0)),
                  full((GDN_CONV_WIDTH, 3 * GW)), full((1, H_SMALL)), full((1, H_SMALL)), full((1, GW)),
                  full((GW, GW)), full((H_SMALL, GW)), full((H_SMALL, GW))],
        out_specs=[pl.BlockSpec((nb, rows, GW), lambda bi, i: (bi, i, 0)),
                   pl.BlockSpec((nb, HEAD_DIM, GW), lambda bi, i: (bi, 0, 0)),
                   pl.BlockSpec((nb, GDN_CONV_WIDTH - 1, 3 * GW), lambda bi, i: (bi, 0, 0))],
        out_shape=[jax.ShapeDtypeStruct((bsz, l, GW), BF16),
                   jax.ShapeDtypeStruct((bsz, HEAD_DIM, GW), F32),
                   jax.ShapeDtypeStruct((bsz, GDN_CONV_WIDTH - 1, 3 * GW), F32)],
        scratch_shapes=[pltpu.VMEM((nb, rows + 8, 3 * GW), F32), pltpu.VMEM((nb, HEAD_DIM, GW), F32)],
        compiler_params=_params(2),
        name="gdn_mixer",
    )(h3, h3, hs3, buf, _heads_to_wide(s0), cw, pad_r(a_log, S_A), pad_r(dt_bias, S_A),
      jnp.tile(norm_w, GROUP_HEADS)[None, :], _ones_block_diag(), _lane_spread(S_A), _lane_spread(S_B))
    return o, _wide_to_heads(s_new), new_buf


def _rwkv_kernel(rkv_ref, hs_ref, sh_ref, shs_ref, s0_ref, mu_ref, mus_ref, w0_ref, wup_ref, a0_ref, aup_ref,
                 gup_ref, kk_ref, ka_ref, rk_ref, lng_ref, lnb_ref, ones_ref, o_ref, sout_ref, nsh_ref, nshs_ref,
                 win_ref, wins_ref, s_scr, *, nb, nc):
    c = pl.program_id(1)
    n = CHUNK

    rows = nc * n
    items = [(b, ci) for b in range(nb) for ci in range(nc)]

    @pl.when(c == 0)
    def _():
        for b in range(nb):
            win_ref[b, 0:8, :] = jnp.zeros((8, 3 * GW), F32)
            wins_ref[b, 0:8, :] = jnp.zeros((8, H_SMALL), F32)
            win_ref[b, 7:8, :] = sh_ref[b]
            wins_ref[b, 7:8, :] = shs_ref[b]
        s_scr[...] = s0_ref[...]

    wup_r, aup_r, gup_r = (_rhs3(r[...], precise=True) for r in (wup_ref, aup_ref, gup_ref))
    per_b = []
    for b in range(nb):
        x = rkv_ref[b]
        xs = hs_ref[b]
        win_ref[b, 8:8 + rows, :] = x
        wins_ref[b, 8:8 + rows, :] = xs
        xm = x + (win_ref[b, 7:7 + rows, :] - x) * mu_ref[...]
        xms = xs + (wins_ref[b, 7:7 + rows, :] - xs) * mus_ref[...]
        last = win_ref[b, rows + 7:rows + 8, :]
        win_ref[b, 7:8, :] = last
        lasts = wins_ref[b, rows + 7:rows + 8, :]
        wins_ref[b, 7:8, :] = lasts
        w_pre = w0_ref[...] + _mm3(_lhs3(jnp.tanh(xms), precise=True), wup_r)
        logw = -jnp.exp(-_softplus(-w_pre) - 0.5)
        a_sig = _sigmoid(a0_ref[...] + _mm3(_lhs3(xms, precise=True), aup_r))
        gate = _mm3(_lhs3(_sigmoid(xms), precise=True), gup_r)
        kx = xm[:, GW:2 * GW]
        per_b.append(dict(rr=xm[:, 0:GW], vv=xm[:, 2 * GW:3 * GW], logw=logw, a_sig=a_sig, gate=gate,
                          kkp=kx * kk_ref[...], k2=kx * (1.0 + (a_sig - 1.0) * ka_ref[...])))

    w = _Wide()
    ones_bd = ones_ref[...]
    tril = _tril_ones(n, BF16)
    sl = lambda ci: slice(ci * n, (ci + 1) * n)
    get = lambda name: [per_b[b][name][sl(ci), :] for b, ci in items]
    rrs, vvs, logws, a_sigs, gates, kkps, k2s = (get(k) for k in ('rr', 'vv', 'logw', 'a_sig', 'gate', 'kkp', 'k2'))
    kks = [kkp * lax.rsqrt(_mm_x01(kkp * kkp, ones_bd) + 1e-6) for kkp in kkps]
    cums = [_tri_mm(tril, logw) for logw in logws]
    w_incls = [jnp.exp(cum) for cum in cums]
    w_lasts = [wi[n - 1:n, :] for wi in w_incls]
    w_invs = [jnp.exp(-cum) for cum in cums]
    ats = [-kk * jnp.exp(cum - logw) for kk, cum, logw in zip(kks, cums, logws)]
    bts = [kk * a_sig * wv for kk, a_sig, wv in zip(kks, a_sigs, w_invs)]
    kts = [k2 * wv for k2, wv in zip(k2s, w_invs)]
    rts = [rr * wi for rr, wi in zip(rrs, w_incls)]
    ar_ls = [_lhs3(jnp.concatenate([at, rt], axis=0)) for at, rt in zip(ats, rts)]
    gbs = [_mm3(ar_l, _rhs3(bt, w), NT) for ar_l, bt in zip(ar_ls, bts)]
    gks = [_mm3(ar_l, _rhs3(kt, w), NT) for ar_l, kt in zip(ar_ls, kts)]
    t_ls = [_lhs3(t) for t in _unit_lower_inverse_wide([jnp.where(w.strict, -gb[:n], 0.0) for gb in gbs], w)]
    arb_ls = [_lhs3(jnp.where(w.incl, gb[n:], 0.0)) for gb in gbs]
    ark_ls = [_lhs3(jnp.where(w.incl, gk[n:], 0.0)) for gk in gks]
    v_rs = [_rhs3(vv, w) for vv in vvs]
    aakvs = [_mm3(_lhs3(jnp.where(w.strict, gk[:n], 0.0)), v_r) for gk, v_r in zip(gks, v_rs)]
    arkvs = [_mm3(ark_l, v_r) for ark_l, v_r in zip(ark_ls, v_rs)]
    bks = [jnp.concatenate([bt * wl, kt * wl], axis=0) for bt, kt, wl in zip(bts, kts, w_lasts)]

    states = [s_scr[b] for b in range(nb)]
    ys = {}
    for ci in range(nc):
        for b in range(nb):
            it = b * nc + ci
            s0 = states[b]
            ars = _mm3(ar_ls[it], _rhs3(s0, w), NT)
            u = _mm3(t_ls[it], _rhs3(ars[:n] + aakvs[it], w))
            ys[it] = ars[n:] + _mm3(arb_ls[it], _rhs3(u, w)) + arkvs[it]
            upd = _mm3_tn(jnp.concatenate([u, vvs[it]], axis=0), bks[it])
            states[b] = s0 * w_lasts[it] + _collapse_heads(upd, w)
    for b in range(nb):
        s_scr[b] = states[b]

    inv_d = 1.0 / HEAD_DIM
    for it, (b, ci) in enumerate(items):
        y = ys[it]
        yc = y - _mm_x01(y, ones_bd) * inv_d
        var_y = _mm_x01(yc * yc, ones_bd) * inv_d
        yn = yc * lax.rsqrt(var_y + GN_EPS) * lng_ref[...] + lnb_ref[...]
        bonus = _mm_x01(rrs[it] * k2s[it] * rk_ref[...], ones_bd) * vvs[it]
        o_ref[b, sl(ci), :] = ((yn + bonus) * gates[it]).astype(o_ref.dtype)

    @pl.when(c == pl.num_programs(1) - 1)
    def _():
        sout_ref[...] = s_scr[...]
        for b in range(nb):
            nsh_ref[b] = win_ref[b, 7:8, :]
            nshs_ref[b] = wins_ref[b, 7:8, :]


def _rwkv_mixer(h3, hs3, shift, s0, lp):
    bsz, l, _ = h3.shape
    nb, nc = _recurrent_tiling(bsz, l)
    rows = nc * CHUNK
    sh_main = shift[:, None, 0:3 * GW]
    sh_small = jnp.pad(shift[:, None, 3 * GW:], ((0, 0), (0, 0), (0, H_SMALL - LORA)))
    mu = lp['rwkv_mu']
    mu_main = mu[None, 0:3 * GW]
    mu_small = jnp.pad(mu[None, 3 * GW:], ((0, 0), (0, H_SMALL - LORA)))
    place = lambda w, off: jnp.zeros((H_SMALL, GW), F32).at[off:off + w.shape[0]].set(w)
    wup = place(lp['rwkv_w_up'], 0)
    aup = place(lp['rwkv_a_up'], RWKV_W_LORA)
    gup = place(lp['rwkv_g_up'], RWKV_W_LORA + RWKV_A_LORA)
    row = lambda v: v.reshape(1, GW)
    full = lambda shape: pl.BlockSpec(shape, lambda bi, i: (0,) * len(shape))
    o, s_new, sh_new, shs_new = pl.pallas_call(
        functools.partial(_rwkv_kernel, nb=nb, nc=nc),
        grid=(bsz // nb, l // rows),
        in_specs=[pl.BlockSpec((nb, rows, 3 * GW), lambda bi, i: (bi, i, H_CRKV // (3 * GW))),
                  pl.BlockSpec((nb, rows, H_SMALL), lambda bi, i: (bi, i, 0)),
                  pl.BlockSpec((nb, 1, 3 * GW), lambda bi, i: (bi, 0, 0)),
                  pl.BlockSpec((nb, 1, H_SMALL), lambda bi, i: (bi, 0, 0)),
                  pl.BlockSpec((nb, HEAD_DIM, GW), lambda bi, i: (bi, 0, 0)),
                  full((1, 3 * GW)), full((1, H_SMALL)), full((1, GW)), full((H_SMALL, GW)), full((1, GW)),
                  full((H_SMALL, GW)), full((H_SMALL, GW)), full((1, GW)), full((1, GW)), full((1, GW)),
                  full((1, GW)), full((1, GW)), full((GW, GW))],
        out_specs=[pl.BlockSpec((nb, rows, GW), lambda bi, i: (bi, i, 0)),
                   pl.BlockSpec((nb, HEAD_DIM, GW), lambda bi, i: (bi, 0, 0)),
                   pl.BlockSpec((nb, 1, 3 * GW), lambda bi, i: (bi, 0, 0)),
                   pl.BlockSpec((nb, 1, H_SMALL), lambda bi, i: (bi, 0, 0))],
        out_shape=[jax.ShapeDtypeStruct((bsz, l, GW), BF16),
                   jax.ShapeDtypeStruct((bsz, HEAD_DIM, GW), F32),
                   jax.ShapeDtypeStruct((bsz, 1, 3 * GW), F32),
                   jax.ShapeDtypeStruct((bsz, 1, H_SMALL), F32)],
        scratch_shapes=[pltpu.VMEM((nb, rows + 8, 3 * GW), F32), pltpu.VMEM((nb, rows + 8, H_SMALL), F32),
                        pltpu.VMEM((nb, HEAD_DIM, GW), F32)],
        compiler_params=_params(2),
        name="rwkv_mixer",
    )(h3, hs3, sh_main, sh_small, _heads_to_wide(s0), mu_main, mu_small, row(lp['rwkv_w0']), wup,
      row(lp['rwkv_a0']), aup, gup, row(lp['rwkv_k_k']), row(lp['rwkv_k_a']), row(lp['rwkv_r_k']),
      row(lp['rwkv_lnx_g']), row(lp['rwkv_lnx_b']), _ones_block_diag())
    new_shift = jnp.concatenate([sh_new[:, 0, :], shs_new[:, 0, S_LORA:S_LORA + LORA]], axis=-1)
    return o, _wide_to_heads(s_new), new_shift


_FB = 128
_SAMPLE_BK = 1024


def _log_forget(hs, bf):
    return -_softplus(-(hs + bf))


def _head_slice(h):
    return slice(h * HEAD_DIM, (h + 1) * HEAD_DIM)


def _attend_heads(qs, kblk, vblk, key_bias, mask, carry):
    m, l, acc = carry
    bq = qs[0].shape[0]
    kb16 = kblk.astype(BF16)
    vb16 = vblk.astype(BF16)
    parts = []
    for h in range(GROUP_HEADS):
        s = _dot(qs[h], kb16[_head_slice(h), :]) + key_bias[h:h + 1, :]
        parts.append(s if mask is None else jnp.where(mask, s, NEG))
    s = jnp.concatenate(parts, axis=0)
    m_new = jnp.maximum(m, jnp.max(s, -1, keepdims=True))
    alpha = jnp.exp(m - m_new)
    p = jnp.exp(s - m_new)
    l = alpha * l + jnp.sum(p, -1, keepdims=True)
    pb = p.astype(BF16)
    pv = jnp.concatenate([_dot(pb[h * bq:(h + 1) * bq, :], vb16[_head_slice(h), :], NT)
                          for h in range(GROUP_HEADS)], axis=0)
    return m_new, l, alpha * acc + pv


def _attend_init(bq):
    rows = GROUP_HEADS * bq
    return jnp.full((rows, 1), NEG, F32), jnp.zeros((rows, 1), F32), jnp.zeros((rows, HEAD_DIM), F32)


def _attend_store(o_ref, carry, bq):
    _, l, acc = carry
    for h in range(GROUP_HEADS):
        o_ref[:, _head_slice(h)] = (acc[h * bq:(h + 1) * bq, :] / l[h * bq:(h + 1) * bq, :]).astype(o_ref.dtype)


def _sel_rows(sel01, x):
    t = x.shape[0]
    r = _dot(sel01, jnp.concatenate(_split3(x), axis=0), NT)
    return r[:, :t] + r[:, t:2 * t] + r[:, 2 * t:]


def _attend_heads_t(qts, k_blk, vt_blk, bias_tiles, mask, carry):
    m, l, acc = carry
    bq = qts[0].shape[1]
    kb16 = k_blk.astype(BF16)
    vb16 = vt_blk.astype(BF16)
    parts = []
    for h in range(GROUP_HEADS):
        s = _dot(kb16[:, _head_slice(h)], qts[h]) + jnp.tile(bias_tiles[h], (1, bq // LANE))
        parts.append(s if mask is None else jnp.where(mask, s, NEG))
    s = jnp.concatenate(parts, axis=1)
    m_new = jnp.maximum(m, jnp.max(s, 0, keepdims=True))
    alpha = jnp.exp(m - m_new)
    p = jnp.exp(s - m_new)
    l = alpha * l + jnp.sum(p, 0, keepdims=True)
    pb = p.astype(BF16)
    pv = jnp.concatenate([_dot(vb16[_head_slice(h), :], pb[:, h * bq:(h + 1) * bq])
                          for h in range(GROUP_HEADS)], axis=1)
    return m_new, l, alpha * acc + pv


def _fox_prompt_kernel(qt_ref, k_ref, vt_ref, hs_ref, bf_ref, o_ref, lf_ref, bias_scr, *, l, bq):
    i = pl.program_id(1)

    @pl.when(i == 0)
    def _():
        ltri = _tril_ones(_FB, BF16)
        carry = jnp.zeros((1, H_SMALL), F32)
        for jb in range(l // _FB):
            rows = slice(jb * _FB, (jb + 1) * _FB)
            lf = _log_forget(hs_ref[rows, :], bf_ref[...])
            lf_ref[rows, :] = lf[:, S_F:S_F + GROUP_HEADS]
            loc = _tri_mm(ltri, lf, 3)
            nf = -(loc + carry)
            for h in range(GROUP_HEADS):
                bias_scr[h, rows, :] = jnp.broadcast_to(nf[:, S_F + h:S_F + h + 1], (_FB, LANE))
            carry = carry + loc[_FB - 1:_FB, :]

    qt = qt_ref[...]
    r_i, c_i = _iota2((bq, bq), 0), _iota2((bq, bq), 1)
    qts = [(qt[_head_slice(h), :] * (HEAD_DIM ** -0.5)).astype(BF16) for h in range(GROUP_HEADS)]

    def step(j, mask, carry):
        keys = pl.ds(pl.multiple_of(j * bq, bq), bq)
        biases = [bias_scr[h, keys, :] for h in range(GROUP_HEADS)]
        return _attend_heads_t(qts, k_ref[keys, :], vt_ref[:, keys], biases, mask, carry)

    init = (jnp.full((1, GROUP_HEADS * bq), NEG, F32), jnp.zeros((1, GROUP_HEADS * bq), F32),
            jnp.zeros((HEAD_DIM, GROUP_HEADS * bq), F32))
    carry = lax.fori_loop(0, i, lambda j, c: step(j, None, c), init)
    _, lsum, acc = step(i, r_i <= c_i, carry)
    o_t = acc / lsum
    o_t = jnp.concatenate([o_t[:, h * bq:(h + 1) * bq] for h in range(GROUP_HEADS)], axis=0)
    o_ref[...] = o_t.T.astype(o_ref.dtype)


def _fox_prompt(h3, hs3, q_t, v_t, bf_pad):
    bsz, l, _ = h3.shape
    bq = min(l, 4 * _FB)
    kern = functools.partial(_fox_prompt_kernel, l=l, bq=bq)
    return pl.pallas_call(
        kern,
        grid=(bsz, l // bq),
        in_specs=[pl.BlockSpec((None, GW, bq), lambda bi, i: (bi, 0, i)),
                  pl.BlockSpec((None, l, GW), lambda bi, i: (bi, 0, H_DK // GW)),
                  pl.BlockSpec((None, GW, l), lambda bi, i: (bi, 0, 0)),
                  pl.BlockSpec((None, l, H_SMALL), lambda bi, i: (bi, 0, 0)),
                  pl.BlockSpec((1, H_SMALL), lambda bi, i: (0, 0))],
        out_specs=[pl.BlockSpec((None, bq, GW), lambda bi, i: (bi, i, 0)),
                   pl.BlockSpec((None, l, GROUP_HEADS), lambda bi, i: (bi, 0, 0))],
        out_shape=[jax.ShapeDtypeStruct((bsz, l, GW), BF16),
                   jax.ShapeDtypeStruct((bsz, l, GROUP_HEADS), F32)],
        scratch_shapes=[pltpu.VMEM((GROUP_HEADS, l, LANE), F32)],
        compiler_params=_params(2),
        name="fox_prompt",
    )(q_t, h3, v_t, hs3, bf_pad)


def _fox_sample_kernel(qt_ref, kt_ref, vt_ref, hs_ref, bf_ref, ck_ref, cv_ref, clf_ref, o_ref, lf_ref, cum_scr,
                       *, l, p):
    nblk = p // _FB
    per = _SAMPLE_BK // _FB

    loc = _mm_x01(clf_ref[...].reshape(nblk * 8, _FB), _triu_ones(_FB, BF16), 3)
    carry = jnp.zeros((8, 1), F32)
    for jb in range(nblk):
        blk = loc[jb * 8:(jb + 1) * 8, :]
        cum_scr[jb // per, :, (jb % per) * _FB:(jb % per + 1) * _FB] = blk + carry
        carry = carry + blk[:, _FB - 1:_FB]
    total = carry

    lf = _log_forget(hs_ref[...], bf_ref[...])
    lf_ref[...] = lf[:, S_F:S_F + GROUP_HEADS]
    cum_r = _mm_x01(_sel_rows(_col_selector(8, H_SMALL, S_F).astype(BF16), lf), _triu_ones(l, BF16), 3)
    q = qt_ref[...].T
    r_i, c_i = _iota2((l, l), 0), _iota2((l, l), 1)
    qs = [(q[:, _head_slice(h)] * (HEAD_DIM ** -0.5)).astype(BF16) for h in range(GROUP_HEADS)]

    def body(j, carry):
        keys = pl.ds(pl.multiple_of(j * _SAMPLE_BK, _SAMPLE_BK), _SAMPLE_BK)
        suffix = total - cum_scr[j]
        return _attend_heads(qs, ck_ref[:, keys], cv_ref[:, keys], suffix, None, carry)

    carry = lax.fori_loop(0, p // _SAMPLE_BK, body, _attend_init(l))
    carry = _attend_heads(qs, kt_ref[...], vt_ref[...], -cum_r, c_i <= r_i, carry)
    _attend_store(o_ref, carry, l)


def _fox_sample(hs3, q_t, k_t, v_t, bf_pad, layer, ck_t, cv_t, clogf):
    bsz, l, _ = hs3.shape
    p = ck_t.shape[-1]
    clf = clogf.reshape(bsz, p // _FB, _FB, GROUP_HEADS).transpose(0, 1, 3, 2)
    clf = jnp.pad(clf, ((0, 0), (0, 0), (0, 8 - GROUP_HEADS), (0, 0)))
    kern = functools.partial(_fox_sample_kernel, l=l, p=p)
    return pl.pallas_call(
        kern,
        grid=(bsz,),
        in_specs=[pl.BlockSpec((None, GW, l), lambda bi: (bi, 0, 0)),
                  pl.BlockSpec((None, GW, l), lambda bi: (bi, 0, 0)),
                  pl.BlockSpec((None, GW, l), lambda bi: (bi, 0, 0)),
                  pl.BlockSpec((None, l, H_SMALL), lambda bi: (bi, 0, 0)),
                  pl.BlockSpec((1, H_SMALL), lambda bi: (0, 0)),
                  pl.BlockSpec((None, None, GW, p), lambda bi: (layer, bi, 0, 0)),
                  pl.BlockSpec((None, None, GW, p), lambda bi: (layer, bi, 0, 0)),
                  pl.BlockSpec((None, p // _FB, 8, _FB), lambda bi: (bi, 0, 0, 0))],
        out_specs=[pl.BlockSpec((None, l, GW), lambda bi: (bi, 0, 0)),
                   pl.BlockSpec((None, l, GROUP_HEADS), lambda bi: (bi, 0, 0))],
        out_shape=[jax.ShapeDtypeStruct((bsz, l, GW), BF16),
                   jax.ShapeDtypeStruct((bsz, l, GROUP_HEADS), F32)],
        scratch_shapes=[pltpu.VMEM((p // _SAMPLE_BK, 8, _SAMPLE_BK), F32)],
        compiler_params=_params(1),
        name="fox_sample",
    )(q_t, k_t, v_t, hs3, bf_pad, ck_t, cv_t, clf)


_R_G, _R_E, _R_ROWS = 0, 8, 32
_MOE_EXPERTS_PER_STEP = 2


def _ffn_kernel(oa_ref, ob_ref, oc_ref, od_ref, x_ref, wo_ref, g1_ref, b1_ref, wrh_ref, wrl_ref, brc_ref, wg_ref,
                wu_ref, wd_ref, g2_ref, b2_ref, y_ref, comb_scr, xb_scr, x1_scr):
    e = pl.program_id(1)
    tm = x_ref.shape[0]
    lane = _iota2((tm, LANE), 1)

    @pl.when(e == 0)
    def _():
        mix = jnp.concatenate([ref[...] for ref in (oa_ref, ob_ref, oc_ref, od_ref)], axis=1)
        x1 = _layer_norm(DN_ALPHA * x_ref[...] + _dot(mix, wo_ref[...]), g1_ref[...], b1_ref[...])
        x1_scr[...] = x1
        y_ref[...] = jnp.zeros_like(y_ref)
        x_hi, x_lo = _split2(x1)
        xb_scr[...] = x_hi
        wr_hi = wrh_ref[...]
        r = _dot(jnp.concatenate([wr_hi, wrl_ref[...]], axis=0), x_hi, NT)
        lt = r[:_R_ROWS] + r[_R_ROWS:] + _dot(wr_hi, x_lo, NT) + brc_ref[...]
        gl = lt[_R_G:_R_G + N_EXPERT_GROUPS]
        grow = _iota2(gl.shape, 0)
        ge = jnp.exp(gl - jnp.max(gl, 0, keepdims=True))
        pg = ge / jnp.sum(ge, 0, keepdims=True)
        gp = jnp.max(pg, 0, keepdims=True)
        gi = jnp.min(jnp.where(pg == gp, grow, N_EXPERT_GROUPS), 0, keepdims=True)
        le = lt[_R_E:_R_E + N_EXPERTS]
        erow = _iota2(le.shape, 0)
        emask = (erow >> 2) == gi
        el = jnp.where(emask, le, NEG)
        ee = jnp.exp(el - jnp.max(el, 0, keepdims=True))
        ep = ee / jnp.sum(ee, 0, keepdims=True)
        m1 = jnp.max(jnp.where(emask, ep, -1.0), 0, keepdims=True)
        i1 = jnp.min(jnp.where(jnp.logical_and(emask, ep == m1), erow, N_EXPERTS), 0, keepdims=True)
        rest = jnp.logical_and(emask, erow != i1)
        m2 = jnp.max(jnp.where(rest, ep, -1.0), 0, keepdims=True)
        i2 = jnp.min(jnp.where(jnp.logical_and(rest, ep == m2), erow, N_EXPERTS), 0, keepdims=True)
        den = m1 + m2
        comb_t = jnp.where(erow == i1, gp * m1 / den, jnp.where(erow == i2, gp * m2 / den, 0.0))
        pieces = jnp.concatenate(_split3(comb_t), axis=0)
        sel = ((_iota2((3 * N_EXPERTS, LANE), 0) & (N_EXPERTS - 1)) == _iota2((3 * N_EXPERTS, LANE), 1)).astype(BF16)
        comb_scr[...] = _dot(pieces, sel, TN)

    xb = xb_scr[...]
    comb = comb_scr[...]
    hidden = []
    for k in range(_MOE_EXPERTS_PER_STEP):
        ce = jnp.sum(jnp.where(lane == e * _MOE_EXPERTS_PER_STEP + k, comb, 0.0), -1, keepdims=True)
        hh = _silu(_mm(xb, wg_ref[k])) * _mm(xb, wu_ref[k])
        hidden.append((hh * ce).astype(BF16))
    hidden = jnp.concatenate(hidden, axis=1)
    w_down = wd_ref[...].reshape(_MOE_EXPERTS_PER_STEP * D_EXPERT, D_MODEL)
    y_ref[...] += _mm(hidden, w_down)

    @pl.when(e == pl.num_programs(1) - 1)
    def _():
        y_ref[...] = _layer_norm(DN_ALPHA * x1_scr[...] + y_ref[...], g2_ref[...], b2_ref[...])


def _ffn(mixers, x2d, w_out_bf16, g1, b1, wr_hi, wr_lo, br_col, layer, wg, wu, wd, g2, b2):
    t = x2d.shape[0]
    tm = min(t, 1024)
    ne = _MOE_EXPERTS_PER_STEP
    mix = pl.BlockSpec((tm, GW), lambda i, e: (i, 0))
    vec = pl.BlockSpec((1, D_MODEL), lambda i, e: (0, 0))
    return pl.pallas_call(
        _ffn_kernel,
        grid=(t // tm, N_EXPERTS // ne),
        in_specs=[mix, mix, mix, mix,
                  pl.BlockSpec((tm, D_MODEL), lambda i, e: (i, 0)),
                  pl.BlockSpec((D_MODEL, D_MODEL), lambda i, e: (0, 0)), vec, vec,
                  pl.BlockSpec((_R_ROWS, D_MODEL), lambda i, e: (0, 0)),
                  pl.BlockSpec((_R_ROWS, D_MODEL), lambda i, e: (0, 0)),
                  pl.BlockSpec((_R_ROWS, 1), lambda i, e: (0, 0)),
                  pl.BlockSpec((None, ne, D_MODEL, D_EXPERT), lambda i, e: (layer, e, 0, 0)),
                  pl.BlockSpec((None, ne, D_MODEL, D_EXPERT), lambda i, e: (layer, e, 0, 0)),
                  pl.BlockSpec((None, ne, D_EXPERT, D_MODEL), lambda i, e: (layer, e, 0, 0)), vec, vec],
        out_specs=pl.BlockSpec((tm, D_MODEL), lambda i, e: (i, 0)),
        out_shape=jax.ShapeDtypeStruct((t, D_MODEL), F32),
        scratch_shapes=[pltpu.VMEM((tm, LANE), F32), pltpu.VMEM((tm, D_MODEL), BF16),
                        pltpu.VMEM((tm, D_MODEL), F32)],
        compiler_params=_params(2),
        name="ffn",
    )(*mixers, x2d, w_out_bf16, g1, b1, wr_hi, wr_lo, br_col, wg, wu, wd, g2, b2)


def _prep_layer(lp):
    w = lp['w_in_t']
    c0 = CONV_COLS
    c1 = c0 + GDN_COLS
    c2 = c1 + RWKV_COLS
    main = jnp.concatenate([w[c0:c0 + 3 * GW], w[c1:c1 + 3 * GW], w[c2 + GW:c2 + 2 * GW],
                            w[0:GW], w[GW:2 * GW], w[c0 + 3 * GW:c0 + 4 * GW]], axis=0).astype(BF16)
    kv_t = w[c2:c2 + 3 * GW].astype(BF16)
    small = jnp.concatenate([w[c1 + 3 * GW:c2], w[c0 + 4 * GW:c1], w[c2 + 3 * GW:],
                             jnp.zeros((H_SMALL - LORA - 3 * GROUP_HEADS, D_MODEL), F32)], axis=0)
    gap = jnp.zeros((_R_E - N_EXPERT_GROUPS, D_MODEL), F32)
    tail = jnp.zeros((_R_ROWS - _R_E - N_EXPERTS, D_MODEL), F32)
    wr_t = jnp.concatenate([lp['router_g_w'].T, gap, lp['router_e_w'].T, tail], axis=0)
    wr_hi = wr_t.astype(BF16)
    wr_lo = (wr_t - wr_hi.astype(F32)).astype(BF16)
    br = jnp.concatenate([lp['router_g_b'], gap[:, 0], lp['router_e_b'], tail[:, 0]])[:, None]
    bf_pad = jnp.zeros((1, H_SMALL), F32).at[0, S_F:S_F + GROUP_HEADS].set(lp['fox_b_f'])
    return dict(w_main=main, w_small=small, w_kv_t=kv_t, wr_hi=wr_hi, wr_lo=wr_lo, br=br, bf_pad=bf_pad)


def _trunk_layer(x, lp, pp, layer, big, conv_buf, gdn_buf, gdn_s, rw_shift, rw_s, fox_cache):
    bsz, l, d = x.shape
    x2d = x.reshape(bsz * l, d)
    h3, hs3, q_t, k_t, v_t = _inproj(x, pp['w_main'], pp['w_small'], pp['w_kv_t'])

    o_a, new_conv = _conv_mixer(h3, conv_buf, lp['conv_w'], lp['conv_b'][None], lp['conv_ln_g'][None],
                                lp['conv_ln_b'][None])
    o_b, new_gdn_s, new_gdn_buf = _gdn_mixer(h3, hs3, gdn_buf, gdn_s, lp['gdn_conv_w'], lp['gdn_a_log'],
                                             lp['gdn_dt_bias'], lp['gdn_norm_w'])
    o_c, new_rw_s, new_shift = _rwkv_mixer(h3, hs3, rw_shift, rw_s, lp)
    if fox_cache is None:
        o_d, logf = _fox_prompt(h3, hs3, q_t, v_t, pp['bf_pad'])
    else:
        o_d, logf = _fox_sample(hs3, q_t, k_t, v_t, pp['bf_pad'], layer, *fox_cache)

    mixers = [o.reshape(bsz * l, GW) for o in (o_a, o_b, o_c, o_d)]
    x2 = _ffn(mixers, x2d, big['w_out'][layer], lp['ln1_g'][None], lp['ln1_b'][None], pp['wr_hi'], pp['wr_lo'],
              pp['br'], layer, *big['experts'], lp['ln2_g'][None], lp['ln2_b'][None])
    return x2.reshape(bsz, l, d), (new_conv, new_gdn_buf, new_gdn_s, new_shift, new_rw_s, k_t, v_t, logf)


def _untranspose(xs_t):
    bsz, _, l = xs_t[0].shape
    return jnp.stack(xs_t).reshape(len(xs_t), bsz, GROUP_HEADS, HEAD_DIM, l).transpose(0, 1, 4, 2, 3)


_LAYER_KEYS = ('w_in_t', 'conv_w', 'conv_b', 'conv_ln_g', 'conv_ln_b', 'gdn_conv_w', 'gdn_a_log', 'gdn_dt_bias',
               'gdn_norm_w', 'rwkv_mu', 'rwkv_w0', 'rwkv_w_up', 'rwkv_a0', 'rwkv_a_up', 'rwkv_g_up', 'rwkv_k_k',
               'rwkv_k_a', 'rwkv_r_k', 'rwkv_lnx_g', 'rwkv_lnx_b', 'fox_b_f', 'ln1_g', 'ln1_b',
               'router_g_w', 'router_g_b', 'router_e_w', 'router_e_b', 'ln2_g', 'ln2_b')


def kernel(x_prompt, x_sample, cache_fox_k, cache_fox_v, cache_fox_logf, state_conv, state_gdn_conv, state_gdn, state_rwkv_shift, state_rwkv, w_in, conv_w, conv_b, conv_ln_g, conv_ln_b, gdn_conv_w, gdn_a_log, gdn_dt_bias, gdn_norm_w, rwkv_mu, rwkv_w0, rwkv_w_up, rwkv_a0, rwkv_a_up, rwkv_g_up, rwkv_k_k, rwkv_k_a, rwkv_r_k, rwkv_lnx_g, rwkv_lnx_b, fox_b_f, w_out, ln1_g, ln1_b, router_g_w, router_g_b, router_e_w, router_e_b, exp_w_gate, exp_w_up, exp_w_down, ln2_g, ln2_b):
    w_in_t = w_in.transpose(0, 2, 1)
    weights = dict(zip(_LAYER_KEYS, (w_in_t, conv_w, conv_b, conv_ln_g, conv_ln_b, gdn_conv_w, gdn_a_log,
                                     gdn_dt_bias, gdn_norm_w, rwkv_mu, rwkv_w0, rwkv_w_up, rwkv_a0, rwkv_a_up,
                                     rwkv_g_up, rwkv_k_k, rwkv_k_a, rwkv_r_k, rwkv_lnx_g, rwkv_lnx_b, fox_b_f,
                                     ln1_g, ln1_b, router_g_w, router_g_b, router_e_w, router_e_b,
                                     ln2_g, ln2_b)))
    big = dict(w_out=w_out.astype(BF16), experts=(exp_w_gate, exp_w_up, exp_w_down))
    depth, bs, past = cache_fox_k.shape[:3]
    ck_t = cache_fox_k.transpose(0, 1, 3, 4, 2).reshape(depth, bs, GW, past)
    cv_t = cache_fox_v.transpose(0, 1, 3, 4, 2).reshape(depth, bs, GW, past)
    xp, xs = x_prompt, x_sample
    bp = x_prompt.shape[0]
    outs_p = [[] for _ in range(8)]
    outs_s = [[] for _ in range(8)]
    for l in range(DEPTH):
        lp = {k: v[l] for k, v in weights.items()}
        pp = _prep_layer(lp)
        xp, st_p = _trunk_layer(xp, lp, pp, l, big,
                                jnp.zeros((bp, CONV_WIDTH - 1, GW), F32),
                                jnp.zeros((bp, GDN_CONV_WIDTH - 1, 3 * GW), F32),
                                jnp.zeros((bp, GROUP_HEADS, HEAD_DIM, HEAD_DIM), F32),
                                jnp.zeros((bp, RWKV_COLS), F32),
                                jnp.zeros((bp, GROUP_HEADS, HEAD_DIM, HEAD_DIM), F32),
                                None)
        xs, st_s = _trunk_layer(xs, lp, pp, l, big, state_conv[l], state_gdn_conv[l], state_gdn[l],
                                state_rwkv_shift[l], state_rwkv[l], (ck_t, cv_t, cache_fox_logf[l]))
        for i in range(8):
            outs_p[i].append(st_p[i])
            outs_s[i].append(st_s[i])

    def assemble(outs):
        conv, gdn_buf, gdn_s, shift, rw_s = (jnp.stack(o) for o in outs[:5])
        return conv, gdn_buf, gdn_s, shift, rw_s, _untranspose(outs[5]), _untranspose(outs[6]), jnp.stack(outs[7])

    return (xp, xs, *assemble(outs_p), *assemble(outs_s))
```

```python
import functools

import jax
import jax.numpy as jnp
from jax import lax
from jax.experimental import pallas as pl
from jax.experimental.pallas import tpu as pltpu

F32 = jnp.float32
BF16 = jnp.bfloat16

D_MODEL = 1024
DEPTH = 2
CHUNK = 64
HEAD_DIM = 64
GW = D_MODEL // 4
GROUP_HEADS = GW // HEAD_DIM
CONV_WIDTH = 31
GDN_CONV_WIDTH = 4
RWKV_W_LORA = 16
RWKV_A_LORA = 16
RWKV_G_LORA = 32
N_EXPERT_GROUPS = 4
EXPERTS_PER_GROUP = 4
N_EXPERTS = N_EXPERT_GROUPS * EXPERTS_PER_GROUP
D_EXPERT = 256
DN_ALPHA = (2 * DEPTH) ** 0.25
LN_EPS = 1e-5
GN_EPS = 64e-5

CONV_COLS = 2 * GW
GDN_COLS = 4 * GW + 2 * GROUP_HEADS
RWKV_COLS = 3 * GW + RWKV_W_LORA + RWKV_A_LORA + RWKV_G_LORA
FOX_COLS = 3 * GW + GROUP_HEADS
LORA = RWKV_W_LORA + RWKV_A_LORA + RWKV_G_LORA

H_BQKV, H_CRKV, H_DK, H_AVAL, H_AGATE, H_BZ = 0, 768, 1536, 1792, 2048, 2304
H_MAIN = 2560
QKV_ROWS = 3 * GW
S_LORA, S_A, S_B, S_F = 0, 64, 68, 72
H_SMALL = 128
LANE = 128

VMEM_LIMIT = 56 * 1024 * 1024

NN = (((1,), (0,)), ((), ()))
NT = (((1,), (1,)), ((), ()))
TN = (((0,), (0,)), ((), ()))
NEG = -1e30


def _dot(a, b, dims=NN):
    return lax.dot_general(a, b, dims, preferred_element_type=F32)


def _mm(a, b, dims=NN):
    return _dot(a.astype(BF16), b.astype(BF16), dims)


def _sigmoid(x):
    return 0.5 * (jnp.tanh(0.5 * x) + 1.0)


def _silu(x):
    return x * _sigmoid(x)


def _softplus(x):
    return jnp.maximum(x, 0.0) + jnp.log(1.0 + jnp.exp(-jnp.abs(x)))


def _iota2(shape, dim):
    return lax.broadcasted_iota(jnp.int32, shape, dim)


def _tril_ones(n, dtype=F32):
    r, c = _iota2((n, n), 0), _iota2((n, n), 1)
    return (r >= c).astype(dtype)


def _triu_ones(n, dtype=F32):
    r, c = _iota2((n, n), 0), _iota2((n, n), 1)
    return (r <= c).astype(dtype)


def _col_selector(rows, width, col0):
    r, c = _iota2((rows, width), 0), _iota2((rows, width), 1)
    return (c == r + col0).astype(F32)


def _layer_norm(x, g, b):
    mu = jnp.mean(x, axis=-1, keepdims=True)
    xc = x - mu
    var = jnp.mean(xc * xc, axis=-1, keepdims=True)
    return xc * lax.rsqrt(var + LN_EPS) * g + b


def _split2(x):
    hi = x.astype(BF16)
    lo = (x - hi.astype(F32)).astype(BF16)
    return hi, lo


def _split3(x):
    hi = x.astype(BF16)
    r = x - hi.astype(F32)
    mid = r.astype(BF16)
    lo = (r - mid.astype(F32)).astype(BF16)
    return hi, mid, lo


class _Wide:
    def __init__(self):
        self.i = _iota2((CHUNK, GW), 0)
        lane = _iota2((CHUNK, GW), 1)
        self.j = lane & (HEAD_DIM - 1)
        self.head = lane >> 6
        self.incl = self.i >= self.j
        self.strict = self.i > self.j
        self.eye = self.i == self.j
        r, c = _iota2((GW, GW), 0), _iota2((GW, GW), 1)
        self.block_diag = (r >> 6) == (c >> 6)


def _head_block_diag(x, w):
    t = jnp.concatenate([x] * GROUP_HEADS, axis=0)
    return jnp.where(w.block_diag, t, jnp.zeros_like(t))


def _collapse_heads(full, w):
    out = jnp.zeros((HEAD_DIM, GW), F32)
    for h in range(GROUP_HEADS):
        out = out + jnp.where(w.head == h, full[h * HEAD_DIM:(h + 1) * HEAD_DIM, :], 0.0)
    return out


def _lhs3(a, precise=False):
    if not precise:
        return None, a.astype(BF16)
    hi, lo = _split2(a)
    return jnp.concatenate([hi, lo], axis=0), hi


def _rhs3(b, w=None, precise=False):
    if not precise:
        hi = b.astype(BF16)
        return (hi if w is None else _head_block_diag(hi, w)), None
    hi, lo = _split2(b)
    if w is None:
        return hi, lo
    return _head_block_diag(hi, w), _head_block_diag(lo, w)


def _mm3(lhs, rhs, dims=NN):
    cat, hi = lhs
    bh, bl = rhs
    if cat is None or bl is None:
        return _dot(hi, bh, dims)
    m = hi.shape[0]
    r = _dot(cat, bh, dims)
    return r[:m] + r[m:] + _dot(hi, bl, dims)


def _mm3_tn(a, b, precise=False):
    if not precise:
        return _dot(a.astype(BF16), b.astype(BF16), TN)
    ah, al = _split2(a)
    bh, bl = _split2(b)
    m = a.shape[1]
    r = _dot(jnp.concatenate([ah, al], axis=1), bh, TN)
    return r[:m] + r[m:] + _dot(ah, bl, TN)


def _mm_x01(a, m01, pieces=2):
    m = a.shape[0]
    r = _dot(jnp.concatenate(_split3(a) if pieces == 3 else _split2(a), axis=0), m01)
    return sum(r[i * m:(i + 1) * m] for i in range(pieces))


def _tri_mm(tri01, x, pieces=2):
    n = x.shape[1]
    r = _dot(tri01, jnp.concatenate(_split3(x) if pieces == 3 else _split2(x), axis=1))
    return sum(r[:, i * n:(i + 1) * n] for i in range(pieces))


def _unit_lower_inverse_wide(mats, w):
    eye = w.eye.astype(F32)
    same8 = (w.i >> 3) == (w.j >> 3)
    ds = [jnp.where(same8, a, 0.0) for a in mats]
    d2 = [_mm3(_lhs3(d), _rhs3(d, w)) for d in ds]
    d2_r = [_rhs3(x, w) for x in d2]
    d4 = [_mm3(_lhs3(x), r) for x, r in zip(d2, d2_r)]
    imd = [eye - d for d in ds]
    p1 = [i + _mm3(_lhs3(i), r) for i, r in zip(imd, d2_r)]
    ts = [p + _mm3(_lhs3(p), _rhs3(x, w)) for p, x in zip(p1, d4)]
    for s in range(3, 6):
        big = (w.i >> (s + 1)) == (w.j >> (s + 1))
        small = (w.i >> s) == (w.j >> s)
        sel = jnp.logical_and(big, jnp.logical_not(small))
        inner = [_mm3(_lhs3(jnp.where(sel, a, 0.0)), _rhs3(t, w)) for a, t in zip(mats, ts)]
        ts = [t - _mm3(_lhs3(t), _rhs3(x, w)) for t, x in zip(ts, inner)]
    return ts


def _params(n_axes):
    return pltpu.CompilerParams(dimension_semantics=("arbitrary",) * n_axes,
                                vmem_limit_bytes=VMEM_LIMIT)


def _heads_to_wide(s):
    b = s.shape[0]
    return s.transpose(0, 2, 1, 3).reshape(b, HEAD_DIM, GW)


def _wide_to_heads(s):
    b = s.shape[0]
    return s.reshape(b, HEAD_DIM, GROUP_HEADS, HEAD_DIM).transpose(0, 2, 1, 3)


def _ones_block_diag():
    r, c = jnp.arange(GW)[:, None], jnp.arange(GW)[None, :]
    return ((r // HEAD_DIM) == (c // HEAD_DIM)).astype(BF16)


def _lane_spread(col0):
    r, c = jnp.arange(H_SMALL)[:, None], jnp.arange(GW)[None, :]
    return ((r - col0) == (c // HEAD_DIM)).astype(BF16)


def _inproj_kernel(x_ref, wm_ref, wsh_ref, wsl_ref, wkv_ref, h_ref, hs_ref, qt_ref, kt_ref, vt_ref):
    x = x_ref[...]
    x_hi, x_lo = _split2(x)
    tm = x.shape[0]
    h_ref[...] = _dot(x_hi, wm_ref[...], NT)
    r = _dot(jnp.concatenate([x_hi, x_lo], axis=0), wsh_ref[...], NT)
    hs_ref[...] = r[:tm] + r[tm:] + _dot(x_hi, wsl_ref[...], NT)
    qkv = _dot(wkv_ref[...], x_hi, NT)
    qt_ref[...] = qkv[:GW]
    kt_ref[...] = qkv[GW:2 * GW]
    vt_ref[...] = qkv[2 * GW:]


def _inproj(x3, w_main_t, w_small_t, w_kv_t):
    bsz, l, _ = x3.shape
    tm = min(l, 512)
    const = lambda shape: pl.BlockSpec(shape, lambda bi, i: (0, 0))
    ws_hi = w_small_t.astype(BF16)
    ws_lo = (w_small_t - ws_hi.astype(F32)).astype(BF16)
    kv_spec = pl.BlockSpec((None, GW, tm), lambda bi, i: (bi, 0, i))
    return pl.pallas_call(
        _inproj_kernel,
        grid=(bsz, l // tm),
        in_specs=[pl.BlockSpec((None, tm, D_MODEL), lambda bi, i: (bi, i, 0)),
                  const((H_MAIN, D_MODEL)), const((H_SMALL, D_MODEL)), const((H_SMALL, D_MODEL)),
                  const((QKV_ROWS, D_MODEL))],
        out_specs=[pl.BlockSpec((None, tm, H_MAIN), lambda bi, i: (bi, i, 0)),
                   pl.BlockSpec((None, tm, H_SMALL), lambda bi, i: (bi, i, 0)), kv_spec, kv_spec, kv_spec],
        out_shape=[jax.ShapeDtypeStruct((bsz, l, H_MAIN), F32), jax.ShapeDtypeStruct((bsz, l, H_SMALL), F32)]
        + [jax.ShapeDtypeStruct((bsz, GW, l), F32)] * 3,
        compiler_params=_params(2),
        name="inproj",
    )(x3, w_main_t, ws_hi, ws_lo, w_kv_t)


_HALO = 32
_SUBLANES = 8


def _conv_kernel(val_ref, gate_ref, buf_ref, cw_ref, cb_ref, g_ref, b_ref, o_ref, nbuf_ref, win_ref, sh_ref, *, tl):
    i = pl.program_id(1)
    pad = _HALO - (CONV_WIDTH - 1)
    span = tl + _HALO - _SUBLANES

    @pl.when(i == 0)
    def _():
        win_ref[0:8, :] = jnp.zeros((8, GW), F32)
        win_ref[pad:_HALO, :] = buf_ref[...]

    u = val_ref[...] * _sigmoid(gate_ref[...])
    win_ref[_HALO:_HALO + tl, :] = u
    for s in range(1, _SUBLANES):
        sh_ref[s, :, :] = win_ref[s:s + span, :]
    sub = min(tl, 64)
    for r0 in range(0, tl, sub):
        acc = jnp.zeros((sub, GW), F32)
        for j in range(CONV_WIDTH):
            s, base = (pad + j) % _SUBLANES, (pad + j) // _SUBLANES * _SUBLANES + r0
            rows = win_ref[base:base + sub, :] if s == 0 else sh_ref[s, base:base + sub, :]
            acc = acc + cw_ref[j:j + 1, :] * rows
        y = _layer_norm(acc + cb_ref[...], g_ref[...], b_ref[...])
        o_ref[r0:r0 + sub, :] = _silu(y).astype(o_ref.dtype)
    tail = win_ref[tl:tl + _HALO, :]
    win_ref[0:_HALO, :] = tail

    @pl.when(i == pl.num_programs(1) - 1)
    def _():
        nbuf_ref[...] = win_ref[pad:_HALO, :]


def _conv_mixer(h3, buf, cw, cb, g, b):
    bsz, l, _ = h3.shape
    tl = min(l, 512)
    kern = functools.partial(_conv_kernel, tl=tl)
    full = lambda shape: pl.BlockSpec(shape, lambda bi, i: (0,) * len(shape))
    return pl.pallas_call(
        kern,
        grid=(bsz, l // tl),
        in_specs=[pl.BlockSpec((None, tl, GW), lambda bi, i: (bi, i, H_AVAL // GW)),
                  pl.BlockSpec((None, tl, GW), lambda bi, i: (bi, i, H_AGATE // GW)),
                  pl.BlockSpec((None, CONV_WIDTH - 1, GW), lambda bi, i: (bi, 0, 0)),
                  full((CONV_WIDTH, GW)), full((1, GW)), full((1, GW)), full((1, GW))],
        out_specs=[pl.BlockSpec((None, tl, GW), lambda bi, i: (bi, i, 0)),
                   pl.BlockSpec((None, CONV_WIDTH - 1, GW), lambda bi, i: (bi, 0, 0))],
        out_shape=[jax.ShapeDtypeStruct((bsz, l, GW), BF16),
                   jax.ShapeDtypeStruct((bsz, CONV_WIDTH - 1, GW), F32)],
        scratch_shapes=[pltpu.VMEM((tl + _HALO, GW), F32),
                        pltpu.VMEM((_SUBLANES, tl + _HALO - _SUBLANES, GW), F32)],
        compiler_params=_params(2),
        name="conv_mixer",
    )(h3, h3, buf, cw, cb, g, b)


def _gdn_kernel(qkv_ref, z_ref, hs_ref, buf_ref, s0_ref, cw_ref, alog_ref, dtb_ref, nw_ref, ones_ref, sela_ref,
                selb_ref, o_ref, sout_ref, nbuf_ref, win_ref, s_scr, *, nb, nc):
    c = pl.program_id(1)
    n = CHUNK
    rows = nc * n
    kw = GDN_CONV_WIDTH - 1
    items = [(b, ci) for b in range(nb) for ci in range(nc)]

    @pl.when(c == 0)
    def _():
        for b in range(nb):
            win_ref[b, 0:8, :] = jnp.zeros((8, 3 * GW), F32)
            win_ref[b, 8 - kw:8, :] = buf_ref[b]
        s_scr[...] = s0_ref[...]

    qkvs = []
    for b in range(nb):
        win_ref[b, 8:8 + rows, :] = qkv_ref[b]
        conv = jnp.zeros((rows, 3 * GW), F32)
        for j in range(GDN_CONV_WIDTH):
            conv = conv + cw_ref[j:j + 1, :] * win_ref[b, 8 - kw + j:8 - kw + j + rows, :]
        tail = win_ref[b, rows:rows + 8, :]
        win_ref[b, 0:8, :] = tail
        qkvs.append(_silu(conv))

    w = _Wide()
    ones_bd = ones_ref[...]
    tril = _tril_ones(n, BF16)
    sl = lambda ci: slice(ci * n, (ci + 1) * n)
    qs = [qkvs[b][sl(ci), 0:GW] for b, ci in items]
    ks = [qkvs[b][sl(ci), GW:2 * GW] for b, ci in items]
    vs = [qkvs[b][sl(ci), 2 * GW:3 * GW] for b, ci in items]
    sss = [_mm_x01(jnp.concatenate([q * q, k * k], axis=0), ones_bd) for q, k in zip(qs, ks)]
    qs = [q * lax.rsqrt(ss[:n] + 1e-6) * (HEAD_DIM ** -0.5) for q, ss in zip(qs, sss)]
    ks = [k * lax.rsqrt(ss[n:] + 1e-6) for k, ss in zip(ks, sss)]

    hss = [hs_ref[b, sl(ci), :] for b, ci in items]
    gs = [_mm_x01(-jnp.exp(alog_ref[...]) * _softplus(hs + dtb_ref[...]), sela_ref[...]) for hs in hss]
    betas = [_mm_x01(_sigmoid(hs), selb_ref[...]) for hs in hss]
    gcs = [_tri_mm(tril, g) for g in gs]
    grs = [jnp.sum(jnp.where(w.eye, gc, 0.0), axis=0, keepdims=True) for gc in gcs]
    decays = [jnp.exp(jnp.where(w.incl, gc - gr, NEG)) for gc, gr in zip(gcs, grs)]

    kbs = [k * beta for k, beta in zip(ks, betas)]
    grams = [_mm3(_lhs3(jnp.concatenate([kb, q], axis=0)), _rhs3(k, w), NT)
             for kb, q, k in zip(kbs, qs, ks)]
    amats = [jnp.where(w.strict, gram[:n] * decay, 0.0) for gram, decay in zip(grams, decays)]
    qks = [jnp.where(w.incl, gram[n:] * decay, 0.0) for gram, decay in zip(grams, decays)]
    t_ls = [_lhs3(t) for t in _unit_lower_inverse_wide(amats, w)]
    egs = [jnp.exp(gc) for gc in gcs]
    xvs = [_mm3(t_l, _rhs3(v * beta, w)) for t_l, v, beta in zip(t_ls, vs, betas)]
    xks = [_mm3(t_l, _rhs3(kb * eg, w)) for t_l, kb, eg in zip(t_ls, kbs, egs)]
    xq_ls = [_lhs3(jnp.concatenate([xk, q * eg], axis=0)) for xk, q, eg in zip(xks, qs, egs)]
    qk_ls = [_lhs3(qk) for qk in qks]
    gls = [gc[n - 1:n, :] for gc in gcs]
    kds = [k * jnp.exp(gl - gc) for k, gl, gc in zip(ks, gls, gcs)]

    states = [s_scr[b] for b in range(nb)]
    outs = {}
    for ci in range(nc):
        for b in range(nb):
            it = b * nc + ci
            s = states[b]
            rs = _mm3(xq_ls[it], _rhs3(s, w))
            u = xvs[it] - rs[:n]
            outs[it] = rs[n:] + _mm3(qk_ls[it], _rhs3(u, w))
            states[b] = s * jnp.exp(gls[it]) + _collapse_heads(_mm3_tn(kds[it], u), w)
    for b in range(nb):
        s_scr[b] = states[b]

    for it, (b, ci) in enumerate(items):
        o = outs[it]
        ms = _mm_x01(o * o, ones_bd) * (1.0 / HEAD_DIM)
        o = o * lax.rsqrt(ms + 1e-6) * nw_ref[...] * _silu(z_ref[b, sl(ci), :])
        o_ref[b, sl(ci), :] = o.astype(o_ref.dtype)

    @pl.when(c == pl.num_programs(1) - 1)
    def _():
        sout_ref[...] = s_scr[...]
        for b in range(nb):
            nbuf_ref[b] = win_ref[b, 8 - kw:8, :]


def _recurrent_tiling(bsz, l):
    nc = min(l // CHUNK, 4)
    nb = 2 if nc > 1 else min(bsz, 4)
    return nb, nc


def _gdn_mixer(h3, hs3, buf, s0, cw, a_log, dt_bias, norm_w):
    bsz, l, _ = h3.shape
    nb, nc = _recurrent_tiling(bsz, l)
    rows = nc * CHUNK
    pad_r = lambda v, off: jnp.zeros((1, H_SMALL), F32).at[0, off:off + GROUP_HEADS].set(v)
    full = lambda shape: pl.BlockSpec(shape, lambda bi, i: (0,) * len(shape))
    o, s_new, new_buf = pl.pallas_call(
        functools.partial(_gdn_kernel, nb=nb, nc=nc),
        grid=(bsz // nb, l // rows),
        in_specs=[pl.BlockSpec((nb, rows, 3 * GW), lambda bi, i: (bi, i, H_BQKV // (3 * GW))),
                  pl.BlockSpec((nb, rows, GW), lambda bi, i: (bi, i, H_BZ // GW)),
                  pl.BlockSpec((nb, rows, H_SMALL), lambda bi, i: (bi, i, 0)),
                  pl.BlockSpec((nb, GDN_CONV_WIDTH - 1, 3 * GW), lambda bi, i: (bi, 0, 0)),
                  pl.BlockSpec((nb, HEAD_DIM, GW), lambda bi, i: (bi, 0, 0)),
                  full((GDN_CONV_WIDTH, 3 * GW)), full((1, H_SMALL)), full((1, H_SMALL)), full((1, GW)),
                  full((GW, GW)), full((H_SMALL, GW)), full((H_SMALL, GW))],
        out_specs=[pl.BlockSpec((nb, rows, GW), lambda bi, i: (bi, i, 0)),
                   pl.BlockSpec((nb, HEAD_DIM, GW), lambda bi, i: (bi, 0, 0)),
                   pl.BlockSpec((nb, GDN_CONV_WIDTH - 1, 3 * GW), lambda bi, i: (bi, 0, 0))],
        out_shape=[jax.ShapeDtypeStruct((bsz, l, GW), BF16),
                   jax.ShapeDtypeStruct((bsz, HEAD_DIM, GW), F32),
                   jax.ShapeDtypeStruct((bsz, GDN_CONV_WIDTH - 1, 3 * GW), F32)],
        scratch_shapes=[pltpu.VMEM((nb, rows + 8, 3 * GW), F32), pltpu.VMEM((nb, HEAD_DIM, GW), F32)],
        compiler_params=_params(2),
        name="gdn_mixer",
    )(h3, h3, hs3, buf, _heads_to_wide(s0), cw, pad_r(a_log, S_A), pad_r(dt_bias, S_A),
      jnp.tile(norm_w, GROUP_HEADS)[None, :], _ones_block_diag(), _lane_spread(S_A), _lane_spread(S_B))
    return o, _wide_to_heads(s_new), new_buf


def _rwkv_kernel(rkv_ref, hs_ref, sh_ref, shs_ref, s0_ref, mu_ref, mus_ref, w0_ref, wup_ref, a0_ref, aup_ref,
                 gup_ref, kk_ref, ka_ref, rk_ref, lng_ref, lnb_ref, ones_ref, o_ref, sout_ref, nsh_ref, nshs_ref,
                 win_ref, wins_ref, s_scr, *, nb, nc):
    c = pl.program_id(1)
    n = CHUNK

    rows = nc * n
    items = [(b, ci) for b in range(nb) for ci in range(nc)]

    @pl.when(c == 0)
    def _():
        for b in range(nb):
            win_ref[b, 0:8, :] = jnp.zeros((8, 3 * GW), F32)
            wins_ref[b, 0:8, :] = jnp.zeros((8, H_SMALL), F32)
            win_ref[b, 7:8, :] = sh_ref[b]
            wins_ref[b, 7:8, :] = shs_ref[b]
        s_scr[...] = s0_ref[...]

    wup_r, aup_r, gup_r = (_rhs3(r[...], precise=True) for r in (wup_ref, aup_ref, gup_ref))
    per_b = []
    for b in range(nb):
        x = rkv_ref[b]
        xs = hs_ref[b]
        win_ref[b, 8:8 + rows, :] = x
        wins_ref[b, 8:8 + rows, :] = xs
        xm = x + (win_ref[b, 7:7 + rows, :] - x) * mu_ref[...]
        xms = xs + (wins_ref[b, 7:7 + rows, :] - xs) * mus_ref[...]
        last = win_ref[b, rows + 7:rows + 8, :]
        win_ref[b, 7:8, :] = last
        lasts = wins_ref[b, rows + 7:rows + 8, :]
        wins_ref[b, 7:8, :] = lasts
        w_pre = w0_ref[...] + _mm3(_lhs3(jnp.tanh(xms), precise=True), wup_r)
        logw = -jnp.exp(-_softplus(-w_pre) - 0.5)
        a_sig = _sigmoid(a0_ref[...] + _mm3(_lhs3(xms, precise=True), aup_r))
        gate = _mm3(_lhs3(_sigmoid(xms), precise=True), gup_r)
        kx = xm[:, GW:2 * GW]
        per_b.append(dict(rr=xm[:, 0:GW], vv=xm[:, 2 * GW:3 * GW], logw=logw, a_sig=a_sig, gate=gate,
                          kkp=kx * kk_ref[...], k2=kx * (1.0 + (a_sig - 1.0) * ka_ref[...])))

    w = _Wide()
    ones_bd = ones_ref[...]
    tril = _tril_ones(n, BF16)
    sl = lambda ci: slice(ci * n, (ci + 1) * n)
    get = lambda name: [per_b[b][name][sl(ci), :] for b, ci in items]
    rrs, vvs, logws, a_sigs, gates, kkps, k2s = (get(k) for k in ('rr', 'vv', 'logw', 'a_sig', 'gate', 'kkp', 'k2'))
    kks = [kkp * lax.rsqrt(_mm_x01(kkp * kkp, ones_bd) + 1e-6) for kkp in kkps]
    cums = [_tri_mm(tril, logw) for logw in logws]
    w_incls = [jnp.exp(cum) for cum in cums]
    w_lasts = [wi[n - 1:n, :] for wi in w_incls]
    w_invs = [jnp.exp(-cum) for cum in cums]
    ats = [-kk * jnp.exp(cum - logw) for kk, cum, logw in zip(kks, cums, logws)]
    bts = [kk * a_sig * wv for kk, a_sig, wv in zip(kks, a_sigs, w_invs)]
    kts = [k2 * wv for k2, wv in zip(k2s, w_invs)]
    rts = [rr * wi for rr, wi in zip(rrs, w_incls)]
    ar_ls = [_lhs3(jnp.concatenate([at, rt], axis=0)) for at, rt in zip(ats, rts)]
    gbs = [_mm3(ar_l, _rhs3(bt, w), NT) for ar_l, bt in zip(ar_ls, bts)]
    gks = [_mm3(ar_l, _rhs3(kt, w), NT) for ar_l, kt in zip(ar_ls, kts)]
    t_ls = [_lhs3(t) for t in _unit_lower_inverse_wide([jnp.where(w.strict, -gb[:n], 0.0) for gb in gbs], w)]
    arb_ls = [_lhs3(jnp.where(w.incl, gb[n:], 0.0)) for gb in gbs]
    ark_ls = [_lhs3(jnp.where(w.incl, gk[n:], 0.0)) for gk in gks]
    v_rs = [_rhs3(vv, w) for vv in vvs]
    aakvs = [_mm3(_lhs3(jnp.where(w.strict, gk[:n], 0.0)), v_r) for gk, v_r in zip(gks, v_rs)]
    arkvs = [_mm3(ark_l, v_r) for ark_l, v_r in zip(ark_ls, v_rs)]
    bks = [jnp.concatenate([bt * wl, kt * wl], axis=0) for bt, kt, wl in zip(bts, kts, w_lasts)]

    states = [s_scr[b] for b in range(nb)]
    ys = {}
    for ci in range(nc):
        for b in range(nb):
            it = b * nc + ci
            s0 = states[b]
            ars = _mm3(ar_ls[it], _rhs3(s0, w), NT)
            u = _mm3(t_ls[it], _rhs3(ars[:n] + aakvs[it], w))
            ys[it] = ars[n:] + _mm3(arb_ls[it], _rhs3(u, w)) + arkvs[it]
            upd = _mm3_tn(jnp.concatenate([u, vvs[it]], axis=0), bks[it])
            states[b] = s0 * w_lasts[it] + _collapse_heads(upd, w)
    for b in range(nb):
        s_scr[b] = states[b]

    inv_d = 1.0 / HEAD_DIM
    for it, (b, ci) in enumerate(items):
        y = ys[it]
        yc = y - _mm_x01(y, ones_bd) * inv_d
        var_y = _mm_x01(yc * yc, ones_bd) * inv_d
        yn = yc * lax.rsqrt(var_y + GN_EPS) * lng_ref[...] + lnb_ref[...]
        bonus = _mm_x01(rrs[it] * k2s[it] * rk_ref[...], ones_bd) * vvs[it]
        o_ref[b, sl(ci), :] = ((yn + bonus) * gates[it]).astype(o_ref.dtype)

    @pl.when(c == pl.num_programs(1) - 1)
    def _():
        sout_ref[...] = s_scr[...]
        for b in range(nb):
            nsh_ref[b] = win_ref[b, 7:8, :]
            nshs_ref[b] = wins_ref[b, 7:8, :]


def _rwkv_mixer(h3, hs3, shift, s0, lp):
    bsz, l, _ = h3.shape
    nb, nc = _recurrent_tiling(bsz, l)
    rows = nc * CHUNK
    sh_main = shift[:, None, 0:3 * GW]
    sh_small = jnp.pad(shift[:, None, 3 * GW:], ((0, 0), (0, 0), (0, H_SMALL - LORA)))
    mu = lp['rwkv_mu']
    mu_main = mu[None, 0:3 * GW]
    mu_small = jnp.pad(mu[None, 3 * GW:], ((0, 0), (0, H_SMALL - LORA)))
    place = lambda w, off: jnp.zeros((H_SMALL, GW), F32).at[off:off + w.shape[0]].set(w)
    wup = place(lp['rwkv_w_up'], 0)
    aup = place(lp['rwkv_a_up'], RWKV_W_LORA)
    gup = place(lp['rwkv_g_up'], RWKV_W_LORA + RWKV_A_LORA)
    row = lambda v: v.reshape(1, GW)
    full = lambda shape: pl.BlockSpec(shape, lambda bi, i: (0,) * len(shape))
    o, s_new, sh_new, shs_new = pl.pallas_call(
        functools.partial(_rwkv_kernel, nb=nb, nc=nc),
        grid=(bsz // nb, l // rows),
        in_specs=[pl.BlockSpec((nb, rows, 3 * GW), lambda bi, i: (bi, i, H_CRKV // (3 * GW))),
                  pl.BlockSpec((nb, rows, H_SMALL), lambda bi, i: (bi, i, 0)),
                  pl.BlockSpec((nb, 1, 3 * GW), lambda bi, i: (bi, 0, 0)),
                  pl.BlockSpec((nb, 1, H_SMALL), lambda bi, i: (bi, 0, 0)),
                  pl.BlockSpec((nb, HEAD_DIM, GW), lambda bi, i: (bi, 0, 0)),
                  full((1, 3 * GW)), full((1, H_SMALL)), full((1, GW)), full((H_SMALL, GW)), full((1, GW)),
                  full((H_SMALL, GW)), full((H_SMALL, GW)), full((1, GW)), full((1, GW)), full((1, GW)),
                  full((1, GW)), full((1, GW)), full((GW, GW))],
        out_specs=[pl.BlockSpec((nb, rows, GW), lambda bi, i: (bi, i, 0)),
                   pl.BlockSpec((nb, HEAD_DIM, GW), lambda bi, i: (bi, 0, 0)),
                   pl.BlockSpec((nb, 1, 3 * GW), lambda bi, i: (bi, 0, 0)),
                   pl.BlockSpec((nb, 1, H_SMALL), lambda bi, i: (bi, 0, 0))],
        out_shape=[jax.ShapeDtypeStruct((bsz, l, GW), BF16),
                   jax.ShapeDtypeStruct((bsz, HEAD_DIM, GW), F32),
                   jax.ShapeDtypeStruct((bsz, 1, 3 * GW), F32),
                   jax.ShapeDtypeStruct((bsz, 1, H_SMALL), F32)],
        scratch_shapes=[pltpu.VMEM((nb, rows + 8, 3 * GW), F32), pltpu.VMEM((nb, rows + 8, H_SMALL), F32),
                        pltpu.VMEM((nb, HEAD_DIM, GW), F32)],
        compiler_params=_params(2),
        name="rwkv_mixer",
    )(h3, hs3, sh_main, sh_small, _heads_to_wide(s0), mu_main, mu_small, row(lp['rwkv_w0']), wup,
      row(lp['rwkv_a0']), aup, gup, row(lp['rwkv_k_k']), row(lp['rwkv_k_a']), row(lp['rwkv_r_k']),
      row(lp['rwkv_lnx_g']), row(lp['rwkv_lnx_b']), _ones_block_diag())
    new_shift = jnp.concatenate([sh_new[:, 0, :], shs_new[:, 0, S_LORA:S_LORA + LORA]], axis=-1)
    return o, _wide_to_heads(s_new), new_shift


_FB = 128
_SAMPLE_BK = 2048


def _log_forget(hs, bf):
    return -_softplus(-(hs + bf))


def _head_slice(h):
    return slice(h * HEAD_DIM, (h + 1) * HEAD_DIM)


def _attend_heads(qs, kblk, vblk, key_bias, mask, carry):
    m, l, acc = carry
    bq = qs[0].shape[0]
    kb16 = kblk.astype(BF16)
    vb16 = vblk.astype(BF16)
    parts = []
    for h in range(GROUP_HEADS):
        s = _dot(qs[h], kb16[_head_slice(h), :]) + key_bias[h:h + 1, :]
        parts.append(s if mask is None else jnp.where(mask, s, NEG))
    s = jnp.concatenate(parts, axis=0)
    m_new = jnp.maximum(m, jnp.max(s, -1, keepdims=True))
    alpha = jnp.exp(m - m_new)
    p = jnp.exp(s - m_new)
    l = alpha * l + jnp.sum(p, -1, keepdims=True)
    pb = p.astype(BF16)
    pv = jnp.concatenate([_dot(pb[h * bq:(h + 1) * bq, :], vb16[_head_slice(h), :], NT)
                          for h in range(GROUP_HEADS)], axis=0)
    return m_new, l, alpha * acc + pv


def _attend_init(bq):
    rows = GROUP_HEADS * bq
    return jnp.full((rows, 1), NEG, F32), jnp.zeros((rows, 1), F32), jnp.zeros((rows, HEAD_DIM), F32)


def _attend_store(o_ref, carry, bq):
    _, l, acc = carry
    for h in range(GROUP_HEADS):
        o_ref[:, _head_slice(h)] = (acc[h * bq:(h + 1) * bq, :] / l[h * bq:(h + 1) * bq, :]).astype(o_ref.dtype)


def _sel_rows(sel01, x):
    t = x.shape[0]
    r = _dot(sel01, jnp.concatenate(_split3(x), axis=0), NT)
    return r[:, :t] + r[:, t:2 * t] + r[:, 2 * t:]


def _attend_heads_t(qts, k_blk, vt_blk, bias_tiles, mask, carry):
    m, l, acc = carry
    bq = qts[0].shape[1]
    kb16 = k_blk.astype(BF16)
    vb16 = vt_blk.astype(BF16)
    parts = []
    for h in range(GROUP_HEADS):
        s = _dot(kb16[:, _head_slice(h)], qts[h]) + jnp.tile(bias_tiles[h], (1, bq // LANE))
        parts.append(s if mask is None else jnp.where(mask, s, NEG))
    s = jnp.concatenate(parts, axis=1)
    m_new = jnp.maximum(m, jnp.max(s, 0, keepdims=True))
    alpha = jnp.exp(m - m_new)
    p = jnp.exp(s - m_new)
    l = alpha * l + jnp.sum(p, 0, keepdims=True)
    pb = p.astype(BF16)
    pv = jnp.concatenate([_dot(vb16[_head_slice(h), :], pb[:, h * bq:(h + 1) * bq])
                          for h in range(GROUP_HEADS)], axis=1)
    return m_new, l, alpha * acc + pv


def _fox_prompt_kernel(qt_ref, k_ref, vt_ref, hs_ref, bf_ref, o_ref, lf_ref, bias_scr, *, l, bq):
    i = pl.program_id(1)

    @pl.when(i == 0)
    def _():
        ltri = _tril_ones(_FB, BF16)
        carry = jnp.zeros((1, H_SMALL), F32)
        for jb in range(l // _FB):
            rows = slice(jb * _FB, (jb + 1) * _FB)
            lf = _log_forget(hs_ref[rows, :], bf_ref[...])
            lf_ref[rows, :] = lf[:, S_F:S_F + GROUP_HEADS]
            loc = _tri_mm(ltri, lf, 3)
            nf = -(loc + carry)
            for h in range(GROUP_HEADS):
                bias_scr[h, rows, :] = jnp.broadcast_to(nf[:, S_F + h:S_F + h + 1], (_FB, LANE))
            carry = carry + loc[_FB - 1:_FB, :]

    qt = qt_ref[...]
    r_i, c_i = _iota2((bq, bq), 0), _iota2((bq, bq), 1)
    qts = [(qt[_head_slice(h), :] * (HEAD_DIM ** -0.5)).astype(BF16) for h in range(GROUP_HEADS)]

    def step(j, mask, carry):
        keys = pl.ds(pl.multiple_of(j * bq, bq), bq)
        biases = [bias_scr[h, keys, :] for h in range(GROUP_HEADS)]
        return _attend_heads_t(qts, k_ref[keys, :], vt_ref[:, keys], biases, mask, carry)

    init = (jnp.full((1, GROUP_HEADS * bq), NEG, F32), jnp.zeros((1, GROUP_HEADS * bq), F32),
            jnp.zeros((HEAD_DIM, GROUP_HEADS * bq), F32))
    carry = lax.fori_loop(0, i, lambda j, c: step(j, None, c), init)
    _, lsum, acc = step(i, r_i <= c_i, carry)
    o_t = acc / lsum
    o_t = jnp.concatenate([o_t[:, h * bq:(h + 1) * bq] for h in range(GROUP_HEADS)], axis=0)
    o_ref[...] = o_t.T.astype(o_ref.dtype)


def _fox_prompt(h3, hs3, q_t, v_t, bf_pad):
    bsz, l, _ = h3.shape
    bq = min(l, 4 * _FB)
    kern = functools.partial(_fox_prompt_kernel, l=l, bq=bq)
    return pl.pallas_call(
        kern,
        grid=(bsz, l // bq),
        in_specs=[pl.BlockSpec((None, GW, bq), lambda bi, i: (bi, 0, i)),
                  pl.BlockSpec((None, l, GW), lambda bi, i: (bi, 0, H_DK // GW)),
                  pl.BlockSpec((None, GW, l), lambda bi, i: (bi, 0, 0)),
                  pl.BlockSpec((None, l, H_SMALL), lambda bi, i: (bi, 0, 0)),
                  pl.BlockSpec((1, H_SMALL), lambda bi, i: (0, 0))],
        out_specs=[pl.BlockSpec((None, bq, GW), lambda bi, i: (bi, i, 0)),
                   pl.BlockSpec((None, l, GROUP_HEADS), lambda bi, i: (bi, 0, 0))],
        out_shape=[jax.ShapeDtypeStruct((bsz, l, GW), BF16),
                   jax.ShapeDtypeStruct((bsz, l, GROUP_HEADS), F32)],
        scratch_shapes=[pltpu.VMEM((GROUP_HEADS, l, LANE), F32)],
        compiler_params=_params(2),
        name="fox_prompt",
    )(q_t, h3, v_t, hs3, bf_pad)


def _fox_sample_kernel(qt_ref, kt_ref, vt_ref, hs_ref, bf_ref, ck_ref, cv_ref, clf_ref, o_ref, lf_ref, cum_scr,
                       *, l, p):
    nblk = p // _FB
    per = _SAMPLE_BK // _FB

    loc = _mm_x01(clf_ref[...].reshape(nblk * 8, _FB), _triu_ones(_FB, BF16), 3)
    carry = jnp.zeros((8, 1), F32)
    for jb in range(nblk):
        blk = loc[jb * 8:(jb + 1) * 8, :]
        cum_scr[jb // per, :, (jb % per) * _FB:(jb % per + 1) * _FB] = blk + carry
        carry = carry + blk[:, _FB - 1:_FB]
    total = carry

    lf = _log_forget(hs_ref[...], bf_ref[...])
    lf_ref[...] = lf[:, S_F:S_F + GROUP_HEADS]
    cum_r = _mm_x01(_sel_rows(_col_selector(8, H_SMALL, S_F).astype(BF16), lf), _triu_ones(l, BF16), 3)
    q = qt_ref[...].T
    r_i, c_i = _iota2((l, l), 0), _iota2((l, l), 1)
    qs = [(q[:, _head_slice(h)] * (HEAD_DIM ** -0.5)).astype(BF16) for h in range(GROUP_HEADS)]

    def body(j, carry):
        keys = pl.ds(pl.multiple_of(j * _SAMPLE_BK, _SAMPLE_BK), _SAMPLE_BK)
        suffix = total - cum_scr[j]
        return _attend_heads(qs, ck_ref[:, keys], cv_ref[:, keys], suffix, None, carry)

    carry = lax.fori_loop(0, p // _SAMPLE_BK, body, _attend_init(l))
    carry = _attend_heads(qs, kt_ref[...], vt_ref[...], -cum_r, c_i <= r_i, carry)
    _attend_store(o_ref, carry, l)


def _fox_sample(hs3, q_t, k_t, v_t, bf_pad, layer, ck_t, cv_t, clogf):
    bsz, l, _ = hs3.shape
    p = ck_t.shape[-1]
    clf = clogf.reshape(bsz, p // _FB, _FB, GROUP_HEADS).transpose(0, 1, 3, 2)
    clf = jnp.pad(clf, ((0, 0), (0, 0), (0, 8 - GROUP_HEADS), (0, 0)))
    kern = functools.partial(_fox_sample_kernel, l=l, p=p)
    return pl.pallas_call(
        kern,
        grid=(bsz,),
        in_specs=[pl.BlockSpec((None, GW, l), lambda bi: (bi, 0, 0)),
                  pl.BlockSpec((None, GW, l), lambda bi: (bi, 0, 0)),
                  pl.BlockSpec((None, GW, l), lambda bi: (bi, 0, 0)),
                  pl.BlockSpec((None, l, H_SMALL), lambda bi: (bi, 0, 0)),
                  pl.BlockSpec((1, H_SMALL), lambda bi: (0, 0)),
                  pl.BlockSpec((None, None, GW, p), lambda bi: (layer, bi, 0, 0)),
                  pl.BlockSpec((None, None, GW, p), lambda bi: (layer, bi, 0, 0)),
                  pl.BlockSpec((None, p // _FB, 8, _FB), lambda bi: (bi, 0, 0, 0))],
        out_specs=[pl.BlockSpec((None, l, GW), lambda bi: (bi, 0, 0)),
                   pl.BlockSpec((None, l, GROUP_HEADS), lambda bi: (bi, 0, 0))],
        out_shape=[jax.ShapeDtypeStruct((bsz, l, GW), BF16),
                   jax.ShapeDtypeStruct((bsz, l, GROUP_HEADS), F32)],
        scratch_shapes=[pltpu.VMEM((p // _SAMPLE_BK, 8, _SAMPLE_BK), F32)],
        compiler_params=_params(1),
        name="fox_sample",
    )(q_t, k_t, v_t, hs3, bf_pad, ck_t, cv_t, clf)


_R_G, _R_E, _R_ROWS = 0, 8, 32
_MOE_EXPERTS_PER_STEP = 2


def _ffn_kernel(oa_ref, ob_ref, oc_ref, od_ref, x_ref, wo_ref, g1_ref, b1_ref, wrh_ref, wrl_ref, brc_ref, wg_ref,
                wu_ref, wd_ref, g2_ref, b2_ref, y_ref, comb_scr, xb_scr, x1_scr):
    e = pl.program_id(1)
    tm = x_ref.shape[0]
    lane = _iota2((tm, LANE), 1)

    @pl.when(e == 0)
    def _():
        mix = jnp.concatenate([ref[...] for ref in (oa_ref, ob_ref, oc_ref, od_ref)], axis=1)
        x1 = _layer_norm(DN_ALPHA * x_ref[...] + _dot(mix, wo_ref[...]), g1_ref[...], b1_ref[...])
        x1_scr[...] = x1
        y_ref[...] = jnp.zeros_like(y_ref)
        x_hi, x_lo = _split2(x1)
        xb_scr[...] = x_hi
        wr_hi = wrh_ref[...]
        r = _dot(jnp.concatenate([wr_hi, wrl_ref[...]], axis=0), x_hi, NT)
        lt = r[:_R_ROWS] + r[_R_ROWS:] + _dot(wr_hi, x_lo, NT) + brc_ref[...]
        gl = lt[_R_G:_R_G + N_EXPERT_GROUPS]
        grow = _iota2(gl.shape, 0)
        ge = jnp.exp(gl - jnp.max(gl, 0, keepdims=True))
        pg = ge / jnp.sum(ge, 0, keepdims=True)
        gp = jnp.max(pg, 0, keepdims=True)
        gi = jnp.min(jnp.where(pg == gp, grow, N_EXPERT_GROUPS), 0, keepdims=True)
        le = lt[_R_E:_R_E + N_EXPERTS]
        erow = _iota2(le.shape, 0)
        emask = (erow >> 2) == gi
        el = jnp.where(emask, le, NEG)
        ee = jnp.exp(el - jnp.max(el, 0, keepdims=True))
        ep = ee / jnp.sum(ee, 0, keepdims=True)
        m1 = jnp.max(jnp.where(emask, ep, -1.0), 0, keepdims=True)
        i1 = jnp.min(jnp.where(jnp.logical_and(emask, ep == m1), erow, N_EXPERTS), 0, keepdims=True)
        rest = jnp.logical_and(emask, erow != i1)
        m2 = jnp.max(jnp.where(rest, ep, -1.0), 0, keepdims=True)
        i2 = jnp.min(jnp.where(jnp.logical_and(rest, ep == m2), erow, N_EXPERTS), 0, keepdims=True)
        den = m1 + m2
        comb_t = jnp.where(erow == i1, gp * m1 / den, jnp.where(erow == i2, gp * m2 / den, 0.0))
        pieces = jnp.concatenate(_split3(comb_t), axis=0)
        sel = ((_iota2((3 * N_EXPERTS, LANE), 0) & (N_EXPERTS - 1)) == _iota2((3 * N_EXPERTS, LANE), 1)).astype(BF16)
        comb_scr[...] = _dot(pieces, sel, TN)

    xb = xb_scr[...]
    comb = comb_scr[...]
    hidden = []
    for k in range(_MOE_EXPERTS_PER_STEP):
        ce = jnp.sum(jnp.where(lane == e * _MOE_EXPERTS_PER_STEP + k, comb, 0.0), -1, keepdims=True)
        hh = _silu(_mm(xb, wg_ref[k])) * _mm(xb, wu_ref[k])
        hidden.append((hh * ce).astype(BF16))
    hidden = jnp.concatenate(hidden, axis=1)
    w_down = wd_ref[...].reshape(_MOE_EXPERTS_PER_STEP * D_EXPERT, D_MODEL)
    y_ref[...] += _mm(hidden, w_down)

    @pl.when(e == pl.num_programs(1) - 1)
    def _():
        y_ref[...] = _layer_norm(DN_ALPHA * x1_scr[...] + y_ref[...], g2_ref[...], b2_ref[...])


def _ffn(mixers, x2d, w_out_bf16, g1, b1, wr_hi, wr_lo, br_col, layer, wg, wu, wd, g2, b2):
    t = x2d.shape[0]
    tm = min(t, 1024)
    ne = _MOE_EXPERTS_PER_STEP
    mix = pl.BlockSpec((tm, GW), lambda i, e: (i, 0))
    vec = pl.BlockSpec((1, D_MODEL), lambda i, e: (0, 0))
    return pl.pallas_call(
        _ffn_kernel,
        grid=(t // tm, N_EXPERTS // ne),
        in_specs=[mix, mix, mix, mix,
                  pl.BlockSpec((tm, D_MODEL), lambda i, e: (i, 0)),
                  pl.BlockSpec((D_MODEL, D_MODEL), lambda i, e: (0, 0)), vec, vec,
                  pl.BlockSpec((_R_ROWS, D_MODEL), lambda i, e: (0, 0)),
                  pl.BlockSpec((_R_ROWS, D_MODEL), lambda i, e: (0, 0)),
                  pl.BlockSpec((_R_ROWS, 1), lambda i, e: (0, 0)),
                  pl.BlockSpec((None, ne, D_MODEL, D_EXPERT), lambda i, e: (layer, e, 0, 0)),
                  pl.BlockSpec((None, ne, D_MODEL, D_EXPERT), lambda i, e: (layer, e, 0, 0)),
                  pl.BlockSpec((None, ne, D_EXPERT, D_MODEL), lambda i, e: (layer, e, 0, 0)), vec, vec],
        out_specs=pl.BlockSpec((tm, D_MODEL), lambda i, e: (i, 0)),
        out_shape=jax.ShapeDtypeStruct((t, D_MODEL), F32),
        scratch_shapes=[pltpu.VMEM((tm, LANE), F32), pltpu.VMEM((tm, D_MODEL), BF16),
                        pltpu.VMEM((tm, D_MODEL), F32)],
        compiler_params=_params(2),
        name="ffn",
    )(*mixers, x2d, w_out_bf16, g1, b1, wr_hi, wr_lo, br_col, wg, wu, wd, g2, b2)


def _prep_layer(lp):
    w = lp['w_in_t']
    c0 = CONV_COLS
    c1 = c0 + GDN_COLS
    c2 = c1 + RWKV_COLS
    main = jnp.concatenate([w[c0:c0 + 3 * GW], w[c1:c1 + 3 * GW], w[c2 + GW:c2 + 2 * GW],
                            w[0:GW], w[GW:2 * GW], w[c0 + 3 * GW:c0 + 4 * GW]], axis=0).astype(BF16)
    kv_t = w[c2:c2 + 3 * GW].astype(BF16)
    small = jnp.concatenate([w[c1 + 3 * GW:c2], w[c0 + 4 * GW:c1], w[c2 + 3 * GW:],
                             jnp.zeros((H_SMALL - LORA - 3 * GROUP_HEADS, D_MODEL), F32)], axis=0)
    gap = jnp.zeros((_R_E - N_EXPERT_GROUPS, D_MODEL), F32)
    tail = jnp.zeros((_R_ROWS - _R_E - N_EXPERTS, D_MODEL), F32)
    wr_t = jnp.concatenate([lp['router_g_w'].T, gap, lp['router_e_w'].T, tail], axis=0)
    wr_hi = wr_t.astype(BF16)
    wr_lo = (wr_t - wr_hi.astype(F32)).astype(BF16)
    br = jnp.concatenate([lp['router_g_b'], gap[:, 0], lp['router_e_b'], tail[:, 0]])[:, None]
    bf_pad = jnp.zeros((1, H_SMALL), F32).at[0, S_F:S_F + GROUP_HEADS].set(lp['fox_b_f'])
    return dict(w_main=main, w_small=small, w_kv_t=kv_t, wr_hi=wr_hi, wr_lo=wr_lo, br=br, bf_pad=bf_pad)


def _trunk_layer(x, lp, pp, layer, big, conv_buf, gdn_buf, gdn_s, rw_shift, rw_s, fox_cache):
    bsz, l, d = x.shape
    x2d = x.reshape(bsz * l, d)
    h3, hs3, q_t, k_t, v_t = _inproj(x, pp['w_main'], pp['w_small'], pp['w_kv_t'])

    o_a, new_conv = _conv_mixer(h3, conv_buf, lp['conv_w'], lp['conv_b'][None], lp['conv_ln_g'][None],
                                lp['conv_ln_b'][None])
    o_b, new_gdn_s, new_gdn_buf = _gdn_mixer(h3, hs3, gdn_buf, gdn_s, lp['gdn_conv_w'], lp['gdn_a_log'],
                                             lp['gdn_dt_bias'], lp['gdn_norm_w'])
    o_c, new_rw_s, new_shift = _rwkv_mixer(h3, hs3, rw_shift, rw_s, lp)
    if fox_cache is None:
        o_d, logf = _fox_prompt(h3, hs3, q_t, v_t, pp['bf_pad'])
    else:
        o_d, logf = _fox_sample(hs3, q_t, k_t, v_t, pp['bf_pad'], layer, *fox_cache)

    mixers = [o.reshape(bsz * l, GW) for o in (o_a, o_b, o_c, o_d)]
    x2 = _ffn(mixers, x2d, big['w_out'][layer], lp['ln1_g'][None], lp['ln1_b'][None], pp['wr_hi'], pp['wr_lo'],
              pp['br'], layer, *big['experts'], lp['ln2_g'][None], lp['ln2_b'][None])
    return x2.reshape(bsz, l, d), (new_conv, new_gdn_buf, new_gdn_s, new_shift, new_rw_s, k_t, v_t, logf)


def _untranspose(xs_t):
    bsz, _, l = xs_t[0].shape
    return jnp.stack(xs_t).reshape(len(xs_t), bsz, GROUP_HEADS, HEAD_DIM, l).transpose(0, 1, 4, 2, 3)


_LAYER_KEYS = ('w_in_t', 'conv_w', 'conv_b', 'conv_ln_g', 'conv_ln_b', 'gdn_conv_w', 'gdn_a_log', 'gdn_dt_bias',
               'gdn_norm_w', 'rwkv_mu', 'rwkv_w0', 'rwkv_w_up', 'rwkv_a0', 'rwkv_a_up', 'rwkv_g_up', 'rwkv_k_k',
               'rwkv_k_a', 'rwkv_r_k', 'rwkv_lnx_g', 'rwkv_lnx_b', 'fox_b_f', 'ln1_g', 'ln1_b',
               'router_g_w', 'router_g_b', 'router_e_w', 'router_e_b', 'ln2_g', 'ln2_b')


def kernel(x_prompt, x_sample, cache_fox_k, cache_fox_v, cache_fox_logf, state_conv, state_gdn_conv, state_gdn, state_rwkv_shift, state_rwkv, w_in, conv_w, conv_b, conv_ln_g, conv_ln_b, gdn_conv_w, gdn_a_log, gdn_dt_bias, gdn_norm_w, rwkv_mu, rwkv_w0, rwkv_w_up, rwkv_a0, rwkv_a_up, rwkv_g_up, rwkv_k_k, rwkv_k_a, rwkv_r_k, rwkv_lnx_g, rwkv_lnx_b, fox_b_f, w_out, ln1_g, ln1_b, router_g_w, router_g_b, router_e_w, router_e_b, exp_w_gate, exp_w_up, exp_w_down, ln2_g, ln2_b):
    w_in_t = w_in.transpose(0, 2, 1)
    weights = dict(zip(_LAYER_KEYS, (w_in_t, conv_w, conv_b, conv_ln_g, conv_ln_b, gdn_conv_w, gdn_a_log,
                                     gdn_dt_bias, gdn_norm_w, rwkv_mu, rwkv_w0, rwkv_w_up, rwkv_a0, rwkv_a_up,
                                     rwkv_g_up, rwkv_k_k, rwkv_k_a, rwkv_r_k, rwkv_lnx_g, rwkv_lnx_b, fox_b_f,
                                     ln1_g, ln1_b, router_g_w, router_g_b, router_e_w, router_e_b,
                                     ln2_g, ln2_b)))
    big = dict(w_out=w_out.astype(BF16), experts=(exp_w_gate, exp_w_up, exp_w_down))
    depth, bs, past = cache_fox_k.shape[:3]
    ck_t = cache_fox_k.transpose(0, 1, 3, 4, 2).reshape(depth, bs, GW, past)
    cv_t = cache_fox_v.transpose(0, 1, 3, 4, 2).reshape(depth, bs, GW, past)
    xp, xs = x_prompt, x_sample
    bp = x_prompt.shape[0]
    outs_p = [[] for _ in range(8)]
    outs_s = [[] for _ in range(8)]
    for l in range(DEPTH):
        lp = {k: v[l] for k, v in weights.items()}
        pp = _prep_layer(lp)
        xp, st_p = _trunk_layer(xp, lp, pp, l, big,
                                jnp.zeros((bp, CONV_WIDTH - 1, GW), F32),
                                jnp.zeros((bp, GDN_CONV_WIDTH - 1, 3 * GW), F32),
                                jnp.zeros((bp, GROUP_HEADS, HEAD_DIM, HEAD_DIM), F32),
                                jnp.zeros((bp, RWKV_COLS), F32),
                                jnp.zeros((bp, GROUP_HEADS, HEAD_DIM, HEAD_DIM), F32),
                                None)
        xs, st_s = _trunk_layer(xs, lp, pp, l, big, state_conv[l], state_gdn_conv[l], state_gdn[l],
                                state_rwkv_shift[l], state_rwkv[l], (ck_t, cv_t, cache_fox_logf[l]))
        for i in range(8):
            outs_p[i].append(st_p[i])
            outs_s[i].append(st_s[i])

    def assemble(outs):
        conv, gdn_buf, gdn_s, shift, rw_s = (jnp.stack(o) for o in outs[:5])
        return conv, gdn_buf, gdn_s, shift, rw_s, _untranspose(outs[5]), _untranspose(outs[6]), jnp.stack(outs[7])

    return (xp, xs, *assemble(outs_p), *assemble(outs_s))
```

```python
import functools

import jax
import jax.numpy as jnp
from jax import lax
from jax.experimental import pallas as pl
from jax.experimental.pallas import tpu as pltpu

F32 = jnp.float32
BF16 = jnp.bfloat16

D_MODEL = 1024
DEPTH = 2
CHUNK = 64
HEAD_DIM = 64
GW = D_MODEL // 4
GROUP_HEADS = GW // HEAD_DIM
CONV_WIDTH = 31
GDN_CONV_WIDTH = 4
RWKV_W_LORA = 16
RWKV_A_LORA = 16
RWKV_G_LORA = 32
N_EXPERT_GROUPS = 4
EXPERTS_PER_GROUP = 4
N_EXPERTS = N_EXPERT_GROUPS * EXPERTS_PER_GROUP
D_EXPERT = 256
DN_ALPHA = (2 * DEPTH) ** 0.25
LN_EPS = 1e-5
GN_EPS = 64e-5

CONV_COLS = 2 * GW
GDN_COLS = 4 * GW + 2 * GROUP_HEADS
RWKV_COLS = 3 * GW + RWKV_W_LORA + RWKV_A_LORA + RWKV_G_LORA
FOX_COLS = 3 * GW + GROUP_HEADS
LORA = RWKV_W_LORA + RWKV_A_LORA + RWKV_G_LORA

H_BQKV, H_CRKV, H_DK, H_AVAL, H_AGATE, H_BZ = 0, 768, 1536, 1792, 2048, 2304
H_MAIN = 2560
QKV_ROWS = 3 * GW
S_LORA, S_A, S_B, S_F = 0, 64, 68, 72
H_SMALL = 128
LANE = 128

VMEM_LIMIT = 56 * 1024 * 1024

NN = (((1,), (0,)), ((), ()))
NT = (((1,), (1,)), ((), ()))
TN = (((0,), (0,)), ((), ()))
NEG = -1e30


def _dot(a, b, dims=NN):
    return lax.dot_general(a, b, dims, preferred_element_type=F32)


def _mm(a, b, dims=NN):
    return _dot(a.astype(BF16), b.astype(BF16), dims)


def _sigmoid(x):
    return 0.5 * (jnp.tanh(0.5 * x) + 1.0)


def _silu(x):
    return x * _sigmoid(x)


def _softplus(x):
    return jnp.maximum(x, 0.0) + jnp.log(1.0 + jnp.exp(-jnp.abs(x)))


def _iota2(shape, dim):
    return lax.broadcasted_iota(jnp.int32, shape, dim)


def _tril_ones(n, dtype=F32):
    r, c = _iota2((n, n), 0), _iota2((n, n), 1)
    return (r >= c).astype(dtype)


def _triu_ones(n, dtype=F32):
    r, c = _iota2((n, n), 0), _iota2((n, n), 1)
    return (r <= c).astype(dtype)


def _col_selector(rows, width, col0):
    r, c = _iota2((rows, width), 0), _iota2((rows, width), 1)
    return (c == r + col0).astype(F32)


def _layer_norm(x, g, b):
    mu = jnp.mean(x, axis=-1, keepdims=True)
    xc = x - mu
    var = jnp.mean(xc * xc, axis=-1, keepdims=True)
    return xc * lax.rsqrt(var + LN_EPS) * g + b


def _split2(x):
    hi = x.astype(BF16)
    lo = (x - hi.astype(F32)).astype(BF16)
    return hi, lo


def _split3(x):
    hi = x.astype(BF16)
    r = x - hi.astype(F32)
    mid = r.astype(BF16)
    lo = (r - mid.astype(F32)).astype(BF16)
    return hi, mid, lo


class _Wide:
    def __init__(self):
        self.i = _iota2((CHUNK, GW), 0)
        lane = _iota2((CHUNK, GW), 1)
        self.j = lane & (HEAD_DIM - 1)
        self.head = lane >> 6
        self.incl = self.i >= self.j
        self.strict = self.i > self.j
        self.eye = self.i == self.j
        r, c = _iota2((GW, GW), 0), _iota2((GW, GW), 1)
        self.block_diag = (r >> 6) == (c >> 6)


def _head_block_diag(x, w):
    t = jnp.concatenate([x] * GROUP_HEADS, axis=0)
    return jnp.where(w.block_diag, t, jnp.zeros_like(t))


def _collapse_heads(full, w):
    out = jnp.zeros((HEAD_DIM, GW), F32)
    for h in range(GROUP_HEADS):
        out = out + jnp.where(w.head == h, full[h * HEAD_DIM:(h + 1) * HEAD_DIM, :], 0.0)
    return out


def _lhs3(a, precise=False):
    if not precise:
        return None, a.astype(BF16)
    hi, lo = _split2(a)
    return jnp.concatenate([hi, lo], axis=0), hi


def _rhs3(b, w=None, precise=False):
    if not precise:
        hi = b.astype(BF16)
        return (hi if w is None else _head_block_diag(hi, w)), None
    hi, lo = _split2(b)
    if w is None:
        return hi, lo
    return _head_block_diag(hi, w), _head_block_diag(lo, w)


def _mm3(lhs, rhs, dims=NN):
    cat, hi = lhs
    bh, bl = rhs
    if cat is None or bl is None:
        return _dot(hi, bh, dims)
    m = hi.shape[0]
    r = _dot(cat, bh, dims)
    return r[:m] + r[m:] + _dot(hi, bl, dims)


def _mm3_tn(a, b, precise=False):
    if not precise:
        return _dot(a.astype(BF16), b.astype(BF16), TN)
    ah, al = _split2(a)
    bh, bl = _split2(b)
    m = a.shape[1]
    r = _dot(jnp.concatenate([ah, al], axis=1), bh, TN)
    return r[:m] + r[m:] + _dot(ah, bl, TN)


def _mm_x01(a, m01, pieces=2):
    m = a.shape[0]
    r = _dot(jnp.concatenate(_split3(a) if pieces == 3 else _split2(a), axis=0), m01)
    return sum(r[i * m:(i + 1) * m] for i in range(pieces))


def _tri_mm(tri01, x, pieces=2):
    n = x.shape[1]
    r = _dot(tri01, jnp.concatenate(_split3(x) if pieces == 3 else _split2(x), axis=1))
    return sum(r[:, i * n:(i + 1) * n] for i in range(pieces))


def _unit_lower_inverse_wide(mats, w):
    eye = w.eye.astype(F32)
    same8 = (w.i >> 3) == (w.j >> 3)
    ds = [jnp.where(same8, a, 0.0) for a in mats]
    d2 = [_mm3(_lhs3(d), _rhs3(d, w)) for d in ds]
    d2_r = [_rhs3(x, w) for x in d2]
    d4 = [_mm3(_lhs3(x), r) for x, r in zip(d2, d2_r)]
    imd = [eye - d for d in ds]
    p1 = [i + _mm3(_lhs3(i), r) for i, r in zip(imd, d2_r)]
    ts = [p + _mm3(_lhs3(p), _rhs3(x, w)) for p, x in zip(p1, d4)]
    for s in range(3, 6):
        big = (w.i >> (s + 1)) == (w.j >> (s + 1))
        small = (w.i >> s) == (w.j >> s)
        sel = jnp.logical_and(big, jnp.logical_not(small))
        inner = [_mm3(_lhs3(jnp.where(sel, a, 0.0)), _rhs3(t, w)) for a, t in zip(mats, ts)]
        ts = [t - _mm3(_lhs3(t), _rhs3(x, w)) for t, x in zip(ts, inner)]
    return ts


def _params(n_axes):
    return pltpu.CompilerParams(dimension_semantics=("arbitrary",) * n_axes,
                                vmem_limit_bytes=VMEM_LIMIT)


def _heads_to_wide(s):
    b = s.shape[0]
    return s.transpose(0, 2, 1, 3).reshape(b, HEAD_DIM, GW)


def _wide_to_heads(s):
    b = s.shape[0]
    return s.reshape(b, HEAD_DIM, GROUP_HEADS, HEAD_DIM).transpose(0, 2, 1, 3)


def _ones_block_diag():
    r, c = jnp.arange(GW)[:, None], jnp.arange(GW)[None, :]
    return ((r // HEAD_DIM) == (c // HEAD_DIM)).astype(BF16)


def _lane_spread(col0):
    r, c = jnp.arange(H_SMALL)[:, None], jnp.arange(GW)[None, :]
    return ((r - col0) == (c // HEAD_DIM)).astype(BF16)


_INPROJ_ROWS = 512


def _inproj_kernel(x_ref, wm_ref, wsh_ref, wsl_ref, wkv_ref, h_ref, hs_ref, qt_ref, kt_ref, vt_ref):
    nb, tm, _ = x_ref.shape
    rows = nb * tm
    x_hi, x_lo = _split2(x_ref[...].reshape(rows, D_MODEL))
    h_ref[...] = _dot(x_hi, wm_ref[...], NT).reshape(nb, tm, H_MAIN)
    r = _dot(jnp.concatenate([x_hi, x_lo], axis=0), wsh_ref[...], NT)
    hs_ref[...] = (r[:rows] + r[rows:] + _dot(x_hi, wsl_ref[...], NT)).reshape(nb, tm, H_SMALL)
    qkv = _dot(wkv_ref[...], x_hi, NT)
    for b in range(nb):
        cols = slice(b * tm, (b + 1) * tm)
        qt_ref[b] = qkv[:GW, cols]
        kt_ref[b] = qkv[GW:2 * GW, cols]
        vt_ref[b] = qkv[2 * GW:, cols]


def _inproj(x3, w_main_t, w_small_t, w_kv_t):
    bsz, l, _ = x3.shape
    tm = min(l, _INPROJ_ROWS)
    nb = min(bsz, _INPROJ_ROWS // tm)
    const = lambda shape: pl.BlockSpec(shape, lambda bi, i: (0, 0))
    ws_hi = w_small_t.astype(BF16)
    ws_lo = (w_small_t - ws_hi.astype(F32)).astype(BF16)
    kv_spec = pl.BlockSpec((nb, GW, tm), lambda bi, i: (bi, 0, i))
    return pl.pallas_call(
        _inproj_kernel,
        grid=(bsz // nb, l // tm),
        in_specs=[pl.BlockSpec((nb, tm, D_MODEL), lambda bi, i: (bi, i, 0)),
                  const((H_MAIN, D_MODEL)), const((H_SMALL, D_MODEL)), const((H_SMALL, D_MODEL)),
                  const((QKV_ROWS, D_MODEL))],
        out_specs=[pl.BlockSpec((nb, tm, H_MAIN), lambda bi, i: (bi, i, 0)),
                   pl.BlockSpec((nb, tm, H_SMALL), lambda bi, i: (bi, i, 0)), kv_spec, kv_spec, kv_spec],
        out_shape=[jax.ShapeDtypeStruct((bsz, l, H_MAIN), F32), jax.ShapeDtypeStruct((bsz, l, H_SMALL), F32)]
        + [jax.ShapeDtypeStruct((bsz, GW, l), F32)] * 3,
        compiler_params=_params(2),
        name="inproj",
    )(x3, w_main_t, ws_hi, ws_lo, w_kv_t)


_HALO = 32
_SUBLANES = 8


def _conv_kernel(val_ref, gate_ref, buf_ref, cw_ref, cb_ref, g_ref, b_ref, o_ref, nbuf_ref, win_ref, sh_ref, *, tl):
    i = pl.program_id(1)
    pad = _HALO - (CONV_WIDTH - 1)
    span = tl + _HALO - _SUBLANES

    @pl.when(i == 0)
    def _():
        win_ref[0:8, :] = jnp.zeros((8, GW), F32)
        win_ref[pad:_HALO, :] = buf_ref[...]

    u = val_ref[...] * _sigmoid(gate_ref[...])
    win_ref[_HALO:_HALO + tl, :] = u
    for s in range(1, _SUBLANES):
        sh_ref[s, :, :] = win_ref[s:s + span, :]
    sub = min(tl, 64)
    for r0 in range(0, tl, sub):
        acc = jnp.zeros((sub, GW), F32)
        for j in range(CONV_WIDTH):
            s, base = (pad + j) % _SUBLANES, (pad + j) // _SUBLANES * _SUBLANES + r0
            rows = win_ref[base:base + sub, :] if s == 0 else sh_ref[s, base:base + sub, :]
            acc = acc + cw_ref[j:j + 1, :] * rows
        y = _layer_norm(acc + cb_ref[...], g_ref[...], b_ref[...])
        o_ref[r0:r0 + sub, :] = _silu(y).astype(o_ref.dtype)
    tail = win_ref[tl:tl + _HALO, :]
    win_ref[0:_HALO, :] = tail

    @pl.when(i == pl.num_programs(1) - 1)
    def _():
        nbuf_ref[...] = win_ref[pad:_HALO, :]


def _conv_mixer(h3, buf, cw, cb, g, b):
    bsz, l, _ = h3.shape
    tl = min(l, 512)
    kern = functools.partial(_conv_kernel, tl=tl)
    full = lambda shape: pl.BlockSpec(shape, lambda bi, i: (0,) * len(shape))
    return pl.pallas_call(
        kern,
        grid=(bsz, l // tl),
        in_specs=[pl.BlockSpec((None, tl, GW), lambda bi, i: (bi, i, H_AVAL // GW)),
                  pl.BlockSpec((None, tl, GW), lambda bi, i: (bi, i, H_AGATE // GW)),
                  pl.BlockSpec((None, CONV_WIDTH - 1, GW), lambda bi, i: (bi, 0, 0)),
                  full((CONV_WIDTH, GW)), full((1, GW)), full((1, GW)), full((1, GW))],
        out_specs=[pl.BlockSpec((None, tl, GW), lambda bi, i: (bi, i, 0)),
                   pl.BlockSpec((None, CONV_WIDTH - 1, GW), lambda bi, i: (bi, 0, 0))],
        out_shape=[jax.ShapeDtypeStruct((bsz, l, GW), BF16),
                   jax.ShapeDtypeStruct((bsz, CONV_WIDTH - 1, GW), F32)],
        scratch_shapes=[pltpu.VMEM((tl + _HALO, GW), F32),
                        pltpu.VMEM((_SUBLANES, tl + _HALO - _SUBLANES, GW), F32)],
        compiler_params=_params(2),
        name="conv_mixer",
    )(h3, h3, buf, cw, cb, g, b)


def _gdn_kernel(qkv_ref, z_ref, hs_ref, buf_ref, s0_ref, cw_ref, alog_ref, dtb_ref, nw_ref, ones_ref, sela_ref,
                selb_ref, o_ref, sout_ref, nbuf_ref, win_ref, s_scr, *, nb, nc):
    c = pl.program_id(1)
    n = CHUNK
    rows = nc * n
    kw = GDN_CONV_WIDTH - 1
    items = [(b, ci) for b in range(nb) for ci in range(nc)]

    @pl.when(c == 0)
    def _():
        for b in range(nb):
            win_ref[b, 0:8, :] = jnp.zeros((8, 3 * GW), F32)
            win_ref[b, 8 - kw:8, :] = buf_ref[b]
        s_scr[...] = s0_ref[...]

    qkvs = []
    for b in range(nb):
        win_ref[b, 8:8 + rows, :] = qkv_ref[b]
        conv = jnp.zeros((rows, 3 * GW), F32)
        for j in range(GDN_CONV_WIDTH):
            conv = conv + cw_ref[j:j + 1, :] * win_ref[b, 8 - kw + j:8 - kw + j + rows, :]
        tail = win_ref[b, rows:rows + 8, :]
        win_ref[b, 0:8, :] = tail
        qkvs.append(_silu(conv))

    w = _Wide()
    ones_bd = ones_ref[...]
    tril = _tril_ones(n, BF16)
    sl = lambda ci: slice(ci * n, (ci + 1) * n)
    qs = [qkvs[b][sl(ci), 0:GW] for b, ci in items]
    ks = [qkvs[b][sl(ci), GW:2 * GW] for b, ci in items]
    vs = [qkvs[b][sl(ci), 2 * GW:3 * GW] for b, ci in items]
    sss = [_mm_x01(jnp.concatenate([q * q, k * k], axis=0), ones_bd) for q, k in zip(qs, ks)]
    qs = [q * lax.rsqrt(ss[:n] + 1e-6) * (HEAD_DIM ** -0.5) for q, ss in zip(qs, sss)]
    ks = [k * lax.rsqrt(ss[n:] + 1e-6) for k, ss in zip(ks, sss)]

    hss = [hs_ref[b, sl(ci), :] for b, ci in items]
    gs = [_mm_x01(-jnp.exp(alog_ref[...]) * _softplus(hs + dtb_ref[...]), sela_ref[...]) for hs in hss]
    betas = [_mm_x01(_sigmoid(hs), selb_ref[...]) for hs in hss]
    gcs = [_tri_mm(tril, g) for g in gs]
    grs = [jnp.sum(jnp.where(w.eye, gc, 0.0), axis=0, keepdims=True) for gc in gcs]
    decays = [jnp.exp(jnp.where(w.incl, gc - gr, NEG)) for gc, gr in zip(gcs, grs)]

    kbs = [k * beta for k, beta in zip(ks, betas)]
    grams = [_mm3(_lhs3(jnp.concatenate([kb, q], axis=0)), _rhs3(k, w), NT)
             for kb, q, k in zip(kbs, qs, ks)]
    amats = [jnp.where(w.strict, gram[:n] * decay, 0.0) for gram, decay in zip(grams, decays)]
    qks = [jnp.where(w.incl, gram[n:] * decay, 0.0) for gram, decay in zip(grams, decays)]
    t_ls = [_lhs3(t) for t in _unit_lower_inverse_wide(amats, w)]
    egs = [jnp.exp(gc) for gc in gcs]
    xvs = [_mm3(t_l, _rhs3(v * beta, w)) for t_l, v, beta in zip(t_ls, vs, betas)]
    xks = [_mm3(t_l, _rhs3(kb * eg, w)) for t_l, kb, eg in zip(t_ls, kbs, egs)]
    xq_ls = [_lhs3(jnp.concatenate([xk, q * eg], axis=0)) for xk, q, eg in zip(xks, qs, egs)]
    qk_ls = [_lhs3(qk) for qk in qks]
    gls = [gc[n - 1:n, :] for gc in gcs]
    kds = [k * jnp.exp(gl - gc) for k, gl, gc in zip(ks, gls, gcs)]

    states = [s_scr[b] for b in range(nb)]
    outs = {}
    for ci in range(nc):
        for b in range(nb):
            it = b * nc + ci
            s = states[b]
            rs = _mm3(xq_ls[it], _rhs3(s, w))
            u = xvs[it] - rs[:n]
            outs[it] = rs[n:] + _mm3(qk_ls[it], _rhs3(u, w))
            states[b] = s * jnp.exp(gls[it]) + _collapse_heads(_mm3_tn(kds[it], u), w)
    for b in range(nb):
        s_scr[b] = states[b]

    for it, (b, ci) in enumerate(items):
        o = outs[it]
        ms = _mm_x01(o * o, ones_bd) * (1.0 / HEAD_DIM)
        o = o * lax.rsqrt(ms + 1e-6) * nw_ref[...] * _silu(z_ref[b, sl(ci), :])
        o_ref[b, sl(ci), :] = o.astype(o_ref.dtype)

    @pl.when(c == pl.num_programs(1) - 1)
    def _():
        sout_ref[...] = s_scr[...]
        for b in range(nb):
            nbuf_ref[b] = win_ref[b, 8 - kw:8, :]


def _recurrent_tiling(bsz, l):
    nc = min(l // CHUNK, 4)
    nb = 2 if nc > 1 else min(bsz, 4)
    return nb, nc


def _gdn_mixer(h3, hs3, buf, s0, cw, a_log, dt_bias, norm_w):
    bsz, l, _ = h3.shape
    nb, nc = _recurrent_tiling(bsz, l)
    rows = nc * CHUNK
    pad_r = lambda v, off: jnp.zeros((1, H_SMALL), F32).at[0, off:off + GROUP_HEADS].set(v)
    full = lambda shape: pl.BlockSpec(shape, lambda bi, i: (0,) * len(shape))
    o, s_new, new_buf = pl.pallas_call(
        functools.partial(_gdn_kernel, nb=nb, nc=nc),
        grid=(bsz // nb, l // rows),
        in_specs=[pl.BlockSpec((nb, rows, 3 * GW), lambda bi, i: (bi, i, H_BQKV // (3 * GW))),
                  pl.BlockSpec((nb, rows, GW), lambda bi, i: (bi, i, H_BZ // GW)),
                  pl.BlockSpec((nb, rows, H_SMALL), lambda bi, i: (bi, i, 0)),
                  pl.BlockSpec((nb, GDN_CONV_WIDTH - 1, 3 * GW), lambda bi, i: (bi, 0, 0)),
                  pl.BlockSpec((nb, HEAD_DIM, GW), lambda bi, i: (bi, 0, 0)),
                  full((GDN_CONV_WIDTH, 3 * GW)), full((1, H_SMALL)), full((1, H_SMALL)), full((1, GW)),
                  full((GW, GW)), full((H_SMALL, GW)), full((H_SMALL, GW))],
        out_specs=[pl.BlockSpec((nb, rows, GW), lambda bi, i: (bi, i, 0)),
                   pl.BlockSpec((nb, HEAD_DIM, GW), lambda bi, i: (bi, 0, 0)),
                   pl.BlockSpec((nb, GDN_CONV_WIDTH - 1, 3 * GW), lambda bi, i: (bi, 0, 0))],
        out_shape=[jax.ShapeDtypeStruct((bsz, l, GW), BF16),
                   jax.ShapeDtypeStruct((bsz, HEAD_DIM, GW), F32),
                   jax.ShapeDtypeStruct((bsz, GDN_CONV_WIDTH - 1, 3 * GW), F32)],
        scratch_shapes=[pltpu.VMEM((nb, rows + 8, 3 * GW), F32), pltpu.VMEM((nb, HEAD_DIM, GW), F32)],
        compiler_params=_params(2),
        name="gdn_mixer",
    )(h3, h3, hs3, buf, _heads_to_wide(s0), cw, pad_r(a_log, S_A), pad_r(dt_bias, S_A),
      jnp.tile(norm_w, GROUP_HEADS)[None, :], _ones_block_diag(), _lane_spread(S_A), _lane_spread(S_B))
    return o, _wide_to_heads(s_new), new_buf


def _rwkv_kernel(rkv_ref, hs_ref, sh_ref, shs_ref, s0_ref, mu_ref, mus_ref, w0_ref, wup_ref, a0_ref, aup_ref,
                 gup_ref, kk_ref, ka_ref, rk_ref, lng_ref, lnb_ref, ones_ref, o_ref, sout_ref, nsh_ref, nshs_ref,
                 win_ref, wins_ref, s_scr, *, nb, nc):
    c = pl.program_id(1)
    n = CHUNK

    rows = nc * n
    items = [(b, ci) for b in range(nb) for ci in range(nc)]

    @pl.when(c == 0)
    def _():
        for b in range(nb):
            win_ref[b, 0:8, :] = jnp.zeros((8, 3 * GW), F32)
            wins_ref[b, 0:8, :] = jnp.zeros((8, H_SMALL), F32)
            win_ref[b, 7:8, :] = sh_ref[b]
            wins_ref[b, 7:8, :] = shs_ref[b]
        s_scr[...] = s0_ref[...]

    wup_r, aup_r, gup_r = (_rhs3(r[...], precise=True) for r in (wup_ref, aup_ref, gup_ref))
    per_b = []
    for b in range(nb):
        x = rkv_ref[b]
        xs = hs_ref[b]
        win_ref[b, 8:8 + rows, :] = x
        wins_ref[b, 8:8 + rows, :] = xs
        xm = x + (win_ref[b, 7:7 + rows, :] - x) * mu_ref[...]
        xms = xs + (wins_ref[b, 7:7 + rows, :] - xs) * mus_ref[...]
        last = win_ref[b, rows + 7:rows + 8, :]
        win_ref[b, 7:8, :] = last
        lasts = wins_ref[b, rows + 7:rows + 8, :]
        wins_ref[b, 7:8, :] = lasts
        w_pre = w0_ref[...] + _mm3(_lhs3(jnp.tanh(xms), precise=True), wup_r)
        logw = -jnp.exp(-_softplus(-w_pre) - 0.5)
        a_sig = _sigmoid(a0_ref[...] + _mm3(_lhs3(xms, precise=True), aup_r))
        gate = _mm3(_lhs3(_sigmoid(xms), precise=True), gup_r)
        kx = xm[:, GW:2 * GW]
        per_b.append(dict(rr=xm[:, 0:GW], vv=xm[:, 2 * GW:3 * GW], logw=logw, a_sig=a_sig, gate=gate,
                          kkp=kx * kk_ref[...], k2=kx * (1.0 + (a_sig - 1.0) * ka_ref[...])))

    w = _Wide()
    ones_bd = ones_ref[...]
    tril = _tril_ones(n, BF16)
    sl = lambda ci: slice(ci * n, (ci + 1) * n)
    get = lambda name: [per_b[b][name][sl(ci), :] for b, ci in items]
    rrs, vvs, logws, a_sigs, gates, kkps, k2s = (get(k) for k in ('rr', 'vv', 'logw', 'a_sig', 'gate', 'kkp', 'k2'))
    kks = [kkp * lax.rsqrt(_mm_x01(kkp * kkp, ones_bd) + 1e-6) for kkp in kkps]
    cums = [_tri_mm(tril, logw) for logw in logws]
    w_incls = [jnp.exp(cum) for cum in cums]
    w_lasts = [wi[n - 1:n, :] for wi in w_incls]
    w_invs = [jnp.exp(-cum) for cum in cums]
    ats = [-kk * jnp.exp(cum - logw) for kk, cum, logw in zip(kks, cums, logws)]
    bts = [kk * a_sig * wv for kk, a_sig, wv in zip(kks, a_sigs, w_invs)]
    kts = [k2 * wv for k2, wv in zip(k2s, w_invs)]
    rts = [rr * wi for rr, wi in zip(rrs, w_incls)]
    ar_ls = [_lhs3(jnp.concatenate([at, rt], axis=0)) for at, rt in zip(ats, rts)]
    gbs = [_mm3(ar_l, _rhs3(bt, w), NT) for ar_l, bt in zip(ar_ls, bts)]
    gks = [_mm3(ar_l, _rhs3(kt, w), NT) for ar_l, kt in zip(ar_ls, kts)]
    t_ls = [_lhs3(t) for t in _unit_lower_inverse_wide([jnp.where(w.strict, -gb[:n], 0.0) for gb in gbs], w)]
    arb_ls = [_lhs3(jnp.where(w.incl, gb[n:], 0.0)) for gb in gbs]
    ark_ls = [_lhs3(jnp.where(w.incl, gk[n:], 0.0)) for gk in gks]
    v_rs = [_rhs3(vv, w) for vv in vvs]
    aakvs = [_mm3(_lhs3(jnp.where(w.strict, gk[:n], 0.0)), v_r) for gk, v_r in zip(gks, v_rs)]
    arkvs = [_mm3(ark_l, v_r) for ark_l, v_r in zip(ark_ls, v_rs)]
    bks = [jnp.concatenate([bt * wl, kt * wl], axis=0) for bt, kt, wl in zip(bts, kts, w_lasts)]

    states = [s_scr[b] for b in range(nb)]
    ys = {}
    for ci in range(nc):
        for b in range(nb):
            it = b * nc + ci
            s0 = states[b]
            ars = _mm3(ar_ls[it], _rhs3(s0, w), NT)
            u = _mm3(t_ls[it], _rhs3(ars[:n] + aakvs[it], w))
            ys[it] = ars[n:] + _mm3(arb_ls[it], _rhs3(u, w)) + arkvs[it]
            upd = _mm3_tn(jnp.concatenate([u, vvs[it]], axis=0), bks[it])
            states[b] = s0 * w_lasts[it] + _collapse_heads(upd, w)
    for b in range(nb):
        s_scr[b] = states[b]

    inv_d = 1.0 / HEAD_DIM
    for it, (b, ci) in enumerate(items):
        y = ys[it]
        yc = y - _mm_x01(y, ones_bd) * inv_d
        var_y = _mm_x01(yc * yc, ones_bd) * inv_d
        yn = yc * lax.rsqrt(var_y + GN_EPS) * lng_ref[...] + lnb_ref[...]
        bonus = _mm_x01(rrs[it] * k2s[it] * rk_ref[...], ones_bd) * vvs[it]
        o_ref[b, sl(ci), :] = ((yn + bonus) * gates[it]).astype(o_ref.dtype)

    @pl.when(c == pl.num_programs(1) - 1)
    def _():
        sout_ref[...] = s_scr[...]
        for b in range(nb):
            nsh_ref[b] = win_ref[b, 7:8, :]
            nshs_ref[b] = wins_ref[b, 7:8, :]


def _rwkv_mixer(h3, hs3, shift, s0, lp):
    bsz, l, _ = h3.shape
    nb, nc = _recurrent_tiling(bsz, l)
    rows = nc * CHUNK
    sh_main = shift[:, None, 0:3 * GW]
    sh_small = jnp.pad(shift[:, None, 3 * GW:], ((0, 0), (0, 0), (0, H_SMALL - LORA)))
    mu = lp['rwkv_mu']
    mu_main = mu[None, 0:3 * GW]
    mu_small = jnp.pad(mu[None, 3 * GW:], ((0, 0), (0, H_SMALL - LORA)))
    place = lambda w, off: jnp.zeros((H_SMALL, GW), F32).at[off:off + w.shape[0]].set(w)
    wup = place(lp['rwkv_w_up'], 0)
    aup = place(lp['rwkv_a_up'], RWKV_W_LORA)
    gup = place(lp['rwkv_g_up'], RWKV_W_LORA + RWKV_A_LORA)
    row = lambda v: v.reshape(1, GW)
    full = lambda shape: pl.BlockSpec(shape, lambda bi, i: (0,) * len(shape))
    o, s_new, sh_new, shs_new = pl.pallas_call(
        functools.partial(_rwkv_kernel, nb=nb, nc=nc),
        grid=(bsz // nb, l // rows),
        in_specs=[pl.BlockSpec((nb, rows, 3 * GW), lambda bi, i: (bi, i, H_CRKV // (3 * GW))),
                  pl.BlockSpec((nb, rows, H_SMALL), lambda bi, i: (bi, i, 0)),
                  pl.BlockSpec((nb, 1, 3 * GW), lambda bi, i: (bi, 0, 0)),
                  pl.BlockSpec((nb, 1, H_SMALL), lambda bi, i: (bi, 0, 0)),
                  pl.BlockSpec((nb, HEAD_DIM, GW), lambda bi, i: (bi, 0, 0)),
                  full((1, 3 * GW)), full((1, H_SMALL)), full((1, GW)), full((H_SMALL, GW)), full((1, GW)),
                  full((H_SMALL, GW)), full((H_SMALL, GW)), full((1, GW)), full((1, GW)), full((1, GW)),
                  full((1, GW)), full((1, GW)), full((GW, GW))],
        out_specs=[pl.BlockSpec((nb, rows, GW), lambda bi, i: (bi, i, 0)),
                   pl.BlockSpec((nb, HEAD_DIM, GW), lambda bi, i: (bi, 0, 0)),
                   pl.BlockSpec((nb, 1, 3 * GW), lambda bi, i: (bi, 0, 0)),
                   pl.BlockSpec((nb, 1, H_SMALL), lambda bi, i: (bi, 0, 0))],
        out_shape=[jax.ShapeDtypeStruct((bsz, l, GW), BF16),
                   jax.ShapeDtypeStruct((bsz, HEAD_DIM, GW), F32),
                   jax.ShapeDtypeStruct((bsz, 1, 3 * GW), F32),
                   jax.ShapeDtypeStruct((bsz, 1, H_SMALL), F32)],
        scratch_shapes=[pltpu.VMEM((nb, rows + 8, 3 * GW), F32), pltpu.VMEM((nb, rows + 8, H_SMALL), F32),
                        pltpu.VMEM((nb, HEAD_DIM, GW), F32)],
        compiler_params=_params(2),
        name="rwkv_mixer",
    )(h3, hs3, sh_main, sh_small, _heads_to_wide(s0), mu_main, mu_small, row(lp['rwkv_w0']), wup,
      row(lp['rwkv_a0']), aup, gup, row(lp['rwkv_k_k']), row(lp['rwkv_k_a']), row(lp['rwkv_r_k']),
      row(lp['rwkv_lnx_g']), row(lp['rwkv_lnx_b']), _ones_block_diag())
    new_shift = jnp.concatenate([sh_new[:, 0, :], shs_new[:, 0, S_LORA:S_LORA + LORA]], axis=-1)
    return o, _wide_to_heads(s_new), new_shift


_FB = 128
_SAMPLE_BK = 2048


def _log_forget(hs, bf):
    return -_softplus(-(hs + bf))


def _head_slice(h):
    return slice(h * HEAD_DIM, (h + 1) * HEAD_DIM)


def _attend_heads(qs, kblk, vblk, key_bias, mask, carry):
    m, l, acc = carry
    bq = qs[0].shape[0]
    kb16 = kblk.astype(BF16)
    vb16 = vblk.astype(BF16)
    parts = []
    for h in range(GROUP_HEADS):
        s = _dot(qs[h], kb16[_head_slice(h), :]) + key_bias[h:h + 1, :]
        parts.append(s if mask is None else jnp.where(mask, s, NEG))
    s = jnp.concatenate(parts, axis=0)
    m_new = jnp.maximum(m, jnp.max(s, -1, keepdims=True))
    alpha = jnp.exp(m - m_new)
    p = jnp.exp(s - m_new)
    l = alpha * l + jnp.sum(p, -1, keepdims=True)
    pb = p.astype(BF16)
    pv = jnp.concatenate([_dot(pb[h * bq:(h + 1) * bq, :], vb16[_head_slice(h), :], NT)
                          for h in range(GROUP_HEADS)], axis=0)
    return m_new, l, alpha * acc + pv


def _attend_init(bq):
    rows = GROUP_HEADS * bq
    return jnp.full((rows, 1), NEG, F32), jnp.zeros((rows, 1), F32), jnp.zeros((rows, HEAD_DIM), F32)


def _attend_store(o_ref, carry, bq):
    _, l, acc = carry
    for h in range(GROUP_HEADS):
        o_ref[:, _head_slice(h)] = (acc[h * bq:(h + 1) * bq, :] / l[h * bq:(h + 1) * bq, :]).astype(o_ref.dtype)


def _sel_rows(sel01, x):
    t = x.shape[0]
    r = _dot(sel01, jnp.concatenate(_split3(x), axis=0), NT)
    return r[:, :t] + r[:, t:2 * t] + r[:, 2 * t:]


def _attend_heads_t(qts, k_blk, vt_blk, bias_tiles, mask, carry):
    m, l, acc = carry
    bq = qts[0].shape[1]
    kb16 = k_blk.astype(BF16)
    vb16 = vt_blk.astype(BF16)
    parts = []
    for h in range(GROUP_HEADS):
        s = _dot(kb16[:, _head_slice(h)], qts[h]) + jnp.tile(bias_tiles[h], (1, bq // LANE))
        parts.append(s if mask is None else jnp.where(mask, s, NEG))
    s = jnp.concatenate(parts, axis=1)
    m_new = jnp.maximum(m, jnp.max(s, 0, keepdims=True))
    alpha = jnp.exp(m - m_new)
    p = jnp.exp(s - m_new)
    l = alpha * l + jnp.sum(p, 0, keepdims=True)
    pb = p.astype(BF16)
    pv = jnp.concatenate([_dot(vb16[_head_slice(h), :], pb[:, h * bq:(h + 1) * bq])
                          for h in range(GROUP_HEADS)], axis=1)
    return m_new, l, alpha * acc + pv


def _fox_prompt_kernel(qt_ref, k_ref, vt_ref, hs_ref, bf_ref, o_ref, lf_ref, bias_scr, *, l, bq):
    i = pl.program_id(1)

    @pl.when(i == 0)
    def _():
        ltri = _tril_ones(_FB, BF16)
        carry = jnp.zeros((1, H_SMALL), F32)
        for jb in range(l // _FB):
            rows = slice(jb * _FB, (jb + 1) * _FB)
            lf = _log_forget(hs_ref[rows, :], bf_ref[...])
            lf_ref[rows, :] = lf[:, S_F:S_F + GROUP_HEADS]
            loc = _tri_mm(ltri, lf, 3)
            nf = -(loc + carry)
            for h in range(GROUP_HEADS):
                bias_scr[h, rows, :] = jnp.broadcast_to(nf[:, S_F + h:S_F + h + 1], (_FB, LANE))
            carry = carry + loc[_FB - 1:_FB, :]

    qt = qt_ref[...]
    r_i, c_i = _iota2((bq, bq), 0), _iota2((bq, bq), 1)
    qts = [(qt[_head_slice(h), :] * (HEAD_DIM ** -0.5)).astype(BF16) for h in range(GROUP_HEADS)]

    def step(j, mask, carry):
        keys = pl.ds(pl.multiple_of(j * bq, bq), bq)
        biases = [bias_scr[h, keys, :] for h in range(GROUP_HEADS)]
        return _attend_heads_t(qts, k_ref[keys, :], vt_ref[:, keys], biases, mask, carry)

    init = (jnp.full((1, GROUP_HEADS * bq), NEG, F32), jnp.zeros((1, GROUP_HEADS * bq), F32),
            jnp.zeros((HEAD_DIM, GROUP_HEADS * bq), F32))
    carry = lax.fori_loop(0, i, lambda j, c: step(j, None, c), init)
    _, lsum, acc = step(i, r_i <= c_i, carry)
    o_t = acc / lsum
    o_t = jnp.concatenate([o_t[:, h * bq:(h + 1) * bq] for h in range(GROUP_HEADS)], axis=0)
    o_ref[...] = o_t.T.astype(o_ref.dtype)


def _fox_prompt(h3, hs3, q_t, v_t, bf_pad):
    bsz, l, _ = h3.shape
    bq = min(l, 4 * _FB)
    kern = functools.partial(_fox_prompt_kernel, l=l, bq=bq)
    return pl.pallas_call(
        kern,
        grid=(bsz, l // bq),
        in_specs=[pl.BlockSpec((None, GW, bq), lambda bi, i: (bi, 0, i)),
                  pl.BlockSpec((None, l, GW), lambda bi, i: (bi, 0, H_DK // GW)),
                  pl.BlockSpec((None, GW, l), lambda bi, i: (bi, 0, 0)),
                  pl.BlockSpec((None, l, H_SMALL), lambda bi, i: (bi, 0, 0)),
                  pl.BlockSpec((1, H_SMALL), lambda bi, i: (0, 0))],
        out_specs=[pl.BlockSpec((None, bq, GW), lambda bi, i: (bi, i, 0)),
                   pl.BlockSpec((None, l, GROUP_HEADS), lambda bi, i: (bi, 0, 0))],
        out_shape=[jax.ShapeDtypeStruct((bsz, l, GW), BF16),
                   jax.ShapeDtypeStruct((bsz, l, GROUP_HEADS), F32)],
        scratch_shapes=[pltpu.VMEM((GROUP_HEADS, l, LANE), F32)],
        compiler_params=_params(2),
        name="fox_prompt",
    )(q_t, h3, v_t, hs3, bf_pad)


def _fox_sample_kernel(qt_ref, kt_ref, vt_ref, hs_ref, bf_ref, ck_ref, cv_ref, clf_ref, o_ref, lf_ref, cum_scr,
                       *, l, p):
    nblk = p // _FB
    per = _SAMPLE_BK // _FB

    loc = _mm_x01(clf_ref[...].reshape(nblk * 8, _FB), _triu_ones(_FB, BF16), 3)
    carry = jnp.zeros((8, 1), F32)
    for jb in range(nblk):
        blk = loc[jb * 8:(jb + 1) * 8, :]
        cum_scr[jb // per, :, (jb % per) * _FB:(jb % per + 1) * _FB] = blk + carry
        carry = carry + blk[:, _FB - 1:_FB]
    total = carry

    lf = _log_forget(hs_ref[...], bf_ref[...])
    lf_ref[...] = lf[:, S_F:S_F + GROUP_HEADS]
    cum_r = _mm_x01(_sel_rows(_col_selector(8, H_SMALL, S_F).astype(BF16), lf), _triu_ones(l, BF16), 3)
    q = qt_ref[...].T
    r_i, c_i = _iota2((l, l), 0), _iota2((l, l), 1)
    qs = [(q[:, _head_slice(h)] * (HEAD_DIM ** -0.5)).astype(BF16) for h in range(GROUP_HEADS)]

    def body(j, carry):
        keys = pl.ds(pl.multiple_of(j * _SAMPLE_BK, _SAMPLE_BK), _SAMPLE_BK)
        suffix = total - cum_scr[j]
        return _attend_heads(qs, ck_ref[:, keys], cv_ref[:, keys], suffix, None, carry)

    carry = lax.fori_loop(0, p // _SAMPLE_BK, body, _attend_init(l))
    carry = _attend_heads(qs, kt_ref[...], vt_ref[...], -cum_r, c_i <= r_i, carry)
    _attend_store(o_ref, carry, l)


def _fox_sample(hs3, q_t, k_t, v_t, bf_pad, layer, ck_t, cv_t, clogf):
    bsz, l, _ = hs3.shape
    p = ck_t.shape[-1]
    clf = clogf.reshape(bsz, p // _FB, _FB, GROUP_HEADS).transpose(0, 1, 3, 2)
    clf = jnp.pad(clf, ((0, 0), (0, 0), (0, 8 - GROUP_HEADS), (0, 0)))
    kern = functools.partial(_fox_sample_kernel, l=l, p=p)
    return pl.pallas_call(
        kern,
        grid=(bsz,),
        in_specs=[pl.BlockSpec((None, GW, l), lambda bi: (bi, 0, 0)),
                  pl.BlockSpec((None, GW, l), lambda bi: (bi, 0, 0)),
                  pl.BlockSpec((None, GW, l), lambda bi: (bi, 0, 0)),
                  pl.BlockSpec((None, l, H_SMALL), lambda bi: (bi, 0, 0)),
                  pl.BlockSpec((1, H_SMALL), lambda bi: (0, 0)),
                  pl.BlockSpec((None, None, GW, p), lambda bi: (layer, bi, 0, 0)),
                  pl.BlockSpec((None, None, GW, p), lambda bi: (layer, bi, 0, 0)),
                  pl.BlockSpec((None, p // _FB, 8, _FB), lambda bi: (bi, 0, 0, 0))],
        out_specs=[pl.BlockSpec((None, l, GW), lambda bi: (bi, 0, 0)),
                   pl.BlockSpec((None, l, GROUP_HEADS), lambda bi: (bi, 0, 0))],
        out_shape=[jax.ShapeDtypeStruct((bsz, l, GW), BF16),
                   jax.ShapeDtypeStruct((bsz, l, GROUP_HEADS), F32)],
        scratch_shapes=[pltpu.VMEM((p // _SAMPLE_BK, 8, _SAMPLE_BK), F32)],
        compiler_params=_params(1),
        name="fox_sample",
    )(q_t, k_t, v_t, hs3, bf_pad, ck_t, cv_t, clf)


_R_G, _R_E, _R_ROWS = 0, 8, 32
_MOE_EXPERTS_PER_STEP = 2


def _ffn_kernel(oa_ref, ob_ref, oc_ref, od_ref, x_ref, wo_ref, g1_ref, b1_ref, wrh_ref, wrl_ref, brc_ref, wg_ref,
                wu_ref, wd_ref, g2_ref, b2_ref, y_ref, comb_scr, xb_scr, x1_scr):
    e = pl.program_id(1)
    tm = x_ref.shape[0]
    lane = _iota2((tm, LANE), 1)

    @pl.when(e == 0)
    def _():
        mix = jnp.concatenate([ref[...] for ref in (oa_ref, ob_ref, oc_ref, od_ref)], axis=1)
        x1 = _layer_norm(DN_ALPHA * x_ref[...] + _dot(mix, wo_ref[...]), g1_ref[...], b1_ref[...])
        x1_scr[...] = x1
        y_ref[...] = jnp.zeros_like(y_ref)
        x_hi, x_lo = _split2(x1)
        xb_scr[...] = x_hi
        wr_hi = wrh_ref[...]
        r = _dot(jnp.concatenate([wr_hi, wrl_ref[...]], axis=0), x_hi, NT)
        lt = r[:_R_ROWS] + r[_R_ROWS:] + _dot(wr_hi, x_lo, NT) + brc_ref[...]
        gl = lt[_R_G:_R_G + N_EXPERT_GROUPS]
        grow = _iota2(gl.shape, 0)
        ge = jnp.exp(gl - jnp.max(gl, 0, keepdims=True))
        pg = ge / jnp.sum(ge, 0, keepdims=True)
        gp = jnp.max(pg, 0, keepdims=True)
        gi = jnp.min(jnp.where(pg == gp, grow, N_EXPERT_GROUPS), 0, keepdims=True)
        le = lt[_R_E:_R_E + N_EXPERTS]
        erow = _iota2(le.shape, 0)
        emask = (erow >> 2) == gi
        el = jnp.where(emask, le, NEG)
        ee = jnp.exp(el - jnp.max(el, 0, keepdims=True))
        ep = ee / jnp.sum(ee, 0, keepdims=True)
        m1 = jnp.max(jnp.where(emask, ep, -1.0), 0, keepdims=True)
        i1 = jnp.min(jnp.where(jnp.logical_and(emask, ep == m1), erow, N_EXPERTS), 0, keepdims=True)
        rest = jnp.logical_and(emask, erow != i1)
        m2 = jnp.max(jnp.where(rest, ep, -1.0), 0, keepdims=True)
        i2 = jnp.min(jnp.where(jnp.logical_and(rest, ep == m2), erow, N_EXPERTS), 0, keepdims=True)
        den = m1 + m2
        comb_t = jnp.where(erow == i1, gp * m1 / den, jnp.where(erow == i2, gp * m2 / den, 0.0))
        pieces = jnp.concatenate(_split3(comb_t), axis=0)
        sel = ((_iota2((3 * N_EXPERTS, LANE), 0) & (N_EXPERTS - 1)) == _iota2((3 * N_EXPERTS, LANE), 1)).astype(BF16)
        comb_scr[...] = _dot(pieces, sel, TN)

    xb = xb_scr[...]
    comb = comb_scr[...]
    hidden = []
    for k in range(_MOE_EXPERTS_PER_STEP):
        ce = jnp.sum(jnp.where(lane == e * _MOE_EXPERTS_PER_STEP + k, comb, 0.0), -1, keepdims=True)
        hh = _silu(_mm(xb, wg_ref[k])) * _mm(xb, wu_ref[k])
        hidden.append((hh * ce).astype(BF16))
    hidden = jnp.concatenate(hidden, axis=1)
    w_down = wd_ref[...].reshape(_MOE_EXPERTS_PER_STEP * D_EXPERT, D_MODEL)
    y_ref[...] += _mm(hidden, w_down)

    @pl.when(e == pl.num_programs(1) - 1)
    def _():
        y_ref[...] = _layer_norm(DN_ALPHA * x1_scr[...] + y_ref[...], g2_ref[...], b2_ref[...])


def _ffn(mixers, x2d, w_out_bf16, g1, b1, wr_hi, wr_lo, br_col, layer, wg, wu, wd, g2, b2):
    t = x2d.shape[0]
    tm = min(t, 1024)
    ne = _MOE_EXPERTS_PER_STEP
    mix = pl.BlockSpec((tm, GW), lambda i, e: (i, 0))
    vec = pl.BlockSpec((1, D_MODEL), lambda i, e: (0, 0))
    return pl.pallas_call(
        _ffn_kernel,
        grid=(t // tm, N_EXPERTS // ne),
        in_specs=[mix, mix, mix, mix,
                  pl.BlockSpec((tm, D_MODEL), lambda i, e: (i, 0)),
                  pl.BlockSpec((D_MODEL, D_MODEL), lambda i, e: (0, 0)), vec, vec,
                  pl.BlockSpec((_R_ROWS, D_MODEL), lambda i, e: (0, 0)),
                  pl.BlockSpec((_R_ROWS, D_MODEL), lambda i, e: (0, 0)),
                  pl.BlockSpec((_R_ROWS, 1), lambda i, e: (0, 0)),
                  pl.BlockSpec((None, ne, D_MODEL, D_EXPERT), lambda i, e: (layer, e, 0, 0)),
                  pl.BlockSpec((None, ne, D_MODEL, D_EXPERT), lambda i, e: (layer, e, 0, 0)),
                  pl.BlockSpec((None, ne, D_EXPERT, D_MODEL), lambda i, e: (layer, e, 0, 0)), vec, vec],
        out_specs=pl.BlockSpec((tm, D_MODEL), lambda i, e: (i, 0)),
        out_shape=jax.ShapeDtypeStruct((t, D_MODEL), F32),
        scratch_shapes=[pltpu.VMEM((tm, LANE), F32), pltpu.VMEM((tm, D_MODEL), BF16),
                        pltpu.VMEM((tm, D_MODEL), F32)],
        compiler_params=_params(2),
        name="ffn",
    )(*mixers, x2d, w_out_bf16, g1, b1, wr_hi, wr_lo, br_col, wg, wu, wd, g2, b2)


def _prep_layer(lp):
    w = lp['w_in_t']
    c0 = CONV_COLS
    c1 = c0 + GDN_COLS
    c2 = c1 + RWKV_COLS
    main = jnp.concatenate([w[c0:c0 + 3 * GW], w[c1:c1 + 3 * GW], w[c2 + GW:c2 + 2 * GW],
                            w[0:GW], w[GW:2 * GW], w[c0 + 3 * GW:c0 + 4 * GW]], axis=0).astype(BF16)
    kv_t = w[c2:c2 + 3 * GW].astype(BF16)
    small = jnp.concatenate([w[c1 + 3 * GW:c2], w[c0 + 4 * GW:c1], w[c2 + 3 * GW:],
                             jnp.zeros((H_SMALL - LORA - 3 * GROUP_HEADS, D_MODEL), F32)], axis=0)
    gap = jnp.zeros((_R_E - N_EXPERT_GROUPS, D_MODEL), F32)
    tail = jnp.zeros((_R_ROWS - _R_E - N_EXPERTS, D_MODEL), F32)
    wr_t = jnp.concatenate([lp['router_g_w'].T, gap, lp['router_e_w'].T, tail], axis=0)
    wr_hi = wr_t.astype(BF16)
    wr_lo = (wr_t - wr_hi.astype(F32)).astype(BF16)
    br = jnp.concatenate([lp['router_g_b'], gap[:, 0], lp['router_e_b'], tail[:, 0]])[:, None]
    bf_pad = jnp.zeros((1, H_SMALL), F32).at[0, S_F:S_F + GROUP_HEADS].set(lp['fox_b_f'])
    return dict(w_main=main, w_small=small, w_kv_t=kv_t, wr_hi=wr_hi, wr_lo=wr_lo, br=br, bf_pad=bf_pad)


def _trunk_layer(x, lp, pp, layer, big, conv_buf, gdn_buf, gdn_s, rw_shift, rw_s, fox_cache):
    bsz, l, d = x.shape
    x2d = x.reshape(bsz * l, d)
    h3, hs3, q_t, k_t, v_t = _inproj(x, pp['w_main'], pp['w_small'], pp['w_kv_t'])

    o_a, new_conv = _conv_mixer(h3, conv_buf, lp['conv_w'], lp['conv_b'][None], lp['conv_ln_g'][None],
                                lp['conv_ln_b'][None])
    o_b, new_gdn_s, new_gdn_buf = _gdn_mixer(h3, hs3, gdn_buf, gdn_s, lp['gdn_conv_w'], lp['gdn_a_log'],
                                             lp['gdn_dt_bias'], lp['gdn_norm_w'])
    o_c, new_rw_s, new_shift = _rwkv_mixer(h3, hs3, rw_shift, rw_s, lp)
    if fox_cache is None:
        o_d, logf = _fox_prompt(h3, hs3, q_t, v_t, pp['bf_pad'])
    else:
        o_d, logf = _fox_sample(hs3, q_t, k_t, v_t, pp['bf_pad'], layer, *fox_cache)

    mixers = [o.reshape(bsz * l, GW) for o in (o_a, o_b, o_c, o_d)]
    x2 = _ffn(mixers, x2d, big['w_out'][layer], lp['ln1_g'][None], lp['ln1_b'][None], pp['wr_hi'], pp['wr_lo'],
              pp['br'], layer, *big['experts'], lp['ln2_g'][None], lp['ln2_b'][None])
    return x2.reshape(bsz, l, d), (new_conv, new_gdn_buf, new_gdn_s, new_shift, new_rw_s, k_t, v_t, logf)


def _untranspose(xs_t):
    bsz, _, l = xs_t[0].shape
    return jnp.stack(xs_t).reshape(len(xs_t), bsz, GROUP_HEADS, HEAD_DIM, l).transpose(0, 1, 4, 2, 3)


_LAYER_KEYS = ('w_in_t', 'conv_w', 'conv_b', 'conv_ln_g', 'conv_ln_b', 'gdn_conv_w', 'gdn_a_log', 'gdn_dt_bias',
               'gdn_norm_w', 'rwkv_mu', 'rwkv_w0', 'rwkv_w_up', 'rwkv_a0', 'rwkv_a_up', 'rwkv_g_up', 'rwkv_k_k',
               'rwkv_k_a', 'rwkv_r_k', 'rwkv_lnx_g', 'rwkv_lnx_b', 'fox_b_f', 'ln1_g', 'ln1_b',
               'router_g_w', 'router_g_b', 'router_e_w', 'router_e_b', 'ln2_g', 'ln2_b')


def kernel(x_prompt, x_sample, cache_fox_k, cache_fox_v, cache_fox_logf, state_conv, state_gdn_conv, state_gdn, state_rwkv_shift, state_rwkv, w_in, conv_w, conv_b, conv_ln_g, conv_ln_b, gdn_conv_w, gdn_a_log, gdn_dt_bias, gdn_norm_w, rwkv_mu, rwkv_w0, rwkv_w_up, rwkv_a0, rwkv_a_up, rwkv_g_up, rwkv_k_k, rwkv_k_a, rwkv_r_k, rwkv_lnx_g, rwkv_lnx_b, fox_b_f, w_out, ln1_g, ln1_b, router_g_w, router_g_b, router_e_w, router_e_b, exp_w_gate, exp_w_up, exp_w_down, ln2_g, ln2_b):
    w_in_t = w_in.transpose(0, 2, 1)
    weights = dict(zip(_LAYER_KEYS, (w_in_t, conv_w, conv_b, conv_ln_g, conv_ln_b, gdn_conv_w, gdn_a_log,
                                     gdn_dt_bias, gdn_norm_w, rwkv_mu, rwkv_w0, rwkv_w_up, rwkv_a0, rwkv_a_up,
                                     rwkv_g_up, rwkv_k_k, rwkv_k_a, rwkv_r_k, rwkv_lnx_g, rwkv_lnx_b, fox_b_f,
                                     ln1_g, ln1_b, router_g_w, router_g_b, router_e_w, router_e_b,
                                     ln2_g, ln2_b)))
    big = dict(w_out=w_out.astype(BF16), experts=(exp_w_gate, exp_w_up, exp_w_down))
    depth, bs, past = cache_fox_k.shape[:3]
    ck_t = cache_fox_k.transpose(0, 1, 3, 4, 2).reshape(depth, bs, GW, past)
    cv_t = cache_fox_v.transpose(0, 1, 3, 4, 2).reshape(depth, bs, GW, past)
    xp, xs = x_prompt, x_sample
    bp = x_prompt.shape[0]
    outs_p = [[] for _ in range(8)]
    outs_s = [[] for _ in range(8)]
    for l in range(DEPTH):
        lp = {k: v[l] for k, v in weights.items()}
        pp = _prep_layer(lp)
        xp, st_p = _trunk_layer(xp, lp, pp, l, big,
                                jnp.zeros((bp, CONV_WIDTH - 1, GW), F32),
                                jnp.zeros((bp, GDN_CONV_WIDTH - 1, 3 * GW), F32),
                                jnp.zeros((bp, GROUP_HEADS, HEAD_DIM, HEAD_DIM), F32),
                                jnp.zeros((bp, RWKV_COLS), F32),
                                jnp.zeros((bp, GROUP_HEADS, HEAD_DIM, HEAD_DIM), F32),
                                None)
        xs, st_s = _trunk_layer(xs, lp, pp, l, big, state_conv[l], state_gdn_conv[l], state_gdn[l],
                                state_rwkv_shift[l], state_rwkv[l], (ck_t, cv_t, cache_fox_logf[l]))
        for i in range(8):
            outs_p[i].append(st_p[i])
            outs_s[i].append(st_s[i])

    def assemble(outs):
        conv, gdn_buf, gdn_s, shift, rw_s = (jnp.stack(o) for o in outs[:5])
        return conv, gdn_buf, gdn_s, shift, rw_s, _untranspose(outs[5]), _untranspose(outs[6]), jnp.stack(outs[7])

    return (xp, xs, *assemble(outs_p), *assemble(outs_s))
```

```python
import functools

import jax
import jax.numpy as jnp
from jax import lax
from jax.experimental import pallas as pl
from jax.experimental.pallas import tpu as pltpu

F32 = jnp.float32
BF16 = jnp.bfloat16

D_MODEL = 1024
DEPTH = 2
CHUNK = 64
HEAD_DIM = 64
GW = D_MODEL // 4
GROUP_HEADS = GW // HEAD_DIM
CONV_WIDTH = 31
GDN_CONV_WIDTH = 4
RWKV_W_LORA = 16
RWKV_A_LORA = 16
RWKV_G_LORA = 32
N_EXPERT_GROUPS = 4
EXPERTS_PER_GROUP = 4
N_EXPERTS = N_EXPERT_GROUPS * EXPERTS_PER_GROUP
D_EXPERT = 256
DN_ALPHA = (2 * DEPTH) ** 0.25
LN_EPS = 1e-5
GN_EPS = 64e-5

CONV_COLS = 2 * GW
GDN_COLS = 4 * GW + 2 * GROUP_HEADS
RWKV_COLS = 3 * GW + RWKV_W_LORA + RWKV_A_LORA + RWKV_G_LORA
FOX_COLS = 3 * GW + GROUP_HEADS
LORA = RWKV_W_LORA + RWKV_A_LORA + RWKV_G_LORA

H_BQKV, H_CRKV, H_DK, H_AVAL, H_AGATE, H_BZ = 0, 768, 1536, 1792, 2048, 2304
H_MAIN = 2560
QKV_ROWS = 3 * GW
S_LORA, S_A, S_B, S_F = 0, 64, 68, 72
H_SMALL = 128
LANE = 128

VMEM_LIMIT = 56 * 1024 * 1024

NN = (((1,), (0,)), ((), ()))
NT = (((1,), (1,)), ((), ()))
TN = (((0,), (0,)), ((), ()))
NEG = -1e30


def _dot(a, b, dims=NN):
    return lax.dot_general(a, b, dims, preferred_element_type=F32)


def _mm(a, b, dims=NN):
    return _dot(a.astype(BF16), b.astype(BF16), dims)


def _sigmoid(x):
    return 0.5 * (jnp.tanh(0.5 * x) + 1.0)


def _silu(x):
    return x * _sigmoid(x)


def _softplus(x):
    return jnp.maximum(x, 0.0) + jnp.log(1.0 + jnp.exp(-jnp.abs(x)))


def _iota2(shape, dim):
    return lax.broadcasted_iota(jnp.int32, shape, dim)


def _tril_ones(n, dtype=F32):
    r, c = _iota2((n, n), 0), _iota2((n, n), 1)
    return (r >= c).astype(dtype)


def _triu_ones(n, dtype=F32):
    r, c = _iota2((n, n), 0), _iota2((n, n), 1)
    return (r <= c).astype(dtype)


def _col_selector(rows, width, col0):
    r, c = _iota2((rows, width), 0), _iota2((rows, width), 1)
    return (c == r + col0).astype(F32)


def _layer_norm(x, g, b):
    mu = jnp.mean(x, axis=-1, keepdims=True)
    xc = x - mu
    var = jnp.mean(xc * xc, axis=-1, keepdims=True)
    return xc * lax.rsqrt(var + LN_EPS) * g + b


def _split2(x):
    hi = x.astype(BF16)
    lo = (x - hi.astype(F32)).astype(BF16)
    return hi, lo


def _split3(x):
    hi = x.astype(BF16)
    r = x - hi.astype(F32)
    mid = r.astype(BF16)
    lo = (r - mid.astype(F32)).astype(BF16)
    return hi, mid, lo


class _Wide:
    def __init__(self):
        self.i = _iota2((CHUNK, GW), 0)
        lane = _iota2((CHUNK, GW), 1)
        self.j = lane & (HEAD_DIM - 1)
        self.head = lane >> 6
        self.incl = self.i >= self.j
        self.strict = self.i > self.j
        self.eye = self.i == self.j
        r, c = _iota2((GW, GW), 0), _iota2((GW, GW), 1)
        self.block_diag = (r >> 6) == (c >> 6)


def _head_block_diag(x, w):
    t = jnp.concatenate([x] * GROUP_HEADS, axis=0)
    return jnp.where(w.block_diag, t, jnp.zeros_like(t))


def _collapse_heads(full, w):
    out = jnp.zeros((HEAD_DIM, GW), F32)
    for h in range(GROUP_HEADS):
        out = out + jnp.where(w.head == h, full[h * HEAD_DIM:(h + 1) * HEAD_DIM, :], 0.0)
    return out


def _lhs3(a, precise=False):
    if not precise:
        return None, a.astype(BF16)
    hi, lo = _split2(a)
    return jnp.concatenate([hi, lo], axis=0), hi


def _rhs3(b, w=None, precise=False):
    if not precise:
        hi = b.astype(BF16)
        return (hi if w is None else _head_block_diag(hi, w)), None
    hi, lo = _split2(b)
    if w is None:
        return hi, lo
    return _head_block_diag(hi, w), _head_block_diag(lo, w)


def _mm3(lhs, rhs, dims=NN):
    cat, hi = lhs
    bh, bl = rhs
    if cat is None or bl is None:
        return _dot(hi, bh, dims)
    m = hi.shape[0]
    r = _dot(cat, bh, dims)
    return r[:m] + r[m:] + _dot(hi, bl, dims)


def _mm3_tn(a, b, precise=False):
    if not precise:
        return _dot(a.astype(BF16), b.astype(BF16), TN)
    ah, al = _split2(a)
    bh, bl = _split2(b)
    m = a.shape[1]
    r = _dot(jnp.concatenate([ah, al], axis=1), bh, TN)
    return r[:m] + r[m:] + _dot(ah, bl, TN)


def _mm_x01(a, m01, pieces=2):
    m = a.shape[0]
    r = _dot(jnp.concatenate(_split3(a) if pieces == 3 else _split2(a), axis=0), m01)
    return sum(r[i * m:(i + 1) * m] for i in range(pieces))


def _tri_mm(tri01, x, pieces=2):
    n = x.shape[1]
    r = _dot(tri01, jnp.concatenate(_split3(x) if pieces == 3 else _split2(x), axis=1))
    return sum(r[:, i * n:(i + 1) * n] for i in range(pieces))


def _unit_lower_inverse_wide(mats, w):
    eye = w.eye.astype(F32)
    same8 = (w.i >> 3) == (w.j >> 3)
    ds = [jnp.where(same8, a, 0.0) for a in mats]
    d2 = [_mm3(_lhs3(d), _rhs3(d, w)) for d in ds]
    d2_r = [_rhs3(x, w) for x in d2]
    d4 = [_mm3(_lhs3(x), r) for x, r in zip(d2, d2_r)]
    imd = [eye - d for d in ds]
    p1 = [i + _mm3(_lhs3(i), r) for i, r in zip(imd, d2_r)]
    ts = [p + _mm3(_lhs3(p), _rhs3(x, w)) for p, x in zip(p1, d4)]
    for s in range(3, 6):
        big = (w.i >> (s + 1)) == (w.j >> (s + 1))
        small = (w.i >> s) == (w.j >> s)
        sel = jnp.logical_and(big, jnp.logical_not(small))
        inner = [_mm3(_lhs3(jnp.where(sel, a, 0.0)), _rhs3(t, w)) for a, t in zip(mats, ts)]
        ts = [t - _mm3(_lhs3(t), _rhs3(x, w)) for t, x in zip(ts, inner)]
    return ts


def _params(n_axes):
    return pltpu.CompilerParams(dimension_semantics=("arbitrary",) * n_axes,
                                vmem_limit_bytes=VMEM_LIMIT)


def _heads_to_wide(s):
    b = s.shape[0]
    return s.transpose(0, 2, 1, 3).reshape(b, HEAD_DIM, GW)


def _wide_to_heads(s):
    b = s.shape[0]
    return s.reshape(b, HEAD_DIM, GROUP_HEADS, HEAD_DIM).transpose(0, 2, 1, 3)


def _ones_block_diag():
    r, c = jnp.arange(GW)[:, None], jnp.arange(GW)[None, :]
    return ((r // HEAD_DIM) == (c // HEAD_DIM)).astype(BF16)


def _lane_spread(col0):
    r, c = jnp.arange(H_SMALL)[:, None], jnp.arange(GW)[None, :]
    return ((r - col0) == (c // HEAD_DIM)).astype(BF16)


_INPROJ_ROWS = 512


def _inproj_kernel(x_ref, wm_ref, wsh_ref, wsl_ref, wkv_ref, h_ref, hs_ref, qt_ref, kt_ref, vt_ref):
    nb, tm, _ = x_ref.shape
    rows = nb * tm
    x_hi, x_lo = _split2(x_ref[...].reshape(rows, D_MODEL))
    h_ref[...] = _dot(x_hi, wm_ref[...], NT).reshape(nb, tm, H_MAIN)
    r = _dot(jnp.concatenate([x_hi, x_lo], axis=0), wsh_ref[...], NT)
    hs_ref[...] = (r[:rows] + r[rows:] + _dot(x_hi, wsl_ref[...], NT)).reshape(nb, tm, H_SMALL)
    qkv = _dot(wkv_ref[...], x_hi, NT)
    for b in range(nb):
        cols = slice(b * tm, (b + 1) * tm)
        qt_ref[b] = qkv[:GW, cols]
        kt_ref[b] = qkv[GW:2 * GW, cols]
        vt_ref[b] = qkv[2 * GW:, cols]


def _inproj(x3, w_main_t, w_small_t, w_kv_t):
    bsz, l, _ = x3.shape
    tm = min(l, _INPROJ_ROWS)
    nb = min(bsz, _INPROJ_ROWS // tm)
    const = lambda shape: pl.BlockSpec(shape, lambda bi, i: (0, 0))
    ws_hi = w_small_t.astype(BF16)
    ws_lo = (w_small_t - ws_hi.astype(F32)).astype(BF16)
    kv_spec = pl.BlockSpec((nb, GW, tm), lambda bi, i: (bi, 0, i))
    return pl.pallas_call(
        _inproj_kernel,
        grid=(bsz // nb, l // tm),
        in_specs=[pl.BlockSpec((nb, tm, D_MODEL), lambda bi, i: (bi, i, 0)),
                  const((H_MAIN, D_MODEL)), const((H_SMALL, D_MODEL)), const((H_SMALL, D_MODEL)),
                  const((QKV_ROWS, D_MODEL))],
        out_specs=[pl.BlockSpec((nb, tm, H_MAIN), lambda bi, i: (bi, i, 0)),
                   pl.BlockSpec((nb, tm, H_SMALL), lambda bi, i: (bi, i, 0)), kv_spec, kv_spec, kv_spec],
        out_shape=[jax.ShapeDtypeStruct((bsz, l, H_MAIN), F32), jax.ShapeDtypeStruct((bsz, l, H_SMALL), F32)]
        + [jax.ShapeDtypeStruct((bsz, GW, l), F32)] * 3,
        compiler_params=_params(2),
        name="inproj",
    )(x3, w_main_t, ws_hi, ws_lo, w_kv_t)


_HALO = 32
_SUBLANES = 8


def _conv_kernel(val_ref, gate_ref, buf_ref, cw_ref, cb_ref, g_ref, b_ref, o_ref, nbuf_ref, win_ref, sh_ref, *, tl):
    i = pl.program_id(1)
    pad = _HALO - (CONV_WIDTH - 1)
    span = tl + _HALO - _SUBLANES

    @pl.when(i == 0)
    def _():
        win_ref[0:8, :] = jnp.zeros((8, GW), F32)
        win_ref[pad:_HALO, :] = buf_ref[...]

    u = val_ref[...] * _sigmoid(gate_ref[...])
    win_ref[_HALO:_HALO + tl, :] = u
    for s in range(1, _SUBLANES):
        sh_ref[s, :, :] = win_ref[s:s + span, :]
    sub = min(tl, 64)
    for r0 in range(0, tl, sub):
        acc = jnp.zeros((sub, GW), F32)
        for j in range(CONV_WIDTH):
            s, base = (pad + j) % _SUBLANES, (pad + j) // _SUBLANES * _SUBLANES + r0
            rows = win_ref[base:base + sub, :] if s == 0 else sh_ref[s, base:base + sub, :]
            acc = acc + cw_ref[j:j + 1, :] * rows
        y = _layer_norm(acc + cb_ref[...], g_ref[...], b_ref[...])
        o_ref[r0:r0 + sub, :] = _silu(y).astype(o_ref.dtype)
    tail = win_ref[tl:tl + _HALO, :]
    win_ref[0:_HALO, :] = tail

    @pl.when(i == pl.num_programs(1) - 1)
    def _():
        nbuf_ref[...] = win_ref[pad:_HALO, :]


def _conv_mixer(h3, buf, cw, cb, g, b):
    bsz, l, _ = h3.shape
    tl = min(l, 512)
    kern = functools.partial(_conv_kernel, tl=tl)
    full = lambda shape: pl.BlockSpec(shape, lambda bi, i: (0,) * len(shape))
    return pl.pallas_call(
        kern,
        grid=(bsz, l // tl),
        in_specs=[pl.BlockSpec((None, tl, GW), lambda bi, i: (bi, i, H_AVAL // GW)),
                  pl.BlockSpec((None, tl, GW), lambda bi, i: (bi, i, H_AGATE // GW)),
                  pl.BlockSpec((None, CONV_WIDTH - 1, GW), lambda bi, i: (bi, 0, 0)),
                  full((CONV_WIDTH, GW)), full((1, GW)), full((1, GW)), full((1, GW))],
        out_specs=[pl.BlockSpec((None, tl, GW), lambda bi, i: (bi, i, 0)),
                   pl.BlockSpec((None, CONV_WIDTH - 1, GW), lambda bi, i: (bi, 0, 0))],
        out_shape=[jax.ShapeDtypeStruct((bsz, l, GW), BF16),
                   jax.ShapeDtypeStruct((bsz, CONV_WIDTH - 1, GW), F32)],
        scratch_shapes=[pltpu.VMEM((tl + _HALO, GW), F32),
                        pltpu.VMEM((_SUBLANES, tl + _HALO - _SUBLANES, GW), F32)],
        compiler_params=_params(2),
        name="conv_mixer",
    )(h3, h3, buf, cw, cb, g, b)


def _gdn_kernel(qkv_ref, z_ref, hs_ref, buf_ref, s0_ref, cw_ref, alog_ref, dtb_ref, nw_ref, ones_ref, sela_ref,
                selb_ref, o_ref, sout_ref, nbuf_ref, win_ref, s_scr, *, nb, nc):
    c = pl.program_id(1)
    n = CHUNK
    rows = nc * n
    kw = GDN_CONV_WIDTH - 1
    items = [(b, ci) for b in range(nb) for ci in range(nc)]

    @pl.when(c == 0)
    def _():
        for b in range(nb):
            win_ref[b, 0:8, :] = jnp.zeros((8, 3 * GW), F32)
            win_ref[b, 8 - kw:8, :] = buf_ref[b]
        s_scr[...] = s0_ref[...]

    qkvs = []
    for b in range(nb):
        win_ref[b, 8:8 + rows, :] = qkv_ref[b]
        conv = jnp.zeros((rows, 3 * GW), F32)
        for j in range(GDN_CONV_WIDTH):
            conv = conv + cw_ref[j:j + 1, :] * win_ref[b, 8 - kw + j:8 - kw + j + rows, :]
        tail = win_ref[b, rows:rows + 8, :]
        win_ref[b, 0:8, :] = tail
        qkvs.append(_silu(conv))

    w = _Wide()
    ones_bd = ones_ref[...]
    tril = _tril_ones(n, BF16)
    sl = lambda ci: slice(ci * n, (ci + 1) * n)
    qs = [qkvs[b][sl(ci), 0:GW] for b, ci in items]
    ks = [qkvs[b][sl(ci), GW:2 * GW] for b, ci in items]
    vs = [qkvs[b][sl(ci), 2 * GW:3 * GW] for b, ci in items]
    sss = [_mm_x01(jnp.concatenate([q * q, k * k], axis=0), ones_bd) for q, k in zip(qs, ks)]
    qs = [q * lax.rsqrt(ss[:n] + 1e-6) * (HEAD_DIM ** -0.5) for q, ss in zip(qs, sss)]
    ks = [k * lax.rsqrt(ss[n:] + 1e-6) for k, ss in zip(ks, sss)]

    hss = [hs_ref[b, sl(ci), :] for b, ci in items]
    gs = [_mm_x01(-jnp.exp(alog_ref[...]) * _softplus(hs + dtb_ref[...]), sela_ref[...]) for hs in hss]
    betas = [_mm_x01(_sigmoid(hs), selb_ref[...]) for hs in hss]
    gcs = [_tri_mm(tril, g) for g in gs]
    grs = [jnp.sum(jnp.where(w.eye, gc, 0.0), axis=0, keepdims=True) for gc in gcs]
    decays = [jnp.exp(jnp.where(w.incl, gc - gr, NEG)) for gc, gr in zip(gcs, grs)]

    kbs = [k * beta for k, beta in zip(ks, betas)]
    grams = [_mm3(_lhs3(jnp.concatenate([kb, q], axis=0)), _rhs3(k, w), NT)
             for kb, q, k in zip(kbs, qs, ks)]
    amats = [jnp.where(w.strict, gram[:n] * decay, 0.0) for gram, decay in zip(grams, decays)]
    qks = [jnp.where(w.incl, gram[n:] * decay, 0.0) for gram, decay in zip(grams, decays)]
    t_ls = [_lhs3(t) for t in _unit_lower_inverse_wide(amats, w)]
    egs = [jnp.exp(gc) for gc in gcs]
    xvs = [_mm3(t_l, _rhs3(v * beta, w)) for t_l, v, beta in zip(t_ls, vs, betas)]
    xks = [_mm3(t_l, _rhs3(kb * eg, w)) for t_l, kb, eg in zip(t_ls, kbs, egs)]
    xq_ls = [_lhs3(jnp.concatenate([xk, q * eg], axis=0)) for xk, q, eg in zip(xks, qs, egs)]
    qk_ls = [_lhs3(qk) for qk in qks]
    gls = [gc[n - 1:n, :] for gc in gcs]
    kds = [k * jnp.exp(gl - gc) for k, gl, gc in zip(ks, gls, gcs)]

    states = [s_scr[b] for b in range(nb)]
    outs = {}
    for ci in range(nc):
        for b in range(nb):
            it = b * nc + ci
            s = states[b]
            rs = _mm3(xq_ls[it], _rhs3(s, w))
            u = xvs[it] - rs[:n]
            outs[it] = rs[n:] + _mm3(qk_ls[it], _rhs3(u, w))
            states[b] = s * jnp.exp(gls[it]) + _collapse_heads(_mm3_tn(kds[it], u), w)
    for b in range(nb):
        s_scr[b] = states[b]

    for it, (b, ci) in enumerate(items):
        o = outs[it]
        ms = _mm_x01(o * o, ones_bd) * (1.0 / HEAD_DIM)
        o = o * lax.rsqrt(ms + 1e-6) * nw_ref[...] * _silu(z_ref[b, sl(ci), :])
        o_ref[b, sl(ci), :] = o.astype(o_ref.dtype)

    @pl.when(c == pl.num_programs(1) - 1)
    def _():
        sout_ref[...] = s_scr[...]
        for b in range(nb):
            nbuf_ref[b] = win_ref[b, 8 - kw:8, :]


def _recurrent_tiling(bsz, l):
    nc = min(l // CHUNK, 4)
    nb = 2 if nc > 1 else min(bsz, 4)
    return nb, nc


def _gdn_mixer(h3, hs3, buf, s0, cw, a_log, dt_bias, norm_w):
    bsz, l, _ = h3.shape
    nb, nc = _recurrent_tiling(bsz, l)
    rows = nc * CHUNK
    pad_r = lambda v, off: jnp.zeros((1, H_SMALL), F32).at[0, off:off + GROUP_HEADS].set(v)
    full = lambda shape: pl.BlockSpec(shape, lambda bi, i: (0,) * len(shape))
    o, s_new, new_buf = pl.pallas_call(
        functools.partial(_gdn_kernel, nb=nb, nc=nc),
        grid=(bsz // nb, l // rows),
        in_specs=[pl.BlockSpec((nb, rows, 3 * GW), lambda bi, i: (bi, i, H_BQKV // (3 * GW))),
                  pl.BlockSpec((nb, rows, GW), lambda bi, i: (bi, i, H_BZ // GW)),
                  pl.BlockSpec((nb, rows, H_SMALL), lambda bi, i: (bi, i, 0)),
                  pl.BlockSpec((nb, GDN_CONV_WIDTH - 1, 3 * GW), lambda bi, i: (bi, 0, 0)),
                  pl.BlockSpec((nb, HEAD_DIM, GW), lambda bi, i: (bi, 0, 0)),
                  full((GDN_CONV_WIDTH, 3 * GW)), full((1, H_SMALL)), full((1, H_SMALL)), full((1, GW)),
                  full((GW, GW)), full((H_SMALL, GW)), full((H_SMALL, GW))],
        out_specs=[pl.BlockSpec((nb, rows, GW), lambda bi, i: (bi, i, 0)),
                   pl.BlockSpec((nb, HEAD_DIM, GW), lambda bi, i: (bi, 0, 0)),
                   pl.BlockSpec((nb, GDN_CONV_WIDTH - 1, 3 * GW), lambda bi, i: (bi, 0, 0))],
        out_shape=[jax.ShapeDtypeStruct((bsz, l, GW), BF16),
                   jax.ShapeDtypeStruct((bsz, HEAD_DIM, GW), F32),
                   jax.ShapeDtypeStruct((bsz, GDN_CONV_WIDTH - 1, 3 * GW), F32)],
        scratch_shapes=[pltpu.VMEM((nb, rows + 8, 3 * GW), F32), pltpu.VMEM((nb, HEAD_DIM, GW), F32)],
        compiler_params=_params(2),
        name="gdn_mixer",
    )(h3, h3, hs3, buf, _heads_to_wide(s0), cw, pad_r(a_log, S_A), pad_r(dt_bias, S_A),
      jnp.tile(norm_w, GROUP_HEADS)[None, :], _ones_block_diag(), _lane_spread(S_A), _lane_spread(S_B))
    return o, _wide_to_heads(s_new), new_buf


def _rwkv_kernel(rkv_ref, hs_ref, sh_ref, shs_ref, s0_ref, mu_ref, mus_ref, w0_ref, wup_ref, a0_ref, aup_ref,
                 gup_ref, kk_ref, ka_ref, rk_ref, lng_ref, lnb_ref, ones_ref, o_ref, sout_ref, nsh_ref, nshs_ref,
                 win_ref, wins_ref, s_scr, *, nb, nc):
    c = pl.program_id(1)
    n = CHUNK

    rows = nc * n
    items = [(b, ci) for b in range(nb) for ci in range(nc)]

    @pl.when(c == 0)
    def _():
        for b in range(nb):
            win_ref[b, 0:8, :] = jnp.zeros((8, 3 * GW), F32)
            wins_ref[b, 0:8, :] = jnp.zeros((8, H_SMALL), F32)
            win_ref[b, 7:8, :] = sh_ref[b]
            wins_ref[b, 7:8, :] = shs_ref[b]
        s_scr[...] = s0_ref[...]

    wup_r, aup_r, gup_r = (_rhs3(r[...], precise=True) for r in (wup_ref, aup_ref, gup_ref))
    per_b = []
    for b in range(nb):
        x = rkv_ref[b]
        xs = hs_ref[b]
        win_ref[b, 8:8 + rows, :] = x
        wins_ref[b, 8:8 + rows, :] = xs
        xm = x + (win_ref[b, 7:7 + rows, :] - x) * mu_ref[...]
        xms = xs + (wins_ref[b, 7:7 + rows, :] - xs) * mus_ref[...]
        last = win_ref[b, rows + 7:rows + 8, :]
        win_ref[b, 7:8, :] = last
        lasts = wins_ref[b, rows + 7:rows + 8, :]
        wins_ref[b, 7:8, :] = lasts
        w_pre = w0_ref[...] + _mm3(_lhs3(jnp.tanh(xms), precise=True), wup_r)
        logw = -jnp.exp(-_softplus(-w_pre) - 0.5)
        a_sig = _sigmoid(a0_ref[...] + _mm3(_lhs3(xms, precise=True), aup_r))
        gate = _mm3(_lhs3(_sigmoid(xms), precise=True), gup_r)
        kx = xm[:, GW:2 * GW]
        per_b.append(dict(rr=xm[:, 0:GW], vv=xm[:, 2 * GW:3 * GW], logw=logw, a_sig=a_sig, gate=gate,
                          kkp=kx * kk_ref[...], k2=kx * (1.0 + (a_sig - 1.0) * ka_ref[...])))

    w = _Wide()
    ones_bd = ones_ref[...]
    tril = _tril_ones(n, BF16)
    sl = lambda ci: slice(ci * n, (ci + 1) * n)
    get = lambda name: [per_b[b][name][sl(ci), :] for b, ci in items]
    rrs, vvs, logws, a_sigs, gates, kkps, k2s = (get(k) for k in ('rr', 'vv', 'logw', 'a_sig', 'gate', 'kkp', 'k2'))
    kks = [kkp * lax.rsqrt(_mm_x01(kkp * kkp, ones_bd) + 1e-6) for kkp in kkps]
    cums = [_tri_mm(tril, logw) for logw in logws]
    w_incls = [jnp.exp(cum) for cum in cums]
    w_lasts = [wi[n - 1:n, :] for wi in w_incls]
    w_invs = [jnp.exp(-cum) for cum in cums]
    ats = [-kk * jnp.exp(cum - logw) for kk, cum, logw in zip(kks, cums, logws)]
    bts = [kk * a_sig * wv for kk, a_sig, wv in zip(kks, a_sigs, w_invs)]
    kts = [k2 * wv for k2, wv in zip(k2s, w_invs)]
    rts = [rr * wi for rr, wi in zip(rrs, w_incls)]
    ar_ls = [_lhs3(jnp.concatenate([at, rt], axis=0)) for at, rt in zip(ats, rts)]
    gbs = [_mm3(ar_l, _rhs3(bt, w), NT) for ar_l, bt in zip(ar_ls, bts)]
    gks = [_mm3(ar_l, _rhs3(kt, w), NT) for ar_l, kt in zip(ar_ls, kts)]
    t_ls = [_lhs3(t) for t in _unit_lower_inverse_wide([jnp.where(w.strict, -gb[:n], 0.0) for gb in gbs], w)]
    arb_ls = [_lhs3(jnp.where(w.incl, gb[n:], 0.0)) for gb in gbs]
    ark_ls = [_lhs3(jnp.where(w.incl, gk[n:], 0.0)) for gk in gks]
    v_rs = [_rhs3(vv, w) for vv in vvs]
    aakvs = [_mm3(_lhs3(jnp.where(w.strict, gk[:n], 0.0)), v_r) for gk, v_r in zip(gks, v_rs)]
    arkvs = [_mm3(ark_l, v_r) for ark_l, v_r in zip(ark_ls, v_rs)]
    bks = [jnp.concatenate([bt * wl, kt * wl], axis=0) for bt, kt, wl in zip(bts, kts, w_lasts)]

    states = [s_scr[b] for b in range(nb)]
    ys = {}
    for ci in range(nc):
        for b in range(nb):
            it = b * nc + ci
            s0 = states[b]
            ars = _mm3(ar_ls[it], _rhs3(s0, w), NT)
            u = _mm3(t_ls[it], _rhs3(ars[:n] + aakvs[it], w))
            ys[it] = ars[n:] + _mm3(arb_ls[it], _rhs3(u, w)) + arkvs[it]
            upd = _mm3_tn(jnp.concatenate([u, vvs[it]], axis=0), bks[it])
            states[b] = s0 * w_lasts[it] + _collapse_heads(upd, w)
    for b in range(nb):
        s_scr[b] = states[b]

    inv_d = 1.0 / HEAD_DIM
    for it, (b, ci) in enumerate(items):
        y = ys[it]
        yc = y - _mm_x01(y, ones_bd) * inv_d
        var_y = _mm_x01(yc * yc, ones_bd) * inv_d
        yn = yc * lax.rsqrt(var_y + GN_EPS) * lng_ref[...] + lnb_ref[...]
        bonus = _mm_x01(rrs[it] * k2s[it] * rk_ref[...], ones_bd) * vvs[it]
        o_ref[b, sl(ci), :] = ((yn + bonus) * gates[it]).astype(o_ref.dtype)

    @pl.when(c == pl.num_programs(1) - 1)
    def _():
        sout_ref[...] = s_scr[...]
        for b in range(nb):
            nsh_ref[b] = win_ref[b, 7:8, :]
            nshs_ref[b] = wins_ref[b, 7:8, :]


def _rwkv_mixer(h3, hs3, shift, s0, lp):
    bsz, l, _ = h3.shape
    nb, nc = _recurrent_tiling(bsz, l)
    rows = nc * CHUNK
    sh_main = shift[:, None, 0:3 * GW]
    sh_small = jnp.pad(shift[:, None, 3 * GW:], ((0, 0), (0, 0), (0, H_SMALL - LORA)))
    mu = lp['rwkv_mu']
    mu_main = mu[None, 0:3 * GW]
    mu_small = jnp.pad(mu[None, 3 * GW:], ((0, 0), (0, H_SMALL - LORA)))
    place = lambda w, off: jnp.zeros((H_SMALL, GW), F32).at[off:off + w.shape[0]].set(w)
    wup = place(lp['rwkv_w_up'], 0)
    aup = place(lp['rwkv_a_up'], RWKV_W_LORA)
    gup = place(lp['rwkv_g_up'], RWKV_W_LORA + RWKV_A_LORA)
    row = lambda v: v.reshape(1, GW)
    full = lambda shape: pl.BlockSpec(shape, lambda bi, i: (0,) * len(shape))
    o, s_new, sh_new, shs_new = pl.pallas_call(
        functools.partial(_rwkv_kernel, nb=nb, nc=nc),
        grid=(bsz // nb, l // rows),
        in_specs=[pl.BlockSpec((nb, rows, 3 * GW), lambda bi, i: (bi, i, H_CRKV // (3 * GW))),
                  pl.BlockSpec((nb, rows, H_SMALL), lambda bi, i: (bi, i, 0)),
                  pl.BlockSpec((nb, 1, 3 * GW), lambda bi, i: (bi, 0, 0)),
                  pl.BlockSpec((nb, 1, H_SMALL), lambda bi, i: (bi, 0, 0)),
                  pl.BlockSpec((nb, HEAD_DIM, GW), lambda bi, i: (bi, 0, 0)),
                  full((1, 3 * GW)), full((1, H_SMALL)), full((1, GW)), full((H_SMALL, GW)), full((1, GW)),
                  full((H_SMALL, GW)), full((H_SMALL, GW)), full((1, GW)), full((1, GW)), full((1, GW)),
                  full((1, GW)), full((1, GW)), full((GW, GW))],
        out_specs=[pl.BlockSpec((nb, rows, GW), lambda bi, i: (bi, i, 0)),
                   pl.BlockSpec((nb, HEAD_DIM, GW), lambda bi, i: (bi, 0, 0)),
                   pl.BlockSpec((nb, 1, 3 * GW), lambda bi, i: (bi, 0, 0)),
                   pl.BlockSpec((nb, 1, H_SMALL), lambda bi, i: (bi, 0, 0))],
        out_shape=[jax.ShapeDtypeStruct((bsz, l, GW), BF16),
                   jax.ShapeDtypeStruct((bsz, HEAD_DIM, GW), F32),
                   jax.ShapeDtypeStruct((bsz, 1, 3 * GW), F32),
                   jax.ShapeDtypeStruct((bsz, 1, H_SMALL), F32)],
        scratch_shapes=[pltpu.VMEM((nb, rows + 8, 3 * GW), F32), pltpu.VMEM((nb, rows + 8, H_SMALL), F32),
                        pltpu.VMEM((nb, HEAD_DIM, GW), F32)],
        compiler_params=_params(2),
        name="rwkv_mixer",
    )(h3, hs3, sh_main, sh_small, _heads_to_wide(s0), mu_main, mu_small, row(lp['rwkv_w0']), wup,
      row(lp['rwkv_a0']), aup, gup, row(lp['rwkv_k_k']), row(lp['rwkv_k_a']), row(lp['rwkv_r_k']),
      row(lp['rwkv_lnx_g']), row(lp['rwkv_lnx_b']), _ones_block_diag())
    new_shift = jnp.concatenate([sh_new[:, 0, :], shs_new[:, 0, S_LORA:S_LORA + LORA]], axis=-1)
    return o, _wide_to_heads(s_new), new_shift


_FB = 128
_SAMPLE_BK = 2048


def _log_forget(hs, bf):
    return -_softplus(-(hs + bf))


def _head_slice(h):
    return slice(h * HEAD_DIM, (h + 1) * HEAD_DIM)


def _attend_heads(qs, kblk, vblk, key_bias, mask, carry):
    m, l, acc = carry
    bq = qs[0].shape[0]
    kb16 = kblk.astype(BF16)
    vb16 = vblk.astype(BF16)
    parts = []
    for h in range(GROUP_HEADS):
        s = _dot(qs[h], kb16[_head_slice(h), :]) + key_bias[h:h + 1, :]
        parts.append(s if mask is None else jnp.where(mask, s, NEG))
    s = jnp.concatenate(parts, axis=0)
    m_new = jnp.maximum(m, jnp.max(s, -1, keepdims=True))
    alpha = jnp.exp(m - m_new)
    p = jnp.exp(s - m_new)
    l = alpha * l + jnp.sum(p, -1, keepdims=True)
    pb = p.astype(BF16)
    pv = jnp.concatenate([_dot(pb[h * bq:(h + 1) * bq, :], vb16[_head_slice(h), :], NT)
                          for h in range(GROUP_HEADS)], axis=0)
    return m_new, l, alpha * acc + pv


def _attend_init(bq):
    rows = GROUP_HEADS * bq
    return jnp.full((rows, 1), NEG, F32), jnp.zeros((rows, 1), F32), jnp.zeros((rows, HEAD_DIM), F32)


def _attend_store(o_ref, carry, bq):
    _, l, acc = carry
    for h in range(GROUP_HEADS):
        o_ref[:, _head_slice(h)] = (acc[h * bq:(h + 1) * bq, :] / l[h * bq:(h + 1) * bq, :]).astype(o_ref.dtype)


def _sel_rows(sel01, x):
    t = x.shape[0]
    r = _dot(sel01, jnp.concatenate(_split3(x), axis=0), NT)
    return r[:, :t] + r[:, t:2 * t] + r[:, 2 * t:]


def _attend_heads_t(qts, k_blk, vt_blk, bias_tiles, mask, carry):
    m, l, acc = carry
    bq = qts[0].shape[1]
    kb16 = k_blk.astype(BF16)
    vb16 = vt_blk.astype(BF16)
    parts = []
    for h in range(GROUP_HEADS):
        s = _dot(kb16[:, _head_slice(h)], qts[h]) + jnp.tile(bias_tiles[h], (1, bq // LANE))
        parts.append(s if mask is None else jnp.where(mask, s, NEG))
    s = jnp.concatenate(parts, axis=1)
    m_new = jnp.maximum(m, jnp.max(s, 0, keepdims=True))
    alpha = jnp.exp(m - m_new)
    p = jnp.exp(s - m_new)
    l = alpha * l + jnp.sum(p, 0, keepdims=True)
    pb = p.astype(BF16)
    pv = jnp.concatenate([_dot(vb16[_head_slice(h), :], pb[:, h * bq:(h + 1) * bq])
                          for h in range(GROUP_HEADS)], axis=1)
    return m_new, l, alpha * acc + pv


def _fox_prompt_kernel(qt_ref, k_ref, vt_ref, hs_ref, bf_ref, o_ref, lf_ref, bias_scr, *, l, bq):
    i = pl.program_id(1)

    @pl.when(i == 0)
    def _():
        ltri = _tril_ones(_FB, BF16)
        carry = jnp.zeros((1, H_SMALL), F32)
        for jb in range(l // _FB):
            rows = slice(jb * _FB, (jb + 1) * _FB)
            lf = _log_forget(hs_ref[rows, :], bf_ref[...])
            lf_ref[rows, :] = lf[:, S_F:S_F + GROUP_HEADS]
            loc = _tri_mm(ltri, lf, 3)
            nf = -(loc + carry)
            for h in range(GROUP_HEADS):
                bias_scr[h, rows, :] = jnp.broadcast_to(nf[:, S_F + h:S_F + h + 1], (_FB, LANE))
            carry = carry + loc[_FB - 1:_FB, :]

    qt = qt_ref[...]
    r_i, c_i = _iota2((bq, bq), 0), _iota2((bq, bq), 1)
    qts = [(qt[_head_slice(h), :] * (HEAD_DIM ** -0.5)).astype(BF16) for h in range(GROUP_HEADS)]

    def step(j, mask, carry):
        keys = pl.ds(pl.multiple_of(j * bq, bq), bq)
        biases = [bias_scr[h, keys, :] for h in range(GROUP_HEADS)]
        return _attend_heads_t(qts, k_ref[keys, :], vt_ref[:, keys], biases, mask, carry)

    init = (jnp.full((1, GROUP_HEADS * bq), NEG, F32), jnp.zeros((1, GROUP_HEADS * bq), F32),
            jnp.zeros((HEAD_DIM, GROUP_HEADS * bq), F32))
    carry = lax.fori_loop(0, i, lambda j, c: step(j, None, c), init)
    _, lsum, acc = step(i, r_i <= c_i, carry)
    o_t = acc / lsum
    o_t = jnp.concatenate([o_t[:, h * bq:(h + 1) * bq] for h in range(GROUP_HEADS)], axis=0)
    o_ref[...] = o_t.T.astype(o_ref.dtype)


def _fox_prompt(h3, hs3, q_t, v_t, bf_pad):
    bsz, l, _ = h3.shape
    bq = min(l, 4 * _FB)
    kern = functools.partial(_fox_prompt_kernel, l=l, bq=bq)
    return pl.pallas_call(
        kern,
        grid=(bsz, l // bq),
        in_specs=[pl.BlockSpec((None, GW, bq), lambda bi, i: (bi, 0, i)),
                  pl.BlockSpec((None, l, GW), lambda bi, i: (bi, 0, H_DK // GW)),
                  pl.BlockSpec((None, GW, l), lambda bi, i: (bi, 0, 0)),
                  pl.BlockSpec((None, l, H_SMALL), lambda bi, i: (bi, 0, 0)),
                  pl.BlockSpec((1, H_SMALL), lambda bi, i: (0, 0))],
        out_specs=[pl.BlockSpec((None, bq, GW), lambda bi, i: (bi, i, 0)),
                   pl.BlockSpec((None, l, GROUP_HEADS), lambda bi, i: (bi, 0, 0))],
        out_shape=[jax.ShapeDtypeStruct((bsz, l, GW), BF16),
                   jax.ShapeDtypeStruct((bsz, l, GROUP_HEADS), F32)],
        scratch_shapes=[pltpu.VMEM((GROUP_HEADS, l, LANE), F32)],
        compiler_params=_params(2),
        name="fox_prompt",
    )(q_t, h3, v_t, hs3, bf_pad)


def _fox_sample_kernel(qt_ref, kt_ref, vt_ref, hs_ref, bf_ref, ck_ref, cv_ref, clf_ref, o_ref, lf_ref, cum_scr,
                       *, l, p):
    nblk = p // _FB
    per = _SAMPLE_BK // _FB

    loc = _mm_x01(clf_ref[...].reshape(nblk * 8, _FB), _triu_ones(_FB, BF16), 3)
    carry = jnp.zeros((8, 1), F32)
    for jb in range(nblk):
        blk = loc[jb * 8:(jb + 1) * 8, :]
        cum_scr[jb // per, :, (jb % per) * _FB:(jb % per + 1) * _FB] = blk + carry
        carry = carry + blk[:, _FB - 1:_FB]
    total = carry

    lf = _log_forget(hs_ref[...], bf_ref[...])
    lf_ref[...] = lf[:, S_F:S_F + GROUP_HEADS]
    cum_r = _mm_x01(_sel_rows(_col_selector(8, H_SMALL, S_F).astype(BF16), lf), _triu_ones(l, BF16), 3)
    q = qt_ref[...].T
    r_i, c_i = _iota2((l, l), 0), _iota2((l, l), 1)
    qs = [(q[:, _head_slice(h)] * (HEAD_DIM ** -0.5)).astype(BF16) for h in range(GROUP_HEADS)]

    def body(j, carry):
        keys = pl.ds(pl.multiple_of(j * _SAMPLE_BK, _SAMPLE_BK), _SAMPLE_BK)
        suffix = total - cum_scr[j]
        return _attend_heads(qs, ck_ref[:, keys], cv_ref[:, keys], suffix, None, carry)

    carry = lax.fori_loop(0, p // _SAMPLE_BK, body, _attend_init(l))
    carry = _attend_heads(qs, kt_ref[...], vt_ref[...], -cum_r, c_i <= r_i, carry)
    _attend_store(o_ref, carry, l)


def _fox_sample(hs3, q_t, k_t, v_t, bf_pad, layer, ck_t, cv_t, clogf):
    bsz, l, _ = hs3.shape
    p = ck_t.shape[-1]
    clf = clogf.reshape(bsz, p // _FB, _FB, GROUP_HEADS).transpose(0, 1, 3, 2)
    clf = jnp.pad(clf, ((0, 0), (0, 0), (0, 8 - GROUP_HEADS), (0, 0)))
    kern = functools.partial(_fox_sample_kernel, l=l, p=p)
    return pl.pallas_call(
        kern,
        grid=(bsz,),
        in_specs=[pl.BlockSpec((None, GW, l), lambda bi: (bi, 0, 0)),
                  pl.BlockSpec((None, GW, l), lambda bi: (bi, 0, 0)),
                  pl.BlockSpec((None, GW, l), lambda bi: (bi, 0, 0)),
                  pl.BlockSpec((None, l, H_SMALL), lambda bi: (bi, 0, 0)),
                  pl.BlockSpec((1, H_SMALL), lambda bi: (0, 0)),
                  pl.BlockSpec((None, None, GW, p), lambda bi: (layer, bi, 0, 0)),
                  pl.BlockSpec((None, None, GW, p), lambda bi: (layer, bi, 0, 0)),
                  pl.BlockSpec((None, p // _FB, 8, _FB), lambda bi: (bi, 0, 0, 0))],
        out_specs=[pl.BlockSpec((None, l, GW), lambda bi: (bi, 0, 0)),
                   pl.BlockSpec((None, l, GROUP_HEADS), lambda bi: (bi, 0, 0))],
        out_shape=[jax.ShapeDtypeStruct((bsz, l, GW), BF16),
                   jax.ShapeDtypeStruct((bsz, l, GROUP_HEADS), F32)],
        scratch_shapes=[pltpu.VMEM((p // _SAMPLE_BK, 8, _SAMPLE_BK), F32)],
        compiler_params=_params(1),
        name="fox_sample",
    )(q_t, k_t, v_t, hs3, bf_pad, ck_t, cv_t, clf)


_R_G, _R_E, _R_ROWS = 0, 8, 32
_MOE_EXPERTS_PER_STEP = 2


def _ffn_kernel(oa_ref, ob_ref, oc_ref, od_ref, x_ref, wo_ref, g1_ref, b1_ref, wrh_ref, wrl_ref, brc_ref, wg_ref,
                wu_ref, wd_ref, g2_ref, b2_ref, y_ref, comb_scr, xb_scr, x1_scr):
    e = pl.program_id(1)
    tm = x_ref.shape[0]
    lane = _iota2((tm, LANE), 1)

    @pl.when(e == 0)
    def _():
        mix = jnp.concatenate([ref[...] for ref in (oa_ref, ob_ref, oc_ref, od_ref)], axis=1)
        x1 = _layer_norm(DN_ALPHA * x_ref[...] + _dot(mix, wo_ref[...]), g1_ref[...], b1_ref[...])
        x1_scr[...] = x1
        y_ref[...] = jnp.zeros_like(y_ref)
        x_hi, x_lo = _split2(x1)
        xb_scr[...] = x_hi
        wr_hi = wrh_ref[...]
        r = _dot(jnp.concatenate([wr_hi, wrl_ref[...]], axis=0), x_hi, NT)
        lt = r[:_R_ROWS] + r[_R_ROWS:] + _dot(wr_hi, x_lo, NT) + brc_ref[...]
        gl = lt[_R_G:_R_G + N_EXPERT_GROUPS]
        grow = _iota2(gl.shape, 0)
        ge = jnp.exp(gl - jnp.max(gl, 0, keepdims=True))
        pg = ge / jnp.sum(ge, 0, keepdims=True)
        gp = jnp.max(pg, 0, keepdims=True)
        gi = jnp.min(jnp.where(pg == gp, grow, N_EXPERT_GROUPS), 0, keepdims=True)
        le = lt[_R_E:_R_E + N_EXPERTS]
        erow = _iota2(le.shape, 0)
        emask = (erow >> 2) == gi
        el = jnp.where(emask, le, NEG)
        ee = jnp.exp(el - jnp.max(el, 0, keepdims=True))
        ep = ee / jnp.sum(ee, 0, keepdims=True)
        m1 = jnp.max(jnp.where(emask, ep, -1.0), 0, keepdims=True)
        i1 = jnp.min(jnp.where(jnp.logical_and(emask, ep == m1), erow, N_EXPERTS), 0, keepdims=True)
        rest = jnp.logical_and(emask, erow != i1)
        m2 = jnp.max(jnp.where(rest, ep, -1.0), 0, keepdims=True)
        i2 = jnp.min(jnp.where(jnp.logical_and(rest, ep == m2), erow, N_EXPERTS), 0, keepdims=True)
        den = m1 + m2
        comb_t = jnp.where(erow == i1, gp * m1 / den, jnp.where(erow == i2, gp * m2 / den, 0.0))
        pieces = jnp.concatenate(_split3(comb_t), axis=0)
        sel = ((_iota2((3 * N_EXPERTS, LANE), 0) & (N_EXPERTS - 1)) == _iota2((3 * N_EXPERTS, LANE), 1)).astype(BF16)
        comb_scr[...] = _dot(pieces, sel, TN)

    xb = xb_scr[...]
    comb = comb_scr[...]
    hidden = []
    for k in range(_MOE_EXPERTS_PER_STEP):
        ce = jnp.sum(jnp.where(lane == e * _MOE_EXPERTS_PER_STEP + k, comb, 0.0), -1, keepdims=True)
        hh = _silu(_mm(xb, wg_ref[k])) * _mm(xb, wu_ref[k])
        hidden.append((hh * ce).astype(BF16))
    hidden = jnp.concatenate(hidden, axis=1)
    w_down = wd_ref[...].reshape(_MOE_EXPERTS_PER_STEP * D_EXPERT, D_MODEL)
    y_ref[...] += _mm(hidden, w_down)

    @pl.when(e == pl.num_programs(1) - 1)
    def _():
        y_ref[...] = _layer_norm(DN_ALPHA * x1_scr[...] + y_ref[...], g2_ref[...], b2_ref[...])


def _ffn(mixers, x2d, w_out_bf16, g1, b1, wr_hi, wr_lo, br_col, layer, wg, wu, wd, g2, b2):
    t = x2d.shape[0]
    tm = min(t, 1024)
    ne = _MOE_EXPERTS_PER_STEP
    mix = pl.BlockSpec((tm, GW), lambda i, e: (i, 0))
    vec = pl.BlockSpec((1, D_MODEL), lambda i, e: (0, 0))
    return pl.pallas_call(
        _ffn_kernel,
        grid=(t // tm, N_EXPERTS // ne),
        in_specs=[mix, mix, mix, mix,
                  pl.BlockSpec((tm, D_MODEL), lambda i, e: (i, 0)),
                  pl.BlockSpec((D_MODEL, D_MODEL), lambda i, e: (0, 0)), vec, vec,
                  pl.BlockSpec((_R_ROWS, D_MODEL), lambda i, e: (0, 0)),
                  pl.BlockSpec((_R_ROWS, D_MODEL), lambda i, e: (0, 0)),
                  pl.BlockSpec((_R_ROWS, 1), lambda i, e: (0, 0)),
                  pl.BlockSpec((None, ne, D_MODEL, D_EXPERT), lambda i, e: (layer, e, 0, 0)),
                  pl.BlockSpec((None, ne, D_MODEL, D_EXPERT), lambda i, e: (layer, e, 0, 0)),
                  pl.BlockSpec((None, ne, D_EXPERT, D_MODEL), lambda i, e: (layer, e, 0, 0)), vec, vec],
        out_specs=pl.BlockSpec((tm, D_MODEL), lambda i, e: (i, 0)),
        out_shape=jax.ShapeDtypeStruct((t, D_MODEL), F32),
        scratch_shapes=[pltpu.VMEM((tm, LANE), F32), pltpu.VMEM((tm, D_MODEL), BF16),
                        pltpu.VMEM((tm, D_MODEL), F32)],
        compiler_params=_params(2),
        name="ffn",
    )(*mixers, x2d, w_out_bf16, g1, b1, wr_hi, wr_lo, br_col, wg, wu, wd, g2, b2)


def _prep_layer(lp):
    w = lp['w_in_t']
    c0 = CONV_COLS
    c1 = c0 + GDN_COLS
    c2 = c1 + RWKV_COLS
    main = jnp.concatenate([w[c0:c0 + 3 * GW], w[c1:c1 + 3 * GW], w[c2 + GW:c2 + 2 * GW],
                            w[0:GW], w[GW:2 * GW], w[c0 + 3 * GW:c0 + 4 * GW]], axis=0)
    kv_t = w[c2:c2 + 3 * GW]
    small = jnp.concatenate([lp['w_in_small'].T,
                             jnp.zeros((H_SMALL - LORA - 3 * GROUP_HEADS, D_MODEL), F32)], axis=0)
    gap = jnp.zeros((_R_E - N_EXPERT_GROUPS, D_MODEL), F32)
    tail = jnp.zeros((_R_ROWS - _R_E - N_EXPERTS, D_MODEL), F32)
    wr_t = jnp.concatenate([lp['router_g_w'].T, gap, lp['router_e_w'].T, tail], axis=0)
    wr_hi = wr_t.astype(BF16)
    wr_lo = (wr_t - wr_hi.astype(F32)).astype(BF16)
    br = jnp.concatenate([lp['router_g_b'], gap[:, 0], lp['router_e_b'], tail[:, 0]])[:, None]
    bf_pad = jnp.zeros((1, H_SMALL), F32).at[0, S_F:S_F + GROUP_HEADS].set(lp['fox_b_f'])
    return dict(w_main=main, w_small=small, w_kv_t=kv_t, wr_hi=wr_hi, wr_lo=wr_lo, br=br, bf_pad=bf_pad)


def _trunk_layer(x, lp, pp, layer, big, conv_buf, gdn_buf, gdn_s, rw_shift, rw_s, fox_cache):
    bsz, l, d = x.shape
    x2d = x.reshape(bsz * l, d)
    h3, hs3, q_t, k_t, v_t = _inproj(x, pp['w_main'], pp['w_small'], pp['w_kv_t'])

    o_a, new_conv = _conv_mixer(h3, conv_buf, lp['conv_w'], lp['conv_b'][None], lp['conv_ln_g'][None],
                                lp['conv_ln_b'][None])
    o_b, new_gdn_s, new_gdn_buf = _gdn_mixer(h3, hs3, gdn_buf, gdn_s, lp['gdn_conv_w'], lp['gdn_a_log'],
                                             lp['gdn_dt_bias'], lp['gdn_norm_w'])
    o_c, new_rw_s, new_shift = _rwkv_mixer(h3, hs3, rw_shift, rw_s, lp)
    if fox_cache is None:
        o_d, logf = _fox_prompt(h3, hs3, q_t, v_t, pp['bf_pad'])
    else:
        o_d, logf = _fox_sample(hs3, q_t, k_t, v_t, pp['bf_pad'], layer, *fox_cache)

    mixers = [o.reshape(bsz * l, GW) for o in (o_a, o_b, o_c, o_d)]
    x2 = _ffn(mixers, x2d, big['w_out'][layer], lp['ln1_g'][None], lp['ln1_b'][None], pp['wr_hi'], pp['wr_lo'],
              pp['br'], layer, *big['experts'], lp['ln2_g'][None], lp['ln2_b'][None])
    return x2.reshape(bsz, l, d), (new_conv, new_gdn_buf, new_gdn_s, new_shift, new_rw_s, k_t, v_t, logf)


def _untranspose(xs_t):
    bsz, _, l = xs_t[0].shape
    return jnp.stack(xs_t).reshape(len(xs_t), bsz, GROUP_HEADS, HEAD_DIM, l).transpose(0, 1, 4, 2, 3)


_LAYER_KEYS = ('w_in_t', 'w_in_small', 'conv_w', 'conv_b', 'conv_ln_g', 'conv_ln_b', 'gdn_conv_w', 'gdn_a_log', 'gdn_dt_bias',
               'gdn_norm_w', 'rwkv_mu', 'rwkv_w0', 'rwkv_w_up', 'rwkv_a0', 'rwkv_a_up', 'rwkv_g_up', 'rwkv_k_k',
               'rwkv_k_a', 'rwkv_r_k', 'rwkv_lnx_g', 'rwkv_lnx_b', 'fox_b_f', 'ln1_g', 'ln1_b',
               'router_g_w', 'router_g_b', 'router_e_w', 'router_e_b', 'ln2_g', 'ln2_b')


def kernel(x_prompt, x_sample, cache_fox_k, cache_fox_v, cache_fox_logf, state_conv, state_gdn_conv, state_gdn, state_rwkv_shift, state_rwkv, w_in, conv_w, conv_b, conv_ln_g, conv_ln_b, gdn_conv_w, gdn_a_log, gdn_dt_bias, gdn_norm_w, rwkv_mu, rwkv_w0, rwkv_w_up, rwkv_a0, rwkv_a_up, rwkv_g_up, rwkv_k_k, rwkv_k_a, rwkv_r_k, rwkv_lnx_g, rwkv_lnx_b, fox_b_f, w_out, ln1_g, ln1_b, router_g_w, router_g_b, router_e_w, router_e_b, exp_w_gate, exp_w_up, exp_w_down, ln2_g, ln2_b):
    w_in_t = w_in.astype(BF16).transpose(0, 2, 1)
    c1 = CONV_COLS + GDN_COLS
    c2 = c1 + RWKV_COLS
    w_in_small = jnp.concatenate([w_in[:, :, c1 + 3 * GW:c2], w_in[:, :, CONV_COLS + 4 * GW:c1],
                                  w_in[:, :, c2 + 3 * GW:]], axis=2)
    weights = dict(zip(_LAYER_KEYS, (w_in_t, w_in_small, conv_w, conv_b, conv_ln_g, conv_ln_b, gdn_conv_w, gdn_a_log,
                                     gdn_dt_bias, gdn_norm_w, rwkv_mu, rwkv_w0, rwkv_w_up, rwkv_a0, rwkv_a_up,
                                     rwkv_g_up, rwkv_k_k, rwkv_k_a, rwkv_r_k, rwkv_lnx_g, rwkv_lnx_b, fox_b_f,
                                     ln1_g, ln1_b, router_g_w, router_g_b, router_e_w, router_e_b,
                                     ln2_g, ln2_b)))
    big = dict(w_out=w_out.astype(BF16), experts=(exp_w_gate, exp_w_up, exp_w_down))
    depth, bs, past = cache_fox_k.shape[:3]
    ck_t = cache_fox_k.transpose(0, 1, 3, 4, 2).reshape(depth, bs, GW, past)
    cv_t = cache_fox_v.transpose(0, 1, 3, 4, 2).reshape(depth, bs, GW, past)
    xp, xs = x_prompt, x_sample
    bp = x_prompt.shape[0]
    outs_p = [[] for _ in range(8)]
    outs_s = [[] for _ in range(8)]
    for l in range(DEPTH):
        lp = {k: v[l] for k, v in weights.items()}
        pp = _prep_layer(lp)
        xp, st_p = _trunk_layer(xp, lp, pp, l, big,
                                jnp.zeros((bp, CONV_WIDTH - 1, GW), F32),
                                jnp.zeros((bp, GDN_CONV_WIDTH - 1, 3 * GW), F32),
                                jnp.zeros((bp, GROUP_HEADS, HEAD_DIM, HEAD_DIM), F32),
                                jnp.zeros((bp, RWKV_COLS), F32),
                                jnp.zeros((bp, GROUP_HEADS, HEAD_DIM, HEAD_DIM), F32),
                                None)
        xs, st_s = _trunk_layer(xs, lp, pp, l, big, state_conv[l], state_gdn_conv[l], state_gdn[l],
                                state_rwkv_shift[l], state_rwkv[l], (ck_t, cv_t, cache_fox_logf[l]))
        for i in range(8):
            outs_p[i].append(st_p[i])
            outs_s[i].append(st_s[i])

    def assemble(outs):
        conv, gdn_buf, gdn_s, shift, rw_s = (jnp.stack(o) for o in outs[:5])
        return conv, gdn_buf, gdn_s, shift, rw_s, _untranspose(outs[5]), _untranspose(outs[6]), jnp.stack(outs[7])

    return (xp, xs, *assemble(outs_p), *assemble(outs_s))
```

```python
import functools

import jax
import jax.numpy as jnp
from jax import lax
from jax.experimental import pallas as pl
from jax.experimental.pallas import tpu as pltpu

F32 = jnp.float32
BF16 = jnp.bfloat16

D_MODEL = 1024
DEPTH = 2
CHUNK = 64
HEAD_DIM = 64
GW = D_MODEL // 4
GROUP_HEADS = GW // HEAD_DIM
CONV_WIDTH = 31
GDN_CONV_WIDTH = 4
RWKV_W_LORA = 16
RWKV_A_LORA = 16
RWKV_G_LORA = 32
N_EXPERT_GROUPS = 4
EXPERTS_PER_GROUP = 4
N_EXPERTS = N_EXPERT_GROUPS * EXPERTS_PER_GROUP
D_EXPERT = 256
DN_ALPHA = (2 * DEPTH) ** 0.25
LN_EPS = 1e-5
GN_EPS = 64e-5

CONV_COLS = 2 * GW
GDN_COLS = 4 * GW + 2 * GROUP_HEADS
RWKV_COLS = 3 * GW + RWKV_W_LORA + RWKV_A_LORA + RWKV_G_LORA
FOX_COLS = 3 * GW + GROUP_HEADS
LORA = RWKV_W_LORA + RWKV_A_LORA + RWKV_G_LORA

H_BQKV, H_CRKV, H_DK, H_AVAL, H_AGATE, H_BZ = 0, 768, 1536, 1792, 2048, 2304
H_MAIN = 2560
QKV_ROWS = 3 * GW
S_LORA, S_A, S_B, S_F = 0, 64, 68, 72
H_SMALL = 128
LANE = 128

VMEM_LIMIT = 56 * 1024 * 1024

NN = (((1,), (0,)), ((), ()))
NT = (((1,), (1,)), ((), ()))
TN = (((0,), (0,)), ((), ()))
NEG = -1e30


def _dot(a, b, dims=NN):
    return lax.dot_general(a, b, dims, preferred_element_type=F32)


def _mm(a, b, dims=NN):
    return _dot(a.astype(BF16), b.astype(BF16), dims)


def _sigmoid(x):
    return 0.5 * (jnp.tanh(0.5 * x) + 1.0)


def _silu(x):
    return x * _sigmoid(x)


def _softplus(x):
    return jnp.maximum(x, 0.0) + jnp.log(1.0 + jnp.exp(-jnp.abs(x)))


def _iota2(shape, dim):
    return lax.broadcasted_iota(jnp.int32, shape, dim)


def _tril_ones(n, dtype=F32):
    r, c = _iota2((n, n), 0), _iota2((n, n), 1)
    return (r >= c).astype(dtype)


def _triu_ones(n, dtype=F32):
    r, c = _iota2((n, n), 0), _iota2((n, n), 1)
    return (r <= c).astype(dtype)


def _col_selector(rows, width, col0):
    r, c = _iota2((rows, width), 0), _iota2((rows, width), 1)
    return (c == r + col0).astype(F32)


def _layer_norm(x, g, b):
    mu = jnp.mean(x, axis=-1, keepdims=True)
    xc = x - mu
    var = jnp.mean(xc * xc, axis=-1, keepdims=True)
    return xc * lax.rsqrt(var + LN_EPS) * g + b


def _split2(x):
    hi = x.astype(BF16)
    lo = (x - hi.astype(F32)).astype(BF16)
    return hi, lo


def _split3(x):
    hi = x.astype(BF16)
    r = x - hi.astype(F32)
    mid = r.astype(BF16)
    lo = (r - mid.astype(F32)).astype(BF16)
    return hi, mid, lo


class _Wide:
    def __init__(self):
        self.i = _iota2((CHUNK, GW), 0)
        lane = _iota2((CHUNK, GW), 1)
        self.j = lane & (HEAD_DIM - 1)
        self.head = lane >> 6
        self.incl = self.i >= self.j
        self.strict = self.i > self.j
        self.eye = self.i == self.j
        r, c = _iota2((GW, GW), 0), _iota2((GW, GW), 1)
        self.block_diag = (r >> 6) == (c >> 6)


def _head_block_diag(x, w):
    t = jnp.concatenate([x] * GROUP_HEADS, axis=0)
    return jnp.where(w.block_diag, t, jnp.zeros_like(t))


def _collapse_heads(full, w):
    out = jnp.zeros((HEAD_DIM, GW), F32)
    for h in range(GROUP_HEADS):
        out = out + jnp.where(w.head == h, full[h * HEAD_DIM:(h + 1) * HEAD_DIM, :], 0.0)
    return out


def _lhs3(a, precise=False):
    if not precise:
        return None, a.astype(BF16)
    hi, lo = _split2(a)
    return jnp.concatenate([hi, lo], axis=0), hi


def _rhs3(b, w=None, precise=False):
    if not precise:
        hi = b.astype(BF16)
        return (hi if w is None else _head_block_diag(hi, w)), None
    hi, lo = _split2(b)
    if w is None:
        return hi, lo
    return _head_block_diag(hi, w), _head_block_diag(lo, w)


def _mm3(lhs, rhs, dims=NN):
    cat, hi = lhs
    bh, bl = rhs
    if cat is None or bl is None:
        return _dot(hi, bh, dims)
    m = hi.shape[0]
    r = _dot(cat, bh, dims)
    return r[:m] + r[m:] + _dot(hi, bl, dims)


def _mm3_tn(a, b, precise=False):
    if not precise:
        return _dot(a.astype(BF16), b.astype(BF16), TN)
    ah, al = _split2(a)
    bh, bl = _split2(b)
    m = a.shape[1]
    r = _dot(jnp.concatenate([ah, al], axis=1), bh, TN)
    return r[:m] + r[m:] + _dot(ah, bl, TN)


def _mm_x01(a, m01, pieces=2):
    m = a.shape[0]
    r = _dot(jnp.concatenate(_split3(a) if pieces == 3 else _split2(a), axis=0), m01)
    return sum(r[i * m:(i + 1) * m] for i in range(pieces))


def _tri_mm(tri01, x, pieces=2):
    n = x.shape[1]
    r = _dot(tri01, jnp.concatenate(_split3(x) if pieces == 3 else _split2(x), axis=1))
    return sum(r[:, i * n:(i + 1) * n] for i in range(pieces))


def _unit_lower_inverse_wide(mats, w):
    eye = w.eye.astype(F32)
    same8 = (w.i >> 3) == (w.j >> 3)
    ds = [jnp.where(same8, a, 0.0) for a in mats]
    d2 = [_mm3(_lhs3(d), _rhs3(d, w)) for d in ds]
    d2_r = [_rhs3(x, w) for x in d2]
    d4 = [_mm3(_lhs3(x), r) for x, r in zip(d2, d2_r)]
    imd = [eye - d for d in ds]
    p1 = [i + _mm3(_lhs3(i), r) for i, r in zip(imd, d2_r)]
    ts = [p + _mm3(_lhs3(p), _rhs3(x, w)) for p, x in zip(p1, d4)]
    for s in range(3, 6):
        big = (w.i >> (s + 1)) == (w.j >> (s + 1))
        small = (w.i >> s) == (w.j >> s)
        sel = jnp.logical_and(big, jnp.logical_not(small))
        inner = [_mm3(_lhs3(jnp.where(sel, a, 0.0)), _rhs3(t, w)) for a, t in zip(mats, ts)]
        ts = [t - _mm3(_lhs3(t), _rhs3(x, w)) for t, x in zip(ts, inner)]
    return ts


def _params(n_axes):
    return pltpu.CompilerParams(dimension_semantics=("arbitrary",) * n_axes,
                                vmem_limit_bytes=VMEM_LIMIT)


def _heads_to_wide(s):
    b = s.shape[0]
    return s.transpose(0, 2, 1, 3).reshape(b, HEAD_DIM, GW)


def _wide_to_heads(s):
    b = s.shape[0]
    return s.reshape(b, HEAD_DIM, GROUP_HEADS, HEAD_DIM).transpose(0, 2, 1, 3)


def _ones_block_diag():
    r, c = jnp.arange(GW)[:, None], jnp.arange(GW)[None, :]
    return ((r // HEAD_DIM) == (c // HEAD_DIM)).astype(BF16)


def _lane_spread(col0):
    r, c = jnp.arange(H_SMALL)[:, None], jnp.arange(GW)[None, :]
    return ((r - col0) == (c // HEAD_DIM)).astype(BF16)


_INPROJ_ROWS = 1024


def _inproj_kernel(x_ref, wm_ref, wsh_ref, wsl_ref, wkv_ref, h_ref, hs_ref, qt_ref, kt_ref, vt_ref):
    nb, tm, _ = x_ref.shape
    rows = nb * tm
    x_hi, x_lo = _split2(x_ref[...].reshape(rows, D_MODEL))
    h_ref[...] = _dot(x_hi, wm_ref[...], NT).reshape(nb, tm, H_MAIN)
    r = _dot(jnp.concatenate([x_hi, x_lo], axis=0), wsh_ref[...], NT)
    hs_ref[...] = (r[:rows] + r[rows:] + _dot(x_hi, wsl_ref[...], NT)).reshape(nb, tm, H_SMALL)
    qkv = _dot(wkv_ref[...], x_hi, NT)
    for b in range(nb):
        cols = slice(b * tm, (b + 1) * tm)
        qt_ref[b] = qkv[:GW, cols]
        kt_ref[b] = qkv[GW:2 * GW, cols]
        vt_ref[b] = qkv[2 * GW:, cols]


def _inproj(x3, w_main_t, w_small_t, w_kv_t):
    bsz, l, _ = x3.shape
    tm = min(l, _INPROJ_ROWS)
    nb = min(bsz, _INPROJ_ROWS // tm)
    const = lambda shape: pl.BlockSpec(shape, lambda bi, i: (0, 0))
    ws_hi = w_small_t.astype(BF16)
    ws_lo = (w_small_t - ws_hi.astype(F32)).astype(BF16)
    kv_spec = pl.BlockSpec((nb, GW, tm), lambda bi, i: (bi, 0, i))
    return pl.pallas_call(
        _inproj_kernel,
        grid=(bsz // nb, l // tm),
        in_specs=[pl.BlockSpec((nb, tm, D_MODEL), lambda bi, i: (bi, i, 0)),
                  const((H_MAIN, D_MODEL)), const((H_SMALL, D_MODEL)), const((H_SMALL, D_MODEL)),
                  const((QKV_ROWS, D_MODEL))],
        out_specs=[pl.BlockSpec((nb, tm, H_MAIN), lambda bi, i: (bi, i, 0)),
                   pl.BlockSpec((nb, tm, H_SMALL), lambda bi, i: (bi, i, 0)), kv_spec, kv_spec, kv_spec],
        out_shape=[jax.ShapeDtypeStruct((bsz, l, H_MAIN), F32), jax.ShapeDtypeStruct((bsz, l, H_SMALL), F32)]
        + [jax.ShapeDtypeStruct((bsz, GW, l), F32)] * 3,
        compiler_params=_params(2),
        name="inproj",
    )(x3, w_main_t, ws_hi, ws_lo, w_kv_t)


_HALO = 32
_SUBLANES = 8


def _conv_kernel(val_ref, gate_ref, buf_ref, cw_ref, cb_ref, g_ref, b_ref, o_ref, nbuf_ref, win_ref, sh_ref, *, tl):
    i = pl.program_id(1)
    pad = _HALO - (CONV_WIDTH - 1)
    span = tl + _HALO - _SUBLANES

    @pl.when(i == 0)
    def _():
        win_ref[0:8, :] = jnp.zeros((8, GW), F32)
        win_ref[pad:_HALO, :] = buf_ref[...]

    u = val_ref[...] * _sigmoid(gate_ref[...])
    win_ref[_HALO:_HALO + tl, :] = u
    for s in range(1, _SUBLANES):
        sh_ref[s, :, :] = win_ref[s:s + span, :]
    sub = min(tl, 64)
    for r0 in range(0, tl, sub):
        acc = jnp.zeros((sub, GW), F32)
        for j in range(CONV_WIDTH):
            s, base = (pad + j) % _SUBLANES, (pad + j) // _SUBLANES * _SUBLANES + r0
            rows = win_ref[base:base + sub, :] if s == 0 else sh_ref[s, base:base + sub, :]
            acc = acc + cw_ref[j:j + 1, :] * rows
        y = _layer_norm(acc + cb_ref[...], g_ref[...], b_ref[...])
        o_ref[r0:r0 + sub, :] = _silu(y).astype(o_ref.dtype)
    tail = win_ref[tl:tl + _HALO, :]
    win_ref[0:_HALO, :] = tail

    @pl.when(i == pl.num_programs(1) - 1)
    def _():
        nbuf_ref[...] = win_ref[pad:_HALO, :]


def _conv_mixer(h3, buf, cw, cb, g, b):
    bsz, l, _ = h3.shape
    tl = min(l, 512)
    kern = functools.partial(_conv_kernel, tl=tl)
    full = lambda shape: pl.BlockSpec(shape, lambda bi, i: (0,) * len(shape))
    return pl.pallas_call(
        kern,
        grid=(bsz, l // tl),
        in_specs=[pl.BlockSpec((None, tl, GW), lambda bi, i: (bi, i, H_AVAL // GW)),
                  pl.BlockSpec((None, tl, GW), lambda bi, i: (bi, i, H_AGATE // GW)),
                  pl.BlockSpec((None, CONV_WIDTH - 1, GW), lambda bi, i: (bi, 0, 0)),
                  full((CONV_WIDTH, GW)), full((1, GW)), full((1, GW)), full((1, GW))],
        out_specs=[pl.BlockSpec((None, tl, GW), lambda bi, i: (bi, i, 0)),
                   pl.BlockSpec((None, CONV_WIDTH - 1, GW), lambda bi, i: (bi, 0, 0))],
        out_shape=[jax.ShapeDtypeStruct((bsz, l, GW), BF16),
                   jax.ShapeDtypeStruct((bsz, CONV_WIDTH - 1, GW), F32)],
        scratch_shapes=[pltpu.VMEM((tl + _HALO, GW), F32),
                        pltpu.VMEM((_SUBLANES, tl + _HALO - _SUBLANES, GW), F32)],
        compiler_params=_params(2),
        name="conv_mixer",
    )(h3, h3, buf, cw, cb, g, b)


def _gdn_kernel(qkv_ref, z_ref, hs_ref, buf_ref, s0_ref, cw_ref, alog_ref, dtb_ref, nw_ref, ones_ref, sela_ref,
                selb_ref, o_ref, sout_ref, nbuf_ref, win_ref, s_scr, *, nb, nc):
    c = pl.program_id(1)
    n = CHUNK
    rows = nc * n
    kw = GDN_CONV_WIDTH - 1
    items = [(b, ci) for b in range(nb) for ci in range(nc)]

    @pl.when(c == 0)
    def _():
        for b in range(nb):
            win_ref[b, 0:8, :] = jnp.zeros((8, 3 * GW), F32)
            win_ref[b, 8 - kw:8, :] = buf_ref[b]
        s_scr[...] = s0_ref[...]

    qkvs = []
    for b in range(nb):
        win_ref[b, 8:8 + rows, :] = qkv_ref[b]
        conv = jnp.zeros((rows, 3 * GW), F32)
        for j in range(GDN_CONV_WIDTH):
            conv = conv + cw_ref[j:j + 1, :] * win_ref[b, 8 - kw + j:8 - kw + j + rows, :]
        tail = win_ref[b, rows:rows + 8, :]
        win_ref[b, 0:8, :] = tail
        qkvs.append(_silu(conv))

    w = _Wide()
    ones_bd = ones_ref[...]
    tril = _tril_ones(n, BF16)
    sl = lambda ci: slice(ci * n, (ci + 1) * n)
    qs = [qkvs[b][sl(ci), 0:GW] for b, ci in items]
    ks = [qkvs[b][sl(ci), GW:2 * GW] for b, ci in items]
    vs = [qkvs[b][sl(ci), 2 * GW:3 * GW] for b, ci in items]
    sss = [_mm_x01(jnp.concatenate([q * q, k * k], axis=0), ones_bd) for q, k in zip(qs, ks)]
    qs = [q * lax.rsqrt(ss[:n] + 1e-6) * (HEAD_DIM ** -0.5) for q, ss in zip(qs, sss)]
    ks = [k * lax.rsqrt(ss[n:] + 1e-6) for k, ss in zip(ks, sss)]

    hss = [hs_ref[b, sl(ci), :] for b, ci in items]
    gs = [_mm_x01(-jnp.exp(alog_ref[...]) * _softplus(hs + dtb_ref[...]), sela_ref[...]) for hs in hss]
    betas = [_mm_x01(_sigmoid(hs), selb_ref[...]) for hs in hss]
    gcs = [_tri_mm(tril, g) for g in gs]
    grs = [jnp.sum(jnp.where(w.eye, gc, 0.0), axis=0, keepdims=True) for gc in gcs]
    decays = [jnp.exp(jnp.where(w.incl, gc - gr, NEG)) for gc, gr in zip(gcs, grs)]

    kbs = [k * beta for k, beta in zip(ks, betas)]
    grams = [_mm3(_lhs3(jnp.concatenate([kb, q], axis=0)), _rhs3(k, w), NT)
             for kb, q, k in zip(kbs, qs, ks)]
    amats = [jnp.where(w.strict, gram[:n] * decay, 0.0) for gram, decay in zip(grams, decays)]
    qks = [jnp.where(w.incl, gram[n:] * decay, 0.0) for gram, decay in zip(grams, decays)]
    t_ls = [_lhs3(t) for t in _unit_lower_inverse_wide(amats, w)]
    egs = [jnp.exp(gc) for gc in gcs]
    xvs = [_mm3(t_l, _rhs3(v * beta, w)) for t_l, v, beta in zip(t_ls, vs, betas)]
    xks = [_mm3(t_l, _rhs3(kb * eg, w)) for t_l, kb, eg in zip(t_ls, kbs, egs)]
    xq_ls = [_lhs3(jnp.concatenate([xk, q * eg], axis=0)) for xk, q, eg in zip(xks, qs, egs)]
    qk_ls = [_lhs3(qk) for qk in qks]
    gls = [gc[n - 1:n, :] for gc in gcs]
    kds = [k * jnp.exp(gl - gc) for k, gl, gc in zip(ks, gls, gcs)]

    states = [s_scr[b] for b in range(nb)]
    outs = {}
    for ci in range(nc):
        for b in range(nb):
            it = b * nc + ci
            s = states[b]
            rs = _mm3(xq_ls[it], _rhs3(s, w))
            u = xvs[it] - rs[:n]
            outs[it] = rs[n:] + _mm3(qk_ls[it], _rhs3(u, w))
            states[b] = s * jnp.exp(gls[it]) + _collapse_heads(_mm3_tn(kds[it], u), w)
    for b in range(nb):
        s_scr[b] = states[b]

    for it, (b, ci) in enumerate(items):
        o = outs[it]
        ms = _mm_x01(o * o, ones_bd) * (1.0 / HEAD_DIM)
        o = o * lax.rsqrt(ms + 1e-6) * nw_ref[...] * _silu(z_ref[b, sl(ci), :])
        o_ref[b, sl(ci), :] = o.astype(o_ref.dtype)

    @pl.when(c == pl.num_programs(1) - 1)
    def _():
        sout_ref[...] = s_scr[...]
        for b in range(nb):
            nbuf_ref[b] = win_ref[b, 8 - kw:8, :]


def _recurrent_tiling(bsz, l):
    nc = min(l // CHUNK, 4)
    nb = 2 if nc > 1 else min(bsz, 4)
    return nb, nc


def _gdn_mixer(h3, hs3, buf, s0, cw, a_log, dt_bias, norm_w):
    bsz, l, _ = h3.shape
    nb, nc = _recurrent_tiling(bsz, l)
    rows = nc * CHUNK
    pad_r = lambda v, off: jnp.zeros((1, H_SMALL), F32).at[0, off:off + GROUP_HEADS].set(v)
    full = lambda shape: pl.BlockSpec(shape, lambda bi, i: (0,) * len(shape))
    o, s_new, new_buf = pl.pallas_call(
        functools.partial(_gdn_kernel, nb=nb, nc=nc),
        grid=(bsz // nb, l // rows),
        in_specs=[pl.BlockSpec((nb, rows, 3 * GW), lambda bi, i: (bi, i, H_BQKV // (3 * GW))),
                  pl.BlockSpec((nb, rows, GW), lambda bi, i: (bi, i, H_BZ // GW)),
                  pl.BlockSpec((nb, rows, H_SMALL), lambda bi, i: (bi, i, 0)),
                  pl.BlockSpec((nb, GDN_CONV_WIDTH - 1, 3 * GW), lambda bi, i: (bi, 0, 0)),
                  pl.BlockSpec((nb, HEAD_DIM, GW), lambda bi, i: (bi, 0, 0)),
                  full((GDN_CONV_WIDTH, 3 * GW)), full((1, H_SMALL)), full((1, H_SMALL)), full((1, GW)),
                  full((GW, GW)), full((H_SMALL, GW)), full((H_SMALL, GW))],
        out_specs=[pl.BlockSpec((nb, rows, GW), lambda bi, i: (bi, i, 0)),
                   pl.BlockSpec((nb, HEAD_DIM, GW), lambda bi, i: (bi, 0, 0)),
                   pl.BlockSpec((nb, GDN_CONV_WIDTH - 1, 3 * GW), lambda bi, i: (bi, 0, 0))],
        out_shape=[jax.ShapeDtypeStruct((bsz, l, GW), BF16),
                   jax.ShapeDtypeStruct((bsz, HEAD_DIM, GW), F32),
                   jax.ShapeDtypeStruct((bsz, GDN_CONV_WIDTH - 1, 3 * GW), F32)],
        scratch_shapes=[pltpu.VMEM((nb, rows + 8, 3 * GW), F32), pltpu.VMEM((nb, HEAD_DIM, GW), F32)],
        compiler_params=_params(2),
        name="gdn_mixer",
    )(h3, h3, hs3, buf, _heads_to_wide(s0), cw, pad_r(a_log, S_A), pad_r(dt_bias, S_A),
      jnp.tile(norm_w, GROUP_HEADS)[None, :], _ones_block_diag(), _lane_spread(S_A), _lane_spread(S_B))
    return o, _wide_to_heads(s_new), new_buf


def _rwkv_kernel(rkv_ref, hs_ref, sh_ref, shs_ref, s0_ref, mu_ref, mus_ref, w0_ref, wup_ref, a0_ref, aup_ref,
                 gup_ref, kk_ref, ka_ref, rk_ref, lng_ref, lnb_ref, ones_ref, o_ref, sout_ref, nsh_ref, nshs_ref,
                 win_ref, wins_ref, s_scr, *, nb, nc):
    c = pl.program_id(1)
    n = CHUNK

    rows = nc * n
    items = [(b, ci) for b in range(nb) for ci in range(nc)]

    @pl.when(c == 0)
    def _():
        for b in range(nb):
            win_ref[b, 0:8, :] = jnp.zeros((8, 3 * GW), F32)
            wins_ref[b, 0:8, :] = jnp.zeros((8, H_SMALL), F32)
            win_ref[b, 7:8, :] = sh_ref[b]
            wins_ref[b, 7:8, :] = shs_ref[b]
        s_scr[...] = s0_ref[...]

    wup_r, aup_r, gup_r = (_rhs3(r[...], precise=True) for r in (wup_ref, aup_ref, gup_ref))
    per_b = []
    for b in range(nb):
        x = rkv_ref[b]
        xs = hs_ref[b]
        win_ref[b, 8:8 + rows, :] = x
        wins_ref[b, 8:8 + rows, :] = xs
        xm = x + (win_ref[b, 7:7 + rows, :] - x) * mu_ref[...]
        xms = xs + (wins_ref[b, 7:7 + rows, :] - xs) * mus_ref[...]
        last = win_ref[b, rows + 7:rows + 8, :]
        win_ref[b, 7:8, :] = last
        lasts = wins_ref[b, rows + 7:rows + 8, :]
        wins_ref[b, 7:8, :] = lasts
        w_pre = w0_ref[...] + _mm3(_lhs3(jnp.tanh(xms), precise=True), wup_r)
        logw = -jnp.exp(-_softplus(-w_pre) - 0.5)
        a_sig = _sigmoid(a0_ref[...] + _mm3(_lhs3(xms, precise=True), aup_r))
        gate = _mm3(_lhs3(_sigmoid(xms), precise=True), gup_r)
        kx = xm[:, GW:2 * GW]
        per_b.append(dict(rr=xm[:, 0:GW], vv=xm[:, 2 * GW:3 * GW], logw=logw, a_sig=a_sig, gate=gate,
                          kkp=kx * kk_ref[...], k2=kx * (1.0 + (a_sig - 1.0) * ka_ref[...])))

    w = _Wide()
    ones_bd = ones_ref[...]
    tril = _tril_ones(n, BF16)
    sl = lambda ci: slice(ci * n, (ci + 1) * n)
    get = lambda name: [per_b[b][name][sl(ci), :] for b, ci in items]
    rrs, vvs, logws, a_sigs, gates, kkps, k2s = (get(k) for k in ('rr', 'vv', 'logw', 'a_sig', 'gate', 'kkp', 'k2'))
    kks = [kkp * lax.rsqrt(_mm_x01(kkp * kkp, ones_bd) + 1e-6) for kkp in kkps]
    cums = [_tri_mm(tril, logw) for logw in logws]
    w_incls = [jnp.exp(cum) for cum in cums]
    w_lasts = [wi[n - 1:n, :] for wi in w_incls]
    w_invs = [jnp.exp(-cum) for cum in cums]
    ats = [-kk * jnp.exp(cum - logw) for kk, cum, logw in zip(kks, cums, logws)]
    bts = [kk * a_sig * wv for kk, a_sig, wv in zip(kks, a_sigs, w_invs)]
    kts = [k2 * wv for k2, wv in zip(k2s, w_invs)]
    rts = [rr * wi for rr, wi in zip(rrs, w_incls)]
    ar_ls = [_lhs3(jnp.concatenate([at, rt], axis=0)) for at, rt in zip(ats, rts)]
    gbs = [_mm3(ar_l, _rhs3(bt, w), NT) for ar_l, bt in zip(ar_ls, bts)]
    gks = [_mm3(ar_l, _rhs3(kt, w), NT) for ar_l, kt in zip(ar_ls, kts)]
    t_ls = [_lhs3(t) for t in _unit_lower_inverse_wide([jnp.where(w.strict, -gb[:n], 0.0) for gb in gbs], w)]
    arb_ls = [_lhs3(jnp.where(w.incl, gb[n:], 0.0)) for gb in gbs]
    ark_ls = [_lhs3(jnp.where(w.incl, gk[n:], 0.0)) for gk in gks]
    v_rs = [_rhs3(vv, w) for vv in vvs]
    aakvs = [_mm3(_lhs3(jnp.where(w.strict, gk[:n], 0.0)), v_r) for gk, v_r in zip(gks, v_rs)]
    arkvs = [_mm3(ark_l, v_r) for ark_l, v_r in zip(ark_ls, v_rs)]
    bks = [jnp.concatenate([bt * wl, kt * wl], axis=0) for bt, kt, wl in zip(bts, kts, w_lasts)]

    states = [s_scr[b] for b in range(nb)]
    ys = {}
    for ci in range(nc):
        for b in range(nb):
            it = b * nc + ci
            s0 = states[b]
            ars = _mm3(ar_ls[it], _rhs3(s0, w), NT)
            u = _mm3(t_ls[it], _rhs3(ars[:n] + aakvs[it], w))
            ys[it] = ars[n:] + _mm3(arb_ls[it], _rhs3(u, w)) + arkvs[it]
            upd = _mm3_tn(jnp.concatenate([u, vvs[it]], axis=0), bks[it])
            states[b] = s0 * w_lasts[it] + _collapse_heads(upd, w)
    for b in range(nb):
        s_scr[b] = states[b]

    inv_d = 1.0 / HEAD_DIM
    for it, (b, ci) in enumerate(items):
        y = ys[it]
        yc = y - _mm_x01(y, ones_bd) * inv_d
        var_y = _mm_x01(yc * yc, ones_bd) * inv_d
        yn = yc * lax.rsqrt(var_y + GN_EPS) * lng_ref[...] + lnb_ref[...]
        bonus = _mm_x01(rrs[it] * k2s[it] * rk_ref[...], ones_bd) * vvs[it]
        o_ref[b, sl(ci), :] = ((yn + bonus) * gates[it]).astype(o_ref.dtype)

    @pl.when(c == pl.num_programs(1) - 1)
    def _():
        sout_ref[...] = s_scr[...]
        for b in range(nb):
            nsh_ref[b] = win_ref[b, 7:8, :]
            nshs_ref[b] = wins_ref[b, 7:8, :]


def _rwkv_mixer(h3, hs3, shift, s0, lp):
    bsz, l, _ = h3.shape
    nb, nc = _recurrent_tiling(bsz, l)
    rows = nc * CHUNK
    sh_main = shift[:, None, 0:3 * GW]
    sh_small = jnp.pad(shift[:, None, 3 * GW:], ((0, 0), (0, 0), (0, H_SMALL - LORA)))
    mu = lp['rwkv_mu']
    mu_main = mu[None, 0:3 * GW]
    mu_small = jnp.pad(mu[None, 3 * GW:], ((0, 0), (0, H_SMALL - LORA)))
    place = lambda w, off: jnp.zeros((H_SMALL, GW), F32).at[off:off + w.shape[0]].set(w)
    wup = place(lp['rwkv_w_up'], 0)
    aup = place(lp['rwkv_a_up'], RWKV_W_LORA)
    gup = place(lp['rwkv_g_up'], RWKV_W_LORA + RWKV_A_LORA)
    row = lambda v: v.reshape(1, GW)
    full = lambda shape: pl.BlockSpec(shape, lambda bi, i: (0,) * len(shape))
    o, s_new, sh_new, shs_new = pl.pallas_call(
        functools.partial(_rwkv_kernel, nb=nb, nc=nc),
        grid=(bsz // nb, l // rows),
        in_specs=[pl.BlockSpec((nb, rows, 3 * GW), lambda bi, i: (bi, i, H_CRKV // (3 * GW))),
                  pl.BlockSpec((nb, rows, H_SMALL), lambda bi, i: (bi, i, 0)),
                  pl.BlockSpec((nb, 1, 3 * GW), lambda bi, i: (bi, 0, 0)),
                  pl.BlockSpec((nb, 1, H_SMALL), lambda bi, i: (bi, 0, 0)),
                  pl.BlockSpec((nb, HEAD_DIM, GW), lambda bi, i: (bi, 0, 0)),
                  full((1, 3 * GW)), full((1, H_SMALL)), full((1, GW)), full((H_SMALL, GW)), full((1, GW)),
                  full((H_SMALL, GW)), full((H_SMALL, GW)), full((1, GW)), full((1, GW)), full((1, GW)),
                  full((1, GW)), full((1, GW)), full((GW, GW))],
        out_specs=[pl.BlockSpec((nb, rows, GW), lambda bi, i: (bi, i, 0)),
                   pl.BlockSpec((nb, HEAD_DIM, GW), lambda bi, i: (bi, 0, 0)),
                   pl.BlockSpec((nb, 1, 3 * GW), lambda bi, i: (bi, 0, 0)),
                   pl.BlockSpec((nb, 1, H_SMALL), lambda bi, i: (bi, 0, 0))],
        out_shape=[jax.ShapeDtypeStruct((bsz, l, GW), BF16),
                   jax.ShapeDtypeStruct((bsz, HEAD_DIM, GW), F32),
                   jax.ShapeDtypeStruct((bsz, 1, 3 * GW), F32),
                   jax.ShapeDtypeStruct((bsz, 1, H_SMALL), F32)],
        scratch_shapes=[pltpu.VMEM((nb, rows + 8, 3 * GW), F32), pltpu.VMEM((nb, rows + 8, H_SMALL), F32),
                        pltpu.VMEM((nb, HEAD_DIM, GW), F32)],
        compiler_params=_params(2),
        name="rwkv_mixer",
    )(h3, hs3, sh_main, sh_small, _heads_to_wide(s0), mu_main, mu_small, row(lp['rwkv_w0']), wup,
      row(lp['rwkv_a0']), aup, gup, row(lp['rwkv_k_k']), row(lp['rwkv_k_a']), row(lp['rwkv_r_k']),
      row(lp['rwkv_lnx_g']), row(lp['rwkv_lnx_b']), _ones_block_diag())
    new_shift = jnp.concatenate([sh_new[:, 0, :], shs_new[:, 0, S_LORA:S_LORA + LORA]], axis=-1)
    return o, _wide_to_heads(s_new), new_shift


_FB = 128
_SAMPLE_BK = 2048


def _log_forget(hs, bf):
    return -_softplus(-(hs + bf))


def _head_slice(h):
    return slice(h * HEAD_DIM, (h + 1) * HEAD_DIM)


def _attend_heads(qs, kblk, vblk, key_bias, mask, carry):
    m, l, acc = carry
    bq = qs[0].shape[0]
    kb16 = kblk.astype(BF16)
    vb16 = vblk.astype(BF16)
    parts = []
    for h in range(GROUP_HEADS):
        s = _dot(qs[h], kb16[_head_slice(h), :]) + key_bias[h:h + 1, :]
        parts.append(s if mask is None else jnp.where(mask, s, NEG))
    s = jnp.concatenate(parts, axis=0)
    m_new = jnp.maximum(m, jnp.max(s, -1, keepdims=True))
    alpha = jnp.exp(m - m_new)
    p = jnp.exp(s - m_new)
    l = alpha * l + jnp.sum(p, -1, keepdims=True)
    pb = p.astype(BF16)
    pv = jnp.concatenate([_dot(pb[h * bq:(h + 1) * bq, :], vb16[_head_slice(h), :], NT)
                          for h in range(GROUP_HEADS)], axis=0)
    return m_new, l, alpha * acc + pv


def _attend_init(bq):
    rows = GROUP_HEADS * bq
    return jnp.full((rows, 1), NEG, F32), jnp.zeros((rows, 1), F32), jnp.zeros((rows, HEAD_DIM), F32)


def _attend_store(o_ref, carry, bq):
    _, l, acc = carry
    for h in range(GROUP_HEADS):
        o_ref[:, _head_slice(h)] = (acc[h * bq:(h + 1) * bq, :] / l[h * bq:(h + 1) * bq, :]).astype(o_ref.dtype)


def _sel_rows(sel01, x):
    t = x.shape[0]
    r = _dot(sel01, jnp.concatenate(_split3(x), axis=0), NT)
    return r[:, :t] + r[:, t:2 * t] + r[:, 2 * t:]


def _attend_heads_t(qts, k_blk, vt_blk, bias_tiles, mask, carry):
    m, l, acc = carry
    bq = qts[0].shape[1]
    kb16 = k_blk.astype(BF16)
    vb16 = vt_blk.astype(BF16)
    parts = []
    for h in range(GROUP_HEADS):
        s = _dot(kb16[:, _head_slice(h)], qts[h]) + jnp.tile(bias_tiles[h], (1, bq // LANE))
        parts.append(s if mask is None else jnp.where(mask, s, NEG))
    s = jnp.concatenate(parts, axis=1)
    m_new = jnp.maximum(m, jnp.max(s, 0, keepdims=True))
    alpha = jnp.exp(m - m_new)
    p = jnp.exp(s - m_new)
    l = alpha * l + jnp.sum(p, 0, keepdims=True)
    pb = p.astype(BF16)
    pv = jnp.concatenate([_dot(vb16[_head_slice(h), :], pb[:, h * bq:(h + 1) * bq])
                          for h in range(GROUP_HEADS)], axis=1)
    return m_new, l, alpha * acc + pv


def _fox_prompt_kernel(qt_ref, k_ref, vt_ref, hs_ref, bf_ref, o_ref, lf_ref, bias_scr, *, l, bq):
    i = pl.program_id(1)

    @pl.when(i == 0)
    def _():
        ltri = _tril_ones(_FB, BF16)
        carry = jnp.zeros((1, H_SMALL), F32)
        for jb in range(l // _FB):
            rows = slice(jb * _FB, (jb + 1) * _FB)
            lf = _log_forget(hs_ref[rows, :], bf_ref[...])
            lf_ref[rows, :] = lf[:, S_F:S_F + GROUP_HEADS]
            loc = _tri_mm(ltri, lf, 3)
            nf = -(loc + carry)
            for h in range(GROUP_HEADS):
                bias_scr[h, rows, :] = jnp.broadcast_to(nf[:, S_F + h:S_F + h + 1], (_FB, LANE))
            carry = carry + loc[_FB - 1:_FB, :]

    qt = qt_ref[...]
    r_i, c_i = _iota2((bq, bq), 0), _iota2((bq, bq), 1)
    qts = [(qt[_head_slice(h), :] * (HEAD_DIM ** -0.5)).astype(BF16) for h in range(GROUP_HEADS)]

    def step(j, mask, carry):
        keys = pl.ds(pl.multiple_of(j * bq, bq), bq)
        biases = [bias_scr[h, keys, :] for h in range(GROUP_HEADS)]
        return _attend_heads_t(qts, k_ref[keys, :], vt_ref[:, keys], biases, mask, carry)

    init = (jnp.full((1, GROUP_HEADS * bq), NEG, F32), jnp.zeros((1, GROUP_HEADS * bq), F32),
            jnp.zeros((HEAD_DIM, GROUP_HEADS * bq), F32))
    carry = lax.fori_loop(0, i, lambda j, c: step(j, None, c), init)
    _, lsum, acc = step(i, r_i <= c_i, carry)
    o_t = acc / lsum
    o_t = jnp.concatenate([o_t[:, h * bq:(h + 1) * bq] for h in range(GROUP_HEADS)], axis=0)
    o_ref[...] = o_t.T.astype(o_ref.dtype)


def _fox_prompt(h3, hs3, q_t, v_t, bf_pad):
    bsz, l, _ = h3.shape
    bq = min(l, 4 * _FB)
    kern = functools.partial(_fox_prompt_kernel, l=l, bq=bq)
    return pl.pallas_call(
        kern,
        grid=(bsz, l // bq),
        in_specs=[pl.BlockSpec((None, GW, bq), lambda bi, i: (bi, 0, i)),
                  pl.BlockSpec((None, l, GW), lambda bi, i: (bi, 0, H_DK // GW)),
                  pl.BlockSpec((None, GW, l), lambda bi, i: (bi, 0, 0)),
                  pl.BlockSpec((None, l, H_SMALL), lambda bi, i: (bi, 0, 0)),
                  pl.BlockSpec((1, H_SMALL), lambda bi, i: (0, 0))],
        out_specs=[pl.BlockSpec((None, bq, GW), lambda bi, i: (bi, i, 0)),
                   pl.BlockSpec((None, l, GROUP_HEADS), lambda bi, i: (bi, 0, 0))],
        out_shape=[jax.ShapeDtypeStruct((bsz, l, GW), BF16),
                   jax.ShapeDtypeStruct((bsz, l, GROUP_HEADS), F32)],
        scratch_shapes=[pltpu.VMEM((GROUP_HEADS, l, LANE), F32)],
        compiler_params=_params(2),
        name="fox_prompt",
    )(q_t, h3, v_t, hs3, bf_pad)


def _fox_sample_kernel(qt_ref, kt_ref, vt_ref, hs_ref, bf_ref, ck_ref, cv_ref, clf_ref, o_ref, lf_ref, cum_scr,
                       *, l, p):
    nblk = p // _FB
    per = _SAMPLE_BK // _FB

    loc = _mm_x01(clf_ref[...].reshape(nblk * 8, _FB), _triu_ones(_FB, BF16), 3)
    carry = jnp.zeros((8, 1), F32)
    for jb in range(nblk):
        blk = loc[jb * 8:(jb + 1) * 8, :]
        cum_scr[jb // per, :, (jb % per) * _FB:(jb % per + 1) * _FB] = blk + carry
        carry = carry + blk[:, _FB - 1:_FB]
    total = carry

    lf = _log_forget(hs_ref[...], bf_ref[...])
    lf_ref[...] = lf[:, S_F:S_F + GROUP_HEADS]
    cum_r = _mm_x01(_sel_rows(_col_selector(8, H_SMALL, S_F).astype(BF16), lf), _triu_ones(l, BF16), 3)
    q = qt_ref[...].T
    r_i, c_i = _iota2((l, l), 0), _iota2((l, l), 1)
    qs = [(q[:, _head_slice(h)] * (HEAD_DIM ** -0.5)).astype(BF16) for h in range(GROUP_HEADS)]

    def body(j, carry):
        keys = pl.ds(pl.multiple_of(j * _SAMPLE_BK, _SAMPLE_BK), _SAMPLE_BK)
        suffix = total - cum_scr[j]
        return _attend_heads(qs, ck_ref[:, keys], cv_ref[:, keys], suffix, None, carry)

    carry = lax.fori_loop(0, p // _SAMPLE_BK, body, _attend_init(l))
    carry = _attend_heads(qs, kt_ref[...], vt_ref[...], -cum_r, c_i <= r_i, carry)
    _attend_store(o_ref, carry, l)


def _fox_sample(hs3, q_t, k_t, v_t, bf_pad, layer, ck_t, cv_t, clogf):
    bsz, l, _ = hs3.shape
    p = ck_t.shape[-1]
    clf = clogf.reshape(bsz, p // _FB, _FB, GROUP_HEADS).transpose(0, 1, 3, 2)
    clf = jnp.pad(clf, ((0, 0), (0, 0), (0, 8 - GROUP_HEADS), (0, 0)))
    kern = functools.partial(_fox_sample_kernel, l=l, p=p)
    return pl.pallas_call(
        kern,
        grid=(bsz,),
        in_specs=[pl.BlockSpec((None, GW, l), lambda bi: (bi, 0, 0)),
                  pl.BlockSpec((None, GW, l), lambda bi: (bi, 0, 0)),
                  pl.BlockSpec((None, GW, l), lambda bi: (bi, 0, 0)),
                  pl.BlockSpec((None, l, H_SMALL), lambda bi: (bi, 0, 0)),
                  pl.BlockSpec((1, H_SMALL), lambda bi: (0, 0)),
                  pl.BlockSpec((None, None, GW, p), lambda bi: (layer, bi, 0, 0)),
                  pl.BlockSpec((None, None, GW, p), lambda bi: (layer, bi, 0, 0)),
                  pl.BlockSpec((None, p // _FB, 8, _FB), lambda bi: (bi, 0, 0, 0))],
        out_specs=[pl.BlockSpec((None, l, GW), lambda bi: (bi, 0, 0)),
                   pl.BlockSpec((None, l, GROUP_HEADS), lambda bi: (bi, 0, 0))],
        out_shape=[jax.ShapeDtypeStruct((bsz, l, GW), BF16),
                   jax.ShapeDtypeStruct((bsz, l, GROUP_HEADS), F32)],
        scratch_shapes=[pltpu.VMEM((p // _SAMPLE_BK, 8, _SAMPLE_BK), F32)],
        compiler_params=_params(1),
        name="fox_sample",
    )(q_t, k_t, v_t, hs3, bf_pad, ck_t, cv_t, clf)


_R_G, _R_E, _R_ROWS = 0, 8, 32
_MOE_EXPERTS_PER_STEP = 2


def _ffn_kernel(oa_ref, ob_ref, oc_ref, od_ref, x_ref, wo_ref, g1_ref, b1_ref, wrh_ref, wrl_ref, brc_ref, wg_ref,
                wu_ref, wd_ref, g2_ref, b2_ref, y_ref, comb_scr, xb_scr, x1_scr):
    e = pl.program_id(1)
    tm = x_ref.shape[0]
    lane = _iota2((tm, LANE), 1)

    @pl.when(e == 0)
    def _():
        mix = jnp.concatenate([ref[...] for ref in (oa_ref, ob_ref, oc_ref, od_ref)], axis=1)
        x1 = _layer_norm(DN_ALPHA * x_ref[...] + _dot(mix, wo_ref[...]), g1_ref[...], b1_ref[...])
        x1_scr[...] = x1
        y_ref[...] = jnp.zeros_like(y_ref)
        x_hi, x_lo = _split2(x1)
        xb_scr[...] = x_hi
        wr_hi = wrh_ref[...]
        r = _dot(jnp.concatenate([wr_hi, wrl_ref[...]], axis=0), x_hi, NT)
        lt = r[:_R_ROWS] + r[_R_ROWS:] + _dot(wr_hi, x_lo, NT) + brc_ref[...]
        gl = lt[_R_G:_R_G + N_EXPERT_GROUPS]
        grow = _iota2(gl.shape, 0)
        ge = jnp.exp(gl - jnp.max(gl, 0, keepdims=True))
        pg = ge / jnp.sum(ge, 0, keepdims=True)
        gp = jnp.max(pg, 0, keepdims=True)
        gi = jnp.min(jnp.where(pg == gp, grow, N_EXPERT_GROUPS), 0, keepdims=True)
        le = lt[_R_E:_R_E + N_EXPERTS]
        erow = _iota2(le.shape, 0)
        emask = (erow >> 2) == gi
        el = jnp.where(emask, le, NEG)
        ee = jnp.exp(el - jnp.max(el, 0, keepdims=True))
        ep = ee / jnp.sum(ee, 0, keepdims=True)
        m1 = jnp.max(jnp.where(emask, ep, -1.0), 0, keepdims=True)
        i1 = jnp.min(jnp.where(jnp.logical_and(emask, ep == m1), erow, N_EXPERTS), 0, keepdims=True)
        rest = jnp.logical_and(emask, erow != i1)
        m2 = jnp.max(jnp.where(rest, ep, -1.0), 0, keepdims=True)
        i2 = jnp.min(jnp.where(jnp.logical_and(rest, ep == m2), erow, N_EXPERTS), 0, keepdims=True)
        den = m1 + m2
        comb_t = jnp.where(erow == i1, gp * m1 / den, jnp.where(erow == i2, gp * m2 / den, 0.0))
        pieces = jnp.concatenate(_split3(comb_t), axis=0)
        sel = ((_iota2((3 * N_EXPERTS, LANE), 0) & (N_EXPERTS - 1)) == _iota2((3 * N_EXPERTS, LANE), 1)).astype(BF16)
        comb_scr[...] = _dot(pieces, sel, TN)

    xb = xb_scr[...]
    comb = comb_scr[...]
    hidden = []
    for k in range(_MOE_EXPERTS_PER_STEP):
        ce = jnp.sum(jnp.where(lane == e * _MOE_EXPERTS_PER_STEP + k, comb, 0.0), -1, keepdims=True)
        hh = _silu(_mm(xb, wg_ref[k])) * _mm(xb, wu_ref[k])
        hidden.append((hh * ce).astype(BF16))
    hidden = jnp.concatenate(hidden, axis=1)
    w_down = wd_ref[...].reshape(_MOE_EXPERTS_PER_STEP * D_EXPERT, D_MODEL)
    y_ref[...] += _mm(hidden, w_down)

    @pl.when(e == pl.num_programs(1) - 1)
    def _():
        y_ref[...] = _layer_norm(DN_ALPHA * x1_scr[...] + y_ref[...], g2_ref[...], b2_ref[...])


def _ffn(mixers, x2d, w_out_bf16, g1, b1, wr_hi, wr_lo, br_col, layer, wg, wu, wd, g2, b2):
    t = x2d.shape[0]
    tm = min(t, 1024)
    ne = _MOE_EXPERTS_PER_STEP
    mix = pl.BlockSpec((tm, GW), lambda i, e: (i, 0))
    vec = pl.BlockSpec((1, D_MODEL), lambda i, e: (0, 0))
    return pl.pallas_call(
        _ffn_kernel,
        grid=(t // tm, N_EXPERTS // ne),
        in_specs=[mix, mix, mix, mix,
                  pl.BlockSpec((tm, D_MODEL), lambda i, e: (i, 0)),
                  pl.BlockSpec((D_MODEL, D_MODEL), lambda i, e: (0, 0)), vec, vec,
                  pl.BlockSpec((_R_ROWS, D_MODEL), lambda i, e: (0, 0)),
                  pl.BlockSpec((_R_ROWS, D_MODEL), lambda i, e: (0, 0)),
                  pl.BlockSpec((_R_ROWS, 1), lambda i, e: (0, 0)),
                  pl.BlockSpec((None, ne, D_MODEL, D_EXPERT), lambda i, e: (layer, e, 0, 0)),
                  pl.BlockSpec((None, ne, D_MODEL, D_EXPERT), lambda i, e: (layer, e, 0, 0)),
                  pl.BlockSpec((None, ne, D_EXPERT, D_MODEL), lambda i, e: (layer, e, 0, 0)), vec, vec],
        out_specs=pl.BlockSpec((tm, D_MODEL), lambda i, e: (i, 0)),
        out_shape=jax.ShapeDtypeStruct((t, D_MODEL), F32),
        scratch_shapes=[pltpu.VMEM((tm, LANE), F32), pltpu.VMEM((tm, D_MODEL), BF16),
                        pltpu.VMEM((tm, D_MODEL), F32)],
        compiler_params=_params(2),
        name="ffn",
    )(*mixers, x2d, w_out_bf16, g1, b1, wr_hi, wr_lo, br_col, wg, wu, wd, g2, b2)


def _prep_layer(lp):
    w = lp['w_in_t']
    c0 = CONV_COLS
    c1 = c0 + GDN_COLS
    c2 = c1 + RWKV_COLS
    main = jnp.concatenate([w[c0:c0 + 3 * GW], w[c1:c1 + 3 * GW], w[c2 + GW:c2 + 2 * GW],
                            w[0:GW], w[GW:2 * GW], w[c0 + 3 * GW:c0 + 4 * GW]], axis=0)
    kv_t = w[c2:c2 + 3 * GW]
    small = jnp.concatenate([lp['w_in_small'].T,
                             jnp.zeros((H_SMALL - LORA - 3 * GROUP_HEADS, D_MODEL), F32)], axis=0)
    gap = jnp.zeros((_R_E - N_EXPERT_GROUPS, D_MODEL), F32)
    tail = jnp.zeros((_R_ROWS - _R_E - N_EXPERTS, D_MODEL), F32)
    wr_t = jnp.concatenate([lp['router_g_w'].T, gap, lp['router_e_w'].T, tail], axis=0)
    wr_hi = wr_t.astype(BF16)
    wr_lo = (wr_t - wr_hi.astype(F32)).astype(BF16)
    br = jnp.concatenate([lp['router_g_b'], gap[:, 0], lp['router_e_b'], tail[:, 0]])[:, None]
    bf_pad = jnp.zeros((1, H_SMALL), F32).at[0, S_F:S_F + GROUP_HEADS].set(lp['fox_b_f'])
    return dict(w_main=main, w_small=small, w_kv_t=kv_t, wr_hi=wr_hi, wr_lo=wr_lo, br=br, bf_pad=bf_pad)


def _trunk_layer(x, lp, pp, layer, big, conv_buf, gdn_buf, gdn_s, rw_shift, rw_s, fox_cache):
    bsz, l, d = x.shape
    x2d = x.reshape(bsz * l, d)
    h3, hs3, q_t, k_t, v_t = _inproj(x, pp['w_main'], pp['w_small'], pp['w_kv_t'])

    o_a, new_conv = _conv_mixer(h3, conv_buf, lp['conv_w'], lp['conv_b'][None], lp['conv_ln_g'][None],
                                lp['conv_ln_b'][None])
    o_b, new_gdn_s, new_gdn_buf = _gdn_mixer(h3, hs3, gdn_buf, gdn_s, lp['gdn_conv_w'], lp['gdn_a_log'],
                                             lp['gdn_dt_bias'], lp['gdn_norm_w'])
    o_c, new_rw_s, new_shift = _rwkv_mixer(h3, hs3, rw_shift, rw_s, lp)
    if fox_cache is None:
        o_d, logf = _fox_prompt(h3, hs3, q_t, v_t, pp['bf_pad'])
    else:
        o_d, logf = _fox_sample(hs3, q_t, k_t, v_t, pp['bf_pad'], layer, *fox_cache)

    mixers = [o.reshape(bsz * l, GW) for o in (o_a, o_b, o_c, o_d)]
    x2 = _ffn(mixers, x2d, big['w_out'][layer], lp['ln1_g'][None], lp['ln1_b'][None], pp['wr_hi'], pp['wr_lo'],
              pp['br'], layer, *big['experts'], lp['ln2_g'][None], lp['ln2_b'][None])
    return x2.reshape(bsz, l, d), (new_conv, new_gdn_buf, new_gdn_s, new_shift, new_rw_s, k_t, v_t, logf)


def _untranspose(xs_t):
    bsz, _, l = xs_t[0].shape
    return jnp.stack(xs_t).reshape(len(xs_t), bsz, GROUP_HEADS, HEAD_DIM, l).transpose(0, 1, 4, 2, 3)


_LAYER_KEYS = ('w_in_t', 'w_in_small', 'conv_w', 'conv_b', 'conv_ln_g', 'conv_ln_b', 'gdn_conv_w', 'gdn_a_log', 'gdn_dt_bias',
               'gdn_norm_w', 'rwkv_mu', 'rwkv_w0', 'rwkv_w_up', 'rwkv_a0', 'rwkv_a_up', 'rwkv_g_up', 'rwkv_k_k',
               'rwkv_k_a', 'rwkv_r_k', 'rwkv_lnx_g', 'rwkv_lnx_b', 'fox_b_f', 'ln1_g', 'ln1_b',
               'router_g_w', 'router_g_b', 'router_e_w', 'router_e_b', 'ln2_g', 'ln2_b')


def kernel(x_prompt, x_sample, cache_fox_k, cache_fox_v, cache_fox_logf, state_conv, state_gdn_conv, state_gdn, state_rwkv_shift, state_rwkv, w_in, conv_w, conv_b, conv_ln_g, conv_ln_b, gdn_conv_w, gdn_a_log, gdn_dt_bias, gdn_norm_w, rwkv_mu, rwkv_w0, rwkv_w_up, rwkv_a0, rwkv_a_up, rwkv_g_up, rwkv_k_k, rwkv_k_a, rwkv_r_k, rwkv_lnx_g, rwkv_lnx_b, fox_b_f, w_out, ln1_g, ln1_b, router_g_w, router_g_b, router_e_w, router_e_b, exp_w_gate, exp_w_up, exp_w_down, ln2_g, ln2_b):
    w_in_t = w_in.astype(BF16).transpose(0, 2, 1)
    c1 = CONV_COLS + GDN_COLS
    c2 = c1 + RWKV_COLS
    w_in_small = jnp.concatenate([w_in[:, :, c1 + 3 * GW:c2], w_in[:, :, CONV_COLS + 4 * GW:c1],
                                  w_in[:, :, c2 + 3 * GW:]], axis=2)
    weights = dict(zip(_LAYER_KEYS, (w_in_t, w_in_small, conv_w, conv_b, conv_ln_g, conv_ln_b, gdn_conv_w, gdn_a_log,
                                     gdn_dt_bias, gdn_norm_w, rwkv_mu, rwkv_w0, rwkv_w_up, rwkv_a0, rwkv_a_up,
                                     rwkv_g_up, rwkv_k_k, rwkv_k_a, rwkv_r_k, rwkv_lnx_g, rwkv_lnx_b, fox_b_f,
                                     ln1_g, ln1_b, router_g_w, router_g_b, router_e_w, router_e_b,
                                     ln2_g, ln2_b)))
    big = dict(w_out=w_out.astype(BF16), experts=(exp_w_gate, exp_w_up, exp_w_down))
    depth, bs, past = cache_fox_k.shape[:3]
    ck_t = cache_fox_k.transpose(0, 1, 3, 4, 2).reshape(depth, bs, GW, past)
    cv_t = cache_fox_v.transpose(0, 1, 3, 4, 2).reshape(depth, bs, GW, past)
    xp, xs = x_prompt, x_sample
    bp = x_prompt.shape[0]
    outs_p = [[] for _ in range(8)]
    outs_s = [[] for _ in range(8)]
    for l in range(DEPTH):
        lp = {k: v[l] for k, v in weights.items()}
        pp = _prep_layer(lp)
        xp, st_p = _trunk_layer(xp, lp, pp, l, big,
                                jnp.zeros((bp, CONV_WIDTH - 1, GW), F32),
                                jnp.zeros((bp, GDN_CONV_WIDTH - 1, 3 * GW), F32),
                                jnp.zeros((bp, GROUP_HEADS, HEAD_DIM, HEAD_DIM), F32),
                                jnp.zeros((bp, RWKV_COLS), F32),
                                jnp.zeros((bp, GROUP_HEADS, HEAD_DIM, HEAD_DIM), F32),
                                None)
        xs, st_s = _trunk_layer(xs, lp, pp, l, big, state_conv[l], state_gdn_conv[l], state_gdn[l],
                                state_rwkv_shift[l], state_rwkv[l], (ck_t, cv_t, cache_fox_logf[l]))
        for i in range(8):
            outs_p[i].append(st_p[i])
            outs_s[i].append(st_s[i])

    def assemble(outs):
        conv, gdn_buf, gdn_s, shift, rw_s = (jnp.stack(o) for o in outs[:5])
        return conv, gdn_buf, gdn_s, shift, rw_s, _untranspose(outs[5]), _untranspose(outs[6]), jnp.stack(outs[7])

    return (xp, xs, *assemble(outs_p), *assemble(outs_s))
```
